```python
import math
import jax, jax.numpy as jnp
from jax import lax
import numpy as np

D_MODEL = 1024
BATCH = 8
SEQ = 4096
DEPTH = 2

DSWA_PATTERNS = ((128, 1), (512, 4), (2048, 16))
N_GROUPS = 3
HEADS_PER_GROUP = 4
N_HEADS_A = N_GROUPS * HEADS_PER_GROUP
HEAD_DIM_A = 128
A_QKV_W = N_HEADS_A * HEAD_DIM_A
A_OUT = HEADS_PER_GROUP * HEAD_DIM_A
NUM_BUCKETS = 32
MAX_DISTANCE = 2048
N_HEADS_B = 8
HEAD_DIM_B = 128
B_W = N_HEADS_B * HEAD_DIM_B
CONV_WIDTH = 4
CHUNK = 64
D_FF = 2816
N_EXPERTS = 8
TOP_K = 2
D_FF_EXPERT = 3584
MOE_BLOCK = 512
N_DENSE = (DEPTH + 1) // 2
N_MOE = DEPTH // 2
ALPHA = (2 * DEPTH) ** 0.25
BETA = (8 * DEPTH) ** -0.25
LN_EPS = 1e-5
RMS_EPS = 1e-6
IN_SIZES = (A_QKV_W, A_QKV_W, A_QKV_W, 3 * B_W, B_W, N_HEADS_B, N_HEADS_B, D_MODEL, D_MODEL)
N_IN = sum(IN_SIZES)
IN_SPLITS = tuple(int(s) for s in np.cumsum(IN_SIZES)[:-1])

kernel_name = "hybrid_dilated_attn_gated_deltanet_moe_deepnorm"


def layer_norm(x, g, b):
    xf = x.astype(jnp.float32)
    mu = jnp.mean(xf, axis=-1, keepdims=True)
    var = jnp.mean(jnp.square(xf - mu), axis=-1, keepdims=True)
    return ((xf - mu) * lax.rsqrt(var + LN_EPS) * g.astype(jnp.float32) + b.astype(jnp.float32)).astype(x.dtype)


def t5_causal_bucket(dist):
    num_exact = NUM_BUCKETS // 2
    d = jnp.maximum(dist, 1).astype(jnp.float32)
    large = num_exact + (jnp.log(d / num_exact) / math.log(MAX_DISTANCE / num_exact)
                         * (NUM_BUCKETS - num_exact)).astype(jnp.int32)
    large = jnp.minimum(large, NUM_BUCKETS - 1)
    return jnp.where(dist < num_exact, dist, large)


def dilated_window_attention(q, k, v, bias_table, window, dilation):
    B, S, H, Dh = q.shape
    blk = window // dilation
    n = S // dilation
    nb = -(-n // blk)
    n_pad = nb * blk

    def to_strided(t):
        t = t.reshape(B, n, dilation, H, Dh).transpose(0, 2, 1, 3, 4)
        t = jnp.pad(t, ((0, 0), (0, 0), (0, n_pad - n), (0, 0), (0, 0)))
        return t.reshape(B, dilation, nb, blk, H, Dh)

    qs, ks, vs = to_strided(q), to_strided(k), to_strided(v)

    def with_prev(t):
        prev = jnp.pad(t, ((0, 0), (0, 0), (1, 0), (0, 0), (0, 0), (0, 0)))[:, :, :-1]
        return jnp.concatenate([prev, t], axis=3)

    kk, vv = with_prev(ks), with_prev(vs)
    qi = jnp.arange(blk)[:, None] + blk
    kj = jnp.arange(2 * blk)[None, :]
    delta = qi - kj
    band = (delta >= 0) & (delta <= blk)
    mask = band[None] & ((jnp.arange(nb)[:, None, None] > 0) | (kj[None] >= blk))
    bias = bias_table[t5_causal_bucket(jnp.maximum(delta, 0) * dilation)]
    bias = bias.transpose(2, 0, 1).astype(jnp.float32)

    s = jnp.einsum('brcqhd,brckhd->brchqk', qs, kk).astype(jnp.float32) * (Dh ** -0.5)
    s = s + bias[None, None, None]
    s = jnp.where(mask[None, None, :, None], s, -jnp.inf)
    lse = jax.nn.logsumexp(s, axis=-1)
    p = jnp.exp(s - lse[..., None])
    o = jnp.einsum('brchqk,brckhd->brcqhd', p.astype(v.dtype), vv)

    o = o.reshape(B, dilation, n_pad, H, Dh)[:, :, :n].transpose(0, 2, 1, 3, 4).reshape(B, S, H, Dh)
    lse = lse.transpose(0, 1, 2, 4, 3).reshape(B, dilation, n_pad, H)[:, :, :n]
    lse = lse.transpose(0, 2, 1, 3).reshape(B, S, H)
    return o, lse


def causal_depthwise_conv(x, w):
    return lax.conv_general_dilated(x, w[:, None, :].astype(x.dtype), window_strides=(1,),
                                    padding=[(CONV_WIDTH - 1, 0)],
                                    dimension_numbers=('NWC', 'WIO', 'NWC'),
                                    feature_group_count=x.shape[-1])


def gated_delta_rule(q, k, v, g, beta):
    B, S, H, Dk = q.shape
    Dv = v.shape[-1]
    nc = S // CHUNK
    f32 = jnp.float32

    def chunks(t):
        t = t.astype(f32).reshape(B, nc, CHUNK, H, *t.shape[3:])
        return jnp.moveaxis(t, 3, 1)

    q, k, v, g, beta = chunks(q), chunks(k), chunks(v), chunks(g), chunks(beta)
    G = jnp.cumsum(g, axis=-1)
    idx = jnp.arange(CHUNK)
    incl = idx[:, None] >= idx[None, :]
    strict = idx[:, None] > idx[None, :]
    decay = jnp.exp(jnp.where(incl, G[..., :, None] - G[..., None, :], -jnp.inf))
    kk = jnp.einsum('bhntd,bhnsd->bhnts', k, k)
    A = jnp.where(strict, beta[..., None] * kk * decay, 0.0)
    eye = jnp.eye(CHUNK, dtype=f32)
    rhs = jnp.concatenate([beta[..., None] * v, (beta * jnp.exp(G))[..., None] * k], axis=-1)
    sol = lax.linalg.triangular_solve(eye + A, rhs, left_side=True, lower=True, unit_diagonal=True)
    u0, w = sol[..., :Dv], sol[..., Dv:]
    qk = jnp.einsum('bhntd,bhnsd->bhnts', q, k) * decay
    q_dec = q * jnp.exp(G)[..., None]
    k_dec = k * jnp.exp(G[..., -1:] - G)[..., None]
    g_last = jnp.exp(G[..., -1])

    def step(state, xs):
        u0_c, w_c, qk_c, qdec_c, kdec_c, gl_c = xs
        u = u0_c - jnp.einsum('bhtk,bhkv->bhtv', w_c, state)
        o = jnp.einsum('bhtk,bhkv->bhtv', qdec_c, state) + jnp.einsum('bhts,bhsv->bhtv', qk_c, u)
        state = gl_c[..., None, None] * state + jnp.einsum('bhsk,bhsv->bhkv', kdec_c, u)
        return state, o

    xs = tuple(jnp.moveaxis(t, 2, 0) for t in (u0, w, qk, q_dec, k_dec, g_last))
    _, o = lax.scan(step, jnp.zeros((B, H, Dk, Dv), f32), xs)
    return o.transpose(1, 0, 3, 2, 4).reshape(B, S, H, Dv)


def l2norm(t):
    return t * lax.rsqrt(jnp.sum(jnp.square(t), axis=-1, keepdims=True) + RMS_EPS)


def hybrid_mixer(x, rel_bias, w_in, conv_w, a_log, dt_bias, o_norm_w, w_oa, w_ob, w_out):
    B, S, _ = x.shape
    f32 = jnp.float32
    proj = x @ w_in
    qa, ka, va, qkv_b, z, b_raw, a_raw, gate_a, gate_b = jnp.split(proj, IN_SPLITS, axis=-1)

    qa = qa.reshape(B, S, N_HEADS_A, HEAD_DIM_A)
    ka = ka.reshape(B, S, N_HEADS_A, HEAD_DIM_A)
    va = va.reshape(B, S, N_HEADS_A, HEAD_DIM_A)
    outs, lses = [], []
    for gi, (window, dilation) in enumerate(DSWA_PATTERNS):
        hs = slice(gi * HEADS_PER_GROUP, (gi + 1) * HEADS_PER_GROUP)
        o, lse = dilated_window_attention(qa[:, :, hs], ka[:, :, hs], va[:, :, hs], rel_bias[:, hs], window, dilation)
        outs.append(o)
        lses.append(lse)
    wgt = jax.nn.softmax(jnp.stack(lses), axis=0)
    y_a = jnp.einsum('gbsh,gbshd->bshd', wgt, jnp.stack(outs).astype(f32))
    y_a = y_a.reshape(B, S, A_OUT).astype(x.dtype)

    qkv_b = jax.nn.silu(causal_depthwise_conv(qkv_b, conv_w))
    qb, kb, vb = jnp.split(qkv_b, 2 * B_W // 2 * np.array([1, 2]) // 2 * 1 if False else (B_W, 2 * B_W), axis=-1)
    qb = l2norm(qb.reshape(B, S, N_HEADS_B, HEAD_DIM_B).astype(f32)) * (HEAD_DIM_B ** -0.5)
    kb = l2norm(kb.reshape(B, S, N_HEADS_B, HEAD_DIM_B).astype(f32))
    vb = vb.reshape(B, S, N_HEADS_B, HEAD_DIM_B)
    beta = jax.nn.sigmoid(b_raw.astype(f32))
    g = -jnp.exp(a_log.astype(f32)) * jax.nn.softplus(a_raw.astype(f32) + dt_bias.astype(f32))
    ob = gated_delta_rule(qb, kb, vb, g, beta)
    ob = ob * lax.rsqrt(jnp.mean(jnp.square(ob), axis=-1, keepdims=True) + RMS_EPS) * o_norm_w.astype(f32)
    ob = ob * jax.nn.silu(z.reshape(B, S, N_HEADS_B, HEAD_DIM_B).astype(f32))
    y_b = ob.reshape(B, S, B_W).astype(x.dtype)

    merged = jax.nn.sigmoid(gate_a) * (y_a @ w_oa) + jax.nn.sigmoid(gate_b) * (y_b @ w_ob)
    return merged @ w_out


def swiglu(x, w_gate, w_up, w_down):
    return (jax.nn.silu(x @ w_gate) * (x @ w_up)) @ w_down


def moe_swiglu(x, w_router, w_gate, w_up, w_down):
    B, S, D = x.shape
    xt = x.reshape(-1, D)
    N = xt.shape[0]
    logits = (xt @ w_router).astype(jnp.float32)
    top_logit, top_idx = lax.top_k(logits, TOP_K)
    gates = jax.nn.softmax(top_logit, axis=-1)
    A = N * TOP_K
    e_flat = top_idx.reshape(-1).astype(jnp.int32)
    tok_flat = jnp.arange(A, dtype=jnp.int32) // TOP_K
    gate_flat = gates.reshape(-1)
    order = jnp.argsort(e_flat)
    e_sorted = e_flat[order]
    counts = jnp.zeros((N_EXPERTS,), jnp.int32).at[e_flat].add(1)
    padded = (counts + MOE_BLOCK - 1) // MOE_BLOCK * MOE_BLOCK
    start = jnp.cumsum(counts) - counts
    pend = jnp.cumsum(padded)
    pstart = pend - padded
    dest = pstart[e_sorted] + (jnp.arange(A, dtype=jnp.int32) - start[e_sorted])
    n_blocks = -(-A // MOE_BLOCK) + N_EXPERTS
    P = n_blocks * MOE_BLOCK
    tok_buf = jnp.zeros((P,), jnp.int32).at[dest].set(tok_flat[order])
    gate_buf = jnp.zeros((P,), jnp.float32).at[dest].set(gate_flat[order])
    block_expert = jnp.minimum(jnp.searchsorted(pend, jnp.arange(n_blocks, dtype=jnp.int32) * MOE_BLOCK,
                                                side='right'), N_EXPERTS - 1)

    def expert_block(args):
        toks, e = args
        xb = xt[toks]
        return (jax.nn.silu(xb @ w_gate[e]) * (xb @ w_up[e])) @ w_down[e]

    yb = lax.map(expert_block, (tok_buf.reshape(n_blocks, MOE_BLOCK), block_expert))
    yb = yb.reshape(P, D) * gate_buf[:, None].astype(x.dtype)
    return jnp.zeros_like(xt).at[tok_buf].add(yb).reshape(B, S, D)


def setup_inputs(seed: int = 0) -> dict:
    key = jax.random.key(seed)
    ks = jax.random.split(key, 24)
    f32 = jnp.float32

    def nrm(k, shape, scale):
        return jax.random.normal(k, shape, f32) * scale

    dt = jnp.exp(jax.random.uniform(ks[5], (DEPTH, N_HEADS_B), f32, math.log(1e-3), math.log(1e-1)))
    return {
        "x": nrm(ks[0], (BATCH, SEQ, D_MODEL), 1.0),
        "rel_bias": nrm(ks[1], (NUM_BUCKETS, N_HEADS_A), 0.2),
        "w_in": nrm(ks[2], (DEPTH, D_MODEL, N_IN), D_MODEL ** -0.5),
        "conv_w": nrm(ks[3], (DEPTH, CONV_WIDTH, 3 * B_W), CONV_WIDTH ** -0.5),
        "a_log": jnp.log(jax.random.uniform(ks[4], (DEPTH, N_HEADS_B), f32, 1.0, 16.0)),
        "dt_bias": dt + jnp.log(-jnp.expm1(-dt)),
        "o_norm_w": 1.0 + nrm(ks[6], (DEPTH, HEAD_DIM_B), 0.1),
        "w_oa": nrm(ks[7], (DEPTH, A_OUT, D_MODEL), BETA * A_OUT ** -0.5),
        "w_ob": nrm(ks[8], (DEPTH, B_W, D_MODEL), BETA * B_W ** -0.5),
        "w_out": nrm(ks[9], (DEPTH, D_MODEL, D_MODEL), BETA * D_MODEL ** -0.5),
        "ln1_g": 1.0 + nrm(ks[10], (DEPTH, D_MODEL), 0.1),
        "ln1_b": nrm(ks[11], (DEPTH, D_MODEL), 0.01),
        "ffn_w_gate": nrm(ks[12], (N_DENSE, D_MODEL, D_FF), D_MODEL ** -0.5),
        "ffn_w_up": nrm(ks[13], (N_DENSE, D_MODEL, D_FF), D_MODEL ** -0.5),
        "ffn_w_down": nrm(ks[14], (N_DENSE, D_FF, D_MODEL), BETA * D_FF ** -0.5),
        "moe_router": nrm(ks[15], (N_MOE, D_MODEL, N_EXPERTS), D_MODEL ** -0.5),
        "moe_w_gate": nrm(ks[16], (N_MOE, N_EXPERTS, D_MODEL, D_FF_EXPERT), D_MODEL ** -0.5),
        "moe_w_up": nrm(ks[17], (N_MOE, N_EXPERTS, D_MODEL, D_FF_EXPERT), D_MODEL ** -0.5),
        "moe_w_down": nrm(ks[18], (N_MOE, N_EXPERTS, D_FF_EXPERT, D_MODEL), BETA * D_FF_EXPERT ** -0.5),
        "ln2_g": 1.0 + nrm(ks[19], (DEPTH, D_MODEL), 0.1),
        "ln2_b": nrm(ks[20], (DEPTH, D_MODEL), 0.01),
    }


def reference(x, rel_bias, w_in, conv_w, a_log, dt_bias, o_norm_w, w_oa, w_ob, w_out, ln1_g, ln1_b,
              ffn_w_gate, ffn_w_up, ffn_w_down, moe_router, moe_w_gate, moe_w_up, moe_w_down, ln2_g, ln2_b):
    for layer in range(DEPTH):
        mix = hybrid_mixer(x, rel_bias, w_in[layer], conv_w[layer], a_log[layer], dt_bias[layer],
                           o_norm_w[layer], w_oa[layer], w_ob[layer], w_out[layer])
        x = layer_norm(ALPHA * x + mix, ln1_g[layer], ln1_b[layer])
        j = layer // 2
        if layer % 2 == 0:
            f = swiglu(x, ffn_w_gate[j], ffn_w_up[j], ffn_w_down[j])
        else:
            f = moe_swiglu(x, moe_router[j], moe_w_gate[j], moe_w_up[j], moe_w_down[j])
        x = layer_norm(ALPHA * x + f, ln2_g[layer], ln2_b[layer])
    return x
```

```python
import functools
import math

import jax
import jax.numpy as jnp
from jax import lax
from jax.experimental import pallas as pl
from jax.experimental.pallas import tpu as pltpu

F32 = jnp.float32
BF16 = jnp.bfloat16

D_MODEL = 1024
DEPTH = 2
DSWA_PATTERNS = ((128, 1), (512, 4), (2048, 16))
N_GROUPS = 3
HEADS_PER_GROUP = 4
HEAD_DIM = 128
A_QKV_W = N_GROUPS * HEADS_PER_GROUP * HEAD_DIM
A_GROUP_W = HEADS_PER_GROUP * HEAD_DIM
NUM_BUCKETS = 32
MAX_DISTANCE = 2048
N_HEADS_B = 8
B_W = N_HEADS_B * HEAD_DIM
CONV_WIDTH = 4
N_EXPERTS = 8
TOP_K = 2
ALPHA = (2 * DEPTH) ** 0.25
LN_EPS = 1e-5
RMS_EPS = 1e-6

LANES = 128
SUBLANES = 8
VMEM_LIMIT = 56 * 1024 * 1024

ATT_BLK = 128
GDN_CHUNK = 128
MM_TM = 1024
MM_TN = 512
OUT_TM = 256
FFN_TM = 512
FFN_TF = 1408
MOE_TB = 512
MOE_TF = 1792
CMB_TM = 256

NEG_BIG = -1e30


def _cparams(sem, vmem=VMEM_LIMIT, **kw):
    return pltpu.CompilerParams(dimension_semantics=sem, vmem_limit_bytes=vmem, **kw)


def _bdot(a, b):
    return jnp.dot(a.astype(BF16), b.astype(BF16), preferred_element_type=F32)


def _bdot_nt(a, b):
    return lax.dot_general(a.astype(BF16), b.astype(BF16), (((1,), (1,)), ((), ())),
                           preferred_element_type=F32)


def _bdot_tn(a, b):
    return lax.dot_general(a.astype(BF16), b.astype(BF16), (((0,), (0,)), ((), ())),
                           preferred_element_type=F32)


def _sigmoid(v):
    return 1.0 / (1.0 + jnp.exp(-v))


def _silu(v):
    return v * _sigmoid(v)


def _layer_norm(v, g, b):
    mu = jnp.mean(v, axis=-1, keepdims=True)
    d = v - mu
    var = jnp.mean(d * d, axis=-1, keepdims=True)
    return d * lax.rsqrt(var + LN_EPS) * g + b


def _mm_kernel(x_ref, w_ref, o_ref):
    o_ref[...] = _bdot(x_ref[...], w_ref[...]).astype(o_ref.dtype)


def _matmul(x, w, out_dtype, tm=MM_TM, tn=MM_TN):
    m, k = x.shape
    n = w.shape[1]
    tn = min(tn, n)
    return pl.pallas_call(
        _mm_kernel,
        grid=(m // tm, n // tn),
        in_specs=[pl.BlockSpec((tm, k), lambda i, j: (i, 0)),
                  pl.BlockSpec((k, tn), lambda i, j: (0, j))],
        out_specs=pl.BlockSpec((tm, tn), lambda i, j: (i, j)),
        out_shape=jax.ShapeDtypeStruct((m, n), out_dtype),
        compiler_params=_cparams(("parallel", "parallel")),
    )(x, w)


def _dswa_kernel(q_ref, kc_ref, kp_ref, vc_ref, vp_ref, bias_ref, o_ref, lse_ref, *, tq):
    j = pl.program_id(2)
    nqb = tq // ATT_BLK
    q = q_ref[...]
    kwin = jnp.concatenate([kp_ref[...], kc_ref[...]], axis=0)
    vwin = jnp.concatenate([vp_ref[...], vc_ref[...]], axis=0)
    row = lax.broadcasted_iota(jnp.int32, (ATT_BLK, 2 * ATT_BLK), 0)
    col = lax.broadcasted_iota(jnp.int32, (ATT_BLK, 2 * ATT_BLK), 1)
    delta = row + ATT_BLK - col
    band = (delta >= 0) & (delta <= ATT_BLK)
    band_first = band & ((col >= ATT_BLK) | (j > 0))
    lane = lax.broadcasted_iota(jnp.int32, (ATT_BLK, LANES), 1)
    scale = HEAD_DIM ** -0.5
    for c in range(nqb):
        mask = band_first if c == 0 else band
        lse_tile = jnp.zeros((ATT_BLK, LANES), F32)
        for h in range(HEADS_PER_GROUP):
            hs = slice(h * HEAD_DIM, (h + 1) * HEAD_DIM)
            qh = q[c * ATT_BLK:(c + 1) * ATT_BLK, hs]
            kh = kwin[c * ATT_BLK:(c + 2) * ATT_BLK, hs]
            vh = vwin[c * ATT_BLK:(c + 2) * ATT_BLK, hs]
            s = _bdot_nt(qh, kh) * scale + bias_ref[h]
            s = jnp.where(mask, s, NEG_BIG)
            m = jnp.max(s, axis=-1, keepdims=True)
            p = jnp.exp(s - m)
            l = jnp.sum(p, axis=-1, keepdims=True)
            o = _bdot(p, vh) / l
            o_ref[c * ATT_BLK:(c + 1) * ATT_BLK, hs] = o
            lse_tile = jnp.where(lane == h, m + jnp.log(l), lse_tile)
        lse_ref[c * ATT_BLK:(c + 1) * ATT_BLK, :] = lse_tile


def _dswa_group(proj_a, bias, gi, dilation, batch, seq):
    n = seq // dilation
    tq = min(512, n)
    nqb = tq // ATT_BLK
    ncol = 3 * A_QKV_W // A_GROUP_W
    pa = proj_a.reshape(batch, n, dilation * 3 * A_QKV_W)
    qoff, koff, voff = gi, N_GROUPS + gi, 2 * N_GROUPS + gi

    def cur(off):
        return pl.BlockSpec((None, tq, A_GROUP_W), lambda b, r, j: (b, j, r * ncol + off))

    def prev(off):
        return pl.BlockSpec((None, ATT_BLK, A_GROUP_W),
                            lambda b, r, j: (b, jnp.maximum(j * nqb - 1, 0), r * ncol + off))

    o, lse = pl.pallas_call(
        functools.partial(_dswa_kernel, tq=tq),
        grid=(batch, dilation, n // tq),
        in_specs=[cur(qoff), cur(koff), prev(koff), cur(voff), prev(voff),
                  pl.BlockSpec((HEADS_PER_GROUP, ATT_BLK, 2 * ATT_BLK), lambda b, r, j: (0, 0, 0))],
        out_specs=[pl.BlockSpec((None, tq, A_GROUP_W), lambda b, r, j: (b, j, r)),
                   pl.BlockSpec((None, tq, LANES), lambda b, r, j: (b, j, r))],
        out_shape=[jax.ShapeDtypeStruct((batch, n, dilation * A_GROUP_W), F32),
                   jax.ShapeDtypeStruct((batch, n, dilation * LANES), F32)],
        compiler_params=_cparams(("parallel", "parallel", "parallel")),
    )(pa, pa, pa, pa, pa, bias)
    return o.reshape(batch * seq, A_GROUP_W), lse.reshape(batch * seq, LANES)


def _t5_causal_bucket(dist):
    num_exact = NUM_BUCKETS // 2
    d = jnp.maximum(dist, 1).astype(F32)
    large = num_exact + (jnp.log(d / num_exact) / math.log(MAX_DISTANCE / num_exact)
                         * (NUM_BUCKETS - num_exact)).astype(jnp.int32)
    large = jnp.minimum(large, NUM_BUCKETS - 1)
    return jnp.where(dist < num_exact, dist, large)


def _band_bias(rel_bias, gi, dilation):
    qi = jnp.arange(ATT_BLK)[:, None] + ATT_BLK
    kj = jnp.arange(2 * ATT_BLK)[None, :]
    delta = jnp.maximum(qi - kj, 0) * dilation
    table = rel_bias[:, gi * HEADS_PER_GROUP:(gi + 1) * HEADS_PER_GROUP]
    return table[_t5_causal_bucket(delta)].transpose(2, 0, 1).astype(F32)


def _unit_lower_inverse(a, row, col):
    base = 16
    eye = (row == col).astype(F32)
    same = lambda sz: (row // sz) == (col // sz)
    a_d = jnp.where(same(base), a, 0.0)
    t = eye - a_d
    p = a_d
    for _ in range(3):
        p = _bdot(p, p)
        t = t + _bdot(t, p)
    sz = 2 * base
    while sz <= GDN_CHUNK:
        a_off = jnp.where(same(sz) & jnp.logical_not(same(sz // 2)), a, 0.0)
        t = t - _bdot(t, _bdot(a_off, t))
        sz *= 2
    return t


def _gdn_kernel(x_ref, halo_ref, z_ref, ba_ref, cw_ref, alog_ref, dtb_ref, onw_ref, y_ref, state_ref):
    c = pl.program_id(1)
    C = GDN_CHUNK

    @pl.when(c == 0)
    def _():
        state_ref[...] = jnp.zeros_like(state_ref)

    keep_halo = (c > 0).astype(F32)
    row = lax.broadcasted_iota(jnp.int32, (C, C), 0)
    col = lax.broadcasted_iota(jnp.int32, (C, C), 1)
    incl = row >= col
    strict = row > col

    ba = ba_ref[...]
    beta_all = _sigmoid(ba[:, :LANES])
    a_in = ba[:, LANES:] + dtb_ref[...]
    softplus = jnp.maximum(a_in, 0.0) + jnp.log(1.0 + jnp.exp(-jnp.abs(a_in)))
    g_all = -jnp.exp(alog_ref[...]) * softplus
    ones_l = incl.astype(BF16)
    g1 = g_all.astype(BF16)
    r1 = g_all - g1.astype(F32)
    g2 = r1.astype(BF16)
    g3 = (r1 - g2.astype(F32)).astype(BF16)
    G = (jnp.dot(ones_l, g1, preferred_element_type=F32) + jnp.dot(ones_l, g2, preferred_element_type=F32)
         + jnp.dot(ones_l, g3, preferred_element_type=F32))
    GT = G.T

    def conv_silu(col0):
        cs = slice(col0, col0 + HEAD_DIM)
        xc = x_ref[:, cs]
        xf = jnp.concatenate([halo_ref[:, cs] * keep_halo, xc], axis=0)
        w = cw_ref[:, cs]
        y = w[CONV_WIDTH - 1:CONV_WIDTH] * xc
        for i in range(CONV_WIDTH - 1):
            off = SUBLANES - (CONV_WIDTH - 1) + i
            y = y + w[i:i + 1] * xf[off:off + C]
        return _silu(y)

    def l2norm(t):
        return t * lax.rsqrt(jnp.sum(t * t, axis=-1, keepdims=True) + RMS_EPS)

    for h in range(N_HEADS_B):
        q = l2norm(conv_silu(h * HEAD_DIM)) * (HEAD_DIM ** -0.5)
        k = l2norm(conv_silu(B_W + h * HEAD_DIM))
        v = conv_silu(2 * B_W + h * HEAD_DIM)
        beta = beta_all[:, h:h + 1]
        gc = G[:, h:h + 1]
        gr = GT[h:h + 1, :]
        g_last = G[C - 1:C, h:h + 1]
        decay = jnp.exp(jnp.where(incl, gc - gr, NEG_BIG))
        eg = jnp.exp(gc)
        kq = _bdot_nt(jnp.concatenate([k, q], axis=0), k)
        a = jnp.where(strict, beta * kq[:C] * decay, 0.0)
        t_inv = _unit_lower_inverse(a, row, col)
        rhs = jnp.concatenate([beta * v, (beta * eg) * k], axis=1)
        sol = _bdot(t_inv, rhs)
        u0, w = sol[:, :HEAD_DIM], sol[:, HEAD_DIM:]
        qk = kq[C:] * decay
        q_dec = q * eg
        k_dec = k * jnp.exp(g_last - gc)

        state = state_ref[h]
        ws = _bdot(jnp.concatenate([w, q_dec], axis=0), state)
        u = u0 - ws[:C]
        o = ws[C:] + _bdot(qk, u)
        state_ref[h] = jnp.exp(g_last) * state + _bdot_tn(k_dec, u)

        o = o * lax.rsqrt(jnp.mean(o * o, axis=-1, keepdims=True) + RMS_EPS) * onw_ref[...]
        hs = slice(h * HEAD_DIM, (h + 1) * HEAD_DIM)
        y_ref[:, hs] = (o * _silu(z_ref[:, hs])).astype(y_ref.dtype)


def _gated_deltanet(proj_b, proj_ba, conv_w, a_log, dt_bias, o_norm_w, batch, seq):
    C = GDN_CHUNK
    wb = proj_b.shape[1]
    pb = proj_b.reshape(batch, seq, wb)
    pba = proj_ba.reshape(batch, seq, 2 * LANES)
    pad = lambda t: jnp.pad(t.astype(F32), (0, LANES - t.shape[0])).reshape(1, LANES)
    const = lambda shape: pl.BlockSpec(shape, lambda b, c: (0,) * len(shape))
    y = pl.pallas_call(
        _gdn_kernel,
        grid=(batch, seq // C),
        in_specs=[pl.BlockSpec((None, C, 3 * B_W), lambda b, c: (b, c, 0)),
                  pl.BlockSpec((None, SUBLANES, 3 * B_W),
                               lambda b, c: (b, jnp.maximum(c * (C // SUBLANES) - 1, 0), 0)),
                  pl.BlockSpec((None, C, B_W), lambda b, c: (b, c, 3)),
                  pl.BlockSpec((None, C, 2 * LANES), lambda b, c: (b, c, 0)),
                  const((CONV_WIDTH, 3 * B_W)), const((1, LANES)), const((1, LANES)), const((1, LANES))],
        out_specs=pl.BlockSpec((None, C, B_W), lambda b, c: (b, c, 0)),
        out_shape=jax.ShapeDtypeStruct((batch, seq, B_W), BF16),
        scratch_shapes=[pltpu.VMEM((N_HEADS_B, HEAD_DIM, HEAD_DIM), F32)],
        compiler_params=_cparams(("parallel", "arbitrary")),
    )(pb, pb, pb, pba, conv_w.astype(F32), pad(a_log), pad(dt_bias), o_norm_w.astype(F32).reshape(1, LANES))
    return y.reshape(batch * seq, B_W)


def _mix_out_kernel(o0, o1, o2, l0, l1, l2, yb_ref, ga_ref, gb_ref, x_ref, woa_ref, wob_ref, wout_ref,
                    g_ref, b_ref, out_ref):
    outs = (o0, o1, o2)
    lses = (l0[...], l1[...], l2[...])
    ya = []
    for h in range(HEADS_PER_GROUP):
        ls = [t[:, h:h + 1] for t in lses]
        m = jnp.maximum(jnp.maximum(ls[0], ls[1]), ls[2])
        es = [jnp.exp(t - m) for t in ls]
        inv = 1.0 / (es[0] + es[1] + es[2])
        hs = slice(h * HEAD_DIM, (h + 1) * HEAD_DIM)
        ya.append((es[0] * inv) * outs[0][:, hs] + (es[1] * inv) * outs[1][:, hs] + (es[2] * inv) * outs[2][:, hs])
    ya = jnp.concatenate(ya, axis=1)
    pa = _bdot(ya, woa_ref[...])
    pb = _bdot(yb_ref[...], wob_ref[...])
    merged = _sigmoid(ga_ref[...]) * pa + _sigmoid(gb_ref[...]) * pb
    mix = _bdot(merged, wout_ref[...])
    out_ref[...] = _layer_norm(ALPHA * x_ref[...] + mix, g_ref[...], b_ref[...])


def _mix_out(outs, lses, yb, proj_b, x, w_oa, w_ob, w_out, ln_g, ln_b, tm=OUT_TM):
    n = x.shape[0]
    rowblk = lambda w, cb=0: pl.BlockSpec((tm, w), lambda i: (i, cb))
    const = lambda a: pl.BlockSpec(a.shape, lambda i: (0, 0))
    wa, wb, wo = w_oa.astype(BF16), w_ob.astype(BF16), w_out.astype(BF16)
    g, b = ln_g.reshape(1, D_MODEL), ln_b.reshape(1, D_MODEL)
    return pl.pallas_call(
        _mix_out_kernel,
        grid=(n // tm,),
        in_specs=[rowblk(A_GROUP_W)] * 3 + [rowblk(LANES)] * 3
        + [rowblk(B_W), rowblk(D_MODEL, 4), rowblk(D_MODEL, 5), rowblk(D_MODEL),
           const(wa), const(wb), const(wo), const(g), const(b)],
        out_specs=rowblk(D_MODEL),
        out_shape=jax.ShapeDtypeStruct((n, D_MODEL), F32),
        compiler_params=_cparams(("parallel",)),
    )(*outs, *lses, yb, proj_b, proj_b, x, wa, wb, wo, g, b)


def _ffn_kernel(x_ref, wg_ref, wu_ref, wd_ref, g_ref, b_ref, out_ref, xb_ref, acc_ref):
    f = pl.program_id(1)

    @pl.when(f == 0)
    def _():
        xb_ref[...] = x_ref[...].astype(BF16)
        acc_ref[...] = jnp.zeros_like(acc_ref)

    xb = xb_ref[...]
    gate = jnp.dot(xb, wg_ref[...], preferred_element_type=F32)
    up = jnp.dot(xb, wu_ref[...], preferred_element_type=F32)
    acc_ref[...] += _bdot(_silu(gate) * up, wd_ref[...])

    @pl.when(f == pl.num_programs(1) - 1)
    def _():
        out_ref[...] = _layer_norm(ALPHA * x_ref[...] + acc_ref[...], g_ref[...], b_ref[...])


def _dense_ffn(x, w_gate, w_up, w_down, ln_g, ln_b, tm=FFN_TM, tf=FFN_TF):
    n = x.shape[0]
    dff = w_gate.shape[1]
    g, b = ln_g.reshape(1, D_MODEL), ln_b.reshape(1, D_MODEL)
    return pl.pallas_call(
        _ffn_kernel,
        grid=(n // tm, dff // tf),
        in_specs=[pl.BlockSpec((tm, D_MODEL), lambda i, f: (i, 0)),
                  pl.BlockSpec((D_MODEL, tf), lambda i, f: (0, f)),
                  pl.BlockSpec((D_MODEL, tf), lambda i, f: (0, f)),
                  pl.BlockSpec((tf, D_MODEL), lambda i, f: (f, 0)),
                  pl.BlockSpec((1, D_MODEL), lambda i, f: (0, 0)),
                  pl.BlockSpec((1, D_MODEL), lambda i, f: (0, 0))],
        out_specs=pl.BlockSpec((tm, D_MODEL), lambda i, f: (i, 0)),
        out_shape=jax.ShapeDtypeStruct((n, D_MODEL), F32),
        scratch_shapes=[pltpu.VMEM((tm, D_MODEL), BF16), pltpu.VMEM((tm, D_MODEL), F32)],
        compiler_params=_cparams(("parallel", "arbitrary")),
    )(x, w_gate.astype(BF16), w_up.astype(BF16), w_down.astype(BF16), g, b)


def _gather_rows(idx_ref, src_hbm, dst, sem, count):
    def body(i, carry):
        pltpu.make_async_copy(src_hbm.at[pl.ds(idx_ref[0, i], 1)], dst.at[pl.ds(i, 1)], sem).start()
        return carry
    lax.fori_loop(0, count, body, 0, unroll=8)


def _wait_rows(src_hbm, dst, sem, count):
    pltpu.make_async_copy(src_hbm.at[pl.ds(0, count)], dst, sem).wait()


def _moe_kernel(be_ref, tok0_ref, tokn_ref, x_hbm, gate_ref, wg_ref, wu_ref, wd_ref, y_ref,
                xbuf, xb_ref, acc_ref, sem):
    del be_ref
    b = pl.program_id(0)
    f = pl.program_id(1)
    nb = pl.num_programs(0)
    slot = b % 2

    @pl.when(f == 0)
    def _():
        @pl.when(b == 0)
        def _():
            _gather_rows(tok0_ref, x_hbm, xbuf.at[0], sem.at[0], MOE_TB)

        _wait_rows(x_hbm, xbuf.at[slot], sem.at[slot], MOE_TB)

        @pl.when(b + 1 < nb)
        def _():
            _gather_rows(tokn_ref, x_hbm, xbuf.at[1 - slot], sem.at[1 - slot], MOE_TB)

        xb_ref[...] = xbuf[slot].astype(BF16)
        acc_ref[...] = jnp.zeros_like(acc_ref)

    xb = xb_ref[...]
    gate = jnp.dot(xb, wg_ref[...], preferred_element_type=F32)
    up = jnp.dot(xb, wu_ref[...], preferred_element_type=F32)
    acc_ref[...] += _bdot(_silu(gate) * up, wd_ref[...])

    @pl.when(f == pl.num_programs(1) - 1)
    def _():
        y_ref[...] = acc_ref[...] * gate_ref[...]


def _moe_experts(x, tok_buf, gate_buf, block_expert, w_gate, w_up, w_down, tb=MOE_TB, tf=MOE_TF):
    nblk = block_expert.shape[0]
    dffe = w_gate.shape[2]
    tok3 = tok_buf.reshape(nblk, 1, tb)
    smem_blk = lambda imap: pl.BlockSpec((None, 1, tb), imap, memory_space=pltpu.SMEM)
    grid_spec = pltpu.PrefetchScalarGridSpec(
        num_scalar_prefetch=1,
        grid=(nblk, dffe // tf),
        in_specs=[smem_blk(lambda b, f, be: (0, 0, 0)),
                  smem_blk(lambda b, f, be: (jnp.minimum(b + 1, nblk - 1), 0, 0)),
                  pl.BlockSpec(memory_space=pl.ANY),
                  pl.BlockSpec((tb, 1), lambda b, f, be: (b, 0)),
                  pl.BlockSpec((None, D_MODEL, tf), lambda b, f, be: (be[b], 0, f)),
                  pl.BlockSpec((None, D_MODEL, tf), lambda b, f, be: (be[b], 0, f)),
                  pl.BlockSpec((None, tf, D_MODEL), lambda b, f, be: (be[b], f, 0))],
        out_specs=pl.BlockSpec((tb, D_MODEL), lambda b, f, be: (b, 0)),
        scratch_shapes=[pltpu.VMEM((2, tb, D_MODEL), F32), pltpu.VMEM((tb, D_MODEL), BF16),
                        pltpu.VMEM((tb, D_MODEL), F32), pltpu.SemaphoreType.DMA((2,))],
    )
    return pl.pallas_call(
        _moe_kernel,
        grid_spec=grid_spec,
        out_shape=jax.ShapeDtypeStruct((nblk * tb, D_MODEL), F32),
        compiler_params=_cparams(("arbitrary", "arbitrary"), disable_bounds_checks=True),
    )(block_expert, tok3, tok3, x, gate_buf.reshape(nblk * tb, 1),
      w_gate.astype(BF16), w_up.astype(BF16), w_down.astype(BF16))


def _combine_kernel(pos_ref, y_hbm, x_ref, g_ref, b_ref, out_ref, ybuf, sem):
    tm = out_ref.shape[0]
    _gather_rows(pos_ref, y_hbm, ybuf, sem.at[0], TOP_K * tm)
    _wait_rows(y_hbm, ybuf, sem.at[0], TOP_K * tm)
    f = ybuf[:tm] + ybuf[tm:]
    out_ref[...] = _layer_norm(ALPHA * x_ref[...] + f, g_ref[...], b_ref[...])


def _moe_combine(y, pos, x, ln_g, ln_b, tm=CMB_TM):
    n = x.shape[0]
    pos3 = pos.reshape(n // tm, tm, TOP_K).transpose(0, 2, 1).reshape(n // tm, 1, TOP_K * tm)
    g, b = ln_g.reshape(1, D_MODEL), ln_b.reshape(1, D_MODEL)
    return pl.pallas_call(
        _combine_kernel,
        grid=(n // tm,),
        in_specs=[pl.BlockSpec((None, 1, TOP_K * tm), lambda i: (i, 0, 0), memory_space=pltpu.SMEM),
                  pl.BlockSpec(memory_space=pl.ANY),
                  pl.BlockSpec((tm, D_MODEL), lambda i: (i, 0)),
                  pl.BlockSpec((1, D_MODEL), lambda i: (0, 0)),
                  pl.BlockSpec((1, D_MODEL), lambda i: (0, 0))],
        out_specs=pl.BlockSpec((tm, D_MODEL), lambda i: (i, 0)),
        out_shape=jax.ShapeDtypeStruct((n, D_MODEL), F32),
        scratch_shapes=[pltpu.VMEM((TOP_K * tm, D_MODEL), F32), pltpu.SemaphoreType.DMA((1,))],
        compiler_params=_cparams(("arbitrary",), disable_bounds_checks=True),
    )(pos3, y, x, g, b)


def _moe_routing(logits, tb):
    n = logits.shape[0]
    top_logit, top_idx = lax.top_k(logits, TOP_K)
    gates = jax.nn.softmax(top_logit, axis=-1)
    na = n * TOP_K
    e_flat = top_idx.reshape(-1).astype(jnp.int32)
    tok_flat = jnp.arange(na, dtype=jnp.int32) // TOP_K
    order = jnp.argsort(e_flat)
    e_sorted = e_flat[order]
    counts = jnp.zeros((N_EXPERTS,), jnp.int32).at[e_flat].add(1)
    padded = (counts + tb - 1) // tb * tb
    start = jnp.cumsum(counts) - counts
    pend = jnp.cumsum(padded)
    pstart = pend - padded
    dest = pstart[e_sorted] + (jnp.arange(na, dtype=jnp.int32) - start[e_sorted])
    nblk = -(-na // tb) + N_EXPERTS
    tok_buf = jnp.zeros((nblk * tb,), jnp.int32).at[dest].set(tok_flat[order])
    gate_buf = jnp.zeros((nblk * tb,), F32).at[dest].set(gates.reshape(-1)[order])
    block_expert = jnp.minimum(jnp.searchsorted(pend, jnp.arange(nblk, dtype=jnp.int32) * tb, side='right'),
                               N_EXPERTS - 1).astype(jnp.int32)
    pos = jnp.zeros((na,), jnp.int32).at[order].set(dest).reshape(n, TOP_K)
    return tok_buf, gate_buf, block_expert, pos


def _moe_ffn(x, w_router, w_gate, w_up, w_down, ln_g, ln_b):
    wr = jnp.pad(w_router, ((0, 0), (0, LANES - N_EXPERTS))).astype(BF16)
    logits = _matmul(x, wr, F32)[:, :N_EXPERTS]
    tok_buf, gate_buf, block_expert, pos = _moe_routing(logits, MOE_TB)
    y = _moe_experts(x, tok_buf, gate_buf, block_expert, w_gate, w_up, w_down)
    return _moe_combine(y, pos, x, ln_g, ln_b)


def _split_w_in(w):
    a_end = 3 * A_QKV_W
    bz_end = a_end + 3 * B_W + B_W
    w_a = w[:, :a_end]
    w_b = jnp.concatenate([w[:, a_end:bz_end], w[:, bz_end + 2 * N_HEADS_B:]], axis=1)
    zpad = jnp.zeros((w.shape[0], LANES - N_HEADS_B), w.dtype)
    w_ba = jnp.concatenate([w[:, bz_end:bz_end + N_HEADS_B], zpad,
                            w[:, bz_end + N_HEADS_B:bz_end + 2 * N_HEADS_B], zpad], axis=1)
    return w_a.astype(BF16), w_b.astype(BF16), w_ba.astype(BF16)


def _hybrid_layer(x, batch, seq, rel_bias, w_in, conv_w, a_log, dt_bias, o_norm_w, w_oa, w_ob, w_out, ln_g, ln_b):
    w_a, w_b, w_ba = _split_w_in(w_in)
    proj_a = _matmul(x, w_a, BF16)
    proj_b = _matmul(x, w_b, F32)
    proj_ba = _matmul(x, w_ba, F32)
    outs, lses = [], []
    for gi, (_, dilation) in enumerate(DSWA_PATTERNS):
        o, lse = _dswa_group(proj_a, _band_bias(rel_bias, gi, dilation), gi, dilation, batch, seq)
        outs.append(o)
        lses.append(lse)
    yb = _gated_deltanet(proj_b, proj_ba, conv_w, a_log, dt_bias, o_norm_w, batch, seq)
    return _mix_out(outs, lses, yb, proj_b, x, w_oa, w_ob, w_out, ln_g, ln_b)


def kernel(x, rel_bias, w_in, conv_w, a_log, dt_bias, o_norm_w, w_oa, w_ob, w_out, ln1_g, ln1_b,
           ffn_w_gate, ffn_w_up, ffn_w_down, moe_router, moe_w_gate, moe_w_up, moe_w_down, ln2_g, ln2_b):
    batch, seq, d = x.shape
    h = x.reshape(batch * seq, d)
    for layer in range(DEPTH):
        h = _hybrid_layer(h, batch, seq, rel_bias, w_in[layer], conv_w[layer], a_log[layer], dt_bias[layer],
                          o_norm_w[layer], w_oa[layer], w_ob[layer], w_out[layer], ln1_g[layer], ln1_b[layer])
        j = layer // 2
        if layer % 2 == 0:
            h = _dense_ffn(h, ffn_w_gate[j], ffn_w_up[j], ffn_w_down[j], ln2_g[layer], ln2_b[layer])
        else:
            h = _moe_ffn(h, moe_router[j], moe_w_gate[j], moe_w_up[j], moe_w_down[j], ln2_g[layer], ln2_b[layer])
    return h.reshape(batch, seq, d)
```

```python
import functools
import math

import jax
import jax.numpy as jnp
from jax import lax
from jax.experimental import pallas as pl
from jax.experimental.pallas import tpu as pltpu

F32 = jnp.float32
BF16 = jnp.bfloat16

D_MODEL = 1024
DEPTH = 2
DSWA_PATTERNS = ((128, 1), (512, 4), (2048, 16))
N_GROUPS = 3
HEADS_PER_GROUP = 4
HEAD_DIM = 128
A_QKV_W = N_GROUPS * HEADS_PER_GROUP * HEAD_DIM
A_GROUP_W = HEADS_PER_GROUP * HEAD_DIM
NUM_BUCKETS = 32
MAX_DISTANCE = 2048
N_HEADS_B = 8
B_W = N_HEADS_B * HEAD_DIM
CONV_WIDTH = 4
N_EXPERTS = 8
TOP_K = 2
ALPHA = (2 * DEPTH) ** 0.25
LN_EPS = 1e-5
RMS_EPS = 1e-6

LANES = 128
SUBLANES = 8
VMEM_LIMIT = 56 * 1024 * 1024

ATT_BLK = 128
GDN_CHUNK = 128
MM_TM = 1024
MM_TN = 512
OUT_TM = 256
FFN_TM = 512
FFN_TF = 1408
MOE_TB = 512
MOE_TF = 1792
CMB_TM = 256

NEG_BIG = -1e30


def _cparams(sem, vmem=VMEM_LIMIT, **kw):
    return pltpu.CompilerParams(dimension_semantics=sem, vmem_limit_bytes=vmem, **kw)


def _bdot(a, b):
    return jnp.dot(a.astype(BF16), b.astype(BF16), preferred_element_type=F32)


def _bdot_nt(a, b):
    return lax.dot_general(a.astype(BF16), b.astype(BF16), (((1,), (1,)), ((), ())),
                           preferred_element_type=F32)


def _bdot_tn(a, b):
    return lax.dot_general(a.astype(BF16), b.astype(BF16), (((0,), (0,)), ((), ())),
                           preferred_element_type=F32)


def _sigmoid(v):
    return 1.0 / (1.0 + jnp.exp(-v))


def _silu(v):
    return v * _sigmoid(v)


def _layer_norm(v, g, b):
    mu = jnp.mean(v, axis=-1, keepdims=True)
    d = v - mu
    var = jnp.mean(d * d, axis=-1, keepdims=True)
    return d * lax.rsqrt(var + LN_EPS) * g + b


def _mm_kernel(x_ref, w_ref, o_ref):
    o_ref[...] = _bdot(x_ref[...], w_ref[...]).astype(o_ref.dtype)


def _matmul(x, w, out_dtype, name, tm=MM_TM, tn=MM_TN):
    m, k = x.shape
    n = w.shape[1]
    tn = min(tn, n)
    return pl.pallas_call(
        _mm_kernel,
        name=name,
        grid=(m // tm, n // tn),
        in_specs=[pl.BlockSpec((tm, k), lambda i, j: (i, 0)),
                  pl.BlockSpec((k, tn), lambda i, j: (0, j))],
        out_specs=pl.BlockSpec((tm, tn), lambda i, j: (i, j)),
        out_shape=jax.ShapeDtypeStruct((m, n), out_dtype),
        compiler_params=_cparams(("parallel", "parallel")),
    )(x, w)


def _dswa_kernel(q_ref, kc_ref, kp_ref, vc_ref, vp_ref, bias_ref, o_ref, lse_ref, *, tq):
    j = pl.program_id(2)
    nqb = tq // ATT_BLK
    q = q_ref[...]
    kwin = jnp.concatenate([kp_ref[...], kc_ref[...]], axis=0)
    vwin = jnp.concatenate([vp_ref[...], vc_ref[...]], axis=0)
    row = lax.broadcasted_iota(jnp.int32, (ATT_BLK, 2 * ATT_BLK), 0)
    col = lax.broadcasted_iota(jnp.int32, (ATT_BLK, 2 * ATT_BLK), 1)
    delta = row + ATT_BLK - col
    band = (delta >= 0) & (delta <= ATT_BLK)
    band_first = band & ((col >= ATT_BLK) | (j > 0))
    lane = lax.broadcasted_iota(jnp.int32, (ATT_BLK, LANES), 1)
    scale = HEAD_DIM ** -0.5
    for c in range(nqb):
        mask = band_first if c == 0 else band
        lse_tile = jnp.zeros((ATT_BLK, LANES), F32)
        for h in range(HEADS_PER_GROUP):
            hs = slice(h * HEAD_DIM, (h + 1) * HEAD_DIM)
            qh = q[c * ATT_BLK:(c + 1) * ATT_BLK, hs]
            kh = kwin[c * ATT_BLK:(c + 2) * ATT_BLK, hs]
            vh = vwin[c * ATT_BLK:(c + 2) * ATT_BLK, hs]
            s = _bdot_nt(qh, kh) * scale + bias_ref[h]
            s = jnp.where(mask, s, NEG_BIG)
            m = jnp.max(s, axis=-1, keepdims=True)
            p = jnp.exp(s - m)
            l = jnp.sum(p, axis=-1, keepdims=True)
            o = _bdot(p, vh) / l
            o_ref[c * ATT_BLK:(c + 1) * ATT_BLK, hs] = o
            lse_tile = jnp.where(lane == h, m + jnp.log(l), lse_tile)
        lse_ref[c * ATT_BLK:(c + 1) * ATT_BLK, :] = lse_tile


def _in_proj_strided_kernel(x_ref, w_ref, o_ref, *scratch, dilation):
    res = _bdot(x_ref[...], w_ref[...])
    if dilation == 1:
        o_ref[0] = res.astype(o_ref.dtype)
    else:
        res_ref, = scratch
        rows = res.shape[0] // dilation
        for t in range(res.shape[1] // LANES):
            ls = slice(t * LANES, (t + 1) * LANES)
            res_ref[t] = res[:, ls]
            for r in range(dilation):
                o_ref[r, :, ls] = res_ref[t, pl.ds(r, rows, stride=dilation), :].astype(o_ref.dtype)


def _in_proj_strided(x, w, dilation, batch, seq, tm=MM_TM, tn=MM_TN):
    k = x.shape[1]
    wn = w.shape[1]
    tpb = seq // tm
    scratch = [] if dilation == 1 else [pltpu.VMEM((tn // LANES, tm, LANES), F32)]
    return pl.pallas_call(
        functools.partial(_in_proj_strided_kernel, dilation=dilation),
        name=f"in_proj_attn_d{dilation}",
        grid=(batch * tpb, wn // tn),
        in_specs=[pl.BlockSpec((tm, k), lambda i, j: (i, 0)),
                  pl.BlockSpec((k, tn), lambda i, j: (0, j))],
        out_specs=pl.BlockSpec((None, dilation, tm // dilation, tn), lambda i, j: (i // tpb, 0, i % tpb, j)),
        out_shape=jax.ShapeDtypeStruct((batch, dilation, seq // dilation, wn), BF16),
        scratch_shapes=scratch,
        compiler_params=_cparams(("parallel", "parallel")),
    )(x, w)


def _dswa_group(qkv, bias, dilation):
    batch, _, n, _ = qkv.shape
    tq = min(512, n)
    nqb = tq // ATT_BLK

    def cur(off):
        return pl.BlockSpec((None, None, tq, A_GROUP_W), lambda b, r, j: (b, r, j, off))

    def prev(off):
        return pl.BlockSpec((None, None, ATT_BLK, A_GROUP_W),
                            lambda b, r, j: (b, r, jnp.maximum(j * nqb - 1, 0), off))

    return pl.pallas_call(
        functools.partial(_dswa_kernel, tq=tq),
        name=f"dswa_d{dilation}",
        grid=(batch, dilation, n // tq),
        in_specs=[cur(0), cur(1), prev(1), cur(2), prev(2),
                  pl.BlockSpec((HEADS_PER_GROUP, ATT_BLK, 2 * ATT_BLK), lambda b, r, j: (0, 0, 0))],
        out_specs=[pl.BlockSpec((None, None, tq, A_GROUP_W), lambda b, r, j: (b, r, j, 0)),
                   pl.BlockSpec((None, None, tq, LANES), lambda b, r, j: (b, r, j, 0))],
        out_shape=[jax.ShapeDtypeStruct((batch, dilation, n, A_GROUP_W), F32),
                   jax.ShapeDtypeStruct((batch, dilation, n, LANES), F32)],
        compiler_params=_cparams(("parallel", "parallel", "parallel")),
    )(qkv, qkv, qkv, qkv, qkv, bias)


def _t5_causal_bucket(dist):
    num_exact = NUM_BUCKETS // 2
    d = jnp.maximum(dist, 1).astype(F32)
    large = num_exact + (jnp.log(d / num_exact) / math.log(MAX_DISTANCE / num_exact)
                         * (NUM_BUCKETS - num_exact)).astype(jnp.int32)
    large = jnp.minimum(large, NUM_BUCKETS - 1)
    return jnp.where(dist < num_exact, dist, large)


def _band_bias(rel_bias, gi, dilation):
    qi = jnp.arange(ATT_BLK)[:, None] + ATT_BLK
    kj = jnp.arange(2 * ATT_BLK)[None, :]
    delta = jnp.maximum(qi - kj, 0) * dilation
    table = rel_bias[:, gi * HEADS_PER_GROUP:(gi + 1) * HEADS_PER_GROUP]
    return table[_t5_causal_bucket(delta)].transpose(2, 0, 1).astype(F32)


def _dot16(a, b):
    return jnp.dot(a, b, preferred_element_type=F32)


def _unit_lower_inverse(a, row, col):
    base = 16
    heads = range(len(a))
    eye = (row == col).astype(F32)
    same = lambda sz: (row // sz) == (col // sz)
    blk = same(base)
    a_d = [jnp.where(blk, a[h], 0.0) for h in heads]
    t = [eye - a_d[h] for h in heads]
    p = [a_d[h].astype(BF16) for h in heads]
    for _ in range(3):
        p = [_dot16(p[h], p[h]).astype(BF16) for h in heads]
        t = [t[h] + _dot16(t[h].astype(BF16), p[h]) for h in heads]
    sz = 2 * base
    while sz <= GDN_CHUNK:
        off = same(sz) & jnp.logical_not(same(sz // 2))
        tb = [t[h].astype(BF16) for h in heads]
        m = [_dot16(jnp.where(off, a[h], 0.0).astype(BF16), tb[h]).astype(BF16) for h in heads]
        t = [t[h] - _dot16(tb[h], m[h]) for h in heads]
        sz *= 2
    return t


def _gdn_kernel(x_ref, halo_ref, z_ref, ba_ref, cw_ref, alog_ref, dtb_ref, onw_ref, y_ref, state_ref):
    c = pl.program_id(1)
    C = GDN_CHUNK
    heads = range(N_HEADS_B)

    @pl.when(c == 0)
    def _():
        state_ref[...] = jnp.zeros_like(state_ref)

    keep_halo = (c > 0).astype(F32)
    row = lax.broadcasted_iota(jnp.int32, (C, C), 0)
    col = lax.broadcasted_iota(jnp.int32, (C, C), 1)
    incl = row >= col
    strict = row > col

    ba = ba_ref[...]
    beta_all = _sigmoid(ba[:, :LANES])
    a_in = ba[:, LANES:] + dtb_ref[...]
    softplus = jnp.maximum(a_in, 0.0) + jnp.log(1.0 + jnp.exp(-jnp.abs(a_in)))
    g_all = -jnp.exp(alog_ref[...]) * softplus
    ones_l = incl.astype(BF16)
    g1 = g_all.astype(BF16)
    r1 = g_all - g1.astype(F32)
    g2 = r1.astype(BF16)
    g3 = (r1 - g2.astype(F32)).astype(BF16)
    G = (jnp.dot(ones_l, g1, preferred_element_type=F32) + jnp.dot(ones_l, g2, preferred_element_type=F32)
         + jnp.dot(ones_l, g3, preferred_element_type=F32))
    GT = G.T

    def conv_silu(col0):
        cs = slice(col0, col0 + HEAD_DIM)
        xc = x_ref[:, cs]
        xf = jnp.concatenate([halo_ref[:, cs] * keep_halo, xc], axis=0)
        w = cw_ref[:, cs]
        y = w[CONV_WIDTH - 1:CONV_WIDTH] * xc
        for i in range(CONV_WIDTH - 1):
            off = SUBLANES - (CONV_WIDTH - 1) + i
            y = y + w[i:i + 1] * xf[off:off + C]
        return _silu(y)

    def l2norm(t):
        return t * lax.rsqrt(jnp.sum(t * t, axis=-1, keepdims=True) + RMS_EPS)

    q = [l2norm(conv_silu(h * HEAD_DIM)) * (HEAD_DIM ** -0.5) for h in heads]
    k = [l2norm(conv_silu(B_W + h * HEAD_DIM)) for h in heads]
    v = [conv_silu(2 * B_W + h * HEAD_DIM) for h in heads]
    beta = [beta_all[:, h:h + 1] for h in heads]
    gc = [G[:, h:h + 1] for h in heads]
    g_last = [G[C - 1:C, h:h + 1] for h in heads]
    decay = [jnp.exp(jnp.where(incl, gc[h] - GT[h:h + 1, :], NEG_BIG)) for h in heads]
    eg = [jnp.exp(gc[h]) for h in heads]
    kb = [k[h].astype(BF16) for h in heads]
    kq = [lax.dot_general(jnp.concatenate([kb[h], q[h].astype(BF16)], axis=0), kb[h],
                          (((1,), (1,)), ((), ())), preferred_element_type=F32) for h in heads]
    a = [jnp.where(strict, beta[h] * kq[h][:C] * decay[h], 0.0) for h in heads]
    t_inv = _unit_lower_inverse(a, row, col)
    rhs = [jnp.concatenate([beta[h] * v[h], (beta[h] * eg[h]) * k[h]], axis=1).astype(BF16) for h in heads]
    sol = [_dot16(t_inv[h].astype(BF16), rhs[h]) for h in heads]
    qk = [(kq[h][C:] * decay[h]).astype(BF16) for h in heads]
    wq = [jnp.concatenate([sol[h][:, HEAD_DIM:], q[h] * eg[h]], axis=0).astype(BF16) for h in heads]
    k_dec = [(k[h] * jnp.exp(g_last[h] - gc[h])).astype(BF16) for h in heads]

    state = [state_ref[h] for h in heads]
    ws = [_dot16(wq[h], state[h].astype(BF16)) for h in heads]
    u = [(sol[h][:, :HEAD_DIM] - ws[h][:C]).astype(BF16) for h in heads]
    o = [ws[h][C:] + _dot16(qk[h], u[h]) for h in heads]
    for h in heads:
        state_ref[h] = jnp.exp(g_last[h]) * state[h] + lax.dot_general(
            k_dec[h], u[h], (((0,), (0,)), ((), ())), preferred_element_type=F32)
    for h in heads:
        oh = o[h] * lax.rsqrt(jnp.mean(o[h] * o[h], axis=-1, keepdims=True) + RMS_EPS) * onw_ref[...]
        hs = slice(h * HEAD_DIM, (h + 1) * HEAD_DIM)
        y_ref[:, hs] = (oh * _silu(z_ref[:, hs])).astype(y_ref.dtype)


def _gated_deltanet(proj_b, proj_ba, conv_w, a_log, dt_bias, o_norm_w, batch, seq):
    C = GDN_CHUNK
    wb = proj_b.shape[1]
    pb = proj_b.reshape(batch, seq, wb)
    pba = proj_ba.reshape(batch, seq, 2 * LANES)
    pad = lambda t: jnp.pad(t.astype(F32), (0, LANES - t.shape[0])).reshape(1, LANES)
    const = lambda shape: pl.BlockSpec(shape, lambda b, c: (0,) * len(shape))
    y = pl.pallas_call(
        _gdn_kernel,
        name="gated_deltanet",
        grid=(batch, seq // C),
        in_specs=[pl.BlockSpec((None, C, 3 * B_W), lambda b, c: (b, c, 0)),
                  pl.BlockSpec((None, SUBLANES, 3 * B_W),
                               lambda b, c: (b, jnp.maximum(c * (C // SUBLANES) - 1, 0), 0)),
                  pl.BlockSpec((None, C, B_W), lambda b, c: (b, c, 3)),
                  pl.BlockSpec((None, C, 2 * LANES), lambda b, c: (b, c, 0)),
                  const((CONV_WIDTH, 3 * B_W)), const((1, LANES)), const((1, LANES)), const((1, LANES))],
        out_specs=pl.BlockSpec((None, C, B_W), lambda b, c: (b, c, 0)),
        out_shape=jax.ShapeDtypeStruct((batch, seq, B_W), BF16),
        scratch_shapes=[pltpu.VMEM((N_HEADS_B, HEAD_DIM, HEAD_DIM), F32)],
        compiler_params=_cparams(("parallel", "arbitrary")),
    )(pb, pb, pb, pba, conv_w.astype(F32), pad(a_log), pad(dt_bias), o_norm_w.astype(F32).reshape(1, LANES))
    return y.reshape(batch * seq, B_W)


def _mix_out_kernel(o0, o1, o2, l0, l1, l2, yb_ref, ga_ref, gb_ref, x_ref, woa_ref, wob_ref, wout_ref,
                    g_ref, b_ref, out_ref, *scratch):
    def token_order(ref, scr):
        dilation, rows, width = ref.shape
        if dilation == 1:
            return ref[0]
        planes = []
        for t in range(width // LANES):
            for r in range(dilation):
                scr[t, pl.ds(r, rows, stride=dilation), :] = ref[r, :, t * LANES:(t + 1) * LANES]
            planes.append(scr[t])
        return jnp.concatenate(planes, axis=1)

    outs = (token_order(o0, None), token_order(o1, scratch[0]), token_order(o2, scratch[1]))
    lses = (token_order(l0, None), token_order(l1, scratch[2]), token_order(l2, scratch[3]))
    ya = []
    for h in range(HEADS_PER_GROUP):
        ls = [t[:, h:h + 1] for t in lses]
        m = jnp.maximum(jnp.maximum(ls[0], ls[1]), ls[2])
        es = [jnp.exp(t - m) for t in ls]
        inv = 1.0 / (es[0] + es[1] + es[2])
        hs = slice(h * HEAD_DIM, (h + 1) * HEAD_DIM)
        ya.append((es[0] * inv) * outs[0][:, hs] + (es[1] * inv) * outs[1][:, hs] + (es[2] * inv) * outs[2][:, hs])
    ya = jnp.concatenate(ya, axis=1)
    pa = _bdot(ya, woa_ref[...])
    pb = _bdot(yb_ref[...], wob_ref[...])
    merged = _sigmoid(ga_ref[...]) * pa + _sigmoid(gb_ref[...]) * pb
    mix = _bdot(merged, wout_ref[...])
    out_ref[...] = _layer_norm(ALPHA * x_ref[...] + mix, g_ref[...], b_ref[...])


def _mix_out(outs, lses, yb, proj_b, x, w_oa, w_ob, w_out, ln_g, ln_b, tm=OUT_TM):
    n = x.shape[0]
    seq = outs[0].shape[1] * outs[0].shape[2]
    tpb = seq // tm
    rowblk = lambda w, cb=0: pl.BlockSpec((tm, w), lambda i: (i, cb))
    const = lambda a: pl.BlockSpec(a.shape, lambda i: (0, 0))

    def grouped(a):
        d, w = a.shape[1], a.shape[3]
        return pl.BlockSpec((None, d, tm // d, w), lambda i: (i // tpb, 0, i % tpb, 0))

    wa, wb, wo = w_oa.astype(BF16), w_ob.astype(BF16), w_out.astype(BF16)
    g, b = ln_g.reshape(1, D_MODEL), ln_b.reshape(1, D_MODEL)
    return pl.pallas_call(
        _mix_out_kernel,
        name="mix_out_ln",
        grid=(n // tm,),
        in_specs=[grouped(a) for a in (*outs, *lses)]
        + [rowblk(B_W), rowblk(D_MODEL, 4), rowblk(D_MODEL, 5), rowblk(D_MODEL),
           const(wa), const(wb), const(wo), const(g), const(b)],
        out_specs=rowblk(D_MODEL),
        out_shape=jax.ShapeDtypeStruct((n, D_MODEL), F32),
        scratch_shapes=[pltpu.VMEM((A_GROUP_W // LANES, tm, LANES), F32)] * 2 + [pltpu.VMEM((1, tm, LANES), F32)] * 2,
        compiler_params=_cparams(("parallel",)),
    )(*outs, *lses, yb, proj_b, proj_b, x, wa, wb, wo, g, b)


def _ffn_kernel(x_ref, wg_ref, wu_ref, wd_ref, g_ref, b_ref, out_ref, xb_ref, acc_ref):
    f = pl.program_id(1)

    @pl.when(f == 0)
    def _():
        xb_ref[...] = x_ref[...].astype(BF16)
        acc_ref[...] = jnp.zeros_like(acc_ref)

    xb = xb_ref[...]
    gate = jnp.dot(xb, wg_ref[...], preferred_element_type=F32)
    up = jnp.dot(xb, wu_ref[...], preferred_element_type=F32)
    acc_ref[...] += _bdot(_silu(gate) * up, wd_ref[...])

    @pl.when(f == pl.num_programs(1) - 1)
    def _():
        out_ref[...] = _layer_norm(ALPHA * x_ref[...] + acc_ref[...], g_ref[...], b_ref[...])


def _dense_ffn(x, w_gate, w_up, w_down, ln_g, ln_b, tm=FFN_TM, tf=FFN_TF):
    n = x.shape[0]
    dff = w_gate.shape[1]
    g, b = ln_g.reshape(1, D_MODEL), ln_b.reshape(1, D_MODEL)
    return pl.pallas_call(
        _ffn_kernel,
        name="dense_ffn_ln",
        grid=(n // tm, dff // tf),
        in_specs=[pl.BlockSpec((tm, D_MODEL), lambda i, f: (i, 0)),
                  pl.BlockSpec((D_MODEL, tf), lambda i, f: (0, f)),
                  pl.BlockSpec((D_MODEL, tf), lambda i, f: (0, f)),
                  pl.BlockSpec((tf, D_MODEL), lambda i, f: (f, 0)),
                  pl.BlockSpec((1, D_MODEL), lambda i, f: (0, 0)),
                  pl.BlockSpec((1, D_MODEL), lambda i, f: (0, 0))],
        out_specs=pl.BlockSpec((tm, D_MODEL), lambda i, f: (i, 0)),
        out_shape=jax.ShapeDtypeStruct((n, D_MODEL), F32),
        scratch_shapes=[pltpu.VMEM((tm, D_MODEL), BF16), pltpu.VMEM((tm, D_MODEL), F32)],
        compiler_params=_cparams(("parallel", "arbitrary")),
    )(x, w_gate.astype(BF16), w_up.astype(BF16), w_down.astype(BF16), g, b)


def _gather_rows(idx_ref, src_hbm, dst, sem, count):
    def body(i, carry):
        pltpu.make_async_copy(src_hbm.at[pl.ds(idx_ref[0, i], 1)], dst.at[pl.ds(i, 1)], sem).start()
        return carry
    lax.fori_loop(0, count, body, 0, unroll=8)


def _wait_rows(src_hbm, dst, sem, count):
    pltpu.make_async_copy(src_hbm.at[pl.ds(0, count)], dst, sem).wait()


def _moe_kernel(be_ref, tok0_ref, tokn_ref, x_hbm, gate_ref, wg_ref, wu_ref, wd_ref, y_ref,
                xbuf, xb_ref, acc_ref, sem):
    del be_ref
    b = pl.program_id(0)
    f = pl.program_id(1)
    nb = pl.num_programs(0)
    slot = b % 2

    @pl.when(f == 0)
    def _():
        @pl.when(b == 0)
        def _():
            _gather_rows(tok0_ref, x_hbm, xbuf.at[0], sem.at[0], MOE_TB)

        _wait_rows(x_hbm, xbuf.at[slot], sem.at[slot], MOE_TB)

        @pl.when(b + 1 < nb)
        def _():
            _gather_rows(tokn_ref, x_hbm, xbuf.at[1 - slot], sem.at[1 - slot], MOE_TB)

        xb_ref[...] = xbuf[slot].astype(BF16)
        acc_ref[...] = jnp.zeros_like(acc_ref)

    xb = xb_ref[...]
    gate = jnp.dot(xb, wg_ref[...], preferred_element_type=F32)
    up = jnp.dot(xb, wu_ref[...], preferred_element_type=F32)
    acc_ref[...] += _bdot(_silu(gate) * up, wd_ref[...])

    @pl.when(f == pl.num_programs(1) - 1)
    def _():
        y_ref[...] = acc_ref[...] * gate_ref[...]


def _moe_experts(x, tok_buf, gate_buf, block_expert, w_gate, w_up, w_down, tb=MOE_TB, tf=MOE_TF):
    nblk = block_expert.shape[0]
    dffe = w_gate.shape[2]
    tok3 = tok_buf.reshape(nblk, 1, tb)
    smem_blk = lambda imap: pl.BlockSpec((None, 1, tb), imap, memory_space=pltpu.SMEM)
    grid_spec = pltpu.PrefetchScalarGridSpec(
        num_scalar_prefetch=1,
        grid=(nblk, dffe // tf),
        in_specs=[smem_blk(lambda b, f, be: (0, 0, 0)),
                  smem_blk(lambda b, f, be: (jnp.minimum(b + 1, nblk - 1), 0, 0)),
                  pl.BlockSpec(memory_space=pl.ANY),
                  pl.BlockSpec((tb, 1), lambda b, f, be: (b, 0)),
                  pl.BlockSpec((None, D_MODEL, tf), lambda b, f, be: (be[b], 0, f)),
                  pl.BlockSpec((None, D_MODEL, tf), lambda b, f, be: (be[b], 0, f)),
                  pl.BlockSpec((None, tf, D_MODEL), lambda b, f, be: (be[b], f, 0))],
        out_specs=pl.BlockSpec((tb, D_MODEL), lambda b, f, be: (b, 0)),
        scratch_shapes=[pltpu.VMEM((2, tb, D_MODEL), F32), pltpu.VMEM((tb, D_MODEL), BF16),
                        pltpu.VMEM((tb, D_MODEL), F32), pltpu.SemaphoreType.DMA((2,))],
    )
    return pl.pallas_call(
        _moe_kernel,
        name="moe_experts",
        grid_spec=grid_spec,
        out_shape=jax.ShapeDtypeStruct((nblk * tb, D_MODEL), F32),
        compiler_params=_cparams(("arbitrary", "arbitrary"), disable_bounds_checks=True),
    )(block_expert, tok3, tok3, x, gate_buf.reshape(nblk * tb, 1),
      w_gate.astype(BF16), w_up.astype(BF16), w_down.astype(BF16))


def _combine_kernel(pos_ref, y_hbm, x_ref, g_ref, b_ref, out_ref, ybuf, sem):
    tm = out_ref.shape[0]
    _gather_rows(pos_ref, y_hbm, ybuf, sem.at[0], TOP_K * tm)
    _wait_rows(y_hbm, ybuf, sem.at[0], TOP_K * tm)
    f = ybuf[:tm] + ybuf[tm:]
    out_ref[...] = _layer_norm(ALPHA * x_ref[...] + f, g_ref[...], b_ref[...])


def _moe_combine(y, pos, x, ln_g, ln_b, tm=CMB_TM):
    n = x.shape[0]
    pos3 = pos.reshape(n // tm, tm, TOP_K).transpose(0, 2, 1).reshape(n // tm, 1, TOP_K * tm)
    g, b = ln_g.reshape(1, D_MODEL), ln_b.reshape(1, D_MODEL)
    return pl.pallas_call(
        _combine_kernel,
        name="moe_combine_ln",
        grid=(n // tm,),
        in_specs=[pl.BlockSpec((None, 1, TOP_K * tm), lambda i: (i, 0, 0), memory_space=pltpu.SMEM),
                  pl.BlockSpec(memory_space=pl.ANY),
                  pl.BlockSpec((tm, D_MODEL), lambda i: (i, 0)),
                  pl.BlockSpec((1, D_MODEL), lambda i: (0, 0)),
                  pl.BlockSpec((1, D_MODEL), lambda i: (0, 0))],
        out_specs=pl.BlockSpec((tm, D_MODEL), lambda i: (i, 0)),
        out_shape=jax.ShapeDtypeStruct((n, D_MODEL), F32),
        scratch_shapes=[pltpu.VMEM((TOP_K * tm, D_MODEL), F32), pltpu.SemaphoreType.DMA((1,))],
        compiler_params=_cparams(("arbitrary",), disable_bounds_checks=True),
    )(pos3, y, x, g, b)


def _moe_routing(logits, tb):
    n = logits.shape[0]
    top_logit, top_idx = lax.top_k(logits, TOP_K)
    gates = jax.nn.softmax(top_logit, axis=-1)
    na = n * TOP_K
    e_flat = top_idx.reshape(-1).astype(jnp.int32)
    tok_flat = jnp.arange(na, dtype=jnp.int32) // TOP_K
    order = jnp.argsort(e_flat)
    e_sorted = e_flat[order]
    counts = jnp.zeros((N_EXPERTS,), jnp.int32).at[e_flat].add(1)
    padded = (counts + tb - 1) // tb * tb
    start = jnp.cumsum(counts) - counts
    pend = jnp.cumsum(padded)
    pstart = pend - padded
    dest = pstart[e_sorted] + (jnp.arange(na, dtype=jnp.int32) - start[e_sorted])
    nblk = -(-na // tb) + N_EXPERTS
    tok_buf = jnp.zeros((nblk * tb,), jnp.int32).at[dest].set(tok_flat[order])
    gate_buf = jnp.zeros((nblk * tb,), F32).at[dest].set(gates.reshape(-1)[order])
    block_expert = jnp.minimum(jnp.searchsorted(pend, jnp.arange(nblk, dtype=jnp.int32) * tb, side='right'),
                               N_EXPERTS - 1).astype(jnp.int32)
    pos = jnp.zeros((na,), jnp.int32).at[order].set(dest).reshape(n, TOP_K)
    return tok_buf, gate_buf, block_expert, pos


def _moe_ffn(x, w_router, w_gate, w_up, w_down, ln_g, ln_b):
    wr = jnp.pad(w_router, ((0, 0), (0, LANES - N_EXPERTS))).astype(BF16)
    logits = _matmul(x, wr, F32, "moe_router")[:, :N_EXPERTS]
    tok_buf, gate_buf, block_expert, pos = _moe_routing(logits, MOE_TB)
    y = _moe_experts(x, tok_buf, gate_buf, block_expert, w_gate, w_up, w_down)
    return _moe_combine(y, pos, x, ln_g, ln_b)


def _split_w_in(w):
    a_end = 3 * A_QKV_W
    bz_end = a_end + 3 * B_W + B_W
    w_groups = [jnp.concatenate([w[:, s * A_QKV_W + gi * A_GROUP_W:s * A_QKV_W + (gi + 1) * A_GROUP_W]
                                 for s in range(3)], axis=1).astype(BF16) for gi in range(N_GROUPS)]
    w_b = jnp.concatenate([w[:, a_end:bz_end], w[:, bz_end + 2 * N_HEADS_B:]], axis=1)
    zpad = jnp.zeros((w.shape[0], LANES - N_HEADS_B), w.dtype)
    w_ba = jnp.concatenate([w[:, bz_end:bz_end + N_HEADS_B], zpad,
                            w[:, bz_end + N_HEADS_B:bz_end + 2 * N_HEADS_B], zpad], axis=1)
    return w_groups, w_b.astype(BF16), w_ba.astype(BF16)


def _hybrid_layer(x, batch, seq, rel_bias, w_in, conv_w, a_log, dt_bias, o_norm_w, w_oa, w_ob, w_out, ln_g, ln_b):
    w_groups, w_b, w_ba = _split_w_in(w_in)
    proj_b = _matmul(x, w_b, F32, "in_proj_gdn_gates")
    proj_ba = _matmul(x, w_ba, F32, "in_proj_beta_decay")
    outs, lses = [], []
    for gi, (_, dilation) in enumerate(DSWA_PATTERNS):
        qkv = _in_proj_strided(x, w_groups[gi], dilation, batch, seq)
        o, lse = _dswa_group(qkv, _band_bias(rel_bias, gi, dilation), dilation)
        outs.append(o)
        lses.append(lse)
    yb = _gated_deltanet(proj_b, proj_ba, conv_w, a_log, dt_bias, o_norm_w, batch, seq)
    return _mix_out(outs, lses, yb, proj_b, x, w_oa, w_ob, w_out, ln_g, ln_b)


def kernel(x, rel_bias, w_in, conv_w, a_log, dt_bias, o_norm_w, w_oa, w_ob, w_out, ln1_g, ln1_b,
           ffn_w_gate, ffn_w_up, ffn_w_down, moe_router, moe_w_gate, moe_w_up, moe_w_down, ln2_g, ln2_b):
    batch, seq, d = x.shape
    h = x.reshape(batch * seq, d)
    for layer in range(DEPTH):
        h = _hybrid_layer(h, batch, seq, rel_bias, w_in[layer], conv_w[layer], a_log[layer], dt_bias[layer],
                          o_norm_w[layer], w_oa[layer], w_ob[layer], w_out[layer], ln1_g[layer], ln1_b[layer])
        j = layer // 2
        if layer % 2 == 0:
            h = _dense_ffn(h, ffn_w_gate[j], ffn_w_up[j], ffn_w_down[j], ln2_g[layer], ln2_b[layer])
        else:
            h = _moe_ffn(h, moe_router[j], moe_w_gate[j], moe_w_up[j], moe_w_down[j], ln2_g[layer], ln2_b[layer])
    return h.reshape(batch, seq, d)
```

```python
import functools
import math

import jax
import jax.numpy as jnp
from jax import lax
from jax.experimental import pallas as pl
from jax.experimental.pallas import tpu as pltpu

F32 = jnp.float32
BF16 = jnp.bfloat16

D_MODEL = 1024
DEPTH = 2
DSWA_PATTERNS = ((128, 1), (512, 4), (2048, 16))
N_GROUPS = 3
HEADS_PER_GROUP = 4
HEAD_DIM = 128
A_QKV_W = N_GROUPS * HEADS_PER_GROUP * HEAD_DIM
A_GROUP_W = HEADS_PER_GROUP * HEAD_DIM
NUM_BUCKETS = 32
MAX_DISTANCE = 2048
N_HEADS_B = 8
B_W = N_HEADS_B * HEAD_DIM
CONV_WIDTH = 4
N_EXPERTS = 8
TOP_K = 2
ALPHA = (2 * DEPTH) ** 0.25
LN_EPS = 1e-5
RMS_EPS = 1e-6

LANES = 128
SUBLANES = 8
VMEM_LIMIT = 56 * 1024 * 1024

ATT_BLK = 128
GDN_CHUNK = 128
MM_TM = 1024
MM_TN = 512
OUT_TM = 256
FFN_TM = 512
FFN_TF = 1408
MOE_TB = 512
MOE_TF = 1792
CMB_TM = 256
DSP_TM = 512

NEG_BIG = -1e30


def _cparams(sem, vmem=VMEM_LIMIT, **kw):
    return pltpu.CompilerParams(dimension_semantics=sem, vmem_limit_bytes=vmem, **kw)


def _bdot(a, b):
    return jnp.dot(a.astype(BF16), b.astype(BF16), preferred_element_type=F32)


def _bdot_nt(a, b):
    return lax.dot_general(a.astype(BF16), b.astype(BF16), (((1,), (1,)), ((), ())),
                           preferred_element_type=F32)


def _bdot_tn(a, b):
    return lax.dot_general(a.astype(BF16), b.astype(BF16), (((0,), (0,)), ((), ())),
                           preferred_element_type=F32)


def _sigmoid(v):
    return 1.0 / (1.0 + jnp.exp(-v))


def _silu(v):
    return v * _sigmoid(v)


def _layer_norm(v, g, b):
    mu = jnp.mean(v, axis=-1, keepdims=True)
    d = v - mu
    var = jnp.mean(d * d, axis=-1, keepdims=True)
    return d * lax.rsqrt(var + LN_EPS) * g + b


def _mm_kernel(x_ref, w_ref, o_ref):
    o_ref[...] = _bdot(x_ref[...], w_ref[...]).astype(o_ref.dtype)


def _matmul(x, w, out_dtype, name, tm=MM_TM, tn=MM_TN):
    m, k = x.shape
    n = w.shape[1]
    tn = min(tn, n)
    return pl.pallas_call(
        _mm_kernel,
        name=name,
        grid=(m // tm, n // tn),
        in_specs=[pl.BlockSpec((tm, k), lambda i, j: (i, 0)),
                  pl.BlockSpec((k, tn), lambda i, j: (0, j))],
        out_specs=pl.BlockSpec((tm, tn), lambda i, j: (i, j)),
        out_shape=jax.ShapeDtypeStruct((m, n), out_dtype),
        compiler_params=_cparams(("parallel", "parallel")),
    )(x, w)


def _dswa_kernel(q_ref, kc_ref, kp_ref, vc_ref, vp_ref, bias_ref, o_ref, lse_ref, *, tq):
    j = pl.program_id(2)
    nqb = tq // ATT_BLK
    q = q_ref[...]
    kwin = jnp.concatenate([kp_ref[...], kc_ref[...]], axis=0)
    vwin = jnp.concatenate([vp_ref[...], vc_ref[...]], axis=0)
    row = lax.broadcasted_iota(jnp.int32, (ATT_BLK, 2 * ATT_BLK), 0)
    col = lax.broadcasted_iota(jnp.int32, (ATT_BLK, 2 * ATT_BLK), 1)
    delta = row + ATT_BLK - col
    band = (delta >= 0) & (delta <= ATT_BLK)
    band_first = band & ((col >= ATT_BLK) | (j > 0))
    lane = lax.broadcasted_iota(jnp.int32, (ATT_BLK, LANES), 1)
    scale = HEAD_DIM ** -0.5
    for c in range(nqb):
        mask = band_first if c == 0 else band
        lse_tile = jnp.zeros((ATT_BLK, LANES), F32)
        for h in range(HEADS_PER_GROUP):
            hs = slice(h * HEAD_DIM, (h + 1) * HEAD_DIM)
            qh = q[c * ATT_BLK:(c + 1) * ATT_BLK, hs]
            kh = kwin[c * ATT_BLK:(c + 2) * ATT_BLK, hs]
            vh = vwin[c * ATT_BLK:(c + 2) * ATT_BLK, hs]
            s = _bdot_nt(qh, kh) * scale + bias_ref[h]
            s = jnp.where(mask, s, NEG_BIG)
            m = jnp.max(s, axis=-1, keepdims=True)
            p = jnp.exp(s - m)
            l = jnp.sum(p, axis=-1, keepdims=True)
            o = _bdot(p, vh) / l
            o_ref[c * ATT_BLK:(c + 1) * ATT_BLK, hs] = o
            lse_tile = jnp.where(lane == h, m + jnp.log(l), lse_tile)
        lse_ref[c * ATT_BLK:(c + 1) * ATT_BLK, :] = lse_tile


def _in_proj_strided_kernel(x_ref, w_ref, o_ref, *scratch, dilation):
    res = _bdot(x_ref[...], w_ref[...])
    if dilation == 1:
        o_ref[0] = res.astype(o_ref.dtype)
    else:
        res_ref, = scratch
        rows = res.shape[0] // dilation
        for t in range(res.shape[1] // LANES):
            ls = slice(t * LANES, (t + 1) * LANES)
            res_ref[t] = res[:, ls]
            for r in range(dilation):
                o_ref[r, :, ls] = res_ref[t, pl.ds(r, rows, stride=dilation), :].astype(o_ref.dtype)


def _in_proj_strided(x, w, dilation, batch, seq, tm=MM_TM, tn=MM_TN):
    k = x.shape[1]
    wn = w.shape[1]
    tpb = seq // tm
    scratch = [] if dilation == 1 else [pltpu.VMEM((tn // LANES, tm, LANES), F32)]
    return pl.pallas_call(
        functools.partial(_in_proj_strided_kernel, dilation=dilation),
        name=f"in_proj_attn_d{dilation}",
        grid=(batch * tpb, wn // tn),
        in_specs=[pl.BlockSpec((tm, k), lambda i, j: (i, 0)),
                  pl.BlockSpec((k, tn), lambda i, j: (0, j))],
        out_specs=pl.BlockSpec((None, dilation, tm // dilation, tn), lambda i, j: (i // tpb, 0, i % tpb, j)),
        out_shape=jax.ShapeDtypeStruct((batch, dilation, seq // dilation, wn), BF16),
        scratch_shapes=scratch,
        compiler_params=_cparams(("parallel", "parallel")),
    )(x, w)


def _dswa_group(qkv, bias, dilation):
    batch, _, n, _ = qkv.shape
    tq = min(512, n)
    nqb = tq // ATT_BLK

    def cur(off):
        return pl.BlockSpec((None, None, tq, A_GROUP_W), lambda b, r, j: (b, r, j, off))

    def prev(off):
        return pl.BlockSpec((None, None, ATT_BLK, A_GROUP_W),
                            lambda b, r, j: (b, r, jnp.maximum(j * nqb - 1, 0), off))

    return pl.pallas_call(
        functools.partial(_dswa_kernel, tq=tq),
        name=f"dswa_d{dilation}",
        grid=(batch, dilation, n // tq),
        in_specs=[cur(0), cur(1), prev(1), cur(2), prev(2),
                  pl.BlockSpec((HEADS_PER_GROUP, ATT_BLK, 2 * ATT_BLK), lambda b, r, j: (0, 0, 0))],
        out_specs=[pl.BlockSpec((None, None, tq, A_GROUP_W), lambda b, r, j: (b, r, j, 0)),
                   pl.BlockSpec((None, None, tq, LANES), lambda b, r, j: (b, r, j, 0))],
        out_shape=[jax.ShapeDtypeStruct((batch, dilation, n, A_GROUP_W), F32),
                   jax.ShapeDtypeStruct((batch, dilation, n, LANES), F32)],
        compiler_params=_cparams(("parallel", "parallel", "parallel")),
    )(qkv, qkv, qkv, qkv, qkv, bias)


def _t5_causal_bucket(dist):
    num_exact = NUM_BUCKETS // 2
    d = jnp.maximum(dist, 1).astype(F32)
    large = num_exact + (jnp.log(d / num_exact) / math.log(MAX_DISTANCE / num_exact)
                         * (NUM_BUCKETS - num_exact)).astype(jnp.int32)
    large = jnp.minimum(large, NUM_BUCKETS - 1)
    return jnp.where(dist < num_exact, dist, large)


def _band_bias(rel_bias, gi, dilation):
    qi = jnp.arange(ATT_BLK)[:, None] + ATT_BLK
    kj = jnp.arange(2 * ATT_BLK)[None, :]
    delta = jnp.maximum(qi - kj, 0) * dilation
    table = rel_bias[:, gi * HEADS_PER_GROUP:(gi + 1) * HEADS_PER_GROUP]
    return table[_t5_causal_bucket(delta)].transpose(2, 0, 1).astype(F32)


def _dot16(a, b):
    return jnp.dot(a, b, preferred_element_type=F32)


def _unit_lower_inverse(a, row, col):
    base = 16
    heads = range(len(a))
    eye = (row == col).astype(F32)
    same = lambda sz: (row // sz) == (col // sz)
    blk = same(base)
    a_d = [jnp.where(blk, a[h], 0.0) for h in heads]
    t = [eye - a_d[h] for h in heads]
    p = [a_d[h].astype(BF16) for h in heads]
    for _ in range(3):
        p = [_dot16(p[h], p[h]).astype(BF16) for h in heads]
        t = [t[h] + _dot16(t[h].astype(BF16), p[h]) for h in heads]
    sz = 2 * base
    while sz <= GDN_CHUNK:
        off = same(sz) & jnp.logical_not(same(sz // 2))
        tb = [t[h].astype(BF16) for h in heads]
        m = [_dot16(jnp.where(off, a[h], 0.0).astype(BF16), tb[h]).astype(BF16) for h in heads]
        t = [t[h] - _dot16(tb[h], m[h]) for h in heads]
        sz *= 2
    return t


def _gdn_kernel(x_ref, halo_ref, z_ref, ba_ref, cw_ref, alog_ref, dtb_ref, onw_ref, y_ref, state_ref):
    c = pl.program_id(1)
    C = GDN_CHUNK
    heads = range(N_HEADS_B)

    @pl.when(c == 0)
    def _():
        state_ref[...] = jnp.zeros_like(state_ref)

    keep_halo = (c > 0).astype(F32)
    row = lax.broadcasted_iota(jnp.int32, (C, C), 0)
    col = lax.broadcasted_iota(jnp.int32, (C, C), 1)
    incl = row >= col
    strict = row > col

    ba = ba_ref[...]
    beta_all = _sigmoid(ba[:, :LANES])
    a_in = ba[:, LANES:] + dtb_ref[...]
    softplus = jnp.maximum(a_in, 0.0) + jnp.log(1.0 + jnp.exp(-jnp.abs(a_in)))
    g_all = -jnp.exp(alog_ref[...]) * softplus
    ones_l = incl.astype(BF16)
    g1 = g_all.astype(BF16)
    r1 = g_all - g1.astype(F32)
    g2 = r1.astype(BF16)
    g3 = (r1 - g2.astype(F32)).astype(BF16)
    G = (jnp.dot(ones_l, g1, preferred_element_type=F32) + jnp.dot(ones_l, g2, preferred_element_type=F32)
         + jnp.dot(ones_l, g3, preferred_element_type=F32))
    GT = G.T

    def conv_silu(col0):
        cs = slice(col0, col0 + HEAD_DIM)
        xc = x_ref[:, cs]
        xf = jnp.concatenate([halo_ref[:, cs] * keep_halo, xc], axis=0)
        w = cw_ref[:, cs]
        y = w[CONV_WIDTH - 1:CONV_WIDTH] * xc
        for i in range(CONV_WIDTH - 1):
            off = SUBLANES - (CONV_WIDTH - 1) + i
            y = y + w[i:i + 1] * xf[off:off + C]
        return _silu(y)

    def l2norm(t):
        return t * lax.rsqrt(jnp.sum(t * t, axis=-1, keepdims=True) + RMS_EPS)

    q = [l2norm(conv_silu(h * HEAD_DIM)) * (HEAD_DIM ** -0.5) for h in heads]
    k = [l2norm(conv_silu(B_W + h * HEAD_DIM)) for h in heads]
    v = [conv_silu(2 * B_W + h * HEAD_DIM) for h in heads]
    beta = [beta_all[:, h:h + 1] for h in heads]
    gc = [G[:, h:h + 1] for h in heads]
    g_last = [G[C - 1:C, h:h + 1] for h in heads]
    decay = [jnp.exp(jnp.where(incl, gc[h] - GT[h:h + 1, :], NEG_BIG)) for h in heads]
    eg = [jnp.exp(gc[h]) for h in heads]
    kb = [k[h].astype(BF16) for h in heads]
    kq = [lax.dot_general(jnp.concatenate([kb[h], q[h].astype(BF16)], axis=0), kb[h],
                          (((1,), (1,)), ((), ())), preferred_element_type=F32) for h in heads]
    a = [jnp.where(strict, beta[h] * kq[h][:C] * decay[h], 0.0) for h in heads]
    t_inv = _unit_lower_inverse(a, row, col)
    rhs = [jnp.concatenate([beta[h] * v[h], (beta[h] * eg[h]) * k[h]], axis=1).astype(BF16) for h in heads]
    sol = [_dot16(t_inv[h].astype(BF16), rhs[h]) for h in heads]
    qk = [(kq[h][C:] * decay[h]).astype(BF16) for h in heads]
    wq = [jnp.concatenate([sol[h][:, HEAD_DIM:], q[h] * eg[h]], axis=0).astype(BF16) for h in heads]
    k_dec = [(k[h] * jnp.exp(g_last[h] - gc[h])).astype(BF16) for h in heads]

    state = [state_ref[h] for h in heads]
    ws = [_dot16(wq[h], state[h].astype(BF16)) for h in heads]
    u = [(sol[h][:, :HEAD_DIM] - ws[h][:C]).astype(BF16) for h in heads]
    o = [ws[h][C:] + _dot16(qk[h], u[h]) for h in heads]
    for h in heads:
        state_ref[h] = jnp.exp(g_last[h]) * state[h] + lax.dot_general(
            k_dec[h], u[h], (((0,), (0,)), ((), ())), preferred_element_type=F32)
    for h in heads:
        oh = o[h] * lax.rsqrt(jnp.mean(o[h] * o[h], axis=-1, keepdims=True) + RMS_EPS) * onw_ref[...]
        hs = slice(h * HEAD_DIM, (h + 1) * HEAD_DIM)
        y_ref[:, hs] = (oh * _silu(z_ref[:, hs])).astype(y_ref.dtype)


def _gated_deltanet(proj_b, proj_ba, conv_w, a_log, dt_bias, o_norm_w, batch, seq):
    C = GDN_CHUNK
    wb = proj_b.shape[1]
    pb = proj_b.reshape(batch, seq, wb)
    pba = proj_ba.reshape(batch, seq, 2 * LANES)
    pad = lambda t: jnp.pad(t.astype(F32), (0, LANES - t.shape[0])).reshape(1, LANES)
    const = lambda shape: pl.BlockSpec(shape, lambda b, c: (0,) * len(shape))
    y = pl.pallas_call(
        _gdn_kernel,
        name="gated_deltanet",
        grid=(batch, seq // C),
        in_specs=[pl.BlockSpec((None, C, 3 * B_W), lambda b, c: (b, c, 0)),
                  pl.BlockSpec((None, SUBLANES, 3 * B_W),
                               lambda b, c: (b, jnp.maximum(c * (C // SUBLANES) - 1, 0), 0)),
                  pl.BlockSpec((None, C, B_W), lambda b, c: (b, c, 3)),
                  pl.BlockSpec((None, C, 2 * LANES), lambda b, c: (b, c, 0)),
                  const((CONV_WIDTH, 3 * B_W)), const((1, LANES)), const((1, LANES)), const((1, LANES))],
        out_specs=pl.BlockSpec((None, C, B_W), lambda b, c: (b, c, 0)),
        out_shape=jax.ShapeDtypeStruct((batch, seq, B_W), BF16),
        scratch_shapes=[pltpu.VMEM((N_HEADS_B, HEAD_DIM, HEAD_DIM), F32)],
        compiler_params=_cparams(("parallel", "arbitrary")),
    )(pb, pb, pb, pba, conv_w.astype(F32), pad(a_log), pad(dt_bias), o_norm_w.astype(F32).reshape(1, LANES))
    return y.reshape(batch * seq, B_W)


def _mix_out_kernel(o0, o1, o2, l0, l1, l2, yb_ref, ga_ref, gb_ref, x_ref, woa_ref, wob_ref, wout_ref,
                    g_ref, b_ref, out_ref, *scratch):
    def token_order(ref, scr):
        dilation, rows, width = ref.shape
        if dilation == 1:
            return ref[0]
        planes = []
        for t in range(width // LANES):
            for r in range(dilation):
                scr[t, pl.ds(r, rows, stride=dilation), :] = ref[r, :, t * LANES:(t + 1) * LANES]
            planes.append(scr[t])
        return jnp.concatenate(planes, axis=1)

    outs = (token_order(o0, None), token_order(o1, scratch[0]), token_order(o2, scratch[1]))
    lses = (token_order(l0, None), token_order(l1, scratch[2]), token_order(l2, scratch[3]))
    ya = []
    for h in range(HEADS_PER_GROUP):
        ls = [t[:, h:h + 1] for t in lses]
        m = jnp.maximum(jnp.maximum(ls[0], ls[1]), ls[2])
        es = [jnp.exp(t - m) for t in ls]
        inv = 1.0 / (es[0] + es[1] + es[2])
        hs = slice(h * HEAD_DIM, (h + 1) * HEAD_DIM)
        ya.append((es[0] * inv) * outs[0][:, hs] + (es[1] * inv) * outs[1][:, hs] + (es[2] * inv) * outs[2][:, hs])
    ya = jnp.concatenate(ya, axis=1)
    pa = _bdot(ya, woa_ref[...])
    pb = _bdot(yb_ref[...], wob_ref[...])
    merged = _sigmoid(ga_ref[...]) * pa + _sigmoid(gb_ref[...]) * pb
    mix = _bdot(merged, wout_ref[...])
    out_ref[...] = _layer_norm(ALPHA * x_ref[...] + mix, g_ref[...], b_ref[...])


def _mix_out(outs, lses, yb, proj_b, x, w_oa, w_ob, w_out, ln_g, ln_b, tm=OUT_TM):
    n = x.shape[0]
    seq = outs[0].shape[1] * outs[0].shape[2]
    tpb = seq // tm
    rowblk = lambda w, cb=0: pl.BlockSpec((tm, w), lambda i: (i, cb))
    const = lambda a: pl.BlockSpec(a.shape, lambda i: (0, 0))

    def grouped(a):
        d, w = a.shape[1], a.shape[3]
        return pl.BlockSpec((None, d, tm // d, w), lambda i: (i // tpb, 0, i % tpb, 0))

    wa, wb, wo = w_oa.astype(BF16), w_ob.astype(BF16), w_out.astype(BF16)
    g, b = ln_g.reshape(1, D_MODEL), ln_b.reshape(1, D_MODEL)
    return pl.pallas_call(
        _mix_out_kernel,
        name="mix_out_ln",
        grid=(n // tm,),
        in_specs=[grouped(a) for a in (*outs, *lses)]
        + [rowblk(B_W), rowblk(D_MODEL, 4), rowblk(D_MODEL, 5), rowblk(D_MODEL),
           const(wa), const(wb), const(wo), const(g), const(b)],
        out_specs=rowblk(D_MODEL),
        out_shape=jax.ShapeDtypeStruct((n, D_MODEL), F32),
        scratch_shapes=[pltpu.VMEM((A_GROUP_W // LANES, tm, LANES), F32)] * 2 + [pltpu.VMEM((1, tm, LANES), F32)] * 2,
        compiler_params=_cparams(("parallel",)),
    )(*outs, *lses, yb, proj_b, proj_b, x, wa, wb, wo, g, b)


def _ffn_kernel(x_ref, wg_ref, wu_ref, wd_ref, g_ref, b_ref, out_ref, xb_ref, acc_ref):
    f = pl.program_id(1)

    @pl.when(f == 0)
    def _():
        xb_ref[...] = x_ref[...].astype(BF16)
        acc_ref[...] = jnp.zeros_like(acc_ref)

    xb = xb_ref[...]
    gate = jnp.dot(xb, wg_ref[...], preferred_element_type=F32)
    up = jnp.dot(xb, wu_ref[...], preferred_element_type=F32)
    acc_ref[...] += _bdot(_silu(gate) * up, wd_ref[...])

    @pl.when(f == pl.num_programs(1) - 1)
    def _():
        out_ref[...] = _layer_norm(ALPHA * x_ref[...] + acc_ref[...], g_ref[...], b_ref[...])


def _dense_ffn(x, w_gate, w_up, w_down, ln_g, ln_b, tm=FFN_TM, tf=FFN_TF):
    n = x.shape[0]
    dff = w_gate.shape[1]
    g, b = ln_g.reshape(1, D_MODEL), ln_b.reshape(1, D_MODEL)
    return pl.pallas_call(
        _ffn_kernel,
        name="dense_ffn_ln",
        grid=(n // tm, dff // tf),
        in_specs=[pl.BlockSpec((tm, D_MODEL), lambda i, f: (i, 0)),
                  pl.BlockSpec((D_MODEL, tf), lambda i, f: (0, f)),
                  pl.BlockSpec((D_MODEL, tf), lambda i, f: (0, f)),
                  pl.BlockSpec((tf, D_MODEL), lambda i, f: (f, 0)),
                  pl.BlockSpec((1, D_MODEL), lambda i, f: (0, 0)),
                  pl.BlockSpec((1, D_MODEL), lambda i, f: (0, 0))],
        out_specs=pl.BlockSpec((tm, D_MODEL), lambda i, f: (i, 0)),
        out_shape=jax.ShapeDtypeStruct((n, D_MODEL), F32),
        scratch_shapes=[pltpu.VMEM((tm, D_MODEL), BF16), pltpu.VMEM((tm, D_MODEL), F32)],
        compiler_params=_cparams(("parallel", "arbitrary")),
    )(x, w_gate.astype(BF16), w_up.astype(BF16), w_down.astype(BF16), g, b)


def _gather_rows(idx_ref, src_hbm, dst, sem, count):
    def body(i, carry):
        pltpu.make_async_copy(src_hbm.at[pl.ds(idx_ref[0, i], 1)], dst.at[pl.ds(i, 1)], sem).start()
        return carry
    lax.fori_loop(0, count, body, 0, unroll=8)


def _wait_rows(src_hbm, dst, sem, count):
    pltpu.make_async_copy(src_hbm.at[pl.ds(0, count)], dst, sem).wait()


def _dispatch_kernel(dest_ref, x_ref, xs_in, xs_out, sem):
    del xs_in
    tm = x_ref.shape[0]

    def body(i, carry):
        for k in range(TOP_K):
            pltpu.make_async_copy(x_ref.at[pl.ds(i, 1)], xs_out.at[pl.ds(dest_ref[0, TOP_K * i + k], 1)],
                                  sem.at[0]).start()
        return carry
    lax.fori_loop(0, tm, body, 0, unroll=4)
    for _ in range(TOP_K):
        pltpu.make_async_copy(x_ref, xs_out.at[pl.ds(0, tm)], sem.at[0]).wait()


def _moe_dispatch(x, dest, n_rows, tm=DSP_TM):
    n = x.shape[0]
    dest3 = dest.reshape(n // tm, 1, TOP_K * tm)
    return pl.pallas_call(
        _dispatch_kernel,
        name="moe_dispatch",
        grid=(n // tm,),
        in_specs=[pl.BlockSpec((None, 1, TOP_K * tm), lambda i: (i, 0, 0), memory_space=pltpu.SMEM),
                  pl.BlockSpec((tm, D_MODEL), lambda i: (i, 0)),
                  pl.BlockSpec(memory_space=pl.ANY)],
        out_specs=pl.BlockSpec(memory_space=pl.ANY),
        out_shape=jax.ShapeDtypeStruct((n_rows, D_MODEL), F32),
        scratch_shapes=[pltpu.SemaphoreType.DMA((1,))],
        input_output_aliases={2: 0},
        compiler_params=_cparams(("arbitrary",), disable_bounds_checks=True),
    )(dest3, x, jnp.zeros((n_rows, D_MODEL), F32))


def _moe_kernel(meta_ref, x_ref, wg_ref, wu_ref, wd_ref, y_ref, xb_ref, acc_ref):
    b = pl.program_id(0)
    f = pl.program_id(1)
    nblk = pl.num_programs(0)
    used = b < meta_ref[nblk]

    @pl.when(used)
    def _():
        @pl.when(f == 0)
        def _():
            xb_ref[...] = x_ref[...].astype(BF16)
            acc_ref[...] = jnp.zeros_like(acc_ref)

        xb = xb_ref[...]
        gate = jnp.dot(xb, wg_ref[...], preferred_element_type=F32)
        up = jnp.dot(xb, wu_ref[...], preferred_element_type=F32)
        acc_ref[...] += _bdot(_silu(gate) * up, wd_ref[...])

    @pl.when(f == pl.num_programs(1) - 1)
    def _():
        @pl.when(used)
        def _():
            y_ref[...] = acc_ref[...]

        @pl.when(jnp.logical_not(used))
        def _():
            y_ref[...] = jnp.zeros_like(y_ref)


def _moe_experts(xs, meta, w_gate, w_up, w_down, tb=MOE_TB, tf=MOE_TF):
    nblk = xs.shape[0] // tb
    dffe = w_gate.shape[2]
    grid_spec = pltpu.PrefetchScalarGridSpec(
        num_scalar_prefetch=1,
        grid=(nblk, dffe // tf),
        in_specs=[pl.BlockSpec((tb, D_MODEL), lambda b, f, m: (b, 0)),
                  pl.BlockSpec((None, D_MODEL, tf), lambda b, f, m: (m[b], 0, f)),
                  pl.BlockSpec((None, D_MODEL, tf), lambda b, f, m: (m[b], 0, f)),
                  pl.BlockSpec((None, tf, D_MODEL), lambda b, f, m: (m[b], f, 0))],
        out_specs=pl.BlockSpec((tb, D_MODEL), lambda b, f, m: (b, 0)),
        scratch_shapes=[pltpu.VMEM((tb, D_MODEL), BF16), pltpu.VMEM((tb, D_MODEL), F32)],
    )
    return pl.pallas_call(
        _moe_kernel,
        name="moe_experts",
        grid_spec=grid_spec,
        out_shape=jax.ShapeDtypeStruct((nblk * tb, D_MODEL), F32),
        compiler_params=_cparams(("parallel", "arbitrary")),
    )(meta, xs, w_gate.astype(BF16), w_up.astype(BF16), w_down.astype(BF16))


def _combine_kernel(pos0_ref, posn_ref, y_hbm, gates_ref, x_ref, g_ref, b_ref, out_ref, ybuf, sem):
    i = pl.program_id(0)
    tm = out_ref.shape[0]
    slot = i % 2

    @pl.when(i == 0)
    def _():
        _gather_rows(pos0_ref, y_hbm, ybuf.at[0], sem.at[0], TOP_K * tm)

    _wait_rows(y_hbm, ybuf.at[slot], sem.at[slot], TOP_K * tm)

    @pl.when(i + 1 < pl.num_programs(0))
    def _():
        _gather_rows(posn_ref, y_hbm, ybuf.at[1 - slot], sem.at[1 - slot], TOP_K * tm)

    gates = gates_ref[...]
    f = gates[:, 0:1] * ybuf[slot, :tm] + gates[:, 1:2] * ybuf[slot, tm:]
    out_ref[...] = _layer_norm(ALPHA * x_ref[...] + f, g_ref[...], b_ref[...])


def _moe_combine(y, pos, gates, x, ln_g, ln_b, tm=CMB_TM):
    n = x.shape[0]
    nt = n // tm
    pos3 = pos.reshape(nt, tm, TOP_K).transpose(0, 2, 1).reshape(nt, 1, TOP_K * tm)
    g, b = ln_g.reshape(1, D_MODEL), ln_b.reshape(1, D_MODEL)
    smem_blk = lambda imap: pl.BlockSpec((None, 1, TOP_K * tm), imap, memory_space=pltpu.SMEM)
    return pl.pallas_call(
        _combine_kernel,
        name="moe_combine_ln",
        grid=(nt,),
        in_specs=[smem_blk(lambda i: (0, 0, 0)),
                  smem_blk(lambda i: (jnp.minimum(i + 1, nt - 1), 0, 0)),
                  pl.BlockSpec(memory_space=pl.ANY),
                  pl.BlockSpec((tm, TOP_K), lambda i: (i, 0)),
                  pl.BlockSpec((tm, D_MODEL), lambda i: (i, 0)),
                  pl.BlockSpec((1, D_MODEL), lambda i: (0, 0)),
                  pl.BlockSpec((1, D_MODEL), lambda i: (0, 0))],
        out_specs=pl.BlockSpec((tm, D_MODEL), lambda i: (i, 0)),
        out_shape=jax.ShapeDtypeStruct((n, D_MODEL), F32),
        scratch_shapes=[pltpu.VMEM((2, TOP_K * tm, D_MODEL), F32), pltpu.SemaphoreType.DMA((2,))],
        compiler_params=_cparams(("arbitrary",), disable_bounds_checks=True),
    )(pos3, pos3, y, gates, x, g, b)


def _moe_routing(logits, tb):
    n = logits.shape[0]
    top_logit, top_idx = lax.top_k(logits, TOP_K)
    gates = jax.nn.softmax(top_logit, axis=-1)
    na = n * TOP_K
    e_flat = top_idx.reshape(-1).astype(jnp.int32)
    onehot = (e_flat[None, :] == jnp.arange(N_EXPERTS, dtype=jnp.int32)[:, None]).astype(jnp.int32)
    running = jnp.cumsum(onehot, axis=1)
    counts = running[:, -1]
    padded = (counts + tb - 1) // tb * tb
    pend = jnp.cumsum(padded)
    pstart = pend - padded
    dest = jnp.sum(onehot * (pstart[:, None] + running - 1), axis=0).reshape(n, TOP_K)
    nblk = -(-na // tb) + N_EXPERTS
    block_expert = jnp.minimum(jnp.searchsorted(pend, jnp.arange(nblk, dtype=jnp.int32) * tb, side='right'),
                               N_EXPERTS - 1).astype(jnp.int32)
    meta = jnp.concatenate([block_expert, (pend[-1:] // tb).astype(jnp.int32)])
    return dest.astype(jnp.int32), gates, meta, nblk


def _moe_ffn(x, w_router, w_gate, w_up, w_down, ln_g, ln_b):
    wr = jnp.pad(w_router, ((0, 0), (0, LANES - N_EXPERTS))).astype(BF16)
    logits = _matmul(x, wr, F32, "moe_router")[:, :N_EXPERTS]
    dest, gates, meta, nblk = _moe_routing(logits, MOE_TB)
    xs = _moe_dispatch(x, dest, nblk * MOE_TB)
    y = _moe_experts(xs, meta, w_gate, w_up, w_down)
    return _moe_combine(y, dest, gates, x, ln_g, ln_b)


def _split_w_in(w):
    a_end = 3 * A_QKV_W
    bz_end = a_end + 3 * B_W + B_W
    w_groups = [jnp.concatenate([w[:, s * A_QKV_W + gi * A_GROUP_W:s * A_QKV_W + (gi + 1) * A_GROUP_W]
                                 for s in range(3)], axis=1).astype(BF16) for gi in range(N_GROUPS)]
    w_b = jnp.concatenate([w[:, a_end:bz_end], w[:, bz_end + 2 * N_HEADS_B:]], axis=1)
    zpad = jnp.zeros((w.shape[0], LANES - N_HEADS_B), w.dtype)
    w_ba = jnp.concatenate([w[:, bz_end:bz_end + N_HEADS_B], zpad,
                            w[:, bz_end + N_HEADS_B:bz_end + 2 * N_HEADS_B], zpad], axis=1)
    return w_groups, w_b.astype(BF16), w_ba.astype(BF16)


def _hybrid_layer(x, batch, seq, rel_bias, w_in, conv_w, a_log, dt_bias, o_norm_w, w_oa, w_ob, w_out, ln_g, ln_b):
    w_groups, w_b, w_ba = _split_w_in(w_in)
    proj_b = _matmul(x, w_b, F32, "in_proj_gdn_gates")
    proj_ba = _matmul(x, w_ba, F32, "in_proj_beta_decay")
    outs, lses = [], []
    for gi, (_, dilation) in enumerate(DSWA_PATTERNS):
        qkv = _in_proj_strided(x, w_groups[gi], dilation, batch, seq)
        o, lse = _dswa_group(qkv, _band_bias(rel_bias, gi, dilation), dilation)
        outs.append(o)
        lses.append(lse)
    yb = _gated_deltanet(proj_b, proj_ba, conv_w, a_log, dt_bias, o_norm_w, batch, seq)
    return _mix_out(outs, lses, yb, proj_b, x, w_oa, w_ob, w_out, ln_g, ln_b)


def kernel(x, rel_bias, w_in, conv_w, a_log, dt_bias, o_norm_w, w_oa, w_ob, w_out, ln1_g, ln1_b,
           ffn_w_gate, ffn_w_up, ffn_w_down, moe_router, moe_w_gate, moe_w_up, moe_w_down, ln2_g, ln2_b):
    batch, seq, d = x.shape
    h = x.reshape(batch * seq, d)
    for layer in range(DEPTH):
        h = _hybrid_layer(h, batch, seq, rel_bias, w_in[layer], conv_w[layer], a_log[layer], dt_bias[layer],
                          o_norm_w[layer], w_oa[layer], w_ob[layer], w_out[layer], ln1_g[layer], ln1_b[layer])
        j = layer // 2
        if layer % 2 == 0:
            h = _dense_ffn(h, ffn_w_gate[j], ffn_w_up[j], ffn_w_down[j], ln2_g[layer], ln2_b[layer])
        else:
            h = _moe_ffn(h, moe_router[j], moe_w_gate[j], moe_w_up[j], moe_w_down[j], ln2_g[layer], ln2_b[layer])
    return h.reshape(batch, seq, d)
```

```python
import functools
import math

import jax
import jax.numpy as jnp
from jax import lax
from jax.experimental import pallas as pl
from jax.experimental.pallas import tpu as pltpu

F32 = jnp.float32
BF16 = jnp.bfloat16

D_MODEL = 1024
DEPTH = 2
DSWA_PATTERNS = ((128, 1), (512, 4), (2048, 16))
N_GROUPS = 3
HEADS_PER_GROUP = 4
HEAD_DIM = 128
A_QKV_W = N_GROUPS * HEADS_PER_GROUP * HEAD_DIM
A_GROUP_W = HEADS_PER_GROUP * HEAD_DIM
NUM_BUCKETS = 32
MAX_DISTANCE = 2048
N_HEADS_B = 8
B_W = N_HEADS_B * HEAD_DIM
CONV_WIDTH = 4
N_EXPERTS = 8
TOP_K = 2
ALPHA = (2 * DEPTH) ** 0.25
LN_EPS = 1e-5
RMS_EPS = 1e-6

LANES = 128
SUBLANES = 8
VMEM_LIMIT = 56 * 1024 * 1024

ATT_BLK = 128
GDN_CHUNK = 128
MM_TM = 512
MM_TN = 3072
PROJ_A_TM = 1024
OUT_TM = 256
FFN_TM = 512
FFN_TF = 1408
MOE_TB = 512
MOE_TF = 1792
CMB_TM = 256
DSP_TM = 512

NEG_BIG = -1e30


def _cparams(sem, vmem=VMEM_LIMIT, **kw):
    return pltpu.CompilerParams(dimension_semantics=sem, vmem_limit_bytes=vmem, **kw)


def _bdot(a, b):
    return jnp.dot(a.astype(BF16), b.astype(BF16), preferred_element_type=F32)


def _bdot_nt(a, b):
    return lax.dot_general(a.astype(BF16), b.astype(BF16), (((1,), (1,)), ((), ())),
                           preferred_element_type=F32)


def _bdot_tn(a, b):
    return lax.dot_general(a.astype(BF16), b.astype(BF16), (((0,), (0,)), ((), ())),
                           preferred_element_type=F32)


def _sigmoid(v):
    return 1.0 / (1.0 + jnp.exp(-v))


def _silu(v):
    return v * _sigmoid(v)


def _layer_norm(v, g, b):
    mu = jnp.mean(v, axis=-1, keepdims=True)
    d = v - mu
    var = jnp.mean(d * d, axis=-1, keepdims=True)
    return d * lax.rsqrt(var + LN_EPS) * g + b


def _mm_kernel(x_ref, w_ref, o_ref):
    o_ref[...] = _bdot(x_ref[...], w_ref[...]).astype(o_ref.dtype)


def _matmul(x, w, out_dtype, name, tm=MM_TM, tn=MM_TN):
    m, k = x.shape
    n = w.shape[1]
    tn = min(tn, n)
    return pl.pallas_call(
        _mm_kernel,
        name=name,
        grid=(n // tn, m // tm),
        in_specs=[pl.BlockSpec((tm, k), lambda j, i: (i, 0)),
                  pl.BlockSpec((k, tn), lambda j, i: (0, j))],
        out_specs=pl.BlockSpec((tm, tn), lambda j, i: (i, j)),
        out_shape=jax.ShapeDtypeStruct((m, n), out_dtype),
        compiler_params=_cparams(("parallel", "parallel")),
    )(x, w)


def _dswa_kernel(q_ref, kc_ref, kp_ref, vc_ref, vp_ref, bias_ref, o_ref, lse_ref, *, tq):
    j = pl.program_id(2)
    nqb = tq // ATT_BLK
    q = q_ref[...]
    kwin = jnp.concatenate([kp_ref[...], kc_ref[...]], axis=0)
    vwin = jnp.concatenate([vp_ref[...], vc_ref[...]], axis=0)
    row = lax.broadcasted_iota(jnp.int32, (ATT_BLK, 2 * ATT_BLK), 0)
    col = lax.broadcasted_iota(jnp.int32, (ATT_BLK, 2 * ATT_BLK), 1)
    delta = row + ATT_BLK - col
    band = (delta >= 0) & (delta <= ATT_BLK)
    band_first = band & ((col >= ATT_BLK) | (j > 0))
    lane = lax.broadcasted_iota(jnp.int32, (ATT_BLK, LANES), 1)
    scale = HEAD_DIM ** -0.5
    for c in range(nqb):
        mask = band_first if c == 0 else band
        lse_tile = jnp.zeros((ATT_BLK, LANES), F32)
        for h in range(HEADS_PER_GROUP):
            hs = slice(h * HEAD_DIM, (h + 1) * HEAD_DIM)
            qh = q[c * ATT_BLK:(c + 1) * ATT_BLK, hs]
            kh = kwin[c * ATT_BLK:(c + 2) * ATT_BLK, hs]
            vh = vwin[c * ATT_BLK:(c + 2) * ATT_BLK, hs]
            s = _bdot_nt(qh, kh) * scale + bias_ref[h]
            s = jnp.where(mask, s, NEG_BIG)
            m = jnp.max(s, axis=-1, keepdims=True)
            p = jnp.exp(s - m)
            l = jnp.sum(p, axis=-1, keepdims=True)
            o = _bdot(p, vh) / l
            o_ref[c * ATT_BLK:(c + 1) * ATT_BLK, hs] = o
            lse_tile = jnp.where(lane == h, m + jnp.log(l), lse_tile)
        lse_ref[c * ATT_BLK:(c + 1) * ATT_BLK, :] = lse_tile


def _in_proj_strided_kernel(x_ref, w_ref, o_ref, *scratch, dilation):
    res = _bdot(x_ref[...], w_ref[...])
    if dilation == 1:
        o_ref[0] = res.astype(o_ref.dtype)
    else:
        res_ref, = scratch
        rows = res.shape[0] // dilation
        for t in range(res.shape[1] // LANES):
            ls = slice(t * LANES, (t + 1) * LANES)
            res_ref[t] = res[:, ls]
            for r in range(dilation):
                o_ref[r, :, ls] = res_ref[t, pl.ds(r, rows, stride=dilation), :].astype(o_ref.dtype)


def _in_proj_strided(x, w, dilation, batch, seq, tm=PROJ_A_TM):
    k = x.shape[1]
    wn = tn = w.shape[1]
    tpb = seq // tm
    scratch = [] if dilation == 1 else [pltpu.VMEM((tn // LANES, tm, LANES), F32)]
    return pl.pallas_call(
        functools.partial(_in_proj_strided_kernel, dilation=dilation),
        name=f"in_proj_attn_d{dilation}",
        grid=(batch * tpb, wn // tn),
        in_specs=[pl.BlockSpec((tm, k), lambda i, j: (i, 0)),
                  pl.BlockSpec((k, tn), lambda i, j: (0, j))],
        out_specs=pl.BlockSpec((None, dilation, tm // dilation, tn), lambda i, j: (i // tpb, 0, i % tpb, j)),
        out_shape=jax.ShapeDtypeStruct((batch, dilation, seq // dilation, wn), BF16),
        scratch_shapes=scratch,
        compiler_params=_cparams(("parallel", "parallel")),
    )(x, w)


def _dswa_group(qkv, bias, dilation):
    batch, _, n, _ = qkv.shape
    tq = min(512, n)
    nqb = tq // ATT_BLK

    def cur(off):
        return pl.BlockSpec((None, None, tq, A_GROUP_W), lambda b, r, j: (b, r, j, off))

    def prev(off):
        return pl.BlockSpec((None, None, ATT_BLK, A_GROUP_W),
                            lambda b, r, j: (b, r, jnp.maximum(j * nqb - 1, 0), off))

    return pl.pallas_call(
        functools.partial(_dswa_kernel, tq=tq),
        name=f"dswa_d{dilation}",
        grid=(batch, dilation, n // tq),
        in_specs=[cur(0), cur(1), prev(1), cur(2), prev(2),
                  pl.BlockSpec((HEADS_PER_GROUP, ATT_BLK, 2 * ATT_BLK), lambda b, r, j: (0, 0, 0))],
        out_specs=[pl.BlockSpec((None, None, tq, A_GROUP_W), lambda b, r, j: (b, r, j, 0)),
                   pl.BlockSpec((None, None, tq, LANES), lambda b, r, j: (b, r, j, 0))],
        out_shape=[jax.ShapeDtypeStruct((batch, dilation, n, A_GROUP_W), F32),
                   jax.ShapeDtypeStruct((batch, dilation, n, LANES), F32)],
        compiler_params=_cparams(("parallel", "parallel", "parallel")),
    )(qkv, qkv, qkv, qkv, qkv, bias)


def _t5_causal_bucket(dist):
    num_exact = NUM_BUCKETS // 2
    d = jnp.maximum(dist, 1).astype(F32)
    large = num_exact + (jnp.log(d / num_exact) / math.log(MAX_DISTANCE / num_exact)
                         * (NUM_BUCKETS - num_exact)).astype(jnp.int32)
    large = jnp.minimum(large, NUM_BUCKETS - 1)
    return jnp.where(dist < num_exact, dist, large)


def _band_bias(rel_bias, gi, dilation):
    qi = jnp.arange(ATT_BLK)[:, None] + ATT_BLK
    kj = jnp.arange(2 * ATT_BLK)[None, :]
    delta = jnp.maximum(qi - kj, 0) * dilation
    table = rel_bias[:, gi * HEADS_PER_GROUP:(gi + 1) * HEADS_PER_GROUP].astype(F32)
    onehot = (_t5_causal_bucket(delta)[..., None] == jnp.arange(NUM_BUCKETS)).astype(F32)
    return jnp.einsum('qkn,nh->hqk', onehot, table, precision=lax.Precision.HIGHEST)


def _dot16(a, b):
    return jnp.dot(a, b, preferred_element_type=F32)


def _unit_lower_inverse(a, row, col):
    base = 16
    heads = range(len(a))
    eye = (row == col).astype(F32)
    same = lambda sz: (row // sz) == (col // sz)
    blk = same(base)
    a_d = [jnp.where(blk, a[h], 0.0) for h in heads]
    t = [eye - a_d[h] for h in heads]
    p = [a_d[h].astype(BF16) for h in heads]
    for _ in range(3):
        p = [_dot16(p[h], p[h]).astype(BF16) for h in heads]
        t = [t[h] + _dot16(t[h].astype(BF16), p[h]) for h in heads]
    sz = 2 * base
    while sz <= GDN_CHUNK:
        off = same(sz) & jnp.logical_not(same(sz // 2))
        tb = [t[h].astype(BF16) for h in heads]
        m = [_dot16(jnp.where(off, a[h], 0.0).astype(BF16), tb[h]).astype(BF16) for h in heads]
        t = [t[h] - _dot16(tb[h], m[h]) for h in heads]
        sz *= 2
    return t


def _gdn_kernel(x_ref, halo_ref, z_ref, ba_ref, cw_ref, alog_ref, dtb_ref, onw_ref, y_ref, state_ref):
    c = pl.program_id(1)
    C = GDN_CHUNK
    heads = range(N_HEADS_B)

    @pl.when(c == 0)
    def _():
        state_ref[...] = jnp.zeros_like(state_ref)

    keep_halo = (c > 0).astype(F32)
    row = lax.broadcasted_iota(jnp.int32, (C, C), 0)
    col = lax.broadcasted_iota(jnp.int32, (C, C), 1)
    incl = row >= col
    strict = row > col

    ba = ba_ref[...]
    beta_all = _sigmoid(ba[:, :LANES])
    a_in = ba[:, LANES:] + dtb_ref[...]
    softplus = jnp.maximum(a_in, 0.0) + jnp.log(1.0 + jnp.exp(-jnp.abs(a_in)))
    g_all = -jnp.exp(alog_ref[...]) * softplus
    ones_l = incl.astype(BF16)
    g1 = g_all.astype(BF16)
    r1 = g_all - g1.astype(F32)
    g2 = r1.astype(BF16)
    g3 = (r1 - g2.astype(F32)).astype(BF16)
    G = (jnp.dot(ones_l, g1, preferred_element_type=F32) + jnp.dot(ones_l, g2, preferred_element_type=F32)
         + jnp.dot(ones_l, g3, preferred_element_type=F32))
    GT = G.T

    def conv_silu(col0):
        cs = slice(col0, col0 + HEAD_DIM)
        xc = x_ref[:, cs]
        xf = jnp.concatenate([halo_ref[:, cs] * keep_halo, xc], axis=0)
        w = cw_ref[:, cs]
        y = w[CONV_WIDTH - 1:CONV_WIDTH] * xc
        for i in range(CONV_WIDTH - 1):
            off = SUBLANES - (CONV_WIDTH - 1) + i
            y = y + w[i:i + 1] * xf[off:off + C]
        return _silu(y)

    def l2norm(t):
        return t * lax.rsqrt(jnp.sum(t * t, axis=-1, keepdims=True) + RMS_EPS)

    q = [l2norm(conv_silu(h * HEAD_DIM)) * (HEAD_DIM ** -0.5) for h in heads]
    k = [l2norm(conv_silu(B_W + h * HEAD_DIM)) for h in heads]
    v = [conv_silu(2 * B_W + h * HEAD_DIM) for h in heads]
    beta = [beta_all[:, h:h + 1] for h in heads]
    gc = [G[:, h:h + 1] for h in heads]
    g_last = [G[C - 1:C, h:h + 1] for h in heads]
    decay = [jnp.exp(jnp.where(incl, gc[h] - GT[h:h + 1, :], NEG_BIG)) for h in heads]
    eg = [jnp.exp(gc[h]) for h in heads]
    kb = [k[h].astype(BF16) for h in heads]
    kq = [lax.dot_general(jnp.concatenate([kb[h], q[h].astype(BF16)], axis=0), kb[h],
                          (((1,), (1,)), ((), ())), preferred_element_type=F32) for h in heads]
    a = [jnp.where(strict, beta[h] * kq[h][:C] * decay[h], 0.0) for h in heads]
    t_inv = _unit_lower_inverse(a, row, col)
    rhs = [jnp.concatenate([beta[h] * v[h], (beta[h] * eg[h]) * k[h]], axis=1).astype(BF16) for h in heads]
    sol = [_dot16(t_inv[h].astype(BF16), rhs[h]) for h in heads]
    qk = [(kq[h][C:] * decay[h]).astype(BF16) for h in heads]
    wq = [jnp.concatenate([sol[h][:, HEAD_DIM:], q[h] * eg[h]], axis=0).astype(BF16) for h in heads]
    k_dec = [(k[h] * jnp.exp(g_last[h] - gc[h])).astype(BF16) for h in heads]

    state = [state_ref[h] for h in heads]
    ws = [_dot16(wq[h], state[h].astype(BF16)) for h in heads]
    u = [(sol[h][:, :HEAD_DIM] - ws[h][:C]).astype(BF16) for h in heads]
    o = [ws[h][C:] + _dot16(qk[h], u[h]) for h in heads]
    for h in heads:
        state_ref[h] = jnp.exp(g_last[h]) * state[h] + lax.dot_general(
            k_dec[h], u[h], (((0,), (0,)), ((), ())), preferred_element_type=F32)
    for h in heads:
        oh = o[h] * lax.rsqrt(jnp.mean(o[h] * o[h], axis=-1, keepdims=True) + RMS_EPS) * onw_ref[...]
        hs = slice(h * HEAD_DIM, (h + 1) * HEAD_DIM)
        y_ref[:, hs] = (oh * _silu(z_ref[:, hs])).astype(y_ref.dtype)


def _gated_deltanet(proj_b, proj_ba, conv_w, a_log, dt_bias, o_norm_w, batch, seq):
    C = GDN_CHUNK
    wb = proj_b.shape[1]
    pb = proj_b.reshape(batch, seq, wb)
    pba = proj_ba.reshape(batch, seq, 2 * LANES)
    pad = lambda t: jnp.pad(t.astype(F32), (0, LANES - t.shape[0])).reshape(1, LANES)
    const = lambda shape: pl.BlockSpec(shape, lambda b, c: (0,) * len(shape))
    y = pl.pallas_call(
        _gdn_kernel,
        name="gated_deltanet",
        grid=(batch, seq // C),
        in_specs=[pl.BlockSpec((None, C, 3 * B_W), lambda b, c: (b, c, 0)),
                  pl.BlockSpec((None, SUBLANES, 3 * B_W),
                               lambda b, c: (b, jnp.maximum(c * (C // SUBLANES) - 1, 0), 0)),
                  pl.BlockSpec((None, C, B_W), lambda b, c: (b, c, 3)),
                  pl.BlockSpec((None, C, 2 * LANES), lambda b, c: (b, c, 0)),
                  const((CONV_WIDTH, 3 * B_W)), const((1, LANES)), const((1, LANES)), const((1, LANES))],
        out_specs=pl.BlockSpec((None, C, B_W), lambda b, c: (b, c, 0)),
        out_shape=jax.ShapeDtypeStruct((batch, seq, B_W), BF16),
        scratch_shapes=[pltpu.VMEM((N_HEADS_B, HEAD_DIM, HEAD_DIM), F32)],
        compiler_params=_cparams(("parallel", "arbitrary")),
    )(pb, pb, pb, pba, conv_w.astype(F32), pad(a_log), pad(dt_bias), o_norm_w.astype(F32).reshape(1, LANES))
    return y.reshape(batch * seq, B_W)


def _mix_out_kernel(o0, o1, o2, l0, l1, l2, yb_ref, ga_ref, gb_ref, x_ref, woa_ref, wob_ref, wout_ref,
                    g_ref, b_ref, out_ref, *scratch):
    def token_order(ref, scr):
        dilation, rows, width = ref.shape
        if dilation == 1:
            return ref[0]
        planes = []
        for t in range(width // LANES):
            for r in range(dilation):
                scr[t, pl.ds(r, rows, stride=dilation), :] = ref[r, :, t * LANES:(t + 1) * LANES]
            planes.append(scr[t])
        return jnp.concatenate(planes, axis=1)

    outs = (token_order(o0, None), token_order(o1, scratch[0]), token_order(o2, scratch[1]))
    lses = (token_order(l0, None), token_order(l1, scratch[2]), token_order(l2, scratch[3]))
    ya = []
    for h in range(HEADS_PER_GROUP):
        ls = [t[:, h:h + 1] for t in lses]
        m = jnp.maximum(jnp.maximum(ls[0], ls[1]), ls[2])
        es = [jnp.exp(t - m) for t in ls]
        inv = 1.0 / (es[0] + es[1] + es[2])
        hs = slice(h * HEAD_DIM, (h + 1) * HEAD_DIM)
        ya.append((es[0] * inv) * outs[0][:, hs] + (es[1] * inv) * outs[1][:, hs] + (es[2] * inv) * outs[2][:, hs])
    ya = jnp.concatenate(ya, axis=1)
    pa = _bdot(ya, woa_ref[...])
    pb = _bdot(yb_ref[...], wob_ref[...])
    merged = _sigmoid(ga_ref[...]) * pa + _sigmoid(gb_ref[...]) * pb
    mix = _bdot(merged, wout_ref[...])
    out_ref[...] = _layer_norm(ALPHA * x_ref[...] + mix, g_ref[...], b_ref[...])


def _mix_out(outs, lses, yb, proj_b, x, w_oa, w_ob, w_out, ln_g, ln_b, tm=OUT_TM):
    n = x.shape[0]
    seq = outs[0].shape[1] * outs[0].shape[2]
    tpb = seq // tm
    rowblk = lambda w, cb=0: pl.BlockSpec((tm, w), lambda i: (i, cb))
    const = lambda a: pl.BlockSpec(a.shape, lambda i: (0, 0))

    def grouped(a):
        d, w = a.shape[1], a.shape[3]
        return pl.BlockSpec((None, d, tm // d, w), lambda i: (i // tpb, 0, i % tpb, 0))

    wa, wb, wo = w_oa.astype(BF16), w_ob.astype(BF16), w_out.astype(BF16)
    g, b = ln_g.reshape(1, D_MODEL), ln_b.reshape(1, D_MODEL)
    return pl.pallas_call(
        _mix_out_kernel,
        name="mix_out_ln",
        grid=(n // tm,),
        in_specs=[grouped(a) for a in (*outs, *lses)]
        + [rowblk(B_W), rowblk(D_MODEL, 4), rowblk(D_MODEL, 5), rowblk(D_MODEL),
           const(wa), const(wb), const(wo), const(g), const(b)],
        out_specs=rowblk(D_MODEL),
        out_shape=jax.ShapeDtypeStruct((n, D_MODEL), F32),
        scratch_shapes=[pltpu.VMEM((A_GROUP_W // LANES, tm, LANES), F32)] * 2 + [pltpu.VMEM((1, tm, LANES), F32)] * 2,
        compiler_params=_cparams(("parallel",)),
    )(*outs, *lses, yb, proj_b, proj_b, x, wa, wb, wo, g, b)


def _ffn_kernel(x_ref, wg_ref, wu_ref, wd_ref, g_ref, b_ref, out_ref, xb_ref, acc_ref):
    f = pl.program_id(1)

    @pl.when(f == 0)
    def _():
        xb_ref[...] = x_ref[...].astype(BF16)
        acc_ref[...] = jnp.zeros_like(acc_ref)

    xb = xb_ref[...]
    gate = jnp.dot(xb, wg_ref[...], preferred_element_type=F32)
    up = jnp.dot(xb, wu_ref[...], preferred_element_type=F32)
    acc_ref[...] += _bdot(_silu(gate) * up, wd_ref[...])

    @pl.when(f == pl.num_programs(1) - 1)
    def _():
        out_ref[...] = _layer_norm(ALPHA * x_ref[...] + acc_ref[...], g_ref[...], b_ref[...])


def _dense_ffn(x, w_gate, w_up, w_down, ln_g, ln_b, tm=FFN_TM, tf=FFN_TF):
    n = x.shape[0]
    dff = w_gate.shape[1]
    g, b = ln_g.reshape(1, D_MODEL), ln_b.reshape(1, D_MODEL)
    return pl.pallas_call(
        _ffn_kernel,
        name="dense_ffn_ln",
        grid=(n // tm, dff // tf),
        in_specs=[pl.BlockSpec((tm, D_MODEL), lambda i, f: (i, 0)),
                  pl.BlockSpec((D_MODEL, tf), lambda i, f: (0, f)),
                  pl.BlockSpec((D_MODEL, tf), lambda i, f: (0, f)),
                  pl.BlockSpec((tf, D_MODEL), lambda i, f: (f, 0)),
                  pl.BlockSpec((1, D_MODEL), lambda i, f: (0, 0)),
                  pl.BlockSpec((1, D_MODEL), lambda i, f: (0, 0))],
        out_specs=pl.BlockSpec((tm, D_MODEL), lambda i, f: (i, 0)),
        out_shape=jax.ShapeDtypeStruct((n, D_MODEL), F32),
        scratch_shapes=[pltpu.VMEM((tm, D_MODEL), BF16), pltpu.VMEM((tm, D_MODEL), F32)],
        compiler_params=_cparams(("parallel", "arbitrary")),
    )(x, w_gate.astype(BF16), w_up.astype(BF16), w_down.astype(BF16), g, b)


def _gather_rows(idx_ref, src_hbm, dst, sem, count):
    def body(i, carry):
        pltpu.make_async_copy(src_hbm.at[pl.ds(idx_ref[0, i], 1)], dst.at[pl.ds(i, 1)], sem).start()
        return carry
    lax.fori_loop(0, count, body, 0, unroll=8)


def _wait_rows(src_hbm, dst, sem, count):
    pltpu.make_async_copy(src_hbm.at[pl.ds(0, count)], dst, sem).wait()


def _dispatch_kernel(dest_ref, x_ref, xs_in, xs_out, sem):
    del xs_in
    tm = x_ref.shape[0]

    def body(i, carry):
        for k in range(TOP_K):
            pltpu.make_async_copy(x_ref.at[pl.ds(i, 1)], xs_out.at[pl.ds(dest_ref[0, TOP_K * i + k], 1)],
                                  sem.at[0]).start()
        return carry
    lax.fori_loop(0, tm, body, 0, unroll=4)
    for _ in range(TOP_K):
        pltpu.make_async_copy(x_ref, xs_out.at[pl.ds(0, tm)], sem.at[0]).wait()


def _moe_dispatch(x, dest, n_rows, tm=DSP_TM):
    n = x.shape[0]
    dest3 = dest.reshape(n // tm, 1, TOP_K * tm)
    return pl.pallas_call(
        _dispatch_kernel,
        name="moe_dispatch",
        grid=(n // tm,),
        in_specs=[pl.BlockSpec((None, 1, TOP_K * tm), lambda i: (i, 0, 0), memory_space=pltpu.SMEM),
                  pl.BlockSpec((tm, D_MODEL), lambda i: (i, 0)),
                  pl.BlockSpec(memory_space=pl.ANY)],
        out_specs=pl.BlockSpec(memory_space=pl.ANY),
        out_shape=jax.ShapeDtypeStruct((n_rows, D_MODEL), F32),
        scratch_shapes=[pltpu.SemaphoreType.DMA((1,))],
        input_output_aliases={2: 0},
        compiler_params=_cparams(("arbitrary",), disable_bounds_checks=True),
    )(dest3, x, jnp.zeros((n_rows, D_MODEL), F32))


def _moe_kernel(meta_ref, x_ref, wg_ref, wu_ref, wd_ref, y_ref, xb_ref, acc_ref):
    b = pl.program_id(0)
    f = pl.program_id(1)
    nblk = pl.num_programs(0)
    used = b < meta_ref[nblk]

    @pl.when(used)
    def _():
        @pl.when(f == 0)
        def _():
            xb_ref[...] = x_ref[...].astype(BF16)
            acc_ref[...] = jnp.zeros_like(acc_ref)

        xb = xb_ref[...]
        gate = jnp.dot(xb, wg_ref[...], preferred_element_type=F32)
        up = jnp.dot(xb, wu_ref[...], preferred_element_type=F32)
        acc_ref[...] += _bdot(_silu(gate) * up, wd_ref[...])

    @pl.when(f == pl.num_programs(1) - 1)
    def _():
        @pl.when(used)
        def _():
            y_ref[...] = acc_ref[...]

        @pl.when(jnp.logical_not(used))
        def _():
            y_ref[...] = jnp.zeros_like(y_ref)


def _moe_experts(xs, meta, w_gate, w_up, w_down, tb=MOE_TB, tf=MOE_TF):
    nblk = xs.shape[0] // tb
    dffe = w_gate.shape[2]
    grid_spec = pltpu.PrefetchScalarGridSpec(
        num_scalar_prefetch=1,
        grid=(nblk, dffe // tf),
        in_specs=[pl.BlockSpec((tb, D_MODEL), lambda b, f, m: (b, 0)),
                  pl.BlockSpec((None, D_MODEL, tf), lambda b, f, m: (m[b], 0, f)),
                  pl.BlockSpec((None, D_MODEL, tf), lambda b, f, m: (m[b], 0, f)),
                  pl.BlockSpec((None, tf, D_MODEL), lambda b, f, m: (m[b], f, 0))],
        out_specs=pl.BlockSpec((tb, D_MODEL), lambda b, f, m: (b, 0)),
        scratch_shapes=[pltpu.VMEM((tb, D_MODEL), BF16), pltpu.VMEM((tb, D_MODEL), F32)],
    )
    return pl.pallas_call(
        _moe_kernel,
        name="moe_experts",
        grid_spec=grid_spec,
        out_shape=jax.ShapeDtypeStruct((nblk * tb, D_MODEL), F32),
        compiler_params=_cparams(("parallel", "arbitrary")),
    )(meta, xs, w_gate.astype(BF16), w_up.astype(BF16), w_down.astype(BF16))


def _combine_kernel(pos0_ref, posn_ref, y_hbm, gates_ref, x_ref, g_ref, b_ref, out_ref, ybuf, sem):
    i = pl.program_id(0)
    tm = out_ref.shape[0]
    slot = i % 2

    @pl.when(i == 0)
    def _():
        _gather_rows(pos0_ref, y_hbm, ybuf.at[0], sem.at[0], TOP_K * tm)

    _wait_rows(y_hbm, ybuf.at[slot], sem.at[slot], TOP_K * tm)

    @pl.when(i + 1 < pl.num_programs(0))
    def _():
        _gather_rows(posn_ref, y_hbm, ybuf.at[1 - slot], sem.at[1 - slot], TOP_K * tm)

    gates = gates_ref[...]
    f = gates[:, 0:1] * ybuf[slot, :tm] + gates[:, 1:2] * ybuf[slot, tm:]
    out_ref[...] = _layer_norm(ALPHA * x_ref[...] + f, g_ref[...], b_ref[...])


def _moe_combine(y, pos, gates, x, ln_g, ln_b, tm=CMB_TM):
    n = x.shape[0]
    nt = n // tm
    pos3 = pos.reshape(nt, tm, TOP_K).transpose(0, 2, 1).reshape(nt, 1, TOP_K * tm)
    g, b = ln_g.reshape(1, D_MODEL), ln_b.reshape(1, D_MODEL)
    smem_blk = lambda imap: pl.BlockSpec((None, 1, TOP_K * tm), imap, memory_space=pltpu.SMEM)
    return pl.pallas_call(
        _combine_kernel,
        name="moe_combine_ln",
        grid=(nt,),
        in_specs=[smem_blk(lambda i: (0, 0, 0)),
                  smem_blk(lambda i: (jnp.minimum(i + 1, nt - 1), 0, 0)),
                  pl.BlockSpec(memory_space=pl.ANY),
                  pl.BlockSpec((tm, TOP_K), lambda i: (i, 0)),
                  pl.BlockSpec((tm, D_MODEL), lambda i: (i, 0)),
                  pl.BlockSpec((1, D_MODEL), lambda i: (0, 0)),
                  pl.BlockSpec((1, D_MODEL), lambda i: (0, 0))],
        out_specs=pl.BlockSpec((tm, D_MODEL), lambda i: (i, 0)),
        out_shape=jax.ShapeDtypeStruct((n, D_MODEL), F32),
        scratch_shapes=[pltpu.VMEM((2, TOP_K * tm, D_MODEL), F32), pltpu.SemaphoreType.DMA((2,))],
        compiler_params=_cparams(("arbitrary",), disable_bounds_checks=True),
    )(pos3, pos3, y, gates, x, g, b)


def _moe_routing(logits, tb):
    n = logits.shape[0]
    top_logit, top_idx = lax.top_k(logits, TOP_K)
    gates = jax.nn.softmax(top_logit, axis=-1)
    na = n * TOP_K
    e_flat = top_idx.reshape(-1).astype(jnp.int32)
    onehot = (e_flat[None, :] == jnp.arange(N_EXPERTS, dtype=jnp.int32)[:, None]).astype(jnp.int32)
    running = jnp.cumsum(onehot, axis=1)
    counts = running[:, -1]
    padded = (counts + tb - 1) // tb * tb
    pend = jnp.cumsum(padded)
    pstart = pend - padded
    dest = jnp.sum(onehot * (pstart[:, None] + running - 1), axis=0).reshape(n, TOP_K)
    nblk = -(-na // tb) + N_EXPERTS
    block_expert = jnp.minimum(jnp.searchsorted(pend, jnp.arange(nblk, dtype=jnp.int32) * tb, side='right'),
                               N_EXPERTS - 1).astype(jnp.int32)
    meta = jnp.concatenate([block_expert, (pend[-1:] // tb).astype(jnp.int32)])
    return dest.astype(jnp.int32), gates, meta, nblk


def _moe_ffn(x, w_router, w_gate, w_up, w_down, ln_g, ln_b):
    wr = jnp.pad(w_router, ((0, 0), (0, LANES - N_EXPERTS))).astype(BF16)
    logits = _matmul(x, wr, F32, "moe_router")[:, :N_EXPERTS]
    dest, gates, meta, nblk = _moe_routing(logits, MOE_TB)
    xs = _moe_dispatch(x, dest, nblk * MOE_TB)
    y = _moe_experts(xs, meta, w_gate, w_up, w_down)
    return _moe_combine(y, dest, gates, x, ln_g, ln_b)


def _split_w_in(w):
    a_end = 3 * A_QKV_W
    bz_end = a_end + 3 * B_W + B_W
    w_groups = [jnp.concatenate([w[:, s * A_QKV_W + gi * A_GROUP_W:s * A_QKV_W + (gi + 1) * A_GROUP_W]
                                 for s in range(3)], axis=1).astype(BF16) for gi in range(N_GROUPS)]
    w_b = jnp.concatenate([w[:, a_end:bz_end], w[:, bz_end + 2 * N_HEADS_B:]], axis=1)
    zpad = jnp.zeros((w.shape[0], LANES - N_HEADS_B), w.dtype)
    w_ba = jnp.concatenate([w[:, bz_end:bz_end + N_HEADS_B], zpad,
                            w[:, bz_end + N_HEADS_B:bz_end + 2 * N_HEADS_B], zpad], axis=1)
    return w_groups, w_b.astype(BF16), w_ba.astype(BF16)


def _hybrid_layer(x, batch, seq, rel_bias, w_in, conv_w, a_log, dt_bias, o_norm_w, w_oa, w_ob, w_out, ln_g, ln_b):
    w_groups, w_b, w_ba = _split_w_in(w_in)
    proj_b = _matmul(x, w_b, F32, "in_proj_gdn_gates")
    proj_ba = _matmul(x, w_ba, F32, "in_proj_beta_decay")
    outs, lses = [], []
    for gi, (_, dilation) in enumerate(DSWA_PATTERNS):
        qkv = _in_proj_strided(x, w_groups[gi], dilation, batch, seq)
        o, lse = _dswa_group(qkv, _band_bias(rel_bias, gi, dilation), dilation)
        outs.append(o)
        lses.append(lse)
    yb = _gated_deltanet(proj_b, proj_ba, conv_w, a_log, dt_bias, o_norm_w, batch, seq)
    return _mix_out(outs, lses, yb, proj_b, x, w_oa, w_ob, w_out, ln_g, ln_b)


def kernel(x, rel_bias, w_in, conv_w, a_log, dt_bias, o_norm_w, w_oa, w_ob, w_out, ln1_g, ln1_b,
           ffn_w_gate, ffn_w_up, ffn_w_down, moe_router, moe_w_gate, moe_w_up, moe_w_down, ln2_g, ln2_b):
    batch, seq, d = x.shape
    h = x.reshape(batch * seq, d)
    for layer in range(DEPTH):
        h = _hybrid_layer(h, batch, seq, rel_bias, w_in[layer], conv_w[layer], a_log[layer], dt_bias[layer],
                          o_norm_w[layer], w_oa[layer], w_ob[layer], w_out[layer], ln1_g[layer], ln1_b[layer])
        j = layer // 2
        if layer % 2 == 0:
            h = _dense_ffn(h, ffn_w_gate[j], ffn_w_up[j], ffn_w_down[j], ln2_g[layer], ln2_b[layer])
        else:
            h = _moe_ffn(h, moe_router[j], moe_w_gate[j], moe_w_up[j], moe_w_down[j], ln2_g[layer], ln2_b[layer])
    return h.reshape(batch, seq, d)
```

```python
import functools
import math

import jax
import jax.numpy as jnp
from jax import lax
from jax.experimental import pallas as pl
from jax.experimental.pallas import tpu as pltpu

F32 = jnp.float32
BF16 = jnp.bfloat16

D_MODEL = 1024
DEPTH = 2
DSWA_PATTERNS = ((128, 1), (512, 4), (2048, 16))
N_GROUPS = 3
HEADS_PER_GROUP = 4
HEAD_DIM = 128
A_QKV_W = N_GROUPS * HEADS_PER_GROUP * HEAD_DIM
A_GROUP_W = HEADS_PER_GROUP * HEAD_DIM
NUM_BUCKETS = 32
MAX_DISTANCE = 2048
N_HEADS_B = 8
B_W = N_HEADS_B * HEAD_DIM
CONV_WIDTH = 4
N_EXPERTS = 8
TOP_K = 2
ALPHA = (2 * DEPTH) ** 0.25
LN_EPS = 1e-5
RMS_EPS = 1e-6

LANES = 128
SUBLANES = 8
VMEM_LIMIT = 56 * 1024 * 1024

ATT_BLK = 128
GDN_CHUNK = 128
MM_TM = 512
MM_TN = 3200
PROJ_A_TM = 1024
OUT_TM = 256
FFN_TM = 512
FFN_TF = 1024
MOE_TB = 512
MOE_TF = 1792
CMB_TM = 256
DSP_TM = 512

NEG_BIG = -1e30


def _cparams(sem, vmem=VMEM_LIMIT, **kw):
    return pltpu.CompilerParams(dimension_semantics=sem, vmem_limit_bytes=vmem, **kw)


def _bdot(a, b):
    return jnp.dot(a.astype(BF16), b.astype(BF16), preferred_element_type=F32)


def _bdot_nt(a, b):
    return lax.dot_general(a.astype(BF16), b.astype(BF16), (((1,), (1,)), ((), ())),
                           preferred_element_type=F32)


def _bdot_tn(a, b):
    return lax.dot_general(a.astype(BF16), b.astype(BF16), (((0,), (0,)), ((), ())),
                           preferred_element_type=F32)


def _sigmoid(v):
    return 1.0 / (1.0 + jnp.exp(-v))


def _silu(v):
    return v * _sigmoid(v)


def _layer_norm(v, g, b):
    mu = jnp.mean(v, axis=-1, keepdims=True)
    d = v - mu
    var = jnp.mean(d * d, axis=-1, keepdims=True)
    return d * lax.rsqrt(var + LN_EPS) * g + b


def _mm_kernel(x_ref, w_ref, o_ref):
    o_ref[...] = _bdot(x_ref[...], w_ref[...]).astype(o_ref.dtype)


def _matmul(x, w, out_dtype, name, tm=MM_TM, tn=MM_TN):
    m, k = x.shape
    n = w.shape[1]
    tn = min(tn, n)
    return pl.pallas_call(
        _mm_kernel,
        name=name,
        grid=(n // tn, m // tm),
        in_specs=[pl.BlockSpec((tm, k), lambda j, i: (i, 0)),
                  pl.BlockSpec((k, tn), lambda j, i: (0, j))],
        out_specs=pl.BlockSpec((tm, tn), lambda j, i: (i, j)),
        out_shape=jax.ShapeDtypeStruct((m, n), out_dtype),
        compiler_params=_cparams(("parallel", "parallel")),
    )(x, w)


def _dswa_kernel(q_ref, kc_ref, kp_ref, vc_ref, vp_ref, bias_ref, o_ref, lse_ref, *, tq):
    j = pl.program_id(2)
    nqb = tq // ATT_BLK
    q = q_ref[...]
    kwin = jnp.concatenate([kp_ref[...], kc_ref[...]], axis=0)
    vwin = jnp.concatenate([vp_ref[...], vc_ref[...]], axis=0)
    row = lax.broadcasted_iota(jnp.int32, (ATT_BLK, 2 * ATT_BLK), 0)
    col = lax.broadcasted_iota(jnp.int32, (ATT_BLK, 2 * ATT_BLK), 1)
    delta = row + ATT_BLK - col
    band = (delta >= 0) & (delta <= ATT_BLK)
    band_first = band & ((col >= ATT_BLK) | (j > 0))
    lane = lax.broadcasted_iota(jnp.int32, (ATT_BLK, LANES), 1)
    scale = HEAD_DIM ** -0.5
    for c in range(nqb):
        mask = band_first if c == 0 else band
        lse_tile = jnp.zeros((ATT_BLK, LANES), F32)
        for h in range(HEADS_PER_GROUP):
            hs = slice(h * HEAD_DIM, (h + 1) * HEAD_DIM)
            qh = q[c * ATT_BLK:(c + 1) * ATT_BLK, hs]
            kh = kwin[c * ATT_BLK:(c + 2) * ATT_BLK, hs]
            vh = vwin[c * ATT_BLK:(c + 2) * ATT_BLK, hs]
            s = _bdot_nt(qh, kh) * scale + bias_ref[h]
            s = jnp.where(mask, s, NEG_BIG)
            m = jnp.max(s, axis=-1, keepdims=True)
            p = jnp.exp(s - m)
            l = jnp.sum(p, axis=-1, keepdims=True)
            o = _bdot(p, vh) / l
            o_ref[c * ATT_BLK:(c + 1) * ATT_BLK, hs] = o
            lse_tile = jnp.where(lane == h, m + jnp.log(l), lse_tile)
        lse_ref[c * ATT_BLK:(c + 1) * ATT_BLK, :] = lse_tile


def _in_proj_strided_kernel(x_ref, w_ref, o_ref, *scratch, dilation):
    res = _bdot(x_ref[...], w_ref[...])
    if dilation == 1:
        o_ref[0] = res.astype(o_ref.dtype)
    else:
        res_ref, = scratch
        rows = res.shape[0] // dilation
        for t in range(res.shape[1] // LANES):
            ls = slice(t * LANES, (t + 1) * LANES)
            res_ref[t] = res[:, ls]
            for r in range(dilation):
                o_ref[r, :, ls] = res_ref[t, pl.ds(r, rows, stride=dilation), :].astype(o_ref.dtype)


def _in_proj_strided(x, w, dilation, batch, seq, tm=PROJ_A_TM):
    k = x.shape[1]
    wn = tn = w.shape[1]
    tpb = seq // tm
    scratch = [] if dilation == 1 else [pltpu.VMEM((tn // LANES, tm, LANES), F32)]
    return pl.pallas_call(
        functools.partial(_in_proj_strided_kernel, dilation=dilation),
        name=f"in_proj_attn_d{dilation}",
        grid=(batch * tpb, wn // tn),
        in_specs=[pl.BlockSpec((tm, k), lambda i, j: (i, 0)),
                  pl.BlockSpec((k, tn), lambda i, j: (0, j))],
        out_specs=pl.BlockSpec((None, dilation, tm // dilation, tn), lambda i, j: (i // tpb, 0, i % tpb, j)),
        out_shape=jax.ShapeDtypeStruct((batch, dilation, seq // dilation, wn), BF16),
        scratch_shapes=scratch,
        compiler_params=_cparams(("parallel", "parallel")),
    )(x, w)


def _dswa_group(qkv, bias, dilation):
    batch, _, n, _ = qkv.shape
    tq = min(512, n)
    nqb = tq // ATT_BLK

    def cur(off):
        return pl.BlockSpec((None, None, tq, A_GROUP_W), lambda b, r, j: (b, r, j, off))

    def prev(off):
        return pl.BlockSpec((None, None, ATT_BLK, A_GROUP_W),
                            lambda b, r, j: (b, r, jnp.maximum(j * nqb - 1, 0), off))

    return pl.pallas_call(
        functools.partial(_dswa_kernel, tq=tq),
        name=f"dswa_d{dilation}",
        grid=(batch, dilation, n // tq),
        in_specs=[cur(0), cur(1), prev(1), cur(2), prev(2),
                  pl.BlockSpec((HEADS_PER_GROUP, ATT_BLK, 2 * ATT_BLK), lambda b, r, j: (0, 0, 0))],
        out_specs=[pl.BlockSpec((None, None, tq, A_GROUP_W), lambda b, r, j: (b, r, j, 0)),
                   pl.BlockSpec((None, None, tq, LANES), lambda b, r, j: (b, r, j, 0))],
        out_shape=[jax.ShapeDtypeStruct((batch, dilation, n, A_GROUP_W), F32),
                   jax.ShapeDtypeStruct((batch, dilation, n, LANES), F32)],
        compiler_params=_cparams(("parallel", "parallel", "parallel")),
    )(qkv, qkv, qkv, qkv, qkv, bias)


def _t5_causal_bucket(dist):
    num_exact = NUM_BUCKETS // 2
    d = jnp.maximum(dist, 1).astype(F32)
    large = num_exact + (jnp.log(d / num_exact) / math.log(MAX_DISTANCE / num_exact)
                         * (NUM_BUCKETS - num_exact)).astype(jnp.int32)
    large = jnp.minimum(large, NUM_BUCKETS - 1)
    return jnp.where(dist < num_exact, dist, large)


def _band_bias(rel_bias, gi, dilation):
    qi = jnp.arange(ATT_BLK)[:, None] + ATT_BLK
    kj = jnp.arange(2 * ATT_BLK)[None, :]
    delta = jnp.maximum(qi - kj, 0) * dilation
    table = rel_bias[:, gi * HEADS_PER_GROUP:(gi + 1) * HEADS_PER_GROUP].astype(F32)
    onehot = (_t5_causal_bucket(delta)[..., None] == jnp.arange(NUM_BUCKETS)).astype(F32)
    return jnp.einsum('qkn,nh->hqk', onehot, table, precision=lax.Precision.HIGHEST)


def _dot16(a, b):
    return jnp.dot(a, b, preferred_element_type=F32)


def _unit_lower_inverse(a, row, col):
    base = 16
    heads = range(len(a))
    eye = (row == col).astype(F32)
    same = lambda sz: (row // sz) == (col // sz)
    blk = same(base)
    a_d = [jnp.where(blk, a[h], 0.0) for h in heads]
    t = [eye - a_d[h] for h in heads]
    p = [a_d[h].astype(BF16) for h in heads]
    for _ in range(3):
        p = [_dot16(p[h], p[h]).astype(BF16) for h in heads]
        t = [t[h] + _dot16(t[h].astype(BF16), p[h]) for h in heads]
    sz = 2 * base
    while sz <= GDN_CHUNK:
        off = same(sz) & jnp.logical_not(same(sz // 2))
        tb = [t[h].astype(BF16) for h in heads]
        m = [_dot16(jnp.where(off, a[h], 0.0).astype(BF16), tb[h]).astype(BF16) for h in heads]
        t = [t[h] - _dot16(tb[h], m[h]) for h in heads]
        sz *= 2
    return t


def _gdn_kernel(x_ref, halo_ref, z_ref, ba_ref, cw_ref, alog_ref, dtb_ref, onw_ref, y_ref, state_ref):
    c = pl.program_id(1)
    C = GDN_CHUNK
    heads = range(N_HEADS_B)

    @pl.when(c == 0)
    def _():
        state_ref[...] = jnp.zeros_like(state_ref)

    keep_halo = (c > 0).astype(F32)
    row = lax.broadcasted_iota(jnp.int32, (C, C), 0)
    col = lax.broadcasted_iota(jnp.int32, (C, C), 1)
    incl = row >= col
    strict = row > col

    ba = ba_ref[...]
    beta_all = _sigmoid(ba[:, :LANES])
    a_in = ba[:, LANES:] + dtb_ref[...]
    softplus = jnp.maximum(a_in, 0.0) + jnp.log(1.0 + jnp.exp(-jnp.abs(a_in)))
    g_all = -jnp.exp(alog_ref[...]) * softplus
    ones_l = incl.astype(BF16)
    g1 = g_all.astype(BF16)
    r1 = g_all - g1.astype(F32)
    g2 = r1.astype(BF16)
    g3 = (r1 - g2.astype(F32)).astype(BF16)
    G = (jnp.dot(ones_l, g1, preferred_element_type=F32) + jnp.dot(ones_l, g2, preferred_element_type=F32)
         + jnp.dot(ones_l, g3, preferred_element_type=F32))
    GT = G.T

    def conv_silu(col0):
        cs = slice(col0, col0 + HEAD_DIM)
        xc = x_ref[:, cs]
        xf = jnp.concatenate([halo_ref[:, cs] * keep_halo, xc], axis=0)
        w = cw_ref[:, cs]
        y = w[CONV_WIDTH - 1:CONV_WIDTH] * xc
        for i in range(CONV_WIDTH - 1):
            off = SUBLANES - (CONV_WIDTH - 1) + i
            y = y + w[i:i + 1] * xf[off:off + C]
        return _silu(y)

    def l2norm(t):
        return t * lax.rsqrt(jnp.sum(t * t, axis=-1, keepdims=True) + RMS_EPS)

    q = [l2norm(conv_silu(h * HEAD_DIM)) * (HEAD_DIM ** -0.5) for h in heads]
    k = [l2norm(conv_silu(B_W + h * HEAD_DIM)) for h in heads]
    v = [conv_silu(2 * B_W + h * HEAD_DIM) for h in heads]
    beta = [beta_all[:, h:h + 1] for h in heads]
    gc = [G[:, h:h + 1] for h in heads]
    g_last = [G[C - 1:C, h:h + 1] for h in heads]
    decay = [jnp.exp(jnp.where(incl, gc[h] - GT[h:h + 1, :], NEG_BIG)) for h in heads]
    eg = [jnp.exp(gc[h]) for h in heads]
    kb = [k[h].astype(BF16) for h in heads]
    kq = [lax.dot_general(jnp.concatenate([kb[h], q[h].astype(BF16)], axis=0), kb[h],
                          (((1,), (1,)), ((), ())), preferred_element_type=F32) for h in heads]
    a = [jnp.where(strict, beta[h] * kq[h][:C] * decay[h], 0.0) for h in heads]
    t_inv = _unit_lower_inverse(a, row, col)
    rhs = [jnp.concatenate([beta[h] * v[h], (beta[h] * eg[h]) * k[h]], axis=1).astype(BF16) for h in heads]
    sol = [_dot16(t_inv[h].astype(BF16), rhs[h]) for h in heads]
    qk = [(kq[h][C:] * decay[h]).astype(BF16) for h in heads]
    wq = [jnp.concatenate([sol[h][:, HEAD_DIM:], q[h] * eg[h]], axis=0).astype(BF16) for h in heads]
    k_dec = [(k[h] * jnp.exp(g_last[h] - gc[h])).astype(BF16) for h in heads]

    state = [state_ref[h] for h in heads]
    ws = [_dot16(wq[h], state[h].astype(BF16)) for h in heads]
    u = [(sol[h][:, :HEAD_DIM] - ws[h][:C]).astype(BF16) for h in heads]
    o = [ws[h][C:] + _dot16(qk[h], u[h]) for h in heads]
    for h in heads:
        state_ref[h] = jnp.exp(g_last[h]) * state[h] + lax.dot_general(
            k_dec[h], u[h], (((0,), (0,)), ((), ())), preferred_element_type=F32)
    for h in heads:
        oh = o[h] * lax.rsqrt(jnp.mean(o[h] * o[h], axis=-1, keepdims=True) + RMS_EPS) * onw_ref[...]
        hs = slice(h * HEAD_DIM, (h + 1) * HEAD_DIM)
        y_ref[:, hs] = (oh * _silu(z_ref[:, hs])).astype(y_ref.dtype)


def _gated_deltanet(proj_b, conv_w, a_log, dt_bias, o_norm_w, batch, seq):
    C = GDN_CHUNK
    wb = proj_b.shape[1]
    pb = proj_b.reshape(batch, seq, wb)
    ba_blk = (wb - 2 * LANES) // (2 * LANES)
    pad = lambda t: jnp.pad(t.astype(F32), (0, LANES - t.shape[0])).reshape(1, LANES)
    const = lambda shape: pl.BlockSpec(shape, lambda b, c: (0,) * len(shape))
    y = pl.pallas_call(
        _gdn_kernel,
        name="gated_deltanet",
        grid=(batch, seq // C),
        in_specs=[pl.BlockSpec((None, C, 3 * B_W), lambda b, c: (b, c, 0)),
                  pl.BlockSpec((None, SUBLANES, 3 * B_W),
                               lambda b, c: (b, jnp.maximum(c * (C // SUBLANES) - 1, 0), 0)),
                  pl.BlockSpec((None, C, B_W), lambda b, c: (b, c, 3)),
                  pl.BlockSpec((None, C, 2 * LANES), lambda b, c: (b, c, ba_blk)),
                  const((CONV_WIDTH, 3 * B_W)), const((1, LANES)), const((1, LANES)), const((1, LANES))],
        out_specs=pl.BlockSpec((None, C, B_W), lambda b, c: (b, c, 0)),
        out_shape=jax.ShapeDtypeStruct((batch, seq, B_W), BF16),
        scratch_shapes=[pltpu.VMEM((N_HEADS_B, HEAD_DIM, HEAD_DIM), F32)],
        compiler_params=_cparams(("parallel", "arbitrary")),
    )(pb, pb, pb, pb, conv_w.astype(F32), pad(a_log), pad(dt_bias), o_norm_w.astype(F32).reshape(1, LANES))
    return y.reshape(batch * seq, B_W)


def _mix_out_kernel(o0, o1, o2, l0, l1, l2, yb_ref, ga_ref, gb_ref, x_ref, woa_ref, wob_ref, wout_ref,
                    g_ref, b_ref, out_ref, *scratch):
    def token_order(ref, scr):
        dilation, rows, width = ref.shape
        if dilation == 1:
            return ref[0]
        planes = []
        for t in range(width // LANES):
            for r in range(dilation):
                scr[t, pl.ds(r, rows, stride=dilation), :] = ref[r, :, t * LANES:(t + 1) * LANES]
            planes.append(scr[t])
        return jnp.concatenate(planes, axis=1)

    outs = (token_order(o0, None), token_order(o1, scratch[0]), token_order(o2, scratch[1]))
    lses = (token_order(l0, None), token_order(l1, scratch[2]), token_order(l2, scratch[3]))
    ya = []
    for h in range(HEADS_PER_GROUP):
        ls = [t[:, h:h + 1] for t in lses]
        m = jnp.maximum(jnp.maximum(ls[0], ls[1]), ls[2])
        es = [jnp.exp(t - m) for t in ls]
        inv = 1.0 / (es[0] + es[1] + es[2])
        hs = slice(h * HEAD_DIM, (h + 1) * HEAD_DIM)
        ya.append((es[0] * inv) * outs[0][:, hs] + (es[1] * inv) * outs[1][:, hs] + (es[2] * inv) * outs[2][:, hs])
    ya = jnp.concatenate(ya, axis=1)
    pa = _bdot(ya, woa_ref[...])
    pb = _bdot(yb_ref[...], wob_ref[...])
    merged = _sigmoid(ga_ref[...]) * pa + _sigmoid(gb_ref[...]) * pb
    mix = _bdot(merged, wout_ref[...])
    out_ref[...] = _layer_norm(ALPHA * x_ref[...] + mix, g_ref[...], b_ref[...])


def _mix_out(outs, lses, yb, proj_b, x, w_oa, w_ob, w_out, ln_g, ln_b, tm=OUT_TM):
    n = x.shape[0]
    seq = outs[0].shape[1] * outs[0].shape[2]
    tpb = seq // tm
    rowblk = lambda w, cb=0: pl.BlockSpec((tm, w), lambda i: (i, cb))
    const = lambda a: pl.BlockSpec(a.shape, lambda i: (0, 0))

    def grouped(a):
        d, w = a.shape[1], a.shape[3]
        return pl.BlockSpec((None, d, tm // d, w), lambda i: (i // tpb, 0, i % tpb, 0))

    wa, wb, wo = w_oa.astype(BF16), w_ob.astype(BF16), w_out.astype(BF16)
    g, b = ln_g.reshape(1, D_MODEL), ln_b.reshape(1, D_MODEL)
    return pl.pallas_call(
        _mix_out_kernel,
        name="mix_out_ln",
        grid=(n // tm,),
        in_specs=[grouped(a) for a in (*outs, *lses)]
        + [rowblk(B_W), rowblk(D_MODEL, 4), rowblk(D_MODEL, 5), rowblk(D_MODEL),
           const(wa), const(wb), const(wo), const(g), const(b)],
        out_specs=rowblk(D_MODEL),
        out_shape=jax.ShapeDtypeStruct((n, D_MODEL), F32),
        scratch_shapes=[pltpu.VMEM((A_GROUP_W // LANES, tm, LANES), F32)] * 2 + [pltpu.VMEM((1, tm, LANES), F32)] * 2,
        compiler_params=_cparams(("parallel",)),
    )(*outs, *lses, yb, proj_b, proj_b, x, wa, wb, wo, g, b)


def _ffn_kernel(x_ref, wg_ref, wu_ref, wd_ref, g_ref, b_ref, out_ref):
    x = x_ref[...]
    xb = x.astype(BF16)
    dff = wg_ref.shape[1]
    acc = None
    for c0 in range(0, dff, FFN_TF):
        c1 = min(c0 + FFN_TF, dff)
        gate = jnp.dot(xb, wg_ref[:, c0:c1], preferred_element_type=F32)
        up = jnp.dot(xb, wu_ref[:, c0:c1], preferred_element_type=F32)
        part = _bdot(_silu(gate) * up, wd_ref[c0:c1, :])
        acc = part if acc is None else acc + part
    out_ref[...] = _layer_norm(ALPHA * x + acc, g_ref[...], b_ref[...])


def _dense_ffn(x, w_gate, w_up, w_down, ln_g, ln_b, tm=FFN_TM):
    n = x.shape[0]
    g, b = ln_g.reshape(1, D_MODEL), ln_b.reshape(1, D_MODEL)
    resident = lambda a: pl.BlockSpec(a.shape, lambda i: (0, 0), pipeline_mode=pl.Buffered(1))
    wg, wu, wd = w_gate.astype(BF16), w_up.astype(BF16), w_down.astype(BF16)
    return pl.pallas_call(
        _ffn_kernel,
        name="dense_ffn_ln",
        grid=(n // tm,),
        in_specs=[pl.BlockSpec((tm, D_MODEL), lambda i: (i, 0)),
                  resident(wg), resident(wu), resident(wd), resident(g), resident(b)],
        out_specs=pl.BlockSpec((tm, D_MODEL), lambda i: (i, 0)),
        out_shape=jax.ShapeDtypeStruct((n, D_MODEL), F32),
        compiler_params=_cparams(("parallel",)),
    )(x, wg, wu, wd, g, b)


def _gather_rows(idx_ref, src_hbm, dst, sem, count):
    def body(i, carry):
        pltpu.make_async_copy(src_hbm.at[pl.ds(idx_ref[0, i], 1)], dst.at[pl.ds(i, 1)], sem).start()
        return carry
    lax.fori_loop(0, count, body, 0, unroll=8)


def _wait_rows(src_hbm, dst, sem, count):
    pltpu.make_async_copy(src_hbm.at[pl.ds(0, count)], dst, sem).wait()


def _dispatch_kernel(dest_ref, x_ref, xs_in, xs_out, sem):
    del xs_in
    tm = x_ref.shape[0]

    def body(i, carry):
        for k in range(TOP_K):
            pltpu.make_async_copy(x_ref.at[pl.ds(i, 1)], xs_out.at[pl.ds(dest_ref[0, TOP_K * i + k], 1)],
                                  sem.at[0]).start()
        return carry
    lax.fori_loop(0, tm, body, 0, unroll=4)
    for _ in range(TOP_K):
        pltpu.make_async_copy(x_ref, xs_out.at[pl.ds(0, tm)], sem.at[0]).wait()


def _moe_dispatch(x, dest, n_rows, tm=DSP_TM):
    n = x.shape[0]
    dest3 = dest.reshape(n // tm, 1, TOP_K * tm)
    return pl.pallas_call(
        _dispatch_kernel,
        name="moe_dispatch",
        grid=(n // tm,),
        in_specs=[pl.BlockSpec((None, 1, TOP_K * tm), lambda i: (i, 0, 0), memory_space=pltpu.SMEM),
                  pl.BlockSpec((tm, D_MODEL), lambda i: (i, 0)),
                  pl.BlockSpec(memory_space=pl.ANY)],
        out_specs=pl.BlockSpec(memory_space=pl.ANY),
        out_shape=jax.ShapeDtypeStruct((n_rows, D_MODEL), F32),
        scratch_shapes=[pltpu.SemaphoreType.DMA((1,))],
        input_output_aliases={2: 0},
        compiler_params=_cparams(("arbitrary",), disable_bounds_checks=True),
    )(dest3, x, jnp.zeros((n_rows, D_MODEL), F32))


def _moe_kernel(meta_ref, x_ref, wg_ref, wu_ref, wd_ref, y_ref, xb_ref, acc_ref):
    b = pl.program_id(0)
    f = pl.program_id(1)
    nblk = pl.num_programs(0)
    used = b < meta_ref[nblk]

    @pl.when(used)
    def _():
        @pl.when(f == 0)
        def _():
            xb_ref[...] = x_ref[...].astype(BF16)
            acc_ref[...] = jnp.zeros_like(acc_ref)

        xb = xb_ref[...]
        gate = jnp.dot(xb, wg_ref[...], preferred_element_type=F32)
        up = jnp.dot(xb, wu_ref[...], preferred_element_type=F32)
        acc_ref[...] += _bdot(_silu(gate) * up, wd_ref[...])

    @pl.when(f == pl.num_programs(1) - 1)
    def _():
        @pl.when(used)
        def _():
            y_ref[...] = acc_ref[...]

        @pl.when(jnp.logical_not(used))
        def _():
            y_ref[...] = jnp.zeros_like(y_ref)


def _moe_experts(xs, meta, w_gate, w_up, w_down, tb=MOE_TB, tf=MOE_TF):
    nblk = xs.shape[0] // tb
    dffe = w_gate.shape[2]
    grid_spec = pltpu.PrefetchScalarGridSpec(
        num_scalar_prefetch=1,
        grid=(nblk, dffe // tf),
        in_specs=[pl.BlockSpec((tb, D_MODEL), lambda b, f, m: (b, 0)),
                  pl.BlockSpec((None, D_MODEL, tf), lambda b, f, m: (m[b], 0, f)),
                  pl.BlockSpec((None, D_MODEL, tf), lambda b, f, m: (m[b], 0, f)),
                  pl.BlockSpec((None, tf, D_MODEL), lambda b, f, m: (m[b], f, 0))],
        out_specs=pl.BlockSpec((tb, D_MODEL), lambda b, f, m: (b, 0)),
        scratch_shapes=[pltpu.VMEM((tb, D_MODEL), BF16), pltpu.VMEM((tb, D_MODEL), F32)],
    )
    return pl.pallas_call(
        _moe_kernel,
        name="moe_experts",
        grid_spec=grid_spec,
        out_shape=jax.ShapeDtypeStruct((nblk * tb, D_MODEL), F32),
        compiler_params=_cparams(("parallel", "arbitrary")),
    )(meta, xs, w_gate.astype(BF16), w_up.astype(BF16), w_down.astype(BF16))


def _combine_kernel(pos0_ref, posn_ref, y_hbm, gates_ref, x_ref, g_ref, b_ref, out_ref, ybuf, sem):
    i = pl.program_id(0)
    tm = out_ref.shape[0]
    slot = i % 2

    @pl.when(i == 0)
    def _():
        _gather_rows(pos0_ref, y_hbm, ybuf.at[0], sem.at[0], TOP_K * tm)

    _wait_rows(y_hbm, ybuf.at[slot], sem.at[slot], TOP_K * tm)

    @pl.when(i + 1 < pl.num_programs(0))
    def _():
        _gather_rows(posn_ref, y_hbm, ybuf.at[1 - slot], sem.at[1 - slot], TOP_K * tm)

    gates = gates_ref[...]
    f = gates[:, 0:1] * ybuf[slot, :tm] + gates[:, 1:2] * ybuf[slot, tm:]
    out_ref[...] = _layer_norm(ALPHA * x_ref[...] + f, g_ref[...], b_ref[...])


def _moe_combine(y, pos, gates, x, ln_g, ln_b, tm=CMB_TM):
    n = x.shape[0]
    nt = n // tm
    pos3 = pos.reshape(nt, tm, TOP_K).transpose(0, 2, 1).reshape(nt, 1, TOP_K * tm)
    g, b = ln_g.reshape(1, D_MODEL), ln_b.reshape(1, D_MODEL)
    smem_blk = lambda imap: pl.BlockSpec((None, 1, TOP_K * tm), imap, memory_space=pltpu.SMEM)
    return pl.pallas_call(
        _combine_kernel,
        name="moe_combine_ln",
        grid=(nt,),
        in_specs=[smem_blk(lambda i: (0, 0, 0)),
                  smem_blk(lambda i: (jnp.minimum(i + 1, nt - 1), 0, 0)),
                  pl.BlockSpec(memory_space=pl.ANY),
                  pl.BlockSpec((tm, TOP_K), lambda i: (i, 0)),
                  pl.BlockSpec((tm, D_MODEL), lambda i: (i, 0)),
                  pl.BlockSpec((1, D_MODEL), lambda i: (0, 0)),
                  pl.BlockSpec((1, D_MODEL), lambda i: (0, 0))],
        out_specs=pl.BlockSpec((tm, D_MODEL), lambda i: (i, 0)),
        out_shape=jax.ShapeDtypeStruct((n, D_MODEL), F32),
        scratch_shapes=[pltpu.VMEM((2, TOP_K * tm, D_MODEL), F32), pltpu.SemaphoreType.DMA((2,))],
        compiler_params=_cparams(("arbitrary",), disable_bounds_checks=True),
    )(pos3, pos3, y, gates, x, g, b)


def _moe_routing(logits, tb):
    n = logits.shape[0]
    top_logit, top_idx = lax.top_k(logits, TOP_K)
    gates = jax.nn.softmax(top_logit, axis=-1)
    na = n * TOP_K
    e_flat = top_idx.reshape(-1).astype(jnp.int32)
    onehot = (e_flat[None, :] == jnp.arange(N_EXPERTS, dtype=jnp.int32)[:, None]).astype(jnp.int32)
    running = jnp.cumsum(onehot, axis=1)
    counts = running[:, -1]
    padded = (counts + tb - 1) // tb * tb
    pend = jnp.cumsum(padded)
    pstart = pend - padded
    dest = jnp.sum(onehot * (pstart[:, None] + running - 1), axis=0).reshape(n, TOP_K)
    nblk = -(-na // tb) + N_EXPERTS
    block_expert = jnp.minimum(jnp.searchsorted(pend, jnp.arange(nblk, dtype=jnp.int32) * tb, side='right'),
                               N_EXPERTS - 1).astype(jnp.int32)
    meta = jnp.concatenate([block_expert, (pend[-1:] // tb).astype(jnp.int32)])
    return dest.astype(jnp.int32), gates, meta, nblk


def _moe_ffn(x, w_router, w_gate, w_up, w_down, ln_g, ln_b):
    wr = jnp.pad(w_router, ((0, 0), (0, LANES - N_EXPERTS))).astype(BF16)
    logits = _matmul(x, wr, F32, "moe_router")[:, :N_EXPERTS]
    dest, gates, meta, nblk = _moe_routing(logits, MOE_TB)
    xs = _moe_dispatch(x, dest, nblk * MOE_TB)
    y = _moe_experts(xs, meta, w_gate, w_up, w_down)
    return _moe_combine(y, dest, gates, x, ln_g, ln_b)


def _split_w_in(w):
    a_end = 3 * A_QKV_W
    bz_end = a_end + 3 * B_W + B_W
    w_groups = [jnp.concatenate([w[:, s * A_QKV_W + gi * A_GROUP_W:s * A_QKV_W + (gi + 1) * A_GROUP_W]
                                 for s in range(3)], axis=1).astype(BF16) for gi in range(N_GROUPS)]
    zpad = jnp.zeros((w.shape[0], LANES - N_HEADS_B), w.dtype)
    w_b = jnp.concatenate([w[:, a_end:bz_end], w[:, bz_end + 2 * N_HEADS_B:],
                           w[:, bz_end:bz_end + N_HEADS_B], zpad,
                           w[:, bz_end + N_HEADS_B:bz_end + 2 * N_HEADS_B], zpad], axis=1)
    return w_groups, w_b.astype(BF16)


def _hybrid_layer(x, batch, seq, rel_bias, w_in, conv_w, a_log, dt_bias, o_norm_w, w_oa, w_ob, w_out, ln_g, ln_b):
    w_groups, w_b = _split_w_in(w_in)
    proj_b = _matmul(x, w_b, F32, "in_proj_gdn_gates")
    outs, lses = [], []
    for gi, (_, dilation) in enumerate(DSWA_PATTERNS):
        qkv = _in_proj_strided(x, w_groups[gi], dilation, batch, seq)
        o, lse = _dswa_group(qkv, _band_bias(rel_bias, gi, dilation), dilation)
        outs.append(o)
        lses.append(lse)
    yb = _gated_deltanet(proj_b, conv_w, a_log, dt_bias, o_norm_w, batch, seq)
    return _mix_out(outs, lses, yb, proj_b, x, w_oa, w_ob, w_out, ln_g, ln_b)


def kernel(x, rel_bias, w_in, conv_w, a_log, dt_bias, o_norm_w, w_oa, w_ob, w_out, ln1_g, ln1_b,
           ffn_w_gate, ffn_w_up, ffn_w_down, moe_router, moe_w_gate, moe_w_up, moe_w_down, ln2_g, ln2_b):
    batch, seq, d = x.shape
    h = x.reshape(batch * seq, d)
    for layer in range(DEPTH):
        h = _hybrid_layer(h, batch, seq, rel_bias, w_in[layer], conv_w[layer], a_log[layer], dt_bias[layer],
                          o_norm_w[layer], w_oa[layer], w_ob[layer], w_out[layer], ln1_g[layer], ln1_b[layer])
        j = layer // 2
        if layer % 2 == 0:
            h = _dense_ffn(h, ffn_w_gate[j], ffn_w_up[j], ffn_w_down[j], ln2_g[layer], ln2_b[layer])
        else:
            h = _moe_ffn(h, moe_router[j], moe_w_gate[j], moe_w_up[j], moe_w_down[j], ln2_g[layer], ln2_b[layer])
    return h.reshape(batch, seq, d)
```

```python
import functools
import math

import jax
import jax.numpy as jnp
from jax import lax
from jax.experimental import pallas as pl
from jax.experimental.pallas import tpu as pltpu

F32 = jnp.float32
BF16 = jnp.bfloat16

D_MODEL = 1024
DEPTH = 2
DSWA_PATTERNS = ((128, 1), (512, 4), (2048, 16))
N_GROUPS = 3
HEADS_PER_GROUP = 4
HEAD_DIM = 128
A_QKV_W = N_GROUPS * HEADS_PER_GROUP * HEAD_DIM
A_GROUP_W = HEADS_PER_GROUP * HEAD_DIM
NUM_BUCKETS = 32
MAX_DISTANCE = 2048
N_HEADS_B = 8
B_W = N_HEADS_B * HEAD_DIM
CONV_WIDTH = 4
N_EXPERTS = 8
TOP_K = 2
ALPHA = (2 * DEPTH) ** 0.25
LN_EPS = 1e-5
RMS_EPS = 1e-6

LANES = 128
SUBLANES = 8
VMEM_LIMIT = 56 * 1024 * 1024

ATT_BLK = 128
GDN_CHUNK = 128
GDN_STEP_CHUNKS = 2
MM_TM = 512
MM_TN = 3200
PROJ_A_TM = 1024
OUT_TM = 512
FFN_TM = 512
FFN_TF = 1024
MOE_TB = 512
MOE_TF = 1792
CMB_TM = 256
DSP_TM = 512

NEG_BIG = -1e30


def _cparams(sem, vmem=VMEM_LIMIT, **kw):
    return pltpu.CompilerParams(dimension_semantics=sem, vmem_limit_bytes=vmem, **kw)


def _bdot(a, b):
    return jnp.dot(a.astype(BF16), b.astype(BF16), preferred_element_type=F32)


def _bdot_nt(a, b):
    return lax.dot_general(a.astype(BF16), b.astype(BF16), (((1,), (1,)), ((), ())),
                           preferred_element_type=F32)


def _bdot_tn(a, b):
    return lax.dot_general(a.astype(BF16), b.astype(BF16), (((0,), (0,)), ((), ())),
                           preferred_element_type=F32)


def _sigmoid(v):
    return 1.0 / (1.0 + jnp.exp(-v))


def _silu(v):
    return v * _sigmoid(v)


def _layer_norm(v, g, b):
    mu = jnp.mean(v, axis=-1, keepdims=True)
    d = v - mu
    var = jnp.mean(d * d, axis=-1, keepdims=True)
    return d * lax.rsqrt(var + LN_EPS) * g + b


def _mm_kernel(x_ref, w_ref, o_ref):
    o_ref[...] = _bdot(x_ref[...], w_ref[...]).astype(o_ref.dtype)


def _matmul(x, w, out_dtype, name, tm=MM_TM, tn=MM_TN):
    m, k = x.shape
    n = w.shape[1]
    tn = min(tn, n)
    return pl.pallas_call(
        _mm_kernel,
        name=name,
        grid=(n // tn, m // tm),
        in_specs=[pl.BlockSpec((tm, k), lambda j, i: (i, 0)),
                  pl.BlockSpec((k, tn), lambda j, i: (0, j))],
        out_specs=pl.BlockSpec((tm, tn), lambda j, i: (i, j)),
        out_shape=jax.ShapeDtypeStruct((m, n), out_dtype),
        compiler_params=_cparams(("parallel", "parallel")),
    )(x, w)


def _dswa_kernel(q_ref, kc_ref, kp_ref, vc_ref, vp_ref, bias_ref, o_ref, lse_ref, *, tq):
    j = pl.program_id(2)
    nqb = tq // ATT_BLK
    q = q_ref[...]
    kwin = jnp.concatenate([kp_ref[...], kc_ref[...]], axis=0)
    vwin = jnp.concatenate([vp_ref[...], vc_ref[...]], axis=0)
    row = lax.broadcasted_iota(jnp.int32, (ATT_BLK, 2 * ATT_BLK), 0)
    col = lax.broadcasted_iota(jnp.int32, (ATT_BLK, 2 * ATT_BLK), 1)
    delta = row + ATT_BLK - col
    band = (delta >= 0) & (delta <= ATT_BLK)
    band_first = band & ((col >= ATT_BLK) | (j > 0))
    lane = lax.broadcasted_iota(jnp.int32, (ATT_BLK, LANES), 1)
    scale = HEAD_DIM ** -0.5
    for c in range(nqb):
        mask = band_first if c == 0 else band
        lse_tile = jnp.zeros((ATT_BLK, LANES), F32)
        for h in range(HEADS_PER_GROUP):
            hs = slice(h * HEAD_DIM, (h + 1) * HEAD_DIM)
            qh = q[c * ATT_BLK:(c + 1) * ATT_BLK, hs]
            kh = kwin[c * ATT_BLK:(c + 2) * ATT_BLK, hs]
            vh = vwin[c * ATT_BLK:(c + 2) * ATT_BLK, hs]
            s = _bdot_nt(qh, kh) * scale + bias_ref[h]
            s = jnp.where(mask, s, NEG_BIG)
            m = jnp.max(s, axis=-1, keepdims=True)
            p = jnp.exp(s - m)
            l = jnp.sum(p, axis=-1, keepdims=True)
            o = _bdot(p, vh) / l
            o_ref[c * ATT_BLK:(c + 1) * ATT_BLK, hs] = o
            lse_tile = jnp.where(lane == h, m + jnp.log(l), lse_tile)
        lse_ref[c * ATT_BLK:(c + 1) * ATT_BLK, :] = lse_tile


def _in_proj_strided_kernel(x_ref, w_ref, o_ref, *scratch, dilation):
    res = _bdot(x_ref[...], w_ref[...])
    if dilation == 1:
        o_ref[0] = res.astype(o_ref.dtype)
    else:
        res_ref, = scratch
        rows = res.shape[0] // dilation
        for t in range(res.shape[1] // LANES):
            ls = slice(t * LANES, (t + 1) * LANES)
            res_ref[t] = res[:, ls]
            for r in range(dilation):
                o_ref[r, :, ls] = res_ref[t, pl.ds(r, rows, stride=dilation), :].astype(o_ref.dtype)


def _in_proj_strided(x, w, dilation, batch, seq, tm=PROJ_A_TM):
    k = x.shape[1]
    wn = tn = w.shape[1]
    tpb = seq // tm
    scratch = [] if dilation == 1 else [pltpu.VMEM((tn // LANES, tm, LANES), F32)]
    return pl.pallas_call(
        functools.partial(_in_proj_strided_kernel, dilation=dilation),
        name=f"in_proj_attn_d{dilation}",
        grid=(batch * tpb, wn // tn),
        in_specs=[pl.BlockSpec((tm, k), lambda i, j: (i, 0)),
                  pl.BlockSpec((k, tn), lambda i, j: (0, j))],
        out_specs=pl.BlockSpec((None, dilation, tm // dilation, tn), lambda i, j: (i // tpb, 0, i % tpb, j)),
        out_shape=jax.ShapeDtypeStruct((batch, dilation, seq // dilation, wn), BF16),
        scratch_shapes=scratch,
        compiler_params=_cparams(("parallel", "parallel")),
    )(x, w)


def _dswa_group(qkv, bias, dilation):
    batch, _, n, _ = qkv.shape
    tq = min(512, n)
    nqb = tq // ATT_BLK

    def cur(off):
        return pl.BlockSpec((None, None, tq, A_GROUP_W), lambda b, r, j: (b, r, j, off))

    def prev(off):
        return pl.BlockSpec((None, None, ATT_BLK, A_GROUP_W),
                            lambda b, r, j: (b, r, jnp.maximum(j * nqb - 1, 0), off))

    return pl.pallas_call(
        functools.partial(_dswa_kernel, tq=tq),
        name=f"dswa_d{dilation}",
        grid=(batch, dilation, n // tq),
        in_specs=[cur(0), cur(1), prev(1), cur(2), prev(2),
                  pl.BlockSpec((HEADS_PER_GROUP, ATT_BLK, 2 * ATT_BLK), lambda b, r, j: (0, 0, 0))],
        out_specs=[pl.BlockSpec((None, None, tq, A_GROUP_W), lambda b, r, j: (b, r, j, 0)),
                   pl.BlockSpec((None, None, tq, LANES), lambda b, r, j: (b, r, j, 0))],
        out_shape=[jax.ShapeDtypeStruct((batch, dilation, n, A_GROUP_W), F32),
                   jax.ShapeDtypeStruct((batch, dilation, n, LANES), F32)],
        compiler_params=_cparams(("parallel", "parallel", "parallel")),
    )(qkv, qkv, qkv, qkv, qkv, bias)


def _t5_causal_bucket(dist):
    num_exact = NUM_BUCKETS // 2
    d = jnp.maximum(dist, 1).astype(F32)
    large = num_exact + (jnp.log(d / num_exact) / math.log(MAX_DISTANCE / num_exact)
                         * (NUM_BUCKETS - num_exact)).astype(jnp.int32)
    large = jnp.minimum(large, NUM_BUCKETS - 1)
    return jnp.where(dist < num_exact, dist, large)


def _band_bias(rel_bias, gi, dilation):
    qi = jnp.arange(ATT_BLK)[:, None] + ATT_BLK
    kj = jnp.arange(2 * ATT_BLK)[None, :]
    delta = jnp.maximum(qi - kj, 0) * dilation
    table = rel_bias[:, gi * HEADS_PER_GROUP:(gi + 1) * HEADS_PER_GROUP].astype(F32)
    onehot = (_t5_causal_bucket(delta)[..., None] == jnp.arange(NUM_BUCKETS)).astype(F32)
    return jnp.einsum('qkn,nh->hqk', onehot, table, precision=lax.Precision.HIGHEST)


def _dot16(a, b):
    return jnp.dot(a, b, preferred_element_type=F32)


def _unit_lower_inverse(a, row, col):
    base = 16
    heads = range(len(a))
    eye = (row == col).astype(F32)
    same = lambda sz: (row // sz) == (col // sz)
    blk = same(base)
    a_d = [jnp.where(blk, a[h], 0.0) for h in heads]
    t = [eye - a_d[h] for h in heads]
    p = [a_d[h].astype(BF16) for h in heads]
    for _ in range(3):
        p = [_dot16(p[h], p[h]).astype(BF16) for h in heads]
        t = [t[h] + _dot16(t[h].astype(BF16), p[h]) for h in heads]
    sz = 2 * base
    while sz <= GDN_CHUNK:
        off = same(sz) & jnp.logical_not(same(sz // 2))
        tb = [t[h].astype(BF16) for h in heads]
        m = [_dot16(jnp.where(off, a[h], 0.0).astype(BF16), tb[h]).astype(BF16) for h in heads]
        t = [t[h] - _dot16(tb[h], m[h]) for h in heads]
        sz *= 2
    return t


def _gdn_kernel(x_ref, halo_ref, z_ref, ba_ref, cw_ref, alog_ref, dtb_ref, onw_ref, y_ref, state_ref):
    c = pl.program_id(1)

    @pl.when(c == 0)
    def _():
        state_ref[...] = jnp.zeros_like(state_ref)

    keep_halo = (c > 0).astype(F32)
    for cc in range(x_ref.shape[0] // GDN_CHUNK):
        r0 = cc * GDN_CHUNK
        if cc == 0:
            halo = lambda cs: halo_ref[:, cs] * keep_halo
        else:
            halo = lambda cs, r0=r0: x_ref[r0 - SUBLANES:r0, cs]
        _gdn_chunk(slice(r0, r0 + GDN_CHUNK), halo, x_ref, z_ref, ba_ref, cw_ref, alog_ref, dtb_ref, onw_ref,
                   y_ref, state_ref)


def _gdn_chunk(rows, halo, x_ref, z_ref, ba_ref, cw_ref, alog_ref, dtb_ref, onw_ref, y_ref, state_ref):
    C = GDN_CHUNK
    heads = range(N_HEADS_B)
    row = lax.broadcasted_iota(jnp.int32, (C, C), 0)
    col = lax.broadcasted_iota(jnp.int32, (C, C), 1)
    incl = row >= col
    strict = row > col

    ba = ba_ref[rows, :]
    beta_all = _sigmoid(ba[:, :LANES])
    a_in = ba[:, LANES:] + dtb_ref[...]
    softplus = jnp.maximum(a_in, 0.0) + jnp.log(1.0 + jnp.exp(-jnp.abs(a_in)))
    g_all = -jnp.exp(alog_ref[...]) * softplus
    ones_l = incl.astype(BF16)
    g1 = g_all.astype(BF16)
    r1 = g_all - g1.astype(F32)
    g2 = r1.astype(BF16)
    g3 = (r1 - g2.astype(F32)).astype(BF16)
    G = (jnp.dot(ones_l, g1, preferred_element_type=F32) + jnp.dot(ones_l, g2, preferred_element_type=F32)
         + jnp.dot(ones_l, g3, preferred_element_type=F32))
    GT = G.T

    def conv_silu(col0):
        cs = slice(col0, col0 + HEAD_DIM)
        xc = x_ref[rows, cs]
        xf = jnp.concatenate([halo(cs), xc], axis=0)
        w = cw_ref[:, cs]
        y = w[CONV_WIDTH - 1:CONV_WIDTH] * xc
        for i in range(CONV_WIDTH - 1):
            off = SUBLANES - (CONV_WIDTH - 1) + i
            y = y + w[i:i + 1] * xf[off:off + C]
        return _silu(y)

    def l2norm(t):
        return t * lax.rsqrt(jnp.sum(t * t, axis=-1, keepdims=True) + RMS_EPS)

    q = [l2norm(conv_silu(h * HEAD_DIM)) * (HEAD_DIM ** -0.5) for h in heads]
    k = [l2norm(conv_silu(B_W + h * HEAD_DIM)) for h in heads]
    v = [conv_silu(2 * B_W + h * HEAD_DIM) for h in heads]
    beta = [beta_all[:, h:h + 1] for h in heads]
    gc = [G[:, h:h + 1] for h in heads]
    g_last = [G[C - 1:C, h:h + 1] for h in heads]
    decay = [jnp.exp(jnp.where(incl, gc[h] - GT[h:h + 1, :], NEG_BIG)) for h in heads]
    eg = [jnp.exp(gc[h]) for h in heads]
    kb = [k[h].astype(BF16) for h in heads]
    kq = [lax.dot_general(jnp.concatenate([kb[h], q[h].astype(BF16)], axis=0), kb[h],
                          (((1,), (1,)), ((), ())), preferred_element_type=F32) for h in heads]
    a = [jnp.where(strict, beta[h] * kq[h][:C] * decay[h], 0.0) for h in heads]
    t_inv = _unit_lower_inverse(a, row, col)
    rhs = [jnp.concatenate([beta[h] * v[h], (beta[h] * eg[h]) * k[h]], axis=1).astype(BF16) for h in heads]
    sol = [_dot16(t_inv[h].astype(BF16), rhs[h]) for h in heads]
    qk = [(kq[h][C:] * decay[h]).astype(BF16) for h in heads]
    wq = [jnp.concatenate([sol[h][:, HEAD_DIM:], q[h] * eg[h]], axis=0).astype(BF16) for h in heads]
    k_dec = [(k[h] * jnp.exp(g_last[h] - gc[h])).astype(BF16) for h in heads]

    state = [state_ref[h] for h in heads]
    ws = [_dot16(wq[h], state[h].astype(BF16)) for h in heads]
    u = [(sol[h][:, :HEAD_DIM] - ws[h][:C]).astype(BF16) for h in heads]
    o = [ws[h][C:] + _dot16(qk[h], u[h]) for h in heads]
    for h in heads:
        state_ref[h] = jnp.exp(g_last[h]) * state[h] + lax.dot_general(
            k_dec[h], u[h], (((0,), (0,)), ((), ())), preferred_element_type=F32)
    for h in heads:
        oh = o[h] * lax.rsqrt(jnp.mean(o[h] * o[h], axis=-1, keepdims=True) + RMS_EPS) * onw_ref[...]
        hs = slice(h * HEAD_DIM, (h + 1) * HEAD_DIM)
        y_ref[rows, hs] = (oh * _silu(z_ref[rows, hs])).astype(y_ref.dtype)


def _gated_deltanet(proj_b, conv_w, a_log, dt_bias, o_norm_w, batch, seq):
    C = GDN_STEP_CHUNKS * GDN_CHUNK
    wb = proj_b.shape[1]
    pb = proj_b.reshape(batch, seq, wb)
    ba_blk = (wb - 2 * LANES) // (2 * LANES)
    pad = lambda t: jnp.pad(t.astype(F32), (0, LANES - t.shape[0])).reshape(1, LANES)
    const = lambda shape: pl.BlockSpec(shape, lambda b, c: (0,) * len(shape))
    y = pl.pallas_call(
        _gdn_kernel,
        name="gated_deltanet",
        grid=(batch, seq // C),
        in_specs=[pl.BlockSpec((None, C, 3 * B_W), lambda b, c: (b, c, 0)),
                  pl.BlockSpec((None, SUBLANES, 3 * B_W),
                               lambda b, c: (b, jnp.maximum(c * (C // SUBLANES) - 1, 0), 0)),
                  pl.BlockSpec((None, C, B_W), lambda b, c: (b, c, 3)),
                  pl.BlockSpec((None, C, 2 * LANES), lambda b, c: (b, c, ba_blk)),
                  const((CONV_WIDTH, 3 * B_W)), const((1, LANES)), const((1, LANES)), const((1, LANES))],
        out_specs=pl.BlockSpec((None, C, B_W), lambda b, c: (b, c, 0)),
        out_shape=jax.ShapeDtypeStruct((batch, seq, B_W), BF16),
        scratch_shapes=[pltpu.VMEM((N_HEADS_B, HEAD_DIM, HEAD_DIM), F32)],
        compiler_params=_cparams(("parallel", "arbitrary")),
    )(pb, pb, pb, pb, conv_w.astype(F32), pad(a_log), pad(dt_bias), o_norm_w.astype(F32).reshape(1, LANES))
    return y.reshape(batch * seq, B_W)


def _mix_out_kernel(o0, o1, o2, l0, l1, l2, yb_ref, ga_ref, gb_ref, x_ref, woa_ref, wob_ref, wout_ref,
                    g_ref, b_ref, out_ref, *scratch):
    def token_order(ref, scr):
        dilation, rows, width = ref.shape
        if dilation == 1:
            return ref[0]
        planes = []
        for t in range(width // LANES):
            for r in range(dilation):
                scr[t, pl.ds(r, rows, stride=dilation), :] = ref[r, :, t * LANES:(t + 1) * LANES]
            planes.append(scr[t])
        return jnp.concatenate(planes, axis=1)

    outs = (token_order(o0, None), token_order(o1, scratch[0]), token_order(o2, scratch[1]))
    lses = (token_order(l0, None), token_order(l1, scratch[2]), token_order(l2, scratch[3]))
    ya = []
    for h in range(HEADS_PER_GROUP):
        ls = [t[:, h:h + 1] for t in lses]
        m = jnp.maximum(jnp.maximum(ls[0], ls[1]), ls[2])
        es = [jnp.exp(t - m) for t in ls]
        inv = 1.0 / (es[0] + es[1] + es[2])
        hs = slice(h * HEAD_DIM, (h + 1) * HEAD_DIM)
        ya.append((es[0] * inv) * outs[0][:, hs] + (es[1] * inv) * outs[1][:, hs] + (es[2] * inv) * outs[2][:, hs])
    ya = jnp.concatenate(ya, axis=1)
    pa = _bdot(ya, woa_ref[...])
    pb = _bdot(yb_ref[...], wob_ref[...])
    merged = _sigmoid(ga_ref[...]) * pa + _sigmoid(gb_ref[...]) * pb
    mix = _bdot(merged, wout_ref[...])
    out_ref[...] = _layer_norm(ALPHA * x_ref[...] + mix, g_ref[...], b_ref[...])


def _mix_out(outs, lses, yb, proj_b, x, w_oa, w_ob, w_out, ln_g, ln_b, tm=OUT_TM):
    n = x.shape[0]
    seq = outs[0].shape[1] * outs[0].shape[2]
    tpb = seq // tm
    rowblk = lambda w, cb=0: pl.BlockSpec((tm, w), lambda i: (i, cb))
    const = lambda a: pl.BlockSpec(a.shape, lambda i: (0, 0), pipeline_mode=pl.Buffered(1))

    def grouped(a):
        d, w = a.shape[1], a.shape[3]
        return pl.BlockSpec((None, d, tm // d, w), lambda i: (i // tpb, 0, i % tpb, 0))

    wa, wb, wo = w_oa.astype(BF16), w_ob.astype(BF16), w_out.astype(BF16)
    g, b = ln_g.reshape(1, D_MODEL), ln_b.reshape(1, D_MODEL)
    return pl.pallas_call(
        _mix_out_kernel,
        name="mix_out_ln",
        grid=(n // tm,),
        in_specs=[grouped(a) for a in (*outs, *lses)]
        + [rowblk(B_W), rowblk(D_MODEL, 4), rowblk(D_MODEL, 5), rowblk(D_MODEL),
           const(wa), const(wb), const(wo), const(g), const(b)],
        out_specs=rowblk(D_MODEL),
        out_shape=jax.ShapeDtypeStruct((n, D_MODEL), F32),
        scratch_shapes=[pltpu.VMEM((A_GROUP_W // LANES, tm, LANES), F32)] * 2 + [pltpu.VMEM((1, tm, LANES), F32)] * 2,
        compiler_params=_cparams(("parallel",)),
    )(*outs, *lses, yb, proj_b, proj_b, x, wa, wb, wo, g, b)


def _ffn_kernel(x_ref, wg_ref, wu_ref, wd_ref, g_ref, b_ref, out_ref):
    x = x_ref[...]
    xb = x.astype(BF16)
    dff = wg_ref.shape[1]
    acc = None
    for c0 in range(0, dff, FFN_TF):
        c1 = min(c0 + FFN_TF, dff)
        gate = jnp.dot(xb, wg_ref[:, c0:c1], preferred_element_type=F32)
        up = jnp.dot(xb, wu_ref[:, c0:c1], preferred_element_type=F32)
        part = _bdot(_silu(gate) * up, wd_ref[c0:c1, :])
        acc = part if acc is None else acc + part
    out_ref[...] = _layer_norm(ALPHA * x + acc, g_ref[...], b_ref[...])


def _dense_ffn(x, w_gate, w_up, w_down, ln_g, ln_b, tm=FFN_TM):
    n = x.shape[0]
    g, b = ln_g.reshape(1, D_MODEL), ln_b.reshape(1, D_MODEL)
    resident = lambda a: pl.BlockSpec(a.shape, lambda i: (0, 0), pipeline_mode=pl.Buffered(1))
    wg, wu, wd = w_gate.astype(BF16), w_up.astype(BF16), w_down.astype(BF16)
    return pl.pallas_call(
        _ffn_kernel,
        name="dense_ffn_ln",
        grid=(n // tm,),
        in_specs=[pl.BlockSpec((tm, D_MODEL), lambda i: (i, 0)),
                  resident(wg), resident(wu), resident(wd), resident(g), resident(b)],
        out_specs=pl.BlockSpec((tm, D_MODEL), lambda i: (i, 0)),
        out_shape=jax.ShapeDtypeStruct((n, D_MODEL), F32),
        compiler_params=_cparams(("parallel",)),
    )(x, wg, wu, wd, g, b)


def _gather_rows(idx_ref, src_hbm, dst, sem, count):
    def body(i, carry):
        pltpu.make_async_copy(src_hbm.at[pl.ds(idx_ref[0, i], 1)], dst.at[pl.ds(i, 1)], sem).start()
        return carry
    lax.fori_loop(0, count, body, 0, unroll=8)


def _wait_rows(src_hbm, dst, sem, count):
    pltpu.make_async_copy(src_hbm.at[pl.ds(0, count)], dst, sem).wait()


def _dispatch_kernel(dest_ref, x_hbm, xs_in, xs_out, sem):
    del xs_in
    i = pl.program_id(0)
    tm = dest_ref.shape[1] // TOP_K
    slot = i % 2
    base = i * tm

    def body(r, carry):
        for k in range(TOP_K):
            pltpu.make_async_copy(x_hbm.at[pl.ds(base + r, 1)], xs_out.at[pl.ds(dest_ref[0, TOP_K * r + k], 1)],
                                  sem.at[slot]).start()
        return carry
    lax.fori_loop(0, tm, body, 0, unroll=4)

    def wait_tile(s):
        for _ in range(TOP_K):
            pltpu.make_async_copy(x_hbm.at[pl.ds(0, tm)], xs_out.at[pl.ds(0, tm)], sem.at[s]).wait()

    @pl.when(i > 0)
    def _():
        wait_tile(1 - slot)

    @pl.when(i == pl.num_programs(0) - 1)
    def _():
        wait_tile(slot)


def _moe_dispatch(x, dest, n_rows, tm=DSP_TM):
    n = x.shape[0]
    dest3 = dest.reshape(n // tm, 1, TOP_K * tm)
    return pl.pallas_call(
        _dispatch_kernel,
        name="moe_dispatch",
        grid=(n // tm,),
        in_specs=[pl.BlockSpec((None, 1, TOP_K * tm), lambda i: (i, 0, 0), memory_space=pltpu.SMEM),
                  pl.BlockSpec(memory_space=pl.ANY),
                  pl.BlockSpec(memory_space=pl.ANY)],
        out_specs=pl.BlockSpec(memory_space=pl.ANY),
        out_shape=jax.ShapeDtypeStruct((n_rows, D_MODEL), F32),
        scratch_shapes=[pltpu.SemaphoreType.DMA((2,))],
        input_output_aliases={2: 0},
        compiler_params=_cparams(("arbitrary",), disable_bounds_checks=True),
    )(dest3, x, jnp.zeros((n_rows, D_MODEL), F32))


def _moe_kernel(meta_ref, x_ref, wg_ref, wu_ref, wd_ref, y_ref, xb_ref, acc_ref):
    b = pl.program_id(0)
    f = pl.program_id(1)
    nblk = pl.num_programs(0)
    used = b < meta_ref[nblk]

    @pl.when(used)
    def _():
        @pl.when(f == 0)
        def _():
            xb_ref[...] = x_ref[...].astype(BF16)
            acc_ref[...] = jnp.zeros_like(acc_ref)

        xb = xb_ref[...]
        gate = jnp.dot(xb, wg_ref[...], preferred_element_type=F32)
        up = jnp.dot(xb, wu_ref[...], preferred_element_type=F32)
        acc_ref[...] += _bdot(_silu(gate) * up, wd_ref[...])

    @pl.when(f == pl.num_programs(1) - 1)
    def _():
        @pl.when(used)
        def _():
            y_ref[...] = acc_ref[...]

        @pl.when(jnp.logical_not(used))
        def _():
            y_ref[...] = jnp.zeros_like(y_ref)


def _moe_experts(xs, meta, w_gate, w_up, w_down, tb=MOE_TB, tf=MOE_TF):
    nblk = xs.shape[0] // tb
    dffe = w_gate.shape[2]
    grid_spec = pltpu.PrefetchScalarGridSpec(
        num_scalar_prefetch=1,
        grid=(nblk, dffe // tf),
        in_specs=[pl.BlockSpec((tb, D_MODEL), lambda b, f, m: (b, 0)),
                  pl.BlockSpec((None, D_MODEL, tf), lambda b, f, m: (m[b], 0, f)),
                  pl.BlockSpec((None, D_MODEL, tf), lambda b, f, m: (m[b], 0, f)),
                  pl.BlockSpec((None, tf, D_MODEL), lambda b, f, m: (m[b], f, 0))],
        out_specs=pl.BlockSpec((tb, D_MODEL), lambda b, f, m: (b, 0)),
        scratch_shapes=[pltpu.VMEM((tb, D_MODEL), BF16), pltpu.VMEM((tb, D_MODEL), F32)],
    )
    return pl.pallas_call(
        _moe_kernel,
        name="moe_experts",
        grid_spec=grid_spec,
        out_shape=jax.ShapeDtypeStruct((nblk * tb, D_MODEL), F32),
        compiler_params=_cparams(("parallel", "arbitrary")),
    )(meta, xs, w_gate.astype(BF16), w_up.astype(BF16), w_down.astype(BF16))


def _combine_kernel(pos0_ref, posn_ref, y_hbm, gates_ref, x_ref, g_ref, b_ref, out_ref, ybuf, sem):
    i = pl.program_id(0)
    tm = out_ref.shape[0]
    slot = i % 2

    @pl.when(i == 0)
    def _():
        _gather_rows(pos0_ref, y_hbm, ybuf.at[0], sem.at[0], TOP_K * tm)

    _wait_rows(y_hbm, ybuf.at[slot], sem.at[slot], TOP_K * tm)

    @pl.when(i + 1 < pl.num_programs(0))
    def _():
        _gather_rows(posn_ref, y_hbm, ybuf.at[1 - slot], sem.at[1 - slot], TOP_K * tm)

    gates = gates_ref[...]
    f = gates[:, 0:1] * ybuf[slot, :tm] + gates[:, 1:2] * ybuf[slot, tm:]
    out_ref[...] = _layer_norm(ALPHA * x_ref[...] + f, g_ref[...], b_ref[...])


def _moe_combine(y, pos, gates, x, ln_g, ln_b, tm=CMB_TM):
    n = x.shape[0]
    nt = n // tm
    pos3 = pos.reshape(nt, tm, TOP_K).transpose(0, 2, 1).reshape(nt, 1, TOP_K * tm)
    g, b = ln_g.reshape(1, D_MODEL), ln_b.reshape(1, D_MODEL)
    smem_blk = lambda imap: pl.BlockSpec((None, 1, TOP_K * tm), imap, memory_space=pltpu.SMEM)
    return pl.pallas_call(
        _combine_kernel,
        name="moe_combine_ln",
        grid=(nt,),
        in_specs=[smem_blk(lambda i: (0, 0, 0)),
                  smem_blk(lambda i: (jnp.minimum(i + 1, nt - 1), 0, 0)),
                  pl.BlockSpec(memory_space=pl.ANY),
                  pl.BlockSpec((tm, TOP_K), lambda i: (i, 0)),
                  pl.BlockSpec((tm, D_MODEL), lambda i: (i, 0)),
                  pl.BlockSpec((1, D_MODEL), lambda i: (0, 0)),
                  pl.BlockSpec((1, D_MODEL), lambda i: (0, 0))],
        out_specs=pl.BlockSpec((tm, D_MODEL), lambda i: (i, 0)),
        out_shape=jax.ShapeDtypeStruct((n, D_MODEL), F32),
        scratch_shapes=[pltpu.VMEM((2, TOP_K * tm, D_MODEL), F32), pltpu.SemaphoreType.DMA((2,))],
        compiler_params=_cparams(("arbitrary",), disable_bounds_checks=True),
    )(pos3, pos3, y, gates, x, g, b)


def _moe_routing(logits, tb):
    n = logits.shape[0]
    top_logit, top_idx = lax.top_k(logits, TOP_K)
    gates = jax.nn.softmax(top_logit, axis=-1)
    na = n * TOP_K
    e_flat = top_idx.reshape(-1).astype(jnp.int32)
    onehot = (e_flat[None, :] == jnp.arange(N_EXPERTS, dtype=jnp.int32)[:, None]).astype(jnp.int32)
    running = jnp.cumsum(onehot, axis=1)
    counts = running[:, -1]
    padded = (counts + tb - 1) // tb * tb
    pend = jnp.cumsum(padded)
    pstart = pend - padded
    dest = jnp.sum(onehot * (pstart[:, None] + running - 1), axis=0).reshape(n, TOP_K)
    nblk = -(-na // tb) + N_EXPERTS
    block_expert = jnp.minimum(jnp.searchsorted(pend, jnp.arange(nblk, dtype=jnp.int32) * tb, side='right'),
                               N_EXPERTS - 1).astype(jnp.int32)
    meta = jnp.concatenate([block_expert, (pend[-1:] // tb).astype(jnp.int32)])
    return dest.astype(jnp.int32), gates, meta, nblk


def _moe_ffn(x, w_router, w_gate, w_up, w_down, ln_g, ln_b):
    wr = jnp.pad(w_router, ((0, 0), (0, LANES - N_EXPERTS))).astype(BF16)
    logits = _matmul(x, wr, F32, "moe_router")[:, :N_EXPERTS]
    dest, gates, meta, nblk = _moe_routing(logits, MOE_TB)
    xs = _moe_dispatch(x, dest, nblk * MOE_TB)
    y = _moe_experts(xs, meta, w_gate, w_up, w_down)
    return _moe_combine(y, dest, gates, x, ln_g, ln_b)


def _split_w_in(w):
    a_end = 3 * A_QKV_W
    bz_end = a_end + 3 * B_W + B_W
    w_groups = [jnp.concatenate([w[:, s * A_QKV_W + gi * A_GROUP_W:s * A_QKV_W + (gi + 1) * A_GROUP_W]
                                 for s in range(3)], axis=1).astype(BF16) for gi in range(N_GROUPS)]
    zpad = jnp.zeros((w.shape[0], LANES - N_HEADS_B), w.dtype)
    w_b = jnp.concatenate([w[:, a_end:bz_end], w[:, bz_end + 2 * N_HEADS_B:],
                           w[:, bz_end:bz_end + N_HEADS_B], zpad,
                           w[:, bz_end + N_HEADS_B:bz_end + 2 * N_HEADS_B], zpad], axis=1)
    return w_groups, w_b.astype(BF16)


def _hybrid_layer(x, batch, seq, rel_bias, w_in, conv_w, a_log, dt_bias, o_norm_w, w_oa, w_ob, w_out, ln_g, ln_b):
    w_groups, w_b = _split_w_in(w_in)
    proj_b = _matmul(x, w_b, F32, "in_proj_gdn_gates")
    outs, lses = [], []
    for gi, (_, dilation) in enumerate(DSWA_PATTERNS):
        qkv = _in_proj_strided(x, w_groups[gi], dilation, batch, seq)
        o, lse = _dswa_group(qkv, _band_bias(rel_bias, gi, dilation), dilation)
        outs.append(o)
        lses.append(lse)
    yb = _gated_deltanet(proj_b, conv_w, a_log, dt_bias, o_norm_w, batch, seq)
    return _mix_out(outs, lses, yb, proj_b, x, w_oa, w_ob, w_out, ln_g, ln_b)


def kernel(x, rel_bias, w_in, conv_w, a_log, dt_bias, o_norm_w, w_oa, w_ob, w_out, ln1_g, ln1_b,
           ffn_w_gate, ffn_w_up, ffn_w_down, moe_router, moe_w_gate, moe_w_up, moe_w_down, ln2_g, ln2_b):
    batch, seq, d = x.shape
    h = x.reshape(batch * seq, d)
    for layer in range(DEPTH):
        h = _hybrid_layer(h, batch, seq, rel_bias, w_in[layer], conv_w[layer], a_log[layer], dt_bias[layer],
                          o_norm_w[layer], w_oa[layer], w_ob[layer], w_out[layer], ln1_g[layer], ln1_b[layer])
        j = layer // 2
        if layer % 2 == 0:
            h = _dense_ffn(h, ffn_w_gate[j], ffn_w_up[j], ffn_w_down[j], ln2_g[layer], ln2_b[layer])
        else:
            h = _moe_ffn(h, moe_router[j], moe_w_gate[j], moe_w_up[j], moe_w_down[j], ln2_g[layer], ln2_b[layer])
    return h.reshape(batch, seq, d)
```

```python
import functools
import math

import jax
import jax.numpy as jnp
from jax import lax
from jax.experimental import pallas as pl
from jax.experimental.pallas import tpu as pltpu

F32 = jnp.float32
BF16 = jnp.bfloat16

D_MODEL = 1024
DEPTH = 2
DSWA_PATTERNS = ((128, 1), (512, 4), (2048, 16))
N_GROUPS = 3
HEADS_PER_GROUP = 4
HEAD_DIM = 128
A_QKV_W = N_GROUPS * HEADS_PER_GROUP * HEAD_DIM
A_GROUP_W = HEADS_PER_GROUP * HEAD_DIM
NUM_BUCKETS = 32
MAX_DISTANCE = 2048
N_HEADS_B = 8
B_W = N_HEADS_B * HEAD_DIM
CONV_WIDTH = 4
N_EXPERTS = 8
TOP_K = 2
ALPHA = (2 * DEPTH) ** 0.25
LN_EPS = 1e-5
RMS_EPS = 1e-6

LANES = 128
SUBLANES = 8
VMEM_LIMIT = 56 * 1024 * 1024

ATT_BLK = 128
GDN_CHUNK = 128
GDN_STEP_CHUNKS = 2
MM_TM = 512
MM_TN = 3200
PROJ_A_TM = 1024
OUT_TM = 512
FFN_TM = 512
FFN_TF = 1024
MOE_TB = 512
MOE_TF = 1792
CMB_TM = 256
DSP_TM = 512

NEG_BIG = -1e30


def _cparams(sem, vmem=VMEM_LIMIT, **kw):
    return pltpu.CompilerParams(dimension_semantics=sem, vmem_limit_bytes=vmem, **kw)


def _bdot(a, b):
    return jnp.dot(a.astype(BF16), b.astype(BF16), preferred_element_type=F32)


def _bdot_nt(a, b):
    return lax.dot_general(a.astype(BF16), b.astype(BF16), (((1,), (1,)), ((), ())),
                           preferred_element_type=F32)


def _bdot_tn(a, b):
    return lax.dot_general(a.astype(BF16), b.astype(BF16), (((0,), (0,)), ((), ())),
                           preferred_element_type=F32)


def _sigmoid(v):
    return 1.0 / (1.0 + jnp.exp(-v))


def _silu(v):
    return v * _sigmoid(v)


def _layer_norm(v, g, b):
    mu = jnp.mean(v, axis=-1, keepdims=True)
    d = v - mu
    var = jnp.mean(d * d, axis=-1, keepdims=True)
    return d * lax.rsqrt(var + LN_EPS) * g + b


def _mm_kernel(x_ref, w_ref, o_ref):
    o_ref[...] = _bdot(x_ref[...], w_ref[...]).astype(o_ref.dtype)


def _matmul(x, w, out_dtype, name, tm=MM_TM, tn=MM_TN):
    m, k = x.shape
    n = w.shape[1]
    tn = min(tn, n)
    return pl.pallas_call(
        _mm_kernel,
        name=name,
        grid=(n // tn, m // tm),
        in_specs=[pl.BlockSpec((tm, k), lambda j, i: (i, 0)),
                  pl.BlockSpec((k, tn), lambda j, i: (0, j))],
        out_specs=pl.BlockSpec((tm, tn), lambda j, i: (i, j)),
        out_shape=jax.ShapeDtypeStruct((m, n), out_dtype),
        compiler_params=_cparams(("parallel", "parallel")),
    )(x, w)


def _dswa_kernel(q_ref, kc_ref, kp_ref, vc_ref, vp_ref, bias_ref, o_ref, lse_ref, *, tq):
    j = pl.program_id(2)
    nqb = tq // ATT_BLK
    q = q_ref[...]
    kwin = jnp.concatenate([kp_ref[...], kc_ref[...]], axis=0)
    vwin = jnp.concatenate([vp_ref[...], vc_ref[...]], axis=0)
    row = lax.broadcasted_iota(jnp.int32, (ATT_BLK, 2 * ATT_BLK), 0)
    col = lax.broadcasted_iota(jnp.int32, (ATT_BLK, 2 * ATT_BLK), 1)
    delta = row + ATT_BLK - col
    band = (delta >= 0) & (delta <= ATT_BLK)
    band_first = band & ((col >= ATT_BLK) | (j > 0))
    lane = lax.broadcasted_iota(jnp.int32, (ATT_BLK, LANES), 1)
    scale = HEAD_DIM ** -0.5
    for c in range(nqb):
        mask = band_first if c == 0 else band
        lse_tile = jnp.zeros((ATT_BLK, LANES), F32)
        for h in range(HEADS_PER_GROUP):
            hs = slice(h * HEAD_DIM, (h + 1) * HEAD_DIM)
            qh = q[c * ATT_BLK:(c + 1) * ATT_BLK, hs]
            kh = kwin[c * ATT_BLK:(c + 2) * ATT_BLK, hs]
            vh = vwin[c * ATT_BLK:(c + 2) * ATT_BLK, hs]
            s = _bdot_nt(qh, kh) * scale + bias_ref[h]
            s = jnp.where(mask, s, NEG_BIG)
            m = jnp.max(s, axis=-1, keepdims=True)
            p = jnp.exp(s - m)
            l = jnp.sum(p, axis=-1, keepdims=True)
            o = _bdot(p, vh) / l
            o_ref[c * ATT_BLK:(c + 1) * ATT_BLK, hs] = o
            lse_tile = jnp.where(lane == h, m + jnp.log(l), lse_tile)
        lse_ref[c * ATT_BLK:(c + 1) * ATT_BLK, :] = lse_tile


def _in_proj_strided_kernel(x_ref, w_ref, o_ref, *scratch, dilation):
    res = _bdot(x_ref[...], w_ref[...])
    if dilation == 1:
        o_ref[0] = res.astype(o_ref.dtype)
    else:
        res_ref, = scratch
        rows = res.shape[0] // dilation
        for t in range(res.shape[1] // LANES):
            ls = slice(t * LANES, (t + 1) * LANES)
            res_ref[t] = res[:, ls]
            for r in range(dilation):
                o_ref[r, :, ls] = res_ref[t, pl.ds(r, rows, stride=dilation), :].astype(o_ref.dtype)


def _in_proj_strided(x, w, dilation, batch, seq, tm=PROJ_A_TM):
    k = x.shape[1]
    wn = tn = w.shape[1]
    tpb = seq // tm
    scratch = [] if dilation == 1 else [pltpu.VMEM((tn // LANES, tm, LANES), F32)]
    return pl.pallas_call(
        functools.partial(_in_proj_strided_kernel, dilation=dilation),
        name=f"in_proj_attn_d{dilation}",
        grid=(batch * tpb, wn // tn),
        in_specs=[pl.BlockSpec((tm, k), lambda i, j: (i, 0)),
                  pl.BlockSpec((k, tn), lambda i, j: (0, j))],
        out_specs=pl.BlockSpec((None, dilation, tm // dilation, tn), lambda i, j: (i // tpb, 0, i % tpb, j)),
        out_shape=jax.ShapeDtypeStruct((batch, dilation, seq // dilation, wn), BF16),
        scratch_shapes=scratch,
        compiler_params=_cparams(("parallel", "parallel")),
    )(x, w)


def _dswa_group(qkv, bias, dilation):
    batch, _, n, _ = qkv.shape
    tq = min(512, n)
    nqb = tq // ATT_BLK

    def cur(off):
        return pl.BlockSpec((None, None, tq, A_GROUP_W), lambda b, r, j: (b, r, j, off))

    def prev(off):
        return pl.BlockSpec((None, None, ATT_BLK, A_GROUP_W),
                            lambda b, r, j: (b, r, jnp.maximum(j * nqb - 1, 0), off))

    return pl.pallas_call(
        functools.partial(_dswa_kernel, tq=tq),
        name=f"dswa_d{dilation}",
        grid=(batch, dilation, n // tq),
        in_specs=[cur(0), cur(1), prev(1), cur(2), prev(2),
                  pl.BlockSpec((HEADS_PER_GROUP, ATT_BLK, 2 * ATT_BLK), lambda b, r, j: (0, 0, 0))],
        out_specs=[pl.BlockSpec((None, None, tq, A_GROUP_W), lambda b, r, j: (b, r, j, 0)),
                   pl.BlockSpec((None, None, tq, LANES), lambda b, r, j: (b, r, j, 0))],
        out_shape=[jax.ShapeDtypeStruct((batch, dilation, n, A_GROUP_W), F32),
                   jax.ShapeDtypeStruct((batch, dilation, n, LANES), F32)],
        compiler_params=_cparams(("parallel", "parallel", "parallel")),
    )(qkv, qkv, qkv, qkv, qkv, bias)


def _t5_causal_bucket(dist):
    num_exact = NUM_BUCKETS // 2
    d = jnp.maximum(dist, 1).astype(F32)
    large = num_exact + (jnp.log(d / num_exact) / math.log(MAX_DISTANCE / num_exact)
                         * (NUM_BUCKETS - num_exact)).astype(jnp.int32)
    large = jnp.minimum(large, NUM_BUCKETS - 1)
    return jnp.where(dist < num_exact, dist, large)


def _band_bias(rel_bias, gi, dilation):
    qi = jnp.arange(ATT_BLK)[:, None] + ATT_BLK
    kj = jnp.arange(2 * ATT_BLK)[None, :]
    delta = jnp.maximum(qi - kj, 0) * dilation
    table = rel_bias[:, gi * HEADS_PER_GROUP:(gi + 1) * HEADS_PER_GROUP].astype(F32)
    onehot = (_t5_causal_bucket(delta)[..., None] == jnp.arange(NUM_BUCKETS)).astype(F32)
    return jnp.einsum('qkn,nh->hqk', onehot, table, precision=lax.Precision.HIGHEST)


def _dot16(a, b):
    return jnp.dot(a, b, preferred_element_type=F32)


def _unit_lower_inverse(a, row, col):
    base = 16
    heads = list(a)
    eye = (row == col).astype(F32)
    same = lambda sz: (row // sz) == (col // sz)
    blk = same(base)
    a_d = {h: jnp.where(blk, a[h], 0.0) for h in heads}
    t = {h: eye - a_d[h] for h in heads}
    p = {h: a_d[h].astype(BF16) for h in heads}
    for _ in range(3):
        p = {h: _dot16(p[h], p[h]).astype(BF16) for h in heads}
        t = {h: t[h] + _dot16(t[h].astype(BF16), p[h]) for h in heads}
    sz = 2 * base
    while sz <= GDN_CHUNK:
        off = same(sz) & jnp.logical_not(same(sz // 2))
        tb = {h: t[h].astype(BF16) for h in heads}
        m = {h: _dot16(jnp.where(off, a[h], 0.0).astype(BF16), tb[h]).astype(BF16) for h in heads}
        t = {h: t[h] - _dot16(tb[h], m[h]) for h in heads}
        sz *= 2
    return t


def _gdn_kernel(x_ref, halo_ref, z_ref, ba_ref, cw_ref, alog_ref, dtb_ref, onw_ref, y_ref, state_ref):
    c = pl.program_id(1)

    @pl.when(c == 0)
    def _():
        state_ref[...] = jnp.zeros_like(state_ref)

    C = GDN_CHUNK
    chunks = range(x_ref.shape[0] // C)
    rows = {cc: slice(cc * C, (cc + 1) * C) for cc in chunks}
    keep_halo = (c > 0).astype(F32)
    row = lax.broadcasted_iota(jnp.int32, (C, C), 0)
    col = lax.broadcasted_iota(jnp.int32, (C, C), 1)
    incl = row >= col
    strict = row > col
    ones_l = incl.astype(BF16)

    def log_decay_cumsum(cc):
        ba = ba_ref[rows[cc], :]
        a_in = ba[:, LANES:] + dtb_ref[...]
        softplus = jnp.maximum(a_in, 0.0) + jnp.log(1.0 + jnp.exp(-jnp.abs(a_in)))
        g_all = -jnp.exp(alog_ref[...]) * softplus
        g1 = g_all.astype(BF16)
        r1 = g_all - g1.astype(F32)
        g2 = r1.astype(BF16)
        g3 = (r1 - g2.astype(F32)).astype(BF16)
        return (jnp.dot(ones_l, g1, preferred_element_type=F32) + jnp.dot(ones_l, g2, preferred_element_type=F32)
                + jnp.dot(ones_l, g3, preferred_element_type=F32))

    beta_all = {cc: _sigmoid(ba_ref[rows[cc], :LANES]) for cc in chunks}
    G = {cc: log_decay_cumsum(cc) for cc in chunks}
    GT = {cc: G[cc].T for cc in chunks}

    def conv_silu(cc, col0):
        cs = slice(col0, col0 + HEAD_DIM)
        xc = x_ref[rows[cc], cs]
        halo = halo_ref[:, cs] * keep_halo if cc == 0 else x_ref[cc * C - SUBLANES:cc * C, cs]
        xf = jnp.concatenate([halo, xc], axis=0)
        w = cw_ref[:, cs]
        y = w[CONV_WIDTH - 1:CONV_WIDTH] * xc
        for i in range(CONV_WIDTH - 1):
            off = SUBLANES - (CONV_WIDTH - 1) + i
            y = y + w[i:i + 1] * xf[off:off + C]
        return _silu(y)

    def l2norm(t, scale=1.0):
        return t * (lax.rsqrt(jnp.sum(t * t, axis=-1, keepdims=True) + RMS_EPS) * scale)

    items = [(cc, h) for cc in chunks for h in range(N_HEADS_B)]
    q = {(cc, h): l2norm(conv_silu(cc, h * HEAD_DIM), HEAD_DIM ** -0.5) for cc, h in items}
    k = {(cc, h): l2norm(conv_silu(cc, B_W + h * HEAD_DIM)) for cc, h in items}
    v = {(cc, h): conv_silu(cc, 2 * B_W + h * HEAD_DIM) for cc, h in items}
    beta = {(cc, h): beta_all[cc][:, h:h + 1] for cc, h in items}
    gc = {(cc, h): G[cc][:, h:h + 1] for cc, h in items}
    g_last = {(cc, h): G[cc][C - 1:C, h:h + 1] for cc, h in items}
    decay = {(cc, h): jnp.exp(jnp.where(incl, gc[cc, h] - GT[cc][h:h + 1, :], NEG_BIG)) for cc, h in items}
    eg = {it: jnp.exp(gc[it]) for it in items}
    kb = {it: k[it].astype(BF16) for it in items}
    kq = {it: lax.dot_general(jnp.concatenate([kb[it], q[it].astype(BF16)], axis=0), kb[it],
                              (((1,), (1,)), ((), ())), preferred_element_type=F32) for it in items}
    a = {it: jnp.where(strict, beta[it] * kq[it][:C] * decay[it], 0.0) for it in items}
    t_inv = _unit_lower_inverse(a, row, col)
    rhs = {it: jnp.concatenate([beta[it] * v[it], (beta[it] * eg[it]) * k[it]], axis=1).astype(BF16) for it in items}
    sol = {it: _dot16(t_inv[it].astype(BF16), rhs[it]) for it in items}
    qk = {it: (kq[it][C:] * decay[it]).astype(BF16) for it in items}
    wq = {it: jnp.concatenate([sol[it][:, HEAD_DIM:], q[it] * eg[it]], axis=0).astype(BF16) for it in items}
    k_dec = {it: (k[it] * jnp.exp(g_last[it] - gc[it])).astype(BF16) for it in items}

    heads = range(N_HEADS_B)
    state = {h: state_ref[h] for h in heads}
    for cc in chunks:
        ws = {h: _dot16(wq[cc, h], state[h].astype(BF16)) for h in heads}
        u = {h: (sol[cc, h][:, :HEAD_DIM] - ws[h][:C]).astype(BF16) for h in heads}
        o = {h: ws[h][C:] + _dot16(qk[cc, h], u[h]) for h in heads}
        state = {h: jnp.exp(g_last[cc, h]) * state[h] + lax.dot_general(
            k_dec[cc, h], u[h], (((0,), (0,)), ((), ())), preferred_element_type=F32) for h in heads}
        for h in heads:
            oh = o[h] * lax.rsqrt(jnp.mean(o[h] * o[h], axis=-1, keepdims=True) + RMS_EPS) * onw_ref[...]
            hs = slice(h * HEAD_DIM, (h + 1) * HEAD_DIM)
            y_ref[rows[cc], hs] = (oh * _silu(z_ref[rows[cc], hs])).astype(y_ref.dtype)
    for h in heads:
        state_ref[h] = state[h]


def _gated_deltanet(proj_b, conv_w, a_log, dt_bias, o_norm_w, batch, seq):
    C = GDN_STEP_CHUNKS * GDN_CHUNK
    wb = proj_b.shape[1]
    pb = proj_b.reshape(batch, seq, wb)
    ba_blk = (wb - 2 * LANES) // (2 * LANES)
    pad = lambda t: jnp.pad(t.astype(F32), (0, LANES - t.shape[0])).reshape(1, LANES)
    const = lambda shape: pl.BlockSpec(shape, lambda b, c: (0,) * len(shape))
    y = pl.pallas_call(
        _gdn_kernel,
        name="gated_deltanet",
        grid=(batch, seq // C),
        in_specs=[pl.BlockSpec((None, C, 3 * B_W), lambda b, c: (b, c, 0)),
                  pl.BlockSpec((None, SUBLANES, 3 * B_W),
                               lambda b, c: (b, jnp.maximum(c * (C // SUBLANES) - 1, 0), 0)),
                  pl.BlockSpec((None, C, B_W), lambda b, c: (b, c, 3)),
                  pl.BlockSpec((None, C, 2 * LANES), lambda b, c: (b, c, ba_blk)),
                  const((CONV_WIDTH, 3 * B_W)), const((1, LANES)), const((1, LANES)), const((1, LANES))],
        out_specs=pl.BlockSpec((None, C, B_W), lambda b, c: (b, c, 0)),
        out_shape=jax.ShapeDtypeStruct((batch, seq, B_W), BF16),
        scratch_shapes=[pltpu.VMEM((N_HEADS_B, HEAD_DIM, HEAD_DIM), F32)],
        compiler_params=_cparams(("parallel", "arbitrary")),
    )(pb, pb, pb, pb, conv_w.astype(F32), pad(a_log), pad(dt_bias), o_norm_w.astype(F32).reshape(1, LANES))
    return y.reshape(batch * seq, B_W)


def _mix_out_kernel(o0, o1, o2, l0, l1, l2, yb_ref, ga_ref, gb_ref, x_ref, woa_ref, wob_ref, wout_ref,
                    g_ref, b_ref, out_ref, *scratch):
    def token_order(ref, scr):
        dilation, rows, width = ref.shape
        if dilation == 1:
            return ref[0]
        planes = []
        for t in range(width // LANES):
            for r in range(dilation):
                scr[t, pl.ds(r, rows, stride=dilation), :] = ref[r, :, t * LANES:(t + 1) * LANES]
            planes.append(scr[t])
        return jnp.concatenate(planes, axis=1)

    outs = (token_order(o0, None), token_order(o1, scratch[0]), token_order(o2, scratch[1]))
    lses = (token_order(l0, None), token_order(l1, scratch[2]), token_order(l2, scratch[3]))
    ya = []
    for h in range(HEADS_PER_GROUP):
        ls = [t[:, h:h + 1] for t in lses]
        m = jnp.maximum(jnp.maximum(ls[0], ls[1]), ls[2])
        es = [jnp.exp(t - m) for t in ls]
        inv = 1.0 / (es[0] + es[1] + es[2])
        hs = slice(h * HEAD_DIM, (h + 1) * HEAD_DIM)
        ya.append((es[0] * inv) * outs[0][:, hs] + (es[1] * inv) * outs[1][:, hs] + (es[2] * inv) * outs[2][:, hs])
    ya = jnp.concatenate(ya, axis=1)
    pa = _bdot(ya, woa_ref[...])
    pb = _bdot(yb_ref[...], wob_ref[...])
    merged = _sigmoid(ga_ref[...]) * pa + _sigmoid(gb_ref[...]) * pb
    mix = _bdot(merged, wout_ref[...])
    out_ref[...] = _layer_norm(ALPHA * x_ref[...] + mix, g_ref[...], b_ref[...])


def _mix_out(outs, lses, yb, proj_b, x, w_oa, w_ob, w_out, ln_g, ln_b, tm=OUT_TM):
    n = x.shape[0]
    seq = outs[0].shape[1] * outs[0].shape[2]
    tpb = seq // tm
    rowblk = lambda w, cb=0: pl.BlockSpec((tm, w), lambda i: (i, cb))
    const = lambda a: pl.BlockSpec(a.shape, lambda i: (0, 0), pipeline_mode=pl.Buffered(1))

    def grouped(a):
        d, w = a.shape[1], a.shape[3]
        return pl.BlockSpec((None, d, tm // d, w), lambda i: (i // tpb, 0, i % tpb, 0))

    wa, wb, wo = w_oa.astype(BF16), w_ob.astype(BF16), w_out.astype(BF16)
    g, b = ln_g.reshape(1, D_MODEL), ln_b.reshape(1, D_MODEL)
    return pl.pallas_call(
        _mix_out_kernel,
        name="mix_out_ln",
        grid=(n // tm,),
        in_specs=[grouped(a) for a in (*outs, *lses)]
        + [rowblk(B_W), rowblk(D_MODEL, 4), rowblk(D_MODEL, 5), rowblk(D_MODEL),
           const(wa), const(wb), const(wo), const(g), const(b)],
        out_specs=rowblk(D_MODEL),
        out_shape=jax.ShapeDtypeStruct((n, D_MODEL), F32),
        scratch_shapes=[pltpu.VMEM((A_GROUP_W // LANES, tm, LANES), F32)] * 2 + [pltpu.VMEM((1, tm, LANES), F32)] * 2,
        compiler_params=_cparams(("parallel",)),
    )(*outs, *lses, yb, proj_b, proj_b, x, wa, wb, wo, g, b)


def _ffn_kernel(x_ref, wg_ref, wu_ref, wd_ref, g_ref, b_ref, out_ref):
    x = x_ref[...]
    xb = x.astype(BF16)
    dff = wg_ref.shape[1]
    acc = None
    for c0 in range(0, dff, FFN_TF):
        c1 = min(c0 + FFN_TF, dff)
        gate = jnp.dot(xb, wg_ref[:, c0:c1], preferred_element_type=F32)
        up = jnp.dot(xb, wu_ref[:, c0:c1], preferred_element_type=F32)
        part = _bdot(_silu(gate) * up, wd_ref[c0:c1, :])
        acc = part if acc is None else acc + part
    out_ref[...] = _layer_norm(ALPHA * x + acc, g_ref[...], b_ref[...])


def _dense_ffn(x, w_gate, w_up, w_down, ln_g, ln_b, tm=FFN_TM):
    n = x.shape[0]
    g, b = ln_g.reshape(1, D_MODEL), ln_b.reshape(1, D_MODEL)
    resident = lambda a: pl.BlockSpec(a.shape, lambda i: (0, 0), pipeline_mode=pl.Buffered(1))
    wg, wu, wd = w_gate.astype(BF16), w_up.astype(BF16), w_down.astype(BF16)
    return pl.pallas_call(
        _ffn_kernel,
        name="dense_ffn_ln",
        grid=(n // tm,),
        in_specs=[pl.BlockSpec((tm, D_MODEL), lambda i: (i, 0)),
                  resident(wg), resident(wu), resident(wd), resident(g), resident(b)],
        out_specs=pl.BlockSpec((tm, D_MODEL), lambda i: (i, 0)),
        out_shape=jax.ShapeDtypeStruct((n, D_MODEL), F32),
        compiler_params=_cparams(("parallel",)),
    )(x, wg, wu, wd, g, b)


def _gather_rows(idx_ref, src_hbm, dst, sem, count):
    def body(i, carry):
        pltpu.make_async_copy(src_hbm.at[pl.ds(idx_ref[0, i], 1)], dst.at[pl.ds(i, 1)], sem).start()
        return carry
    lax.fori_loop(0, count, body, 0, unroll=8)


def _wait_rows(src_hbm, dst, sem, count):
    pltpu.make_async_copy(src_hbm.at[pl.ds(0, count)], dst, sem).wait()


def _dispatch_kernel(dest_ref, x_ref, xs_in, xs_out, sem):
    del xs_in
    tm = x_ref.shape[0]

    def body(i, carry):
        for k in range(TOP_K):
            pltpu.make_async_copy(x_ref.at[pl.ds(i, 1)], xs_out.at[pl.ds(dest_ref[0, TOP_K * i + k], 1)],
                                  sem.at[0]).start()
        return carry
    lax.fori_loop(0, tm, body, 0, unroll=4)
    for _ in range(TOP_K):
        pltpu.make_async_copy(x_ref, xs_out.at[pl.ds(0, tm)], sem.at[0]).wait()


def _moe_dispatch(x, dest, n_rows, tm=DSP_TM):
    n = x.shape[0]
    dest3 = dest.reshape(n // tm, 1, TOP_K * tm)
    return pl.pallas_call(
        _dispatch_kernel,
        name="moe_dispatch",
        grid=(n // tm,),
        in_specs=[pl.BlockSpec((None, 1, TOP_K * tm), lambda i: (i, 0, 0), memory_space=pltpu.SMEM),
                  pl.BlockSpec((tm, D_MODEL), lambda i: (i, 0)),
                  pl.BlockSpec(memory_space=pl.ANY)],
        out_specs=pl.BlockSpec(memory_space=pl.ANY),
        out_shape=jax.ShapeDtypeStruct((n_rows, D_MODEL), F32),
        scratch_shapes=[pltpu.SemaphoreType.DMA((1,))],
        input_output_aliases={2: 0},
        compiler_params=_cparams(("arbitrary",), disable_bounds_checks=True),
    )(dest3, x, jnp.zeros((n_rows, D_MODEL), F32))


def _moe_kernel(meta_ref, x_ref, wg_ref, wu_ref, wd_ref, y_ref, xb_ref, acc_ref):
    b = pl.program_id(0)
    f = pl.program_id(1)
    nblk = pl.num_programs(0)
    used = b < meta_ref[nblk]

    @pl.when(used)
    def _():
        @pl.when(f == 0)
        def _():
            xb_ref[...] = x_ref[...].astype(BF16)
            acc_ref[...] = jnp.zeros_like(acc_ref)

        xb = xb_ref[...]
        gate = jnp.dot(xb, wg_ref[...], preferred_element_type=F32)
        up = jnp.dot(xb, wu_ref[...], preferred_element_type=F32)
        acc_ref[...] += _bdot(_silu(gate) * up, wd_ref[...])

    @pl.when(f == pl.num_programs(1) - 1)
    def _():
        @pl.when(used)
        def _():
            y_ref[...] = acc_ref[...]

        @pl.when(jnp.logical_not(used))
        def _():
            y_ref[...] = jnp.zeros_like(y_ref)


def _moe_experts(xs, meta, w_gate, w_up, w_down, tb=MOE_TB, tf=MOE_TF):
    nblk = xs.shape[0] // tb
    dffe = w_gate.shape[2]
    grid_spec = pltpu.PrefetchScalarGridSpec(
        num_scalar_prefetch=1,
        grid=(nblk, dffe // tf),
        in_specs=[pl.BlockSpec((tb, D_MODEL), lambda b, f, m: (b, 0)),
                  pl.BlockSpec((None, D_MODEL, tf), lambda b, f, m: (m[b], 0, f)),
                  pl.BlockSpec((None, D_MODEL, tf), lambda b, f, m: (m[b], 0, f)),
                  pl.BlockSpec((None, tf, D_MODEL), lambda b, f, m: (m[b], f, 0))],
        out_specs=pl.BlockSpec((tb, D_MODEL), lambda b, f, m: (b, 0)),
        scratch_shapes=[pltpu.VMEM((tb, D_MODEL), BF16), pltpu.VMEM((tb, D_MODEL), F32)],
    )
    return pl.pallas_call(
        _moe_kernel,
        name="moe_experts",
        grid_spec=grid_spec,
        out_shape=jax.ShapeDtypeStruct((nblk * tb, D_MODEL), F32),
        compiler_params=_cparams(("parallel", "arbitrary")),
    )(meta, xs, w_gate.astype(BF16), w_up.astype(BF16), w_down.astype(BF16))


def _combine_kernel(pos0_ref, posn_ref, y_hbm, gates_ref, x_ref, g_ref, b_ref, out_ref, ybuf, sem):
    i = pl.program_id(0)
    tm = out_ref.shape[0]
    slot = i % 2

    @pl.when(i == 0)
    def _():
        _gather_rows(pos0_ref, y_hbm, ybuf.at[0], sem.at[0], TOP_K * tm)

    _wait_rows(y_hbm, ybuf.at[slot], sem.at[slot], TOP_K * tm)

    @pl.when(i + 1 < pl.num_programs(0))
    def _():
        _gather_rows(posn_ref, y_hbm, ybuf.at[1 - slot], sem.at[1 - slot], TOP_K * tm)

    gates = gates_ref[...]
    f = gates[:, 0:1] * ybuf[slot, :tm] + gates[:, 1:2] * ybuf[slot, tm:]
    out_ref[...] = _layer_norm(ALPHA * x_ref[...] + f, g_ref[...], b_ref[...])


def _moe_combine(y, pos, gates, x, ln_g, ln_b, tm=CMB_TM):
    n = x.shape[0]
    nt = n // tm
    pos3 = pos.reshape(nt, tm, TOP_K).transpose(0, 2, 1).reshape(nt, 1, TOP_K * tm)
    g, b = ln_g.reshape(1, D_MODEL), ln_b.reshape(1, D_MODEL)
    smem_blk = lambda imap: pl.BlockSpec((None, 1, TOP_K * tm), imap, memory_space=pltpu.SMEM)
    return pl.pallas_call(
        _combine_kernel,
        name="moe_combine_ln",
        grid=(nt,),
        in_specs=[smem_blk(lambda i: (0, 0, 0)),
                  smem_blk(lambda i: (jnp.minimum(i + 1, nt - 1), 0, 0)),
                  pl.BlockSpec(memory_space=pl.ANY),
                  pl.BlockSpec((tm, TOP_K), lambda i: (i, 0)),
                  pl.BlockSpec((tm, D_MODEL), lambda i: (i, 0)),
                  pl.BlockSpec((1, D_MODEL), lambda i: (0, 0)),
                  pl.BlockSpec((1, D_MODEL), lambda i: (0, 0))],
        out_specs=pl.BlockSpec((tm, D_MODEL), lambda i: (i, 0)),
        out_shape=jax.ShapeDtypeStruct((n, D_MODEL), F32),
        scratch_shapes=[pltpu.VMEM((2, TOP_K * tm, D_MODEL), F32), pltpu.SemaphoreType.DMA((2,))],
        compiler_params=_cparams(("arbitrary",), disable_bounds_checks=True),
    )(pos3, pos3, y, gates, x, g, b)


def _moe_routing(logits, tb):
    n = logits.shape[0]
    top_logit, top_idx = lax.top_k(logits, TOP_K)
    gates = jax.nn.softmax(top_logit, axis=-1)
    na = n * TOP_K
    e_flat = top_idx.reshape(-1).astype(jnp.int32)
    onehot = (e_flat[None, :] == jnp.arange(N_EXPERTS, dtype=jnp.int32)[:, None]).astype(jnp.int32)
    running = jnp.cumsum(onehot, axis=1)
    counts = running[:, -1]
    padded = (counts + tb - 1) // tb * tb
    pend = jnp.cumsum(padded)
    pstart = pend - padded
    dest = jnp.sum(onehot * (pstart[:, None] + running - 1), axis=0).reshape(n, TOP_K)
    nblk = -(-na // tb) + N_EXPERTS
    block_expert = jnp.minimum(jnp.searchsorted(pend, jnp.arange(nblk, dtype=jnp.int32) * tb, side='right'),
                               N_EXPERTS - 1).astype(jnp.int32)
    meta = jnp.concatenate([block_expert, (pend[-1:] // tb).astype(jnp.int32)])
    return dest.astype(jnp.int32), gates, meta, nblk


def _moe_ffn(x, w_router, w_gate, w_up, w_down, ln_g, ln_b):
    wr = jnp.pad(w_router, ((0, 0), (0, LANES - N_EXPERTS))).astype(BF16)
    logits = _matmul(x, wr, F32, "moe_router")[:, :N_EXPERTS]
    dest, gates, meta, nblk = _moe_routing(logits, MOE_TB)
    xs = _moe_dispatch(x, dest, nblk * MOE_TB)
    y = _moe_experts(xs, meta, w_gate, w_up, w_down)
    return _moe_combine(y, dest, gates, x, ln_g, ln_b)


def _split_w_in(w):
    a_end = 3 * A_QKV_W
    bz_end = a_end + 3 * B_W + B_W
    w_groups = [jnp.concatenate([w[:, s * A_QKV_W + gi * A_GROUP_W:s * A_QKV_W + (gi + 1) * A_GROUP_W]
                                 for s in range(3)], axis=1).astype(BF16) for gi in range(N_GROUPS)]
    zpad = jnp.zeros((w.shape[0], LANES - N_HEADS_B), w.dtype)
    w_b = jnp.concatenate([w[:, a_end:bz_end], w[:, bz_end + 2 * N_HEADS_B:],
                           w[:, bz_end:bz_end + N_HEADS_B], zpad,
                           w[:, bz_end + N_HEADS_B:bz_end + 2 * N_HEADS_B], zpad], axis=1)
    return w_groups, w_b.astype(BF16)


def _hybrid_layer(x, batch, seq, rel_bias, w_in, conv_w, a_log, dt_bias, o_norm_w, w_oa, w_ob, w_out, ln_g, ln_b):
    w_groups, w_b = _split_w_in(w_in)
    proj_b = _matmul(x, w_b, F32, "in_proj_gdn_gates")
    outs, lses = [], []
    for gi, (_, dilation) in enumerate(DSWA_PATTERNS):
        qkv = _in_proj_strided(x, w_groups[gi], dilation, batch, seq)
        o, lse = _dswa_group(qkv, _band_bias(rel_bias, gi, dilation), dilation)
        outs.append(o)
        lses.append(lse)
    yb = _gated_deltanet(proj_b, conv_w, a_log, dt_bias, o_norm_w, batch, seq)
    return _mix_out(outs, lses, yb, proj_b, x, w_oa, w_ob, w_out, ln_g, ln_b)


def kernel(x, rel_bias, w_in, conv_w, a_log, dt_bias, o_norm_w, w_oa, w_ob, w_out, ln1_g, ln1_b,
           ffn_w_gate, ffn_w_up, ffn_w_down, moe_router, moe_w_gate, moe_w_up, moe_w_down, ln2_g, ln2_b):
    batch, seq, d = x.shape
    h = x.reshape(batch * seq, d)
    for layer in range(DEPTH):
        h = _hybrid_layer(h, batch, seq, rel_bias, w_in[layer], conv_w[layer], a_log[layer], dt_bias[layer],
                          o_norm_w[layer], w_oa[layer], w_ob[layer], w_out[layer], ln1_g[layer], ln1_b[layer])
        j = layer // 2
        if layer % 2 == 0:
            h = _dense_ffn(h, ffn_w_gate[j], ffn_w_up[j], ffn_w_down[j], ln2_g[layer], ln2_b[layer])
        else:
            h = _moe_ffn(h, moe_router[j], moe_w_gate[j], moe_w_up[j], moe_w_down[j], ln2_g[layer], ln2_b[layer])
    return h.reshape(batch, seq, d)
```

```python
import functools
import math

import jax
import jax.numpy as jnp
from jax import lax
from jax.experimental import pallas as pl
from jax.experimental.pallas import tpu as pltpu

F32 = jnp.float32
BF16 = jnp.bfloat16

D_MODEL = 1024
DEPTH = 2
DSWA_PATTERNS = ((128, 1), (512, 4), (2048, 16))
N_GROUPS = 3
HEADS_PER_GROUP = 4
HEAD_DIM = 128
A_QKV_W = N_GROUPS * HEADS_PER_GROUP * HEAD_DIM
A_GROUP_W = HEADS_PER_GROUP * HEAD_DIM
NUM_BUCKETS = 32
MAX_DISTANCE = 2048
N_HEADS_B = 8
B_W = N_HEADS_B * HEAD_DIM
CONV_WIDTH = 4
N_EXPERTS = 8
TOP_K = 2
ALPHA = (2 * DEPTH) ** 0.25
LN_EPS = 1e-5
RMS_EPS = 1e-6

LANES = 128
SUBLANES = 8
VMEM_LIMIT = 56 * 1024 * 1024

ATT_BLK = 128
GDN_CHUNK = 128
GDN_STEP_CHUNKS = 2
MM_TM = 512
MM_TN = 3200
PROJ_A_TM = 1024
OUT_TM = 512
FFN_TM = 512
FFN_TF = 1024
MOE_TB = 512
MOE_TF = 1792
CMB_TM = 256
DSP_TM = 512

NEG_BIG = -1e30


def _cparams(sem, vmem=VMEM_LIMIT, **kw):
    return pltpu.CompilerParams(dimension_semantics=sem, vmem_limit_bytes=vmem, **kw)


def _bdot(a, b):
    return jnp.dot(a.astype(BF16), b.astype(BF16), preferred_element_type=F32)


def _bdot_nt(a, b):
    return lax.dot_general(a.astype(BF16), b.astype(BF16), (((1,), (1,)), ((), ())),
                           preferred_element_type=F32)


def _bdot_tn(a, b):
    return lax.dot_general(a.astype(BF16), b.astype(BF16), (((0,), (0,)), ((), ())),
                           preferred_element_type=F32)


def _sigmoid(v):
    return 1.0 / (1.0 + jnp.exp(-v))


def _silu(v):
    return v * _sigmoid(v)


def _layer_norm(v, g, b):
    mu = jnp.mean(v, axis=-1, keepdims=True)
    d = v - mu
    var = jnp.mean(d * d, axis=-1, keepdims=True)
    return d * lax.rsqrt(var + LN_EPS) * g + b


def _mm_kernel(x_ref, w_ref, o_ref):
    o_ref[...] = _bdot(x_ref[...], w_ref[...]).astype(o_ref.dtype)


def _matmul(x, w, out_dtype, name, tm=MM_TM, tn=MM_TN):
    m, k = x.shape
    n = w.shape[1]
    tn = min(tn, n)
    return pl.pallas_call(
        _mm_kernel,
        name=name,
        grid=(n // tn, m // tm),
        in_specs=[pl.BlockSpec((tm, k), lambda j, i: (i, 0)),
                  pl.BlockSpec((k, tn), lambda j, i: (0, j))],
        out_specs=pl.BlockSpec((tm, tn), lambda j, i: (i, j)),
        out_shape=jax.ShapeDtypeStruct((m, n), out_dtype),
        compiler_params=_cparams(("parallel", "parallel")),
    )(x, w)


def _dswa_kernel(q_ref, kc_ref, kp_ref, vc_ref, vp_ref, bias0_ref, bias_ref, o_ref, lse_ref, *, tq):
    nqb = tq // ATT_BLK
    q = q_ref[...]
    kwin = jnp.concatenate([kp_ref[...], kc_ref[...]], axis=0)
    vwin = jnp.concatenate([vp_ref[...], vc_ref[...]], axis=0)
    lane = lax.broadcasted_iota(jnp.int32, (ATT_BLK, LANES), 1)
    scale = HEAD_DIM ** -0.5
    exp2_scale = scale * math.log2(math.e)
    for c in range(nqb):
        b_ref = bias0_ref if c == 0 else bias_ref
        lse_tile = jnp.zeros((ATT_BLK, LANES), F32)
        for h in range(HEADS_PER_GROUP):
            hs = slice(h * HEAD_DIM, (h + 1) * HEAD_DIM)
            qh = q[c * ATT_BLK:(c + 1) * ATT_BLK, hs]
            kh = kwin[c * ATT_BLK:(c + 2) * ATT_BLK, hs]
            vh = vwin[c * ATT_BLK:(c + 2) * ATT_BLK, hs]
            t = _bdot_nt(qh, kh) + b_ref[h]
            m = jnp.max(t, axis=-1, keepdims=True)
            p = jnp.exp2((t - m) * exp2_scale)
            l = jnp.sum(p, axis=-1, keepdims=True)
            o = _bdot(p, vh) / l
            o_ref[c * ATT_BLK:(c + 1) * ATT_BLK, hs] = o
            lse_tile = jnp.where(lane == h, m * scale + jnp.log(l), lse_tile)
        lse_ref[c * ATT_BLK:(c + 1) * ATT_BLK, :] = lse_tile


def _in_proj_strided_kernel(x_ref, w_ref, o_ref, *scratch, dilation):
    res = _bdot(x_ref[...], w_ref[...])
    if dilation == 1:
        o_ref[0] = res.astype(o_ref.dtype)
    else:
        res_ref, = scratch
        rows = res.shape[0] // dilation
        for t in range(res.shape[1] // LANES):
            ls = slice(t * LANES, (t + 1) * LANES)
            res_ref[t] = res[:, ls]
            for r in range(dilation):
                o_ref[r, :, ls] = res_ref[t, pl.ds(r, rows, stride=dilation), :].astype(o_ref.dtype)


def _in_proj_strided(x, w, dilation, batch, seq, tm=PROJ_A_TM):
    k = x.shape[1]
    wn = tn = w.shape[1]
    tpb = seq // tm
    scratch = [] if dilation == 1 else [pltpu.VMEM((tn // LANES, tm, LANES), F32)]
    return pl.pallas_call(
        functools.partial(_in_proj_strided_kernel, dilation=dilation),
        name=f"in_proj_attn_d{dilation}",
        grid=(batch * tpb, wn // tn),
        in_specs=[pl.BlockSpec((tm, k), lambda i, j: (i, 0)),
                  pl.BlockSpec((k, tn), lambda i, j: (0, j))],
        out_specs=pl.BlockSpec((None, dilation, tm // dilation, tn), lambda i, j: (i // tpb, 0, i % tpb, j)),
        out_shape=jax.ShapeDtypeStruct((batch, dilation, seq // dilation, wn), BF16),
        scratch_shapes=scratch,
        compiler_params=_cparams(("parallel", "parallel")),
    )(x, w)


def _dswa_group(qkv, bias, dilation):
    batch, _, n, _ = qkv.shape
    tq = min(512, n)
    nqb = tq // ATT_BLK

    def cur(off):
        return pl.BlockSpec((None, None, tq, A_GROUP_W), lambda b, r, j: (b, r, j, off))

    def prev(off):
        return pl.BlockSpec((None, None, ATT_BLK, A_GROUP_W),
                            lambda b, r, j: (b, r, jnp.maximum(j * nqb - 1, 0), off))

    return pl.pallas_call(
        functools.partial(_dswa_kernel, tq=tq),
        name=f"dswa_d{dilation}",
        grid=(batch, dilation, n // tq),
        in_specs=[cur(0), cur(1), prev(1), cur(2), prev(2),
                  pl.BlockSpec((None, HEADS_PER_GROUP, ATT_BLK, 2 * ATT_BLK),
                               lambda b, r, j: (jnp.minimum(j, 1), 0, 0, 0)),
                  pl.BlockSpec((None, HEADS_PER_GROUP, ATT_BLK, 2 * ATT_BLK), lambda b, r, j: (1, 0, 0, 0))],
        out_specs=[pl.BlockSpec((None, None, tq, A_GROUP_W), lambda b, r, j: (b, r, j, 0)),
                   pl.BlockSpec((None, None, tq, LANES), lambda b, r, j: (b, r, j, 0))],
        out_shape=[jax.ShapeDtypeStruct((batch, dilation, n, A_GROUP_W), F32),
                   jax.ShapeDtypeStruct((batch, dilation, n, LANES), F32)],
        compiler_params=_cparams(("parallel", "parallel", "parallel")),
    )(qkv, qkv, qkv, qkv, qkv, bias, bias)


def _t5_causal_bucket(dist):
    num_exact = NUM_BUCKETS // 2
    d = jnp.maximum(dist, 1).astype(F32)
    large = num_exact + (jnp.log(d / num_exact) / math.log(MAX_DISTANCE / num_exact)
                         * (NUM_BUCKETS - num_exact)).astype(jnp.int32)
    large = jnp.minimum(large, NUM_BUCKETS - 1)
    return jnp.where(dist < num_exact, dist, large)


def _band_bias(rel_bias, gi, dilation):
    qi = jnp.arange(ATT_BLK)[:, None] + ATT_BLK
    kj = jnp.arange(2 * ATT_BLK)[None, :]
    band = (qi - kj >= 0) & (qi - kj <= ATT_BLK)
    delta = jnp.maximum(qi - kj, 0) * dilation
    table = rel_bias[:, gi * HEADS_PER_GROUP:(gi + 1) * HEADS_PER_GROUP].astype(F32)
    onehot = (_t5_causal_bucket(delta)[..., None] == jnp.arange(NUM_BUCKETS)).astype(F32)
    bias = jnp.einsum('qkn,nh->hqk', onehot, table, precision=lax.Precision.HIGHEST) / (HEAD_DIM ** -0.5)
    return jnp.stack([jnp.where(band & (kj >= ATT_BLK), bias, NEG_BIG), jnp.where(band, bias, NEG_BIG)])


def _dot16(a, b):
    return jnp.dot(a, b, preferred_element_type=F32)


def _unit_lower_inverse(a, row, col):
    base = 16
    heads = list(a)
    eye = (row == col).astype(F32)
    same = lambda sz: (row // sz) == (col // sz)
    blk = same(base)
    a_d = {h: jnp.where(blk, a[h], 0.0) for h in heads}
    t = {h: eye - a_d[h] for h in heads}
    p = {h: a_d[h].astype(BF16) for h in heads}
    for _ in range(3):
        p = {h: _dot16(p[h], p[h]).astype(BF16) for h in heads}
        t = {h: t[h] + _dot16(t[h].astype(BF16), p[h]) for h in heads}
    sz = 2 * base
    while sz <= GDN_CHUNK:
        off = same(sz) & jnp.logical_not(same(sz // 2))
        tb = {h: t[h].astype(BF16) for h in heads}
        m = {h: _dot16(jnp.where(off, a[h], 0.0).astype(BF16), tb[h]).astype(BF16) for h in heads}
        t = {h: t[h] - _dot16(tb[h], m[h]) for h in heads}
        sz *= 2
    return t


def _gdn_kernel(x_ref, halo_ref, z_ref, ba_ref, cw_ref, alog_ref, dtb_ref, onw_ref, y_ref, state_ref):
    c = pl.program_id(1)

    @pl.when(c == 0)
    def _():
        state_ref[...] = jnp.zeros_like(state_ref)

    C = GDN_CHUNK
    chunks = range(x_ref.shape[0] // C)
    rows = {cc: slice(cc * C, (cc + 1) * C) for cc in chunks}
    keep_halo = (c > 0).astype(F32)
    row = lax.broadcasted_iota(jnp.int32, (C, C), 0)
    col = lax.broadcasted_iota(jnp.int32, (C, C), 1)
    incl = row >= col
    strict = row > col
    ones_l = incl.astype(BF16)

    def log_decay_cumsum(cc):
        ba = ba_ref[rows[cc], :]
        a_in = ba[:, LANES:] + dtb_ref[...]
        softplus = jnp.maximum(a_in, 0.0) + jnp.log(1.0 + jnp.exp(-jnp.abs(a_in)))
        g_all = -jnp.exp(alog_ref[...]) * softplus
        g1 = g_all.astype(BF16)
        r1 = g_all - g1.astype(F32)
        g2 = r1.astype(BF16)
        g3 = (r1 - g2.astype(F32)).astype(BF16)
        return (jnp.dot(ones_l, g1, preferred_element_type=F32) + jnp.dot(ones_l, g2, preferred_element_type=F32)
                + jnp.dot(ones_l, g3, preferred_element_type=F32))

    beta_all = {cc: _sigmoid(ba_ref[rows[cc], :LANES]) for cc in chunks}
    G = {cc: log_decay_cumsum(cc) for cc in chunks}
    GT = {cc: G[cc].T for cc in chunks}
    exp_g = {cc: jnp.exp(G[cc]) for cc in chunks}
    beta_exp_g = {cc: beta_all[cc] * exp_g[cc] for cc in chunks}
    exp_rest = {cc: jnp.exp(G[cc][C - 1:C, :] - G[cc]) for cc in chunks}

    def conv_silu(cc, col0):
        cs = slice(col0, col0 + HEAD_DIM)
        xc = x_ref[rows[cc], cs]
        halo = halo_ref[:, cs] * keep_halo if cc == 0 else x_ref[cc * C - SUBLANES:cc * C, cs]
        xf = jnp.concatenate([halo, xc], axis=0)
        w = cw_ref[:, cs]
        y = w[CONV_WIDTH - 1:CONV_WIDTH] * xc
        for i in range(CONV_WIDTH - 1):
            off = SUBLANES - (CONV_WIDTH - 1) + i
            y = y + w[i:i + 1] * xf[off:off + C]
        return _silu(y)

    def l2norm(t, scale=1.0):
        return t * (lax.rsqrt(jnp.sum(t * t, axis=-1, keepdims=True) + RMS_EPS) * scale)

    items = [(cc, h) for cc in chunks for h in range(N_HEADS_B)]
    q = {(cc, h): l2norm(conv_silu(cc, h * HEAD_DIM), HEAD_DIM ** -0.5) for cc, h in items}
    k = {(cc, h): l2norm(conv_silu(cc, B_W + h * HEAD_DIM)) for cc, h in items}
    v = {(cc, h): conv_silu(cc, 2 * B_W + h * HEAD_DIM) for cc, h in items}
    beta = {(cc, h): beta_all[cc][:, h:h + 1] for cc, h in items}
    gc = {(cc, h): G[cc][:, h:h + 1] for cc, h in items}
    g_last = {(cc, h): G[cc][C - 1:C, h:h + 1] for cc, h in items}
    decay = {(cc, h): jnp.exp(jnp.where(incl, gc[cc, h] - GT[cc][h:h + 1, :], NEG_BIG)) for cc, h in items}
    eg = {(cc, h): exp_g[cc][:, h:h + 1] for cc, h in items}
    kb ={it: k[it].astype(BF16) for it in items}
    kq = {it: lax.dot_general(jnp.concatenate([kb[it], q[it].astype(BF16)], axis=0), kb[it],
                              (((1,), (1,)), ((), ())), preferred_element_type=F32) for it in items}
    a = {it: jnp.where(strict, beta[it] * kq[it][:C] * decay[it], 0.0) for it in items}
    t_inv = _unit_lower_inverse(a, row, col)
    rhs = {(cc, h): jnp.concatenate([beta[cc, h] * v[cc, h], beta_exp_g[cc][:, h:h + 1] * k[cc, h]],
                                    axis=1).astype(BF16) for cc, h in items}
    sol = {it: _dot16(t_inv[it].astype(BF16), rhs[it]) for it in items}
    qk = {it: (kq[it][C:] * decay[it]).astype(BF16) for it in items}
    wq = {it: jnp.concatenate([sol[it][:, HEAD_DIM:], q[it] * eg[it]], axis=0).astype(BF16) for it in items}
    k_dec = {(cc, h): (k[cc, h] * exp_rest[cc][:, h:h + 1]).astype(BF16) for cc, h in items}

    heads = range(N_HEADS_B)
    state = {h: state_ref[h] for h in heads}
    for cc in chunks:
        ws = {h: _dot16(wq[cc, h], state[h].astype(BF16)) for h in heads}
        u = {h: (sol[cc, h][:, :HEAD_DIM] - ws[h][:C]).astype(BF16) for h in heads}
        o = {h: ws[h][C:] + _dot16(qk[cc, h], u[h]) for h in heads}
        state = {h: jnp.exp(g_last[cc, h]) * state[h] + lax.dot_general(
            k_dec[cc, h], u[h], (((0,), (0,)), ((), ())), preferred_element_type=F32) for h in heads}
        for h in heads:
            oh = o[h] * lax.rsqrt(jnp.mean(o[h] * o[h], axis=-1, keepdims=True) + RMS_EPS) * onw_ref[...]
            hs = slice(h * HEAD_DIM, (h + 1) * HEAD_DIM)
            y_ref[rows[cc], hs] = (oh * _silu(z_ref[rows[cc], hs])).astype(y_ref.dtype)
    for h in heads:
        state_ref[h] = state[h]


def _gated_deltanet(proj_b, conv_w, a_log, dt_bias, o_norm_w, batch, seq):
    C = GDN_STEP_CHUNKS * GDN_CHUNK
    wb = proj_b.shape[1]
    pb = proj_b.reshape(batch, seq, wb)
    ba_blk = (wb - 2 * LANES) // (2 * LANES)
    pad = lambda t: jnp.pad(t.astype(F32), (0, LANES - t.shape[0])).reshape(1, LANES)
    const = lambda shape: pl.BlockSpec(shape, lambda b, c: (0,) * len(shape))
    y = pl.pallas_call(
        _gdn_kernel,
        name="gated_deltanet",
        grid=(batch, seq // C),
        in_specs=[pl.BlockSpec((None, C, 3 * B_W), lambda b, c: (b, c, 0)),
                  pl.BlockSpec((None, SUBLANES, 3 * B_W),
                               lambda b, c: (b, jnp.maximum(c * (C // SUBLANES) - 1, 0), 0)),
                  pl.BlockSpec((None, C, B_W), lambda b, c: (b, c, 3)),
                  pl.BlockSpec((None, C, 2 * LANES), lambda b, c: (b, c, ba_blk)),
                  const((CONV_WIDTH, 3 * B_W)), const((1, LANES)), const((1, LANES)), const((1, LANES))],
        out_specs=pl.BlockSpec((None, C, B_W), lambda b, c: (b, c, 0)),
        out_shape=jax.ShapeDtypeStruct((batch, seq, B_W), BF16),
        scratch_shapes=[pltpu.VMEM((N_HEADS_B, HEAD_DIM, HEAD_DIM), F32)],
        compiler_params=_cparams(("parallel", "arbitrary")),
    )(pb, pb, pb, pb, conv_w.astype(F32), pad(a_log), pad(dt_bias), o_norm_w.astype(F32).reshape(1, LANES))
    return y.reshape(batch * seq, B_W)


def _mix_out_kernel(o0, o1, o2, l0, l1, l2, yb_ref, ga_ref, gb_ref, x_ref, woa_ref, wob_ref, wout_ref,
                    g_ref, b_ref, out_ref, *scratch):
    def token_order(ref, scr):
        dilation, rows, width = ref.shape
        if dilation == 1:
            return ref[0]
        planes = []
        for t in range(width // LANES):
            for r in range(dilation):
                scr[t, pl.ds(r, rows, stride=dilation), :] = ref[r, :, t * LANES:(t + 1) * LANES]
            planes.append(scr[t])
        return jnp.concatenate(planes, axis=1)

    outs = (token_order(o0, None), token_order(o1, scratch[0]), token_order(o2, scratch[1]))
    lses = (token_order(l0, None), token_order(l1, scratch[2]), token_order(l2, scratch[3]))
    m = jnp.maximum(jnp.maximum(lses[0], lses[1]), lses[2])
    es = [jnp.exp(t - m) for t in lses]
    inv = 1.0 / (es[0] + es[1] + es[2])
    wgt = [e * inv for e in es]
    ya = []
    for h in range(HEADS_PER_GROUP):
        hs = slice(h * HEAD_DIM, (h + 1) * HEAD_DIM)
        ya.append(wgt[0][:, h:h + 1] * outs[0][:, hs] + wgt[1][:, h:h + 1] * outs[1][:, hs]
                  + wgt[2][:, h:h + 1] * outs[2][:, hs])
    ya = jnp.concatenate(ya, axis=1)
    pa = _bdot(ya, woa_ref[...])
    pb = _bdot(yb_ref[...], wob_ref[...])
    merged = _sigmoid(ga_ref[...]) * pa + _sigmoid(gb_ref[...]) * pb
    mix = _bdot(merged, wout_ref[...])
    out_ref[...] = _layer_norm(ALPHA * x_ref[...] + mix, g_ref[...], b_ref[...])


def _mix_out(outs, lses, yb, proj_b, x, w_oa, w_ob, w_out, ln_g, ln_b, tm=OUT_TM):
    n = x.shape[0]
    seq = outs[0].shape[1] * outs[0].shape[2]
    tpb = seq // tm
    rowblk = lambda w, cb=0: pl.BlockSpec((tm, w), lambda i: (i, cb))
    const = lambda a: pl.BlockSpec(a.shape, lambda i: (0, 0), pipeline_mode=pl.Buffered(1))

    def grouped(a):
        d, w = a.shape[1], a.shape[3]
        return pl.BlockSpec((None, d, tm // d, w), lambda i: (i // tpb, 0, i % tpb, 0))

    wa, wb, wo = w_oa.astype(BF16), w_ob.astype(BF16), w_out.astype(BF16)
    g, b = ln_g.reshape(1, D_MODEL), ln_b.reshape(1, D_MODEL)
    return pl.pallas_call(
        _mix_out_kernel,
        name="mix_out_ln",
        grid=(n // tm,),
        in_specs=[grouped(a) for a in (*outs, *lses)]
        + [rowblk(B_W), rowblk(D_MODEL, 4), rowblk(D_MODEL, 5), rowblk(D_MODEL),
           const(wa), const(wb), const(wo), const(g), const(b)],
        out_specs=rowblk(D_MODEL),
        out_shape=jax.ShapeDtypeStruct((n, D_MODEL), F32),
        scratch_shapes=[pltpu.VMEM((A_GROUP_W // LANES, tm, LANES), F32)] * 2 + [pltpu.VMEM((1, tm, LANES), F32)] * 2,
        compiler_params=_cparams(("parallel",)),
    )(*outs, *lses, yb, proj_b, proj_b, x, wa, wb, wo, g, b)


def _ffn_kernel(x_ref, wg_ref, wu_ref, wd_ref, g_ref, b_ref, out_ref):
    x = x_ref[...]
    xb = x.astype(BF16)
    dff = wg_ref.shape[1]
    acc = None
    for c0 in range(0, dff, FFN_TF):
        c1 = min(c0 + FFN_TF, dff)
        gate = jnp.dot(xb, wg_ref[:, c0:c1], preferred_element_type=F32)
        up = jnp.dot(xb, wu_ref[:, c0:c1], preferred_element_type=F32)
        part = _bdot(_silu(gate) * up, wd_ref[c0:c1, :])
        acc = part if acc is None else acc + part
    out_ref[...] = _layer_norm(ALPHA * x + acc, g_ref[...], b_ref[...])


def _dense_ffn(x, w_gate, w_up, w_down, ln_g, ln_b, tm=FFN_TM):
    n = x.shape[0]
    g, b = ln_g.reshape(1, D_MODEL), ln_b.reshape(1, D_MODEL)
    resident = lambda a: pl.BlockSpec(a.shape, lambda i: (0, 0), pipeline_mode=pl.Buffered(1))
    wg, wu, wd = w_gate.astype(BF16), w_up.astype(BF16), w_down.astype(BF16)
    return pl.pallas_call(
        _ffn_kernel,
        name="dense_ffn_ln",
        grid=(n // tm,),
        in_specs=[pl.BlockSpec((tm, D_MODEL), lambda i: (i, 0)),
                  resident(wg), resident(wu), resident(wd), resident(g), resident(b)],
        out_specs=pl.BlockSpec((tm, D_MODEL), lambda i: (i, 0)),
        out_shape=jax.ShapeDtypeStruct((n, D_MODEL), F32),
        compiler_params=_cparams(("parallel",)),
    )(x, wg, wu, wd, g, b)


def _gather_rows(idx_ref, src_hbm, dst, sem, count):
    def body(i, carry):
        pltpu.make_async_copy(src_hbm.at[pl.ds(idx_ref[0, i], 1)], dst.at[pl.ds(i, 1)], sem).start()
        return carry
    lax.fori_loop(0, count, body, 0, unroll=8)


def _wait_rows(src_hbm, dst, sem, count):
    pltpu.make_async_copy(src_hbm.at[pl.ds(0, count)], dst, sem).wait()


def _dispatch_kernel(dest_ref, x_ref, xs_in, xs_out, sem):
    del xs_in
    tm = x_ref.shape[0]

    def body(i, carry):
        for k in range(TOP_K):
            pltpu.make_async_copy(x_ref.at[pl.ds(i, 1)], xs_out.at[pl.ds(dest_ref[0, TOP_K * i + k], 1)],
                                  sem.at[0]).start()
        return carry
    lax.fori_loop(0, tm, body, 0, unroll=4)
    for _ in range(TOP_K):
        pltpu.make_async_copy(x_ref, xs_out.at[pl.ds(0, tm)], sem.at[0]).wait()


def _moe_dispatch(x, dest, n_rows, tm=DSP_TM):
    n = x.shape[0]
    dest3 = dest.reshape(n // tm, 1, TOP_K * tm)
    return pl.pallas_call(
        _dispatch_kernel,
        name="moe_dispatch",
        grid=(n // tm,),
        in_specs=[pl.BlockSpec((None, 1, TOP_K * tm), lambda i: (i, 0, 0), memory_space=pltpu.SMEM),
                  pl.BlockSpec((tm, D_MODEL), lambda i: (i, 0)),
                  pl.BlockSpec(memory_space=pl.ANY)],
        out_specs=pl.BlockSpec(memory_space=pl.ANY),
        out_shape=jax.ShapeDtypeStruct((n_rows, D_MODEL), F32),
        scratch_shapes=[pltpu.SemaphoreType.DMA((1,))],
        input_output_aliases={2: 0},
        compiler_params=_cparams(("arbitrary",), disable_bounds_checks=True),
    )(dest3, x, jnp.zeros((n_rows, D_MODEL), F32))


def _moe_kernel(meta_ref, x_ref, wg_ref, wu_ref, wd_ref, y_ref, xb_ref, acc_ref):
    b = pl.program_id(0)
    f = pl.program_id(1)
    nblk = pl.num_programs(0)
    used = b < meta_ref[nblk]

    @pl.when(used)
    def _():
        @pl.when(f == 0)
        def _():
            xb_ref[...] = x_ref[...].astype(BF16)
            acc_ref[...] = jnp.zeros_like(acc_ref)

        xb = xb_ref[...]
        gate = jnp.dot(xb, wg_ref[...], preferred_element_type=F32)
        up = jnp.dot(xb, wu_ref[...], preferred_element_type=F32)
        acc_ref[...] += _bdot(_silu(gate) * up, wd_ref[...])

    @pl.when(f == pl.num_programs(1) - 1)
    def _():
        @pl.when(used)
        def _():
            y_ref[...] = acc_ref[...]

        @pl.when(jnp.logical_not(used))
        def _():
            y_ref[...] = jnp.zeros_like(y_ref)


def _moe_experts(xs, meta, w_gate, w_up, w_down, tb=MOE_TB, tf=MOE_TF):
    nblk = xs.shape[0] // tb
    dffe = w_gate.shape[2]
    grid_spec = pltpu.PrefetchScalarGridSpec(
        num_scalar_prefetch=1,
        grid=(nblk, dffe // tf),
        in_specs=[pl.BlockSpec((tb, D_MODEL), lambda b, f, m: (b, 0)),
                  pl.BlockSpec((None, D_MODEL, tf), lambda b, f, m: (m[b], 0, f)),
                  pl.BlockSpec((None, D_MODEL, tf), lambda b, f, m: (m[b], 0, f)),
                  pl.BlockSpec((None, tf, D_MODEL), lambda b, f, m: (m[b], f, 0))],
        out_specs=pl.BlockSpec((tb, D_MODEL), lambda b, f, m: (b, 0)),
        scratch_shapes=[pltpu.VMEM((tb, D_MODEL), BF16), pltpu.VMEM((tb, D_MODEL), F32)],
    )
    return pl.pallas_call(
        _moe_kernel,
        name="moe_experts",
        grid_spec=grid_spec,
        out_shape=jax.ShapeDtypeStruct((nblk * tb, D_MODEL), F32),
        compiler_params=_cparams(("parallel", "arbitrary")),
    )(meta, xs, w_gate.astype(BF16), w_up.astype(BF16), w_down.astype(BF16))


def _combine_kernel(pos0_ref, posn_ref, y_hbm, gates_ref, x_ref, g_ref, b_ref, out_ref, ybuf, sem):
    i = pl.program_id(0)
    tm = out_ref.shape[0]
    slot = i % 2

    @pl.when(i == 0)
    def _():
        _gather_rows(pos0_ref, y_hbm, ybuf.at[0], sem.at[0], TOP_K * tm)

    _wait_rows(y_hbm, ybuf.at[slot], sem.at[slot], TOP_K * tm)

    @pl.when(i + 1 < pl.num_programs(0))
    def _():
        _gather_rows(posn_ref, y_hbm, ybuf.at[1 - slot], sem.at[1 - slot], TOP_K * tm)

    gates = gates_ref[...]
    f = gates[:, 0:1] * ybuf[slot, :tm] + gates[:, 1:2] * ybuf[slot, tm:]
    out_ref[...] = _layer_norm(ALPHA * x_ref[...] + f, g_ref[...], b_ref[...])


def _moe_combine(y, pos, gates, x, ln_g, ln_b, tm=CMB_TM):
    n = x.shape[0]
    nt = n // tm
    pos3 = pos.reshape(nt, tm, TOP_K).transpose(0, 2, 1).reshape(nt, 1, TOP_K * tm)
    g, b = ln_g.reshape(1, D_MODEL), ln_b.reshape(1, D_MODEL)
    smem_blk = lambda imap: pl.BlockSpec((None, 1, TOP_K * tm), imap, memory_space=pltpu.SMEM)
    return pl.pallas_call(
        _combine_kernel,
        name="moe_combine_ln",
        grid=(nt,),
        in_specs=[smem_blk(lambda i: (0, 0, 0)),
                  smem_blk(lambda i: (jnp.minimum(i + 1, nt - 1), 0, 0)),
                  pl.BlockSpec(memory_space=pl.ANY),
                  pl.BlockSpec((tm, TOP_K), lambda i: (i, 0)),
                  pl.BlockSpec((tm, D_MODEL), lambda i: (i, 0)),
                  pl.BlockSpec((1, D_MODEL), lambda i: (0, 0)),
                  pl.BlockSpec((1, D_MODEL), lambda i: (0, 0))],
        out_specs=pl.BlockSpec((tm, D_MODEL), lambda i: (i, 0)),
        out_shape=jax.ShapeDtypeStruct((n, D_MODEL), F32),
        scratch_shapes=[pltpu.VMEM((2, TOP_K * tm, D_MODEL), F32), pltpu.SemaphoreType.DMA((2,))],
        compiler_params=_cparams(("arbitrary",), disable_bounds_checks=True),
    )(pos3, pos3, y, gates, x, g, b)


def _moe_routing(logits, tb):
    n = logits.shape[0]
    top_logit, top_idx = lax.top_k(logits, TOP_K)
    gates = jax.nn.softmax(top_logit, axis=-1)
    na = n * TOP_K
    e_flat = top_idx.reshape(-1).astype(jnp.int32)
    onehot = (e_flat[None, :] == jnp.arange(N_EXPERTS, dtype=jnp.int32)[:, None]).astype(jnp.int32)
    running = jnp.cumsum(onehot, axis=1)
    counts = running[:, -1]
    padded = (counts + tb - 1) // tb * tb
    pend = jnp.cumsum(padded)
    pstart = pend - padded
    dest = jnp.sum(onehot * (pstart[:, None] + running - 1), axis=0).reshape(n, TOP_K)
    nblk = -(-na // tb) + N_EXPERTS
    block_expert = jnp.minimum(jnp.searchsorted(pend, jnp.arange(nblk, dtype=jnp.int32) * tb, side='right'),
                               N_EXPERTS - 1).astype(jnp.int32)
    meta = jnp.concatenate([block_expert, (pend[-1:] // tb).astype(jnp.int32)])
    return dest.astype(jnp.int32), gates, meta, nblk


def _moe_ffn(x, w_router, w_gate, w_up, w_down, ln_g, ln_b):
    wr = jnp.pad(w_router, ((0, 0), (0, LANES - N_EXPERTS))).astype(BF16)
    logits = _matmul(x, wr, F32, "moe_router")[:, :N_EXPERTS]
    dest, gates, meta, nblk = _moe_routing(logits, MOE_TB)
    xs = _moe_dispatch(x, dest, nblk * MOE_TB)
    y = _moe_experts(xs, meta, w_gate, w_up, w_down)
    return _moe_combine(y, dest, gates, x, ln_g, ln_b)


def _split_w_in(w):
    a_end = 3 * A_QKV_W
    bz_end = a_end + 3 * B_W + B_W
    w_groups = [jnp.concatenate([w[:, s * A_QKV_W + gi * A_GROUP_W:s * A_QKV_W + (gi + 1) * A_GROUP_W]
                                 for s in range(3)], axis=1).astype(BF16) for gi in range(N_GROUPS)]
    zpad = jnp.zeros((w.shape[0], LANES - N_HEADS_B), w.dtype)
    w_b = jnp.concatenate([w[:, a_end:bz_end], w[:, bz_end + 2 * N_HEADS_B:],
                           w[:, bz_end:bz_end + N_HEADS_B], zpad,
                           w[:, bz_end + N_HEADS_B:bz_end + 2 * N_HEADS_B], zpad], axis=1)
    return w_groups, w_b.astype(BF16)


def _hybrid_layer(x, batch, seq, rel_bias, w_in, conv_w, a_log, dt_bias, o_norm_w, w_oa, w_ob, w_out, ln_g, ln_b):
    w_groups, w_b = _split_w_in(w_in)
    proj_b = _matmul(x, w_b, F32, "in_proj_gdn_gates")
    outs, lses = [], []
    for gi, (_, dilation) in enumerate(DSWA_PATTERNS):
        qkv = _in_proj_strided(x, w_groups[gi], dilation, batch, seq)
        o, lse = _dswa_group(qkv, _band_bias(rel_bias, gi, dilation), dilation)
        outs.append(o)
        lses.append(lse)
    yb = _gated_deltanet(proj_b, conv_w, a_log, dt_bias, o_norm_w, batch, seq)
    return _mix_out(outs, lses, yb, proj_b, x, w_oa, w_ob, w_out, ln_g, ln_b)


def kernel(x, rel_bias, w_in, conv_w, a_log, dt_bias, o_norm_w, w_oa, w_ob, w_out, ln1_g, ln1_b,
           ffn_w_gate, ffn_w_up, ffn_w_down, moe_router, moe_w_gate, moe_w_up, moe_w_down, ln2_g, ln2_b):
    batch, seq, d = x.shape
    h = x.reshape(batch * seq, d)
    for layer in range(DEPTH):
        h = _hybrid_layer(h, batch, seq, rel_bias, w_in[layer], conv_w[layer], a_log[layer], dt_bias[layer],
                          o_norm_w[layer], w_oa[layer], w_ob[layer], w_out[layer], ln1_g[layer], ln1_b[layer])
        j = layer // 2
        if layer % 2 == 0:
            h = _dense_ffn(h, ffn_w_gate[j], ffn_w_up[j], ffn_w_down[j], ln2_g[layer], ln2_b[layer])
        else:
            h = _moe_ffn(h, moe_router[j], moe_w_gate[j], moe_w_up[j], moe_w_down[j], ln2_g[layer], ln2_b[layer])
    return h.reshape(batch, seq, d)
```

```python
import functools
import math

import jax
import jax.numpy as jnp
from jax import lax
from jax.experimental import pallas as pl
from jax.experimental.pallas import tpu as pltpu

F32 = jnp.float32
BF16 = jnp.bfloat16

D_MODEL = 1024
DEPTH = 2
DSWA_PATTERNS = ((128, 1), (512, 4), (2048, 16))
N_GROUPS = 3
HEADS_PER_GROUP = 4
HEAD_DIM = 128
A_QKV_W = N_GROUPS * HEADS_PER_GROUP * HEAD_DIM
A_GROUP_W = HEADS_PER_GROUP * HEAD_DIM
NUM_BUCKETS = 32
MAX_DISTANCE = 2048
N_HEADS_B = 8
B_W = N_HEADS_B * HEAD_DIM
CONV_WIDTH = 4
N_EXPERTS = 8
TOP_K = 2
ALPHA = (2 * DEPTH) ** 0.25
LN_EPS = 1e-5
RMS_EPS = 1e-6

LANES = 128
SUBLANES = 8
VMEM_LIMIT = 56 * 1024 * 1024

ATT_BLK = 128
DSWA_ROWS = 1024
GDN_CHUNK = 128
GDN_STEP_CHUNKS = 2
MM_TM = 512
MM_TN = 3200
PROJ_A_TM = 1024
OUT_TM = 512
FFN_TM = 512
FFN_TF = 1024
MOE_TB = 512
MOE_TF = 1792
CMB_TM = 512
DSP_TM = 512

NEG_BIG = -1e30


def _cparams(sem, vmem=VMEM_LIMIT, **kw):
    return pltpu.CompilerParams(dimension_semantics=sem, vmem_limit_bytes=vmem, **kw)


def _bdot(a, b):
    return jnp.dot(a.astype(BF16), b.astype(BF16), preferred_element_type=F32)


def _bdot_nt(a, b):
    return lax.dot_general(a.astype(BF16), b.astype(BF16), (((1,), (1,)), ((), ())),
                           preferred_element_type=F32)


def _bdot_tn(a, b):
    return lax.dot_general(a.astype(BF16), b.astype(BF16), (((0,), (0,)), ((), ())),
                           preferred_element_type=F32)


def _sigmoid(v):
    return 1.0 / (1.0 + jnp.exp(-v))


def _silu(v):
    return v * _sigmoid(v)


def _layer_norm(v, g, b):
    mu = jnp.mean(v, axis=-1, keepdims=True)
    d = v - mu
    var = jnp.mean(d * d, axis=-1, keepdims=True)
    return d * lax.rsqrt(var + LN_EPS) * g + b


def _mm_kernel(x_ref, w_ref, o_ref):
    o_ref[...] = _bdot(x_ref[...], w_ref[...]).astype(o_ref.dtype)


def _matmul(x, w, out_dtype, name, tm=MM_TM, tn=MM_TN):
    m, k = x.shape
    n = w.shape[1]
    tn = min(tn, n)
    return pl.pallas_call(
        _mm_kernel,
        name=name,
        grid=(n // tn, m // tm),
        in_specs=[pl.BlockSpec((tm, k), lambda j, i: (i, 0)),
                  pl.BlockSpec((k, tn), lambda j, i: (0, j))],
        out_specs=pl.BlockSpec((tm, tn), lambda j, i: (i, j)),
        out_shape=jax.ShapeDtypeStruct((m, n), out_dtype),
        compiler_params=_cparams(("parallel", "parallel")),
    )(x, w)


def _dswa_kernel(q_ref, kc_ref, kp_ref, vc_ref, vp_ref, bias0_ref, bias_ref, o_ref, lse_ref, *, tq):
    nqb = tq // ATT_BLK
    lane = lax.broadcasted_iota(jnp.int32, (ATT_BLK, LANES), 1)
    scale = HEAD_DIM ** -0.5
    exp2_scale = scale * math.log2(math.e)
    for ri in range(q_ref.shape[0]):
        q = q_ref[ri]
        kwin = jnp.concatenate([kp_ref[ri], kc_ref[ri]], axis=0)
        vwin = jnp.concatenate([vp_ref[ri], vc_ref[ri]], axis=0)
        for c in range(nqb):
            b_ref = bias0_ref if c == 0 else bias_ref
            rows = slice(c * ATT_BLK, (c + 1) * ATT_BLK)
            lse_tile = jnp.zeros((ATT_BLK, LANES), F32)
            for h in range(HEADS_PER_GROUP):
                hs = slice(h * HEAD_DIM, (h + 1) * HEAD_DIM)
                kh = kwin[c * ATT_BLK:(c + 2) * ATT_BLK, hs]
                vh = vwin[c * ATT_BLK:(c + 2) * ATT_BLK, hs]
                t = _bdot_nt(q[rows, hs], kh) + b_ref[h]
                m = jnp.max(t, axis=-1, keepdims=True)
                p = jnp.exp2((t - m) * exp2_scale)
                l = jnp.sum(p, axis=-1, keepdims=True)
                o_ref[ri, rows, hs] = _bdot(p, vh) / l
                lse_tile = jnp.where(lane == h, m * scale + jnp.log(l), lse_tile)
            lse_ref[ri, rows, :] = lse_tile


def _in_proj_strided_kernel(x_ref, w_ref, o_ref, *scratch, dilation):
    res = _bdot(x_ref[...], w_ref[...])
    if dilation == 1:
        o_ref[0] = res.astype(o_ref.dtype)
    else:
        res_ref, = scratch
        rows = res.shape[0] // dilation
        for t in range(res.shape[1] // LANES):
            ls = slice(t * LANES, (t + 1) * LANES)
            res_ref[t] = res[:, ls]
            for r in range(dilation):
                o_ref[r, :, ls] = res_ref[t, pl.ds(r, rows, stride=dilation), :].astype(o_ref.dtype)


def _in_proj_strided(x, w, dilation, batch, seq, tm=PROJ_A_TM):
    k = x.shape[1]
    wn = tn = w.shape[1]
    tpb = seq // tm
    scratch = [] if dilation == 1 else [pltpu.VMEM((tn // LANES, tm, LANES), F32)]
    return pl.pallas_call(
        functools.partial(_in_proj_strided_kernel, dilation=dilation),
        name=f"in_proj_attn_d{dilation}",
        grid=(batch * tpb, wn // tn),
        in_specs=[pl.BlockSpec((tm, k), lambda i, j: (i, 0)),
                  pl.BlockSpec((k, tn), lambda i, j: (0, j))],
        out_specs=pl.BlockSpec((None, dilation, tm // dilation, tn), lambda i, j: (i // tpb, 0, i % tpb, j)),
        out_shape=jax.ShapeDtypeStruct((batch, dilation, seq // dilation, wn), BF16),
        scratch_shapes=scratch,
        compiler_params=_cparams(("parallel", "parallel")),
    )(x, w)


def _dswa_group(qkv, bias, dilation):
    batch, _, n, _ = qkv.shape
    tq = min(DSWA_ROWS, n)
    rr = DSWA_ROWS // tq
    nqb = tq // ATT_BLK

    def cur(off):
        return pl.BlockSpec((None, rr, tq, A_GROUP_W), lambda b, r, j: (b, r, j, off))

    def prev(off):
        return pl.BlockSpec((None, rr, ATT_BLK, A_GROUP_W),
                            lambda b, r, j: (b, r, jnp.maximum(j * nqb - 1, 0), off))

    return pl.pallas_call(
        functools.partial(_dswa_kernel, tq=tq),
        name=f"dswa_d{dilation}",
        grid=(batch, dilation // rr, n // tq),
        in_specs=[cur(0), cur(1), prev(1), cur(2), prev(2),
                  pl.BlockSpec((None, HEADS_PER_GROUP, ATT_BLK, 2 * ATT_BLK),
                               lambda b, r, j: (jnp.minimum(j, 1), 0, 0, 0)),
                  pl.BlockSpec((None, HEADS_PER_GROUP, ATT_BLK, 2 * ATT_BLK), lambda b, r, j: (1, 0, 0, 0))],
        out_specs=[pl.BlockSpec((None, rr, tq, A_GROUP_W), lambda b, r, j: (b, r, j, 0)),
                   pl.BlockSpec((None, rr, tq, LANES), lambda b, r, j: (b, r, j, 0))],
        out_shape=[jax.ShapeDtypeStruct((batch, dilation, n, A_GROUP_W), F32),
                   jax.ShapeDtypeStruct((batch, dilation, n, LANES), F32)],
        compiler_params=_cparams(("parallel", "parallel", "parallel")),
    )(qkv, qkv, qkv, qkv, qkv, bias, bias)


def _t5_causal_bucket(dist):
    num_exact = NUM_BUCKETS // 2
    d = jnp.maximum(dist, 1).astype(F32)
    large = num_exact + (jnp.log(d / num_exact) / math.log(MAX_DISTANCE / num_exact)
                         * (NUM_BUCKETS - num_exact)).astype(jnp.int32)
    large = jnp.minimum(large, NUM_BUCKETS - 1)
    return jnp.where(dist < num_exact, dist, large)


def _band_bias(rel_bias, gi, dilation):
    qi = jnp.arange(ATT_BLK)[:, None] + ATT_BLK
    kj = jnp.arange(2 * ATT_BLK)[None, :]
    band = (qi - kj >= 0) & (qi - kj <= ATT_BLK)
    delta = jnp.maximum(qi - kj, 0) * dilation
    table = rel_bias[:, gi * HEADS_PER_GROUP:(gi + 1) * HEADS_PER_GROUP].astype(F32)
    onehot = (_t5_causal_bucket(delta)[..., None] == jnp.arange(NUM_BUCKETS)).astype(F32)
    bias = jnp.einsum('qkn,nh->hqk', onehot, table, precision=lax.Precision.HIGHEST) / (HEAD_DIM ** -0.5)
    return jnp.stack([jnp.where(band & (kj >= ATT_BLK), bias, NEG_BIG), jnp.where(band, bias, NEG_BIG)])


def _dot16(a, b):
    return jnp.dot(a, b, preferred_element_type=F32)


def _unit_lower_inverse(a, row, col):
    base = 16
    heads = list(a)
    eye = (row == col).astype(F32)
    same = lambda sz: (row // sz) == (col // sz)
    blk = same(base)
    a_d = {h: jnp.where(blk, a[h], 0.0) for h in heads}
    t = {h: eye - a_d[h] for h in heads}
    p = {h: a_d[h].astype(BF16) for h in heads}
    for _ in range(3):
        p = {h: _dot16(p[h], p[h]).astype(BF16) for h in heads}
        t = {h: t[h] + _dot16(t[h].astype(BF16), p[h]) for h in heads}
    sz = 2 * base
    while sz <= GDN_CHUNK:
        off = same(sz) & jnp.logical_not(same(sz // 2))
        tb = {h: t[h].astype(BF16) for h in heads}
        m = {h: _dot16(jnp.where(off, a[h], 0.0).astype(BF16), tb[h]).astype(BF16) for h in heads}
        t = {h: t[h] - _dot16(tb[h], m[h]) for h in heads}
        sz *= 2
    return t


def _gdn_kernel(x_ref, halo_ref, z_ref, ba_ref, cw_ref, alog_ref, dtb_ref, onw_ref, y_ref, state_ref):
    c = pl.program_id(1)

    @pl.when(c == 0)
    def _():
        state_ref[...] = jnp.zeros_like(state_ref)

    C = GDN_CHUNK
    chunks = range(x_ref.shape[0] // C)
    rows = {cc: slice(cc * C, (cc + 1) * C) for cc in chunks}
    keep_halo = (c > 0).astype(F32)
    row = lax.broadcasted_iota(jnp.int32, (C, C), 0)
    col = lax.broadcasted_iota(jnp.int32, (C, C), 1)
    incl = row >= col
    strict = row > col
    ones_l = incl.astype(BF16)

    def log_decay_cumsum(cc):
        ba = ba_ref[rows[cc], :]
        a_in = ba[:, LANES:] + dtb_ref[...]
        softplus = jnp.maximum(a_in, 0.0) + jnp.log(1.0 + jnp.exp(-jnp.abs(a_in)))
        g_all = -jnp.exp(alog_ref[...]) * softplus
        g1 = g_all.astype(BF16)
        r1 = g_all - g1.astype(F32)
        g2 = r1.astype(BF16)
        g3 = (r1 - g2.astype(F32)).astype(BF16)
        return (jnp.dot(ones_l, g1, preferred_element_type=F32) + jnp.dot(ones_l, g2, preferred_element_type=F32)
                + jnp.dot(ones_l, g3, preferred_element_type=F32))

    beta_all = {cc: _sigmoid(ba_ref[rows[cc], :LANES]) for cc in chunks}
    G = {cc: log_decay_cumsum(cc) for cc in chunks}
    GT = {cc: G[cc].T for cc in chunks}
    exp_g = {cc: jnp.exp(G[cc]) for cc in chunks}
    beta_exp_g = {cc: beta_all[cc] * exp_g[cc] for cc in chunks}
    exp_rest = {cc: jnp.exp(G[cc][C - 1:C, :] - G[cc]) for cc in chunks}

    def conv_silu(cc, col0):
        cs = slice(col0, col0 + HEAD_DIM)
        xc = x_ref[rows[cc], cs]
        halo = halo_ref[:, cs] * keep_halo if cc == 0 else x_ref[cc * C - SUBLANES:cc * C, cs]
        xf = jnp.concatenate([halo, xc], axis=0)
        w = cw_ref[:, cs]
        y = w[CONV_WIDTH - 1:CONV_WIDTH] * xc
        for i in range(CONV_WIDTH - 1):
            off = SUBLANES - (CONV_WIDTH - 1) + i
            y = y + w[i:i + 1] * xf[off:off + C]
        return _silu(y)

    def l2norm(t, scale=1.0):
        return t * (lax.rsqrt(jnp.sum(t * t, axis=-1, keepdims=True) + RMS_EPS) * scale)

    items = [(cc, h) for cc in chunks for h in range(N_HEADS_B)]
    q = {(cc, h): l2norm(conv_silu(cc, h * HEAD_DIM), HEAD_DIM ** -0.5) for cc, h in items}
    k = {(cc, h): l2norm(conv_silu(cc, B_W + h * HEAD_DIM)) for cc, h in items}
    v = {(cc, h): conv_silu(cc, 2 * B_W + h * HEAD_DIM) for cc, h in items}
    beta = {(cc, h): beta_all[cc][:, h:h + 1] for cc, h in items}
    gc = {(cc, h): G[cc][:, h:h + 1] for cc, h in items}
    g_last = {(cc, h): G[cc][C - 1:C, h:h + 1] for cc, h in items}
    decay = {(cc, h): jnp.exp(jnp.where(incl, gc[cc, h] - GT[cc][h:h + 1, :], NEG_BIG)) for cc, h in items}
    eg = {(cc, h): exp_g[cc][:, h:h + 1] for cc, h in items}
    kb ={it: k[it].astype(BF16) for it in items}
    kq = {it: lax.dot_general(jnp.concatenate([kb[it], q[it].astype(BF16)], axis=0), kb[it],
                              (((1,), (1,)), ((), ())), preferred_element_type=F32) for it in items}
    a = {it: jnp.where(strict, beta[it] * kq[it][:C] * decay[it], 0.0) for it in items}
    t_inv = _unit_lower_inverse(a, row, col)
    rhs = {(cc, h): jnp.concatenate([beta[cc, h] * v[cc, h], beta_exp_g[cc][:, h:h + 1] * k[cc, h]],
                                    axis=1).astype(BF16) for cc, h in items}
    sol = {it: _dot16(t_inv[it].astype(BF16), rhs[it]) for it in items}
    qk = {it: (kq[it][C:] * decay[it]).astype(BF16) for it in items}
    wq = {it: jnp.concatenate([sol[it][:, HEAD_DIM:], q[it] * eg[it]], axis=0).astype(BF16) for it in items}
    k_dec = {(cc, h): (k[cc, h] * exp_rest[cc][:, h:h + 1]).astype(BF16) for cc, h in items}

    heads = range(N_HEADS_B)
    state = {h: state_ref[h] for h in heads}
    for cc in chunks:
        ws = {h: _dot16(wq[cc, h], state[h].astype(BF16)) for h in heads}
        u = {h: (sol[cc, h][:, :HEAD_DIM] - ws[h][:C]).astype(BF16) for h in heads}
        o = {h: ws[h][C:] + _dot16(qk[cc, h], u[h]) for h in heads}
        state = {h: jnp.exp(g_last[cc, h]) * state[h] + lax.dot_general(
            k_dec[cc, h], u[h], (((0,), (0,)), ((), ())), preferred_element_type=F32) for h in heads}
        for h in heads:
            oh = o[h] * lax.rsqrt(jnp.mean(o[h] * o[h], axis=-1, keepdims=True) + RMS_EPS) * onw_ref[...]
            hs = slice(h * HEAD_DIM, (h + 1) * HEAD_DIM)
            y_ref[rows[cc], hs] = (oh * _silu(z_ref[rows[cc], hs])).astype(y_ref.dtype)
    for h in heads:
        state_ref[h] = state[h]


def _gated_deltanet(proj_b, conv_w, a_log, dt_bias, o_norm_w, batch, seq):
    C = GDN_STEP_CHUNKS * GDN_CHUNK
    wb = proj_b.shape[1]
    pb = proj_b.reshape(batch, seq, wb)
    ba_blk = (wb - 2 * LANES) // (2 * LANES)
    pad = lambda t: jnp.pad(t.astype(F32), (0, LANES - t.shape[0])).reshape(1, LANES)
    const = lambda shape: pl.BlockSpec(shape, lambda b, c: (0,) * len(shape))
    y = pl.pallas_call(
        _gdn_kernel,
        name="gated_deltanet",
        grid=(batch, seq // C),
        in_specs=[pl.BlockSpec((None, C, 3 * B_W), lambda b, c: (b, c, 0)),
                  pl.BlockSpec((None, SUBLANES, 3 * B_W),
                               lambda b, c: (b, jnp.maximum(c * (C // SUBLANES) - 1, 0), 0)),
                  pl.BlockSpec((None, C, B_W), lambda b, c: (b, c, 3)),
                  pl.BlockSpec((None, C, 2 * LANES), lambda b, c: (b, c, ba_blk)),
                  const((CONV_WIDTH, 3 * B_W)), const((1, LANES)), const((1, LANES)), const((1, LANES))],
        out_specs=pl.BlockSpec((None, C, B_W), lambda b, c: (b, c, 0)),
        out_shape=jax.ShapeDtypeStruct((batch, seq, B_W), BF16),
        scratch_shapes=[pltpu.VMEM((N_HEADS_B, HEAD_DIM, HEAD_DIM), F32)],
        compiler_params=_cparams(("parallel", "arbitrary")),
    )(pb, pb, pb, pb, conv_w.astype(F32), pad(a_log), pad(dt_bias), o_norm_w.astype(F32).reshape(1, LANES))
    return y.reshape(batch * seq, B_W)


def _mix_out_kernel(o0, o1, o2, l0, l1, l2, yb_ref, ga_ref, gb_ref, x_ref, woa_ref, wob_ref, wout_ref,
                    g_ref, b_ref, out_ref, *scratch):
    def token_order(ref, scr):
        dilation, rows, width = ref.shape
        if dilation == 1:
            return ref[0]
        planes = []
        for t in range(width // LANES):
            for r in range(dilation):
                scr[t, pl.ds(r, rows, stride=dilation), :] = ref[r, :, t * LANES:(t + 1) * LANES]
            planes.append(scr[t])
        return jnp.concatenate(planes, axis=1)

    outs = (token_order(o0, None), token_order(o1, scratch[0]), token_order(o2, scratch[1]))
    lses = (token_order(l0, None), token_order(l1, scratch[2]), token_order(l2, scratch[3]))
    m = jnp.maximum(jnp.maximum(lses[0], lses[1]), lses[2])
    es = [jnp.exp(t - m) for t in lses]
    inv = 1.0 / (es[0] + es[1] + es[2])
    wgt = [e * inv for e in es]
    ya = []
    for h in range(HEADS_PER_GROUP):
        hs = slice(h * HEAD_DIM, (h + 1) * HEAD_DIM)
        ya.append(wgt[0][:, h:h + 1] * outs[0][:, hs] + wgt[1][:, h:h + 1] * outs[1][:, hs]
                  + wgt[2][:, h:h + 1] * outs[2][:, hs])
    ya = jnp.concatenate(ya, axis=1)
    pa = _bdot(ya, woa_ref[...])
    pb = _bdot(yb_ref[...], wob_ref[...])
    merged = _sigmoid(ga_ref[...]) * pa + _sigmoid(gb_ref[...]) * pb
    mix = _bdot(merged, wout_ref[...])
    out_ref[...] = _layer_norm(ALPHA * x_ref[...] + mix, g_ref[...], b_ref[...])


def _mix_out(outs, lses, yb, proj_b, x, w_oa, w_ob, w_out, ln_g, ln_b, tm=OUT_TM):
    n = x.shape[0]
    seq = outs[0].shape[1] * outs[0].shape[2]
    tpb = seq // tm
    rowblk = lambda w, cb=0: pl.BlockSpec((tm, w), lambda i: (i, cb))
    const = lambda a: pl.BlockSpec(a.shape, lambda i: (0, 0), pipeline_mode=pl.Buffered(1))

    def grouped(a):
        d, w = a.shape[1], a.shape[3]
        return pl.BlockSpec((None, d, tm // d, w), lambda i: (i // tpb, 0, i % tpb, 0))

    wa, wb, wo = w_oa.astype(BF16), w_ob.astype(BF16), w_out.astype(BF16)
    g, b = ln_g.reshape(1, D_MODEL), ln_b.reshape(1, D_MODEL)
    return pl.pallas_call(
        _mix_out_kernel,
        name="mix_out_ln",
        grid=(n // tm,),
        in_specs=[grouped(a) for a in (*outs, *lses)]
        + [rowblk(B_W), rowblk(D_MODEL, 4), rowblk(D_MODEL, 5), rowblk(D_MODEL),
           const(wa), const(wb), const(wo), const(g), const(b)],
        out_specs=rowblk(D_MODEL),
        out_shape=jax.ShapeDtypeStruct((n, D_MODEL), F32),
        scratch_shapes=[pltpu.VMEM((A_GROUP_W // LANES, tm, LANES), F32)] * 2 + [pltpu.VMEM((1, tm, LANES), F32)] * 2,
        compiler_params=_cparams(("parallel",)),
    )(*outs, *lses, yb, proj_b, proj_b, x, wa, wb, wo, g, b)


def _ffn_kernel(x_ref, wg_ref, wu_ref, wd_ref, g_ref, b_ref, out_ref):
    x = x_ref[...]
    xb = x.astype(BF16)
    dff = wg_ref.shape[1]
    acc = None
    for c0 in range(0, dff, FFN_TF):
        c1 = min(c0 + FFN_TF, dff)
        gate = jnp.dot(xb, wg_ref[:, c0:c1], preferred_element_type=F32)
        up = jnp.dot(xb, wu_ref[:, c0:c1], preferred_element_type=F32)
        part = _bdot(_silu(gate) * up, wd_ref[c0:c1, :])
        acc = part if acc is None else acc + part
    out_ref[...] = _layer_norm(ALPHA * x + acc, g_ref[...], b_ref[...])


def _dense_ffn(x, w_gate, w_up, w_down, ln_g, ln_b, tm=FFN_TM):
    n = x.shape[0]
    g, b = ln_g.reshape(1, D_MODEL), ln_b.reshape(1, D_MODEL)
    resident = lambda a: pl.BlockSpec(a.shape, lambda i: (0, 0), pipeline_mode=pl.Buffered(1))
    wg, wu, wd = w_gate.astype(BF16), w_up.astype(BF16), w_down.astype(BF16)
    return pl.pallas_call(
        _ffn_kernel,
        name="dense_ffn_ln",
        grid=(n // tm,),
        in_specs=[pl.BlockSpec((tm, D_MODEL), lambda i: (i, 0)),
                  resident(wg), resident(wu), resident(wd), resident(g), resident(b)],
        out_specs=pl.BlockSpec((tm, D_MODEL), lambda i: (i, 0)),
        out_shape=jax.ShapeDtypeStruct((n, D_MODEL), F32),
        compiler_params=_cparams(("parallel",)),
    )(x, wg, wu, wd, g, b)


def _gather_rows(idx_ref, src_hbm, dst, sem, count):
    def body(i, carry):
        pltpu.make_async_copy(src_hbm.at[pl.ds(idx_ref[0, i], 1)], dst.at[pl.ds(i, 1)], sem).start()
        return carry
    lax.fori_loop(0, count, body, 0, unroll=8)


def _wait_rows(src_hbm, dst, sem, count):
    pltpu.make_async_copy(src_hbm.at[pl.ds(0, count)], dst, sem).wait()


def _dispatch_kernel(dest_ref, x_ref, xs_in, xs_out, sem):
    del xs_in
    tm = x_ref.shape[0]

    def body(i, carry):
        for k in range(TOP_K):
            pltpu.make_async_copy(x_ref.at[pl.ds(i, 1)], xs_out.at[pl.ds(dest_ref[0, TOP_K * i + k], 1)],
                                  sem.at[0]).start()
        return carry
    lax.fori_loop(0, tm, body, 0, unroll=4)
    for _ in range(TOP_K):
        pltpu.make_async_copy(x_ref, xs_out.at[pl.ds(0, tm)], sem.at[0]).wait()


def _moe_dispatch(x, dest, n_rows, tm=DSP_TM):
    n = x.shape[0]
    dest3 = dest.reshape(n // tm, 1, TOP_K * tm)
    return pl.pallas_call(
        _dispatch_kernel,
        name="moe_dispatch",
        grid=(n // tm,),
        in_specs=[pl.BlockSpec((None, 1, TOP_K * tm), lambda i: (i, 0, 0), memory_space=pltpu.SMEM),
                  pl.BlockSpec((tm, D_MODEL), lambda i: (i, 0)),
                  pl.BlockSpec(memory_space=pl.ANY)],
        out_specs=pl.BlockSpec(memory_space=pl.ANY),
        out_shape=jax.ShapeDtypeStruct((n_rows, D_MODEL), F32),
        scratch_shapes=[pltpu.SemaphoreType.DMA((1,))],
        input_output_aliases={2: 0},
        compiler_params=_cparams(("arbitrary",), disable_bounds_checks=True),
    )(dest3, x, jnp.zeros((n_rows, D_MODEL), F32))


def _moe_kernel(meta_ref, x_ref, wg_ref, wu_ref, wd_ref, y_ref, xb_ref, acc_ref):
    b = pl.program_id(0)
    f = pl.program_id(1)
    nblk = pl.num_programs(0)
    used = b < meta_ref[nblk]

    @pl.when(used)
    def _():
        @pl.when(f == 0)
        def _():
            xb_ref[...] = x_ref[...].astype(BF16)
            acc_ref[...] = jnp.zeros_like(acc_ref)

        xb = xb_ref[...]
        gate = jnp.dot(xb, wg_ref[...], preferred_element_type=F32)
        up = jnp.dot(xb, wu_ref[...], preferred_element_type=F32)
        acc_ref[...] += _bdot(_silu(gate) * up, wd_ref[...])

    @pl.when(f == pl.num_programs(1) - 1)
    def _():
        @pl.when(used)
        def _():
            y_ref[...] = acc_ref[...]

        @pl.when(jnp.logical_not(used))
        def _():
            y_ref[...] = jnp.zeros_like(y_ref)


def _moe_experts(xs, meta, w_gate, w_up, w_down, tb=MOE_TB, tf=MOE_TF):
    nblk = xs.shape[0] // tb
    dffe = w_gate.shape[2]
    grid_spec = pltpu.PrefetchScalarGridSpec(
        num_scalar_prefetch=1,
        grid=(nblk, dffe // tf),
        in_specs=[pl.BlockSpec((tb, D_MODEL), lambda b, f, m: (b, 0)),
                  pl.BlockSpec((None, D_MODEL, tf), lambda b, f, m: (m[b], 0, f)),
                  pl.BlockSpec((None, D_MODEL, tf), lambda b, f, m: (m[b], 0, f)),
                  pl.BlockSpec((None, tf, D_MODEL), lambda b, f, m: (m[b], f, 0))],
        out_specs=pl.BlockSpec((tb, D_MODEL), lambda b, f, m: (b, 0)),
        scratch_shapes=[pltpu.VMEM((tb, D_MODEL), BF16), pltpu.VMEM((tb, D_MODEL), F32)],
    )
    return pl.pallas_call(
        _moe_kernel,
        name="moe_experts",
        grid_spec=grid_spec,
        out_shape=jax.ShapeDtypeStruct((nblk * tb, D_MODEL), F32),
        compiler_params=_cparams(("parallel", "arbitrary")),
    )(meta, xs, w_gate.astype(BF16), w_up.astype(BF16), w_down.astype(BF16))


def _combine_kernel(pos0_ref, posn_ref, y_hbm, gates_ref, x_ref, g_ref, b_ref, out_ref, ybuf, sem):
    i = pl.program_id(0)
    tm = out_ref.shape[0]
    slot = i % 2

    @pl.when(i == 0)
    def _():
        _gather_rows(pos0_ref, y_hbm, ybuf.at[0], sem.at[0], TOP_K * tm)

    _wait_rows(y_hbm, ybuf.at[slot], sem.at[slot], TOP_K * tm)

    @pl.when(i + 1 < pl.num_programs(0))
    def _():
        _gather_rows(posn_ref, y_hbm, ybuf.at[1 - slot], sem.at[1 - slot], TOP_K * tm)

    gates = gates_ref[...]
    f = gates[:, 0:1] * ybuf[slot, :tm] + gates[:, 1:2] * ybuf[slot, tm:]
    out_ref[...] = _layer_norm(ALPHA * x_ref[...] + f, g_ref[...], b_ref[...])


def _moe_combine(y, pos, gates, x, ln_g, ln_b, tm=CMB_TM):
    n = x.shape[0]
    nt = n // tm
    pos3 = pos.reshape(nt, tm, TOP_K).transpose(0, 2, 1).reshape(nt, 1, TOP_K * tm)
    g, b = ln_g.reshape(1, D_MODEL), ln_b.reshape(1, D_MODEL)
    smem_blk = lambda imap: pl.BlockSpec((None, 1, TOP_K * tm), imap, memory_space=pltpu.SMEM)
    return pl.pallas_call(
        _combine_kernel,
        name="moe_combine_ln",
        grid=(nt,),
        in_specs=[smem_blk(lambda i: (0, 0, 0)),
                  smem_blk(lambda i: (jnp.minimum(i + 1, nt - 1), 0, 0)),
                  pl.BlockSpec(memory_space=pl.ANY),
                  pl.BlockSpec((tm, TOP_K), lambda i: (i, 0)),
                  pl.BlockSpec((tm, D_MODEL), lambda i: (i, 0)),
                  pl.BlockSpec((1, D_MODEL), lambda i: (0, 0)),
                  pl.BlockSpec((1, D_MODEL), lambda i: (0, 0))],
        out_specs=pl.BlockSpec((tm, D_MODEL), lambda i: (i, 0)),
        out_shape=jax.ShapeDtypeStruct((n, D_MODEL), F32),
        scratch_shapes=[pltpu.VMEM((2, TOP_K * tm, D_MODEL), F32), pltpu.SemaphoreType.DMA((2,))],
        compiler_params=_cparams(("arbitrary",), disable_bounds_checks=True),
    )(pos3, pos3, y, gates, x, g, b)


def _moe_routing(logits, tb):
    n = logits.shape[0]
    top_logit, top_idx = lax.top_k(logits, TOP_K)
    gates = jax.nn.softmax(top_logit, axis=-1)
    na = n * TOP_K
    e_flat = top_idx.reshape(-1).astype(jnp.int32)
    onehot = (e_flat[None, :] == jnp.arange(N_EXPERTS, dtype=jnp.int32)[:, None]).astype(jnp.int32)
    running = jnp.cumsum(onehot, axis=1)
    counts = running[:, -1]
    padded = (counts + tb - 1) // tb * tb
    pend = jnp.cumsum(padded)
    pstart = pend - padded
    dest = jnp.sum(onehot * (pstart[:, None] + running - 1), axis=0).reshape(n, TOP_K)
    nblk = -(-na // tb) + N_EXPERTS
    block_expert = jnp.minimum(jnp.searchsorted(pend, jnp.arange(nblk, dtype=jnp.int32) * tb, side='right'),
                               N_EXPERTS - 1).astype(jnp.int32)
    meta = jnp.concatenate([block_expert, (pend[-1:] // tb).astype(jnp.int32)])
    return dest.astype(jnp.int32), gates, meta, nblk


def _moe_ffn(x, w_router, w_gate, w_up, w_down, ln_g, ln_b):
    wr = jnp.pad(w_router, ((0, 0), (0, LANES - N_EXPERTS))).astype(BF16)
    logits = _matmul(x, wr, F32, "moe_router")[:, :N_EXPERTS]
    dest, gates, meta, nblk = _moe_routing(logits, MOE_TB)
    xs = _moe_dispatch(x, dest, nblk * MOE_TB)
    y = _moe_experts(xs, meta, w_gate, w_up, w_down)
    return _moe_combine(y, dest, gates, x, ln_g, ln_b)


def _split_w_in(w):
    a_end = 3 * A_QKV_W
    bz_end = a_end + 3 * B_W + B_W
    w_groups = [jnp.concatenate([w[:, s * A_QKV_W + gi * A_GROUP_W:s * A_QKV_W + (gi + 1) * A_GROUP_W]
                                 for s in range(3)], axis=1).astype(BF16) for gi in range(N_GROUPS)]
    zpad = jnp.zeros((w.shape[0], LANES - N_HEADS_B), w.dtype)
    w_b = jnp.concatenate([w[:, a_end:bz_end], w[:, bz_end + 2 * N_HEADS_B:],
                           w[:, bz_end:bz_end + N_HEADS_B], zpad,
                           w[:, bz_end + N_HEADS_B:bz_end + 2 * N_HEADS_B], zpad], axis=1)
    return w_groups, w_b.astype(BF16)


def _hybrid_layer(x, batch, seq, rel_bias, w_in, conv_w, a_log, dt_bias, o_norm_w, w_oa, w_ob, w_out, ln_g, ln_b):
    w_groups, w_b = _split_w_in(w_in)
    proj_b = _matmul(x, w_b, F32, "in_proj_gdn_gates")
    outs, lses = [], []
    for gi, (_, dilation) in enumerate(DSWA_PATTERNS):
        qkv = _in_proj_strided(x, w_groups[gi], dilation, batch, seq)
        o, lse = _dswa_group(qkv, _band_bias(rel_bias, gi, dilation), dilation)
        outs.append(o)
        lses.append(lse)
    yb = _gated_deltanet(proj_b, conv_w, a_log, dt_bias, o_norm_w, batch, seq)
    return _mix_out(outs, lses, yb, proj_b, x, w_oa, w_ob, w_out, ln_g, ln_b)


def kernel(x, rel_bias, w_in, conv_w, a_log, dt_bias, o_norm_w, w_oa, w_ob, w_out, ln1_g, ln1_b,
           ffn_w_gate, ffn_w_up, ffn_w_down, moe_router, moe_w_gate, moe_w_up, moe_w_down, ln2_g, ln2_b):
    batch, seq, d = x.shape
    h = x.reshape(batch * seq, d)
    for layer in range(DEPTH):
        h = _hybrid_layer(h, batch, seq, rel_bias, w_in[layer], conv_w[layer], a_log[layer], dt_bias[layer],
                          o_norm_w[layer], w_oa[layer], w_ob[layer], w_out[layer], ln1_g[layer], ln1_b[layer])
        j = layer // 2
        if layer % 2 == 0:
            h = _dense_ffn(h, ffn_w_gate[j], ffn_w_up[j], ffn_w_down[j], ln2_g[layer], ln2_b[layer])
        else:
            h = _moe_ffn(h, moe_router[j], moe_w_gate[j], moe_w_up[j], moe_w_down[j], ln2_g[layer], ln2_b[layer])
    return h.reshape(batch, seq, d)
```

```python
import functools
import math

import jax
import jax.numpy as jnp
from jax import lax
from jax.experimental import pallas as pl
from jax.experimental.pallas import tpu as pltpu

F32 = jnp.float32
BF16 = jnp.bfloat16

D_MODEL = 1024
DEPTH = 2
DSWA_PATTERNS = ((128, 1), (512, 4), (2048, 16))
N_GROUPS = 3
HEADS_PER_GROUP = 4
HEAD_DIM = 128
A_QKV_W = N_GROUPS * HEADS_PER_GROUP * HEAD_DIM
A_GROUP_W = HEADS_PER_GROUP * HEAD_DIM
NUM_BUCKETS = 32
MAX_DISTANCE = 2048
N_HEADS_B = 8
B_W = N_HEADS_B * HEAD_DIM
CONV_WIDTH = 4
N_EXPERTS = 8
TOP_K = 2
ALPHA = (2 * DEPTH) ** 0.25
LN_EPS = 1e-5
RMS_EPS = 1e-6

LANES = 128
SUBLANES = 8
VMEM_LIMIT = 56 * 1024 * 1024

ATT_BLK = 128
DSWA_ROWS = 2048
GDN_CHUNK = 128
GDN_STEP_CHUNKS = 4
MM_TM = 512
MM_TN = 3200
PROJ_A_TM = 1024
OUT_TM = 512
FFN_TM = 512
FFN_TF = 1024
MOE_TB = 512
MOE_TF = 1792
CMB_TM = 512
DSP_TM = 512

NEG_BIG = -1e30


def _cparams(sem, vmem=VMEM_LIMIT, **kw):
    return pltpu.CompilerParams(dimension_semantics=sem, vmem_limit_bytes=vmem, **kw)


def _bdot(a, b):
    return jnp.dot(a.astype(BF16), b.astype(BF16), preferred_element_type=F32)


def _bdot_nt(a, b):
    return lax.dot_general(a.astype(BF16), b.astype(BF16), (((1,), (1,)), ((), ())),
                           preferred_element_type=F32)


def _bdot_tn(a, b):
    return lax.dot_general(a.astype(BF16), b.astype(BF16), (((0,), (0,)), ((), ())),
                           preferred_element_type=F32)


def _sigmoid(v):
    return 1.0 / (1.0 + jnp.exp(-v))


def _silu(v):
    return v * _sigmoid(v)


def _layer_norm(v, g, b):
    mu = jnp.mean(v, axis=-1, keepdims=True)
    d = v - mu
    var = jnp.mean(d * d, axis=-1, keepdims=True)
    return d * lax.rsqrt(var + LN_EPS) * g + b


def _mm_kernel(x_ref, w_ref, o_ref):
    o_ref[...] = _bdot(x_ref[...], w_ref[...]).astype(o_ref.dtype)


def _matmul(x, w, out_dtype, name, tm=MM_TM, tn=MM_TN):
    m, k = x.shape
    n = w.shape[1]
    tn = min(tn, n)
    return pl.pallas_call(
        _mm_kernel,
        name=name,
        grid=(n // tn, m // tm),
        in_specs=[pl.BlockSpec((tm, k), lambda j, i: (i, 0)),
                  pl.BlockSpec((k, tn), lambda j, i: (0, j))],
        out_specs=pl.BlockSpec((tm, tn), lambda j, i: (i, j)),
        out_shape=jax.ShapeDtypeStruct((m, n), out_dtype),
        compiler_params=_cparams(("parallel", "parallel")),
    )(x, w)


def _dswa_kernel(q_ref, kc_ref, kp_ref, vc_ref, vp_ref, bias0_ref, bias_ref, o_ref, lse_ref, *, tq):
    nqb = tq // ATT_BLK
    lane = lax.broadcasted_iota(jnp.int32, (ATT_BLK, LANES), 1)
    scale = HEAD_DIM ** -0.5
    exp2_scale = scale * math.log2(math.e)
    for ri in range(q_ref.shape[0]):
        q = q_ref[ri]
        kwin = jnp.concatenate([kp_ref[ri], kc_ref[ri]], axis=0)
        vwin = jnp.concatenate([vp_ref[ri], vc_ref[ri]], axis=0)
        for c in range(nqb):
            b_ref = bias0_ref if c == 0 else bias_ref
            rows = slice(c * ATT_BLK, (c + 1) * ATT_BLK)
            lse_tile = jnp.zeros((ATT_BLK, LANES), F32)
            for h in range(HEADS_PER_GROUP):
                hs = slice(h * HEAD_DIM, (h + 1) * HEAD_DIM)
                kh = kwin[c * ATT_BLK:(c + 2) * ATT_BLK, hs]
                vh = vwin[c * ATT_BLK:(c + 2) * ATT_BLK, hs]
                t = _bdot_nt(q[rows, hs], kh) + b_ref[h]
                m = jnp.max(t, axis=-1, keepdims=True)
                p = jnp.exp2((t - m) * exp2_scale)
                l = jnp.sum(p, axis=-1, keepdims=True)
                o_ref[ri, rows, hs] = _bdot(p, vh) / l
                lse_tile = jnp.where(lane == h, m * scale + jnp.log(l), lse_tile)
            lse_ref[ri, rows, :] = lse_tile


def _in_proj_strided_kernel(x_ref, w_ref, o_ref, *scratch, dilation):
    res = _bdot(x_ref[...], w_ref[...])
    if dilation == 1:
        o_ref[0] = res.astype(o_ref.dtype)
    else:
        res_ref, = scratch
        rows = res.shape[0] // dilation
        for t in range(res.shape[1] // LANES):
            ls = slice(t * LANES, (t + 1) * LANES)
            res_ref[t] = res[:, ls]
            for r in range(dilation):
                o_ref[r, :, ls] = res_ref[t, pl.ds(r, rows, stride=dilation), :].astype(o_ref.dtype)


def _in_proj_strided(x, w, dilation, batch, seq, tm=PROJ_A_TM):
    k = x.shape[1]
    wn = tn = w.shape[1]
    tpb = seq // tm
    scratch = [] if dilation == 1 else [pltpu.VMEM((tn // LANES, tm, LANES), F32)]
    return pl.pallas_call(
        functools.partial(_in_proj_strided_kernel, dilation=dilation),
        name=f"in_proj_attn_d{dilation}",
        grid=(batch * tpb, wn // tn),
        in_specs=[pl.BlockSpec((tm, k), lambda i, j: (i, 0)),
                  pl.BlockSpec((k, tn), lambda i, j: (0, j))],
        out_specs=pl.BlockSpec((None, dilation, tm // dilation, tn), lambda i, j: (i // tpb, 0, i % tpb, j)),
        out_shape=jax.ShapeDtypeStruct((batch, dilation, seq // dilation, wn), BF16),
        scratch_shapes=scratch,
        compiler_params=_cparams(("parallel", "parallel")),
    )(x, w)


def _dswa_group(qkv, bias, dilation):
    batch, _, n, _ = qkv.shape
    tq = min(DSWA_ROWS, n)
    rr = DSWA_ROWS // tq
    nqb = tq // ATT_BLK

    def cur(off):
        return pl.BlockSpec((None, rr, tq, A_GROUP_W), lambda b, r, j: (b, r, j, off))

    def prev(off):
        return pl.BlockSpec((None, rr, ATT_BLK, A_GROUP_W),
                            lambda b, r, j: (b, r, jnp.maximum(j * nqb - 1, 0), off))

    return pl.pallas_call(
        functools.partial(_dswa_kernel, tq=tq),
        name=f"dswa_d{dilation}",
        grid=(batch, dilation // rr, n // tq),
        in_specs=[cur(0), cur(1), prev(1), cur(2), prev(2),
                  pl.BlockSpec((None, HEADS_PER_GROUP, ATT_BLK, 2 * ATT_BLK),
                               lambda b, r, j: (jnp.minimum(j, 1), 0, 0, 0)),
                  pl.BlockSpec((None, HEADS_PER_GROUP, ATT_BLK, 2 * ATT_BLK), lambda b, r, j: (1, 0, 0, 0))],
        out_specs=[pl.BlockSpec((None, rr, tq, A_GROUP_W), lambda b, r, j: (b, r, j, 0)),
                   pl.BlockSpec((None, rr, tq, LANES), lambda b, r, j: (b, r, j, 0))],
        out_shape=[jax.ShapeDtypeStruct((batch, dilation, n, A_GROUP_W), F32),
                   jax.ShapeDtypeStruct((batch, dilation, n, LANES), F32)],
        compiler_params=_cparams(("parallel", "parallel", "parallel")),
    )(qkv, qkv, qkv, qkv, qkv, bias, bias)


def _t5_causal_bucket(dist):
    num_exact = NUM_BUCKETS // 2
    d = jnp.maximum(dist, 1).astype(F32)
    large = num_exact + (jnp.log(d / num_exact) / math.log(MAX_DISTANCE / num_exact)
                         * (NUM_BUCKETS - num_exact)).astype(jnp.int32)
    large = jnp.minimum(large, NUM_BUCKETS - 1)
    return jnp.where(dist < num_exact, dist, large)


def _band_bias(rel_bias, gi, dilation):
    qi = jnp.arange(ATT_BLK)[:, None] + ATT_BLK
    kj = jnp.arange(2 * ATT_BLK)[None, :]
    band = (qi - kj >= 0) & (qi - kj <= ATT_BLK)
    delta = jnp.maximum(qi - kj, 0) * dilation
    table = rel_bias[:, gi * HEADS_PER_GROUP:(gi + 1) * HEADS_PER_GROUP].astype(F32)
    onehot = (_t5_causal_bucket(delta)[..., None] == jnp.arange(NUM_BUCKETS)).astype(F32)
    bias = jnp.einsum('qkn,nh->hqk', onehot, table, precision=lax.Precision.HIGHEST) / (HEAD_DIM ** -0.5)
    return jnp.stack([jnp.where(band & (kj >= ATT_BLK), bias, NEG_BIG), jnp.where(band, bias, NEG_BIG)])


def _dot16(a, b):
    return jnp.dot(a, b, preferred_element_type=F32)


def _unit_lower_inverse(a, row, col):
    base = 16
    heads = list(a)
    eye = (row == col).astype(F32)
    same = lambda sz: (row // sz) == (col // sz)
    blk = same(base)
    a_d = {h: jnp.where(blk, a[h], 0.0) for h in heads}
    t = {h: eye - a_d[h] for h in heads}
    p = {h: a_d[h].astype(BF16) for h in heads}
    for _ in range(3):
        p = {h: _dot16(p[h], p[h]).astype(BF16) for h in heads}
        t = {h: t[h] + _dot16(t[h].astype(BF16), p[h]) for h in heads}
    sz = 2 * base
    while sz <= GDN_CHUNK:
        off = same(sz) & jnp.logical_not(same(sz // 2))
        tb = {h: t[h].astype(BF16) for h in heads}
        m = {h: _dot16(jnp.where(off, a[h], 0.0).astype(BF16), tb[h]).astype(BF16) for h in heads}
        t = {h: t[h] - _dot16(tb[h], m[h]) for h in heads}
        sz *= 2
    return t


def _gdn_kernel(x_ref, halo_ref, z_ref, ba_ref, cw_ref, alog_ref, dtb_ref, onw_ref, y_ref, state_ref):
    c = pl.program_id(1)

    @pl.when(c == 0)
    def _():
        state_ref[...] = jnp.zeros_like(state_ref)

    C = GDN_CHUNK
    chunks = range(x_ref.shape[0] // C)
    rows = {cc: slice(cc * C, (cc + 1) * C) for cc in chunks}
    keep_halo = (c > 0).astype(F32)
    row = lax.broadcasted_iota(jnp.int32, (C, C), 0)
    col = lax.broadcasted_iota(jnp.int32, (C, C), 1)
    incl = row >= col
    strict = row > col
    ones_l = incl.astype(BF16)

    def log_decay_cumsum(cc):
        ba = ba_ref[rows[cc], :]
        a_in = ba[:, LANES:] + dtb_ref[...]
        softplus = jnp.maximum(a_in, 0.0) + jnp.log(1.0 + jnp.exp(-jnp.abs(a_in)))
        g_all = -jnp.exp(alog_ref[...]) * softplus
        g1 = g_all.astype(BF16)
        r1 = g_all - g1.astype(F32)
        g2 = r1.astype(BF16)
        g3 = (r1 - g2.astype(F32)).astype(BF16)
        return (jnp.dot(ones_l, g1, preferred_element_type=F32) + jnp.dot(ones_l, g2, preferred_element_type=F32)
                + jnp.dot(ones_l, g3, preferred_element_type=F32))

    beta_all = {cc: _sigmoid(ba_ref[rows[cc], :LANES]) for cc in chunks}
    G = {cc: log_decay_cumsum(cc) for cc in chunks}
    GT = {cc: G[cc].T for cc in chunks}
    exp_g = {cc: jnp.exp(G[cc]) for cc in chunks}
    beta_exp_g = {cc: beta_all[cc] * exp_g[cc] for cc in chunks}
    exp_rest = {cc: jnp.exp(G[cc][C - 1:C, :] - G[cc]) for cc in chunks}

    def conv_silu(cc, col0):
        cs = slice(col0, col0 + HEAD_DIM)
        xc = x_ref[rows[cc], cs]
        halo = halo_ref[:, cs] * keep_halo if cc == 0 else x_ref[cc * C - SUBLANES:cc * C, cs]
        xf = jnp.concatenate([halo, xc], axis=0)
        w = cw_ref[:, cs]
        y = w[CONV_WIDTH - 1:CONV_WIDTH] * xc
        for i in range(CONV_WIDTH - 1):
            off = SUBLANES - (CONV_WIDTH - 1) + i
            y = y + w[i:i + 1] * xf[off:off + C]
        return _silu(y)

    def l2norm(t, scale=1.0):
        return t * (lax.rsqrt(jnp.sum(t * t, axis=-1, keepdims=True) + RMS_EPS) * scale)

    items = [(cc, h) for cc in chunks for h in range(N_HEADS_B)]
    q = {(cc, h): l2norm(conv_silu(cc, h * HEAD_DIM), HEAD_DIM ** -0.5) for cc, h in items}
    k = {(cc, h): l2norm(conv_silu(cc, B_W + h * HEAD_DIM)) for cc, h in items}
    v = {(cc, h): conv_silu(cc, 2 * B_W + h * HEAD_DIM) for cc, h in items}
    beta = {(cc, h): beta_all[cc][:, h:h + 1] for cc, h in items}
    gc = {(cc, h): G[cc][:, h:h + 1] for cc, h in items}
    g_last = {(cc, h): G[cc][C - 1:C, h:h + 1] for cc, h in items}
    decay = {(cc, h): jnp.exp(jnp.where(incl, gc[cc, h] - GT[cc][h:h + 1, :], NEG_BIG)) for cc, h in items}
    eg = {(cc, h): exp_g[cc][:, h:h + 1] for cc, h in items}
    kb ={it: k[it].astype(BF16) for it in items}
    kq = {it: lax.dot_general(jnp.concatenate([kb[it], q[it].astype(BF16)], axis=0), kb[it],
                              (((1,), (1,)), ((), ())), preferred_element_type=F32) for it in items}
    a = {it: jnp.where(strict, beta[it] * kq[it][:C] * decay[it], 0.0) for it in items}
    t_inv = _unit_lower_inverse(a, row, col)
    rhs = {(cc, h): jnp.concatenate([beta[cc, h] * v[cc, h], beta_exp_g[cc][:, h:h + 1] * k[cc, h]],
                                    axis=1).astype(BF16) for cc, h in items}
    sol = {it: _dot16(t_inv[it].astype(BF16), rhs[it]) for it in items}
    qk = {it: (kq[it][C:] * decay[it]).astype(BF16) for it in items}
    wq = {it: jnp.concatenate([sol[it][:, HEAD_DIM:], q[it] * eg[it]], axis=0).astype(BF16) for it in items}
    k_dec = {(cc, h): (k[cc, h] * exp_rest[cc][:, h:h + 1]).astype(BF16) for cc, h in items}

    heads = range(N_HEADS_B)
    state = {h: state_ref[h] for h in heads}
    for cc in chunks:
        ws = {h: _dot16(wq[cc, h], state[h].astype(BF16)) for h in heads}
        u = {h: (sol[cc, h][:, :HEAD_DIM] - ws[h][:C]).astype(BF16) for h in heads}
        o = {h: ws[h][C:] + _dot16(qk[cc, h], u[h]) for h in heads}
        state = {h: jnp.exp(g_last[cc, h]) * state[h] + lax.dot_general(
            k_dec[cc, h], u[h], (((0,), (0,)), ((), ())), preferred_element_type=F32) for h in heads}
        for h in heads:
            oh = o[h] * lax.rsqrt(jnp.mean(o[h] * o[h], axis=-1, keepdims=True) + RMS_EPS) * onw_ref[...]
            hs = slice(h * HEAD_DIM, (h + 1) * HEAD_DIM)
            y_ref[rows[cc], hs] = (oh * _silu(z_ref[rows[cc], hs])).astype(y_ref.dtype)
    for h in heads:
        state_ref[h] = state[h]


def _gated_deltanet(proj_b, conv_w, a_log, dt_bias, o_norm_w, batch, seq):
    C = GDN_STEP_CHUNKS * GDN_CHUNK
    wb = proj_b.shape[1]
    pb = proj_b.reshape(batch, seq, wb)
    ba_blk = (wb - 2 * LANES) // (2 * LANES)
    pad = lambda t: jnp.pad(t.astype(F32), (0, LANES - t.shape[0])).reshape(1, LANES)
    const = lambda shape: pl.BlockSpec(shape, lambda b, c: (0,) * len(shape))
    y = pl.pallas_call(
        _gdn_kernel,
        name="gated_deltanet",
        grid=(batch, seq // C),
        in_specs=[pl.BlockSpec((None, C, 3 * B_W), lambda b, c: (b, c, 0)),
                  pl.BlockSpec((None, SUBLANES, 3 * B_W),
                               lambda b, c: (b, jnp.maximum(c * (C // SUBLANES) - 1, 0), 0)),
                  pl.BlockSpec((None, C, B_W), lambda b, c: (b, c, 3)),
                  pl.BlockSpec((None, C, 2 * LANES), lambda b, c: (b, c, ba_blk)),
                  const((CONV_WIDTH, 3 * B_W)), const((1, LANES)), const((1, LANES)), const((1, LANES))],
        out_specs=pl.BlockSpec((None, C, B_W), lambda b, c: (b, c, 0)),
        out_shape=jax.ShapeDtypeStruct((batch, seq, B_W), BF16),
        scratch_shapes=[pltpu.VMEM((N_HEADS_B, HEAD_DIM, HEAD_DIM), F32)],
        compiler_params=_cparams(("parallel", "arbitrary")),
    )(pb, pb, pb, pb, conv_w.astype(F32), pad(a_log), pad(dt_bias), o_norm_w.astype(F32).reshape(1, LANES))
    return y.reshape(batch * seq, B_W)


def _mix_out_kernel(o0, o1, o2, l0, l1, l2, yb_ref, ga_ref, gb_ref, x_ref, woa_ref, wob_ref, wout_ref,
                    g_ref, b_ref, *rest, with_router):
    if with_router:
        wr_ref, out_ref, logits_ref, *scratch = rest
    else:
        out_ref, *scratch = rest

    def token_order(ref, scr):
        dilation, rows, width = ref.shape
        if dilation == 1:
            return ref[0]
        planes = []
        for t in range(width // LANES):
            for r in range(dilation):
                scr[t, pl.ds(r, rows, stride=dilation), :] = ref[r, :, t * LANES:(t + 1) * LANES]
            planes.append(scr[t])
        return jnp.concatenate(planes, axis=1)

    outs = (token_order(o0, None), token_order(o1, scratch[0]), token_order(o2, scratch[1]))
    lses = (token_order(l0, None), token_order(l1, scratch[2]), token_order(l2, scratch[3]))
    m = jnp.maximum(jnp.maximum(lses[0], lses[1]), lses[2])
    es = [jnp.exp(t - m) for t in lses]
    inv = 1.0 / (es[0] + es[1] + es[2])
    wgt = [e * inv for e in es]
    ya = []
    for h in range(HEADS_PER_GROUP):
        hs = slice(h * HEAD_DIM, (h + 1) * HEAD_DIM)
        ya.append(wgt[0][:, h:h + 1] * outs[0][:, hs] + wgt[1][:, h:h + 1] * outs[1][:, hs]
                  + wgt[2][:, h:h + 1] * outs[2][:, hs])
    ya = jnp.concatenate(ya, axis=1)
    pa = _bdot(ya, woa_ref[...])
    pb = _bdot(yb_ref[...], wob_ref[...])
    merged = _sigmoid(ga_ref[...]) * pa + _sigmoid(gb_ref[...]) * pb
    mix = _bdot(merged, wout_ref[...])
    out = _layer_norm(ALPHA * x_ref[...] + mix, g_ref[...], b_ref[...])
    out_ref[...] = out
    if with_router:
        logits_ref[...] = _bdot(out, wr_ref[...])


def _mix_out(outs, lses, yb, proj_b, x, w_oa, w_ob, w_out, ln_g, ln_b, w_router=None, tm=OUT_TM):
    n = x.shape[0]
    with_router = w_router is not None
    seq = outs[0].shape[1] * outs[0].shape[2]
    tpb = seq // tm
    rowblk = lambda w, cb=0: pl.BlockSpec((tm, w), lambda i: (i, cb))
    const = lambda a: pl.BlockSpec(a.shape, lambda i: (0, 0), pipeline_mode=pl.Buffered(1))

    def grouped(a):
        d, w = a.shape[1], a.shape[3]
        return pl.BlockSpec((None, d, tm // d, w), lambda i: (i // tpb, 0, i % tpb, 0))

    wa, wb, wo = w_oa.astype(BF16), w_ob.astype(BF16), w_out.astype(BF16)
    g, b = ln_g.reshape(1, D_MODEL), ln_b.reshape(1, D_MODEL)
    router = [w_router] if with_router else []
    out_specs = [rowblk(D_MODEL)] + ([rowblk(LANES)] if with_router else [])
    out_shape = [jax.ShapeDtypeStruct((n, D_MODEL), F32)] + (
        [jax.ShapeDtypeStruct((n, LANES), F32)] if with_router else [])
    res = pl.pallas_call(
        functools.partial(_mix_out_kernel, with_router=with_router),
        name="mix_out_ln",
        grid=(n // tm,),
        in_specs=[grouped(a) for a in (*outs, *lses)]
        + [rowblk(B_W), rowblk(D_MODEL, 4), rowblk(D_MODEL, 5), rowblk(D_MODEL),
           const(wa), const(wb), const(wo), const(g), const(b)] + [const(a) for a in router],
        out_specs=out_specs,
        out_shape=out_shape,
        scratch_shapes=[pltpu.VMEM((A_GROUP_W // LANES, tm, LANES), F32)] * 2 + [pltpu.VMEM((1, tm, LANES), F32)] * 2,
        compiler_params=_cparams(("parallel",)),
    )(*outs, *lses, yb, proj_b, proj_b, x, wa, wb, wo, g, b, *router)
    return tuple(res) if with_router else res[0]


def _ffn_kernel(x_ref, wg_ref, wu_ref, wd_ref, g_ref, b_ref, out_ref):
    x = x_ref[...]
    xb = x.astype(BF16)
    dff = wg_ref.shape[1]
    acc = None
    for c0 in range(0, dff, FFN_TF):
        c1 = min(c0 + FFN_TF, dff)
        gate = jnp.dot(xb, wg_ref[:, c0:c1], preferred_element_type=F32)
        up = jnp.dot(xb, wu_ref[:, c0:c1], preferred_element_type=F32)
        part = _bdot(_silu(gate) * up, wd_ref[c0:c1, :])
        acc = part if acc is None else acc + part
    out_ref[...] = _layer_norm(ALPHA * x + acc, g_ref[...], b_ref[...])


def _dense_ffn(x, w_gate, w_up, w_down, ln_g, ln_b, tm=FFN_TM):
    n = x.shape[0]
    g, b = ln_g.reshape(1, D_MODEL), ln_b.reshape(1, D_MODEL)
    resident = lambda a: pl.BlockSpec(a.shape, lambda i: (0, 0), pipeline_mode=pl.Buffered(1))
    wg, wu, wd = w_gate.astype(BF16), w_up.astype(BF16), w_down.astype(BF16)
    return pl.pallas_call(
        _ffn_kernel,
        name="dense_ffn_ln",
        grid=(n // tm,),
        in_specs=[pl.BlockSpec((tm, D_MODEL), lambda i: (i, 0)),
                  resident(wg), resident(wu), resident(wd), resident(g), resident(b)],
        out_specs=pl.BlockSpec((tm, D_MODEL), lambda i: (i, 0)),
        out_shape=jax.ShapeDtypeStruct((n, D_MODEL), F32),
        compiler_params=_cparams(("parallel",)),
    )(x, wg, wu, wd, g, b)


def _gather_rows(idx_ref, src_hbm, dst, sem, count):
    def body(i, carry):
        pltpu.make_async_copy(src_hbm.at[pl.ds(idx_ref[0, i], 1)], dst.at[pl.ds(i, 1)], sem).start()
        return carry
    lax.fori_loop(0, count, body, 0, unroll=8)


def _wait_rows(src_hbm, dst, sem, count):
    pltpu.make_async_copy(src_hbm.at[pl.ds(0, count)], dst, sem).wait()


def _dispatch_kernel(dest_ref, x_ref, xs_in, xs_out, sem):
    del xs_in
    tm = x_ref.shape[0]

    def body(i, carry):
        for k in range(TOP_K):
            pltpu.make_async_copy(x_ref.at[pl.ds(i, 1)], xs_out.at[pl.ds(dest_ref[0, TOP_K * i + k], 1)],
                                  sem.at[0]).start()
        return carry
    lax.fori_loop(0, tm, body, 0, unroll=4)
    for _ in range(TOP_K):
        pltpu.make_async_copy(x_ref, xs_out.at[pl.ds(0, tm)], sem.at[0]).wait()


def _moe_dispatch(x, dest, n_rows, tm=DSP_TM):
    n = x.shape[0]
    dest3 = dest.reshape(n // tm, 1, TOP_K * tm)
    return pl.pallas_call(
        _dispatch_kernel,
        name="moe_dispatch",
        grid=(n // tm,),
        in_specs=[pl.BlockSpec((None, 1, TOP_K * tm), lambda i: (i, 0, 0), memory_space=pltpu.SMEM),
                  pl.BlockSpec((tm, D_MODEL), lambda i: (i, 0)),
                  pl.BlockSpec(memory_space=pl.ANY)],
        out_specs=pl.BlockSpec(memory_space=pl.ANY),
        out_shape=jax.ShapeDtypeStruct((n_rows, D_MODEL), F32),
        scratch_shapes=[pltpu.SemaphoreType.DMA((1,))],
        input_output_aliases={2: 0},
        compiler_params=_cparams(("arbitrary",), disable_bounds_checks=True),
    )(dest3, x, jnp.zeros((n_rows, D_MODEL), F32))


def _moe_kernel(meta_ref, x_ref, wg_ref, wu_ref, wd_ref, y_ref, xb_ref, acc_ref):
    b = pl.program_id(0)
    f = pl.program_id(1)
    nblk = pl.num_programs(0)
    used = b < meta_ref[nblk]

    @pl.when(used)
    def _():
        @pl.when(f == 0)
        def _():
            xb_ref[...] = x_ref[...].astype(BF16)
            acc_ref[...] = jnp.zeros_like(acc_ref)

        xb = xb_ref[...]
        gate = jnp.dot(xb, wg_ref[...], preferred_element_type=F32)
        up = jnp.dot(xb, wu_ref[...], preferred_element_type=F32)
        acc_ref[...] += _bdot(_silu(gate) * up, wd_ref[...])

    @pl.when(f == pl.num_programs(1) - 1)
    def _():
        @pl.when(used)
        def _():
            y_ref[...] = acc_ref[...]

        @pl.when(jnp.logical_not(used))
        def _():
            y_ref[...] = jnp.zeros_like(y_ref)


def _moe_experts(xs, meta, w_gate, w_up, w_down, tb=MOE_TB, tf=MOE_TF):
    nblk = xs.shape[0] // tb
    dffe = w_gate.shape[2]
    grid_spec = pltpu.PrefetchScalarGridSpec(
        num_scalar_prefetch=1,
        grid=(nblk, dffe // tf),
        in_specs=[pl.BlockSpec((tb, D_MODEL), lambda b, f, m: (b, 0)),
                  pl.BlockSpec((None, D_MODEL, tf), lambda b, f, m: (m[b], 0, f)),
                  pl.BlockSpec((None, D_MODEL, tf), lambda b, f, m: (m[b], 0, f)),
                  pl.BlockSpec((None, tf, D_MODEL), lambda b, f, m: (m[b], f, 0))],
        out_specs=pl.BlockSpec((tb, D_MODEL), lambda b, f, m: (b, 0)),
        scratch_shapes=[pltpu.VMEM((tb, D_MODEL), BF16), pltpu.VMEM((tb, D_MODEL), F32)],
    )
    return pl.pallas_call(
        _moe_kernel,
        name="moe_experts",
        grid_spec=grid_spec,
        out_shape=jax.ShapeDtypeStruct((nblk * tb, D_MODEL), F32),
        compiler_params=_cparams(("parallel", "arbitrary")),
    )(meta, xs, w_gate.astype(BF16), w_up.astype(BF16), w_down.astype(BF16))


def _combine_kernel(pos0_ref, posn_ref, y_hbm, gates_ref, x_ref, g_ref, b_ref, out_ref, ybuf, sem):
    i = pl.program_id(0)
    tm = out_ref.shape[0]
    slot = i % 2

    @pl.when(i == 0)
    def _():
        _gather_rows(pos0_ref, y_hbm, ybuf.at[0], sem.at[0], TOP_K * tm)

    _wait_rows(y_hbm, ybuf.at[slot], sem.at[slot], TOP_K * tm)

    @pl.when(i + 1 < pl.num_programs(0))
    def _():
        _gather_rows(posn_ref, y_hbm, ybuf.at[1 - slot], sem.at[1 - slot], TOP_K * tm)

    gates = gates_ref[...]
    f = gates[:, 0:1] * ybuf[slot, :tm] + gates[:, 1:2] * ybuf[slot, tm:]
    out_ref[...] = _layer_norm(ALPHA * x_ref[...] + f, g_ref[...], b_ref[...])


def _moe_combine(y, pos, gates, x, ln_g, ln_b, tm=CMB_TM):
    n = x.shape[0]
    nt = n // tm
    pos3 = pos.reshape(nt, tm, TOP_K).transpose(0, 2, 1).reshape(nt, 1, TOP_K * tm)
    g, b = ln_g.reshape(1, D_MODEL), ln_b.reshape(1, D_MODEL)
    smem_blk = lambda imap: pl.BlockSpec((None, 1, TOP_K * tm), imap, memory_space=pltpu.SMEM)
    return pl.pallas_call(
        _combine_kernel,
        name="moe_combine_ln",
        grid=(nt,),
        in_specs=[smem_blk(lambda i: (0, 0, 0)),
                  smem_blk(lambda i: (jnp.minimum(i + 1, nt - 1), 0, 0)),
                  pl.BlockSpec(memory_space=pl.ANY),
                  pl.BlockSpec((tm, TOP_K), lambda i: (i, 0)),
                  pl.BlockSpec((tm, D_MODEL), lambda i: (i, 0)),
                  pl.BlockSpec((1, D_MODEL), lambda i: (0, 0)),
                  pl.BlockSpec((1, D_MODEL), lambda i: (0, 0))],
        out_specs=pl.BlockSpec((tm, D_MODEL), lambda i: (i, 0)),
        out_shape=jax.ShapeDtypeStruct((n, D_MODEL), F32),
        scratch_shapes=[pltpu.VMEM((2, TOP_K * tm, D_MODEL), F32), pltpu.SemaphoreType.DMA((2,))],
        compiler_params=_cparams(("arbitrary",), disable_bounds_checks=True),
    )(pos3, pos3, y, gates, x, g, b)


def _moe_routing(logits, tb):
    n = logits.shape[0]
    top_logit, top_idx = lax.top_k(logits, TOP_K)
    gates = jax.nn.softmax(top_logit, axis=-1)
    na = n * TOP_K
    e_flat = top_idx.reshape(-1).astype(jnp.int32)
    onehot = (e_flat[None, :] == jnp.arange(N_EXPERTS, dtype=jnp.int32)[:, None]).astype(jnp.int32)
    running = jnp.cumsum(onehot, axis=1)
    counts = running[:, -1]
    padded = (counts + tb - 1) // tb * tb
    pend = jnp.cumsum(padded)
    pstart = pend - padded
    dest = jnp.sum(onehot * (pstart[:, None] + running - 1), axis=0).reshape(n, TOP_K)
    nblk = -(-na // tb) + N_EXPERTS
    block_expert = jnp.minimum(jnp.searchsorted(pend, jnp.arange(nblk, dtype=jnp.int32) * tb, side='right'),
                               N_EXPERTS - 1).astype(jnp.int32)
    meta = jnp.concatenate([block_expert, (pend[-1:] // tb).astype(jnp.int32)])
    return dest.astype(jnp.int32), gates, meta, nblk


def _router_weight(w_router):
    return jnp.pad(w_router, ((0, 0), (0, LANES - N_EXPERTS))).astype(BF16)


def _moe_ffn(x, logits, w_gate, w_up, w_down, ln_g, ln_b):
    dest, gates, meta, nblk = _moe_routing(logits[:, :N_EXPERTS], MOE_TB)
    xs = _moe_dispatch(x, dest, nblk * MOE_TB)
    y = _moe_experts(xs, meta, w_gate, w_up, w_down)
    return _moe_combine(y, dest, gates, x, ln_g, ln_b)


def _split_w_in(w):
    a_end = 3 * A_QKV_W
    bz_end = a_end + 3 * B_W + B_W
    w_groups = [jnp.concatenate([w[:, s * A_QKV_W + gi * A_GROUP_W:s * A_QKV_W + (gi + 1) * A_GROUP_W]
                                 for s in range(3)], axis=1).astype(BF16) for gi in range(N_GROUPS)]
    zpad = jnp.zeros((w.shape[0], LANES - N_HEADS_B), w.dtype)
    w_b = jnp.concatenate([w[:, a_end:bz_end], w[:, bz_end + 2 * N_HEADS_B:],
                           w[:, bz_end:bz_end + N_HEADS_B], zpad,
                           w[:, bz_end + N_HEADS_B:bz_end + 2 * N_HEADS_B], zpad], axis=1)
    return w_groups, w_b.astype(BF16)


def _hybrid_layer(x, batch, seq, rel_bias, w_in, conv_w, a_log, dt_bias, o_norm_w, w_oa, w_ob, w_out, ln_g, ln_b,
                  w_router=None):
    w_groups, w_b = _split_w_in(w_in)
    proj_b = _matmul(x, w_b, F32, "in_proj_gdn_gates")
    outs, lses = [], []
    for gi, (_, dilation) in enumerate(DSWA_PATTERNS):
        qkv = _in_proj_strided(x, w_groups[gi], dilation, batch, seq)
        o, lse = _dswa_group(qkv, _band_bias(rel_bias, gi, dilation), dilation)
        outs.append(o)
        lses.append(lse)
    yb = _gated_deltanet(proj_b, conv_w, a_log, dt_bias, o_norm_w, batch, seq)
    return _mix_out(outs, lses, yb, proj_b, x, w_oa, w_ob, w_out, ln_g, ln_b, w_router)


def kernel(x, rel_bias, w_in, conv_w, a_log, dt_bias, o_norm_w, w_oa, w_ob, w_out, ln1_g, ln1_b,
           ffn_w_gate, ffn_w_up, ffn_w_down, moe_router, moe_w_gate, moe_w_up, moe_w_down, ln2_g, ln2_b):
    batch, seq, d = x.shape
    h = x.reshape(batch * seq, d)
    for layer in range(DEPTH):
        j = layer // 2
        is_moe = layer % 2 == 1
        h = _hybrid_layer(h, batch, seq, rel_bias, w_in[layer], conv_w[layer], a_log[layer], dt_bias[layer],
                          o_norm_w[layer], w_oa[layer], w_ob[layer], w_out[layer], ln1_g[layer], ln1_b[layer],
                          _router_weight(moe_router[j]) if is_moe else None)
        if is_moe:
            h, logits = h
            h = _moe_ffn(h, logits, moe_w_gate[j], moe_w_up[j], moe_w_down[j], ln2_g[layer], ln2_b[layer])
        else:
            h = _dense_ffn(h, ffn_w_gate[j], ffn_w_up[j], ffn_w_down[j], ln2_g[layer], ln2_b[layer])
    return h.reshape(batch, seq, d)
```

```python
import functools
import math

import jax
import jax.numpy as jnp
from jax import lax
from jax.experimental import pallas as pl
from jax.experimental.pallas import tpu as pltpu

F32 = jnp.float32
BF16 = jnp.bfloat16

D_MODEL = 1024
DEPTH = 2
DSWA_PATTERNS = ((128, 1), (512, 4), (2048, 16))
N_GROUPS = 3
HEADS_PER_GROUP = 4
HEAD_DIM = 128
A_QKV_W = N_GROUPS * HEADS_PER_GROUP * HEAD_DIM
A_GROUP_W = HEADS_PER_GROUP * HEAD_DIM
NUM_BUCKETS = 32
MAX_DISTANCE = 2048
N_HEADS_B = 8
B_W = N_HEADS_B * HEAD_DIM
CONV_WIDTH = 4
N_EXPERTS = 8
TOP_K = 2
ALPHA = (2 * DEPTH) ** 0.25
LN_EPS = 1e-5
RMS_EPS = 1e-6

LANES = 128
SUBLANES = 8
VMEM_LIMIT = 56 * 1024 * 1024

ATT_BLK = 128
DSWA_ROWS = 2048
GDN_CHUNK = 128
GDN_STEP_CHUNKS = 2
MM_TM = 512
MM_TN = 3200
PROJ_A_TM = 1024
OUT_TM = 512
FFN_TM = 512
FFN_TF = 1024
MOE_TB = 512
MOE_TF = 1792
CMB_TM = 512
DSP_TM = 512

NEG_BIG = -1e30


def _cparams(sem, vmem=VMEM_LIMIT, **kw):
    return pltpu.CompilerParams(dimension_semantics=sem, vmem_limit_bytes=vmem, **kw)


def _bdot(a, b):
    return jnp.dot(a.astype(BF16), b.astype(BF16), preferred_element_type=F32)


def _bdot_nt(a, b):
    return lax.dot_general(a.astype(BF16), b.astype(BF16), (((1,), (1,)), ((), ())),
                           preferred_element_type=F32)


def _bdot_tn(a, b):
    return lax.dot_general(a.astype(BF16), b.astype(BF16), (((0,), (0,)), ((), ())),
                           preferred_element_type=F32)


def _sigmoid(v):
    return 1.0 / (1.0 + jnp.exp(-v))


def _silu(v):
    return v * _sigmoid(v)


def _layer_norm(v, g, b):
    mu = jnp.mean(v, axis=-1, keepdims=True)
    d = v - mu
    var = jnp.mean(d * d, axis=-1, keepdims=True)
    return d * lax.rsqrt(var + LN_EPS) * g + b


def _mm_kernel(x_ref, w_ref, o_ref):
    o_ref[...] = _bdot(x_ref[...], w_ref[...]).astype(o_ref.dtype)


def _matmul(x, w, out_dtype, name, tm=MM_TM, tn=MM_TN):
    m, k = x.shape
    n = w.shape[1]
    tn = min(tn, n)
    return pl.pallas_call(
        _mm_kernel,
        name=name,
        grid=(n // tn, m // tm),
        in_specs=[pl.BlockSpec((tm, k), lambda j, i: (i, 0)),
                  pl.BlockSpec((k, tn), lambda j, i: (0, j))],
        out_specs=pl.BlockSpec((tm, tn), lambda j, i: (i, j)),
        out_shape=jax.ShapeDtypeStruct((m, n), out_dtype),
        compiler_params=_cparams(("parallel", "parallel")),
    )(x, w)


def _dswa_kernel(q_ref, kc_ref, kp_ref, vc_ref, vp_ref, bias0_ref, bias_ref, o_ref, lse_ref, *, tq):
    nqb = tq // ATT_BLK
    lane = lax.broadcasted_iota(jnp.int32, (ATT_BLK, LANES), 1)
    scale = HEAD_DIM ** -0.5
    exp2_scale = scale * math.log2(math.e)
    for ri in range(q_ref.shape[0]):
        q = q_ref[ri]
        kwin = jnp.concatenate([kp_ref[ri], kc_ref[ri]], axis=0)
        vwin = jnp.concatenate([vp_ref[ri], vc_ref[ri]], axis=0)
        for c in range(nqb):
            b_ref = bias0_ref if c == 0 else bias_ref
            rows = slice(c * ATT_BLK, (c + 1) * ATT_BLK)
            lse_tile = jnp.zeros((ATT_BLK, LANES), F32)
            for h in range(HEADS_PER_GROUP):
                hs = slice(h * HEAD_DIM, (h + 1) * HEAD_DIM)
                kh = kwin[c * ATT_BLK:(c + 2) * ATT_BLK, hs]
                vh = vwin[c * ATT_BLK:(c + 2) * ATT_BLK, hs]
                t = _bdot_nt(q[rows, hs], kh) + b_ref[h]
                m = jnp.max(t, axis=-1, keepdims=True)
                p = jnp.exp2((t - m) * exp2_scale)
                l = jnp.sum(p, axis=-1, keepdims=True)
                o_ref[ri, rows, hs] = _bdot(p, vh) / l
                lse_tile = jnp.where(lane == h, m * scale + jnp.log(l), lse_tile)
            lse_ref[ri, rows, :] = lse_tile


def _in_proj_strided_kernel(x_ref, w_ref, o_ref, *scratch, dilation):
    res = _bdot(x_ref[...], w_ref[...])
    if dilation == 1:
        o_ref[0] = res.astype(o_ref.dtype)
    else:
        res_ref, = scratch
        rows = res.shape[0] // dilation
        for t in range(res.shape[1] // LANES):
            ls = slice(t * LANES, (t + 1) * LANES)
            res_ref[t] = res[:, ls]
            for r in range(dilation):
                o_ref[r, :, ls] = res_ref[t, pl.ds(r, rows, stride=dilation), :].astype(o_ref.dtype)


def _in_proj_strided(x, w, dilation, batch, seq, tm=PROJ_A_TM):
    k = x.shape[1]
    wn = tn = w.shape[1]
    tpb = seq // tm
    scratch = [] if dilation == 1 else [pltpu.VMEM((tn // LANES, tm, LANES), F32)]
    return pl.pallas_call(
        functools.partial(_in_proj_strided_kernel, dilation=dilation),
        name=f"in_proj_attn_d{dilation}",
        grid=(batch * tpb, wn // tn),
        in_specs=[pl.BlockSpec((tm, k), lambda i, j: (i, 0)),
                  pl.BlockSpec((k, tn), lambda i, j: (0, j))],
        out_specs=pl.BlockSpec((None, dilation, tm // dilation, tn), lambda i, j: (i // tpb, 0, i % tpb, j)),
        out_shape=jax.ShapeDtypeStruct((batch, dilation, seq // dilation, wn), BF16),
        scratch_shapes=scratch,
        compiler_params=_cparams(("parallel", "parallel")),
    )(x, w)


def _dswa_group(qkv, bias, dilation):
    batch, _, n, _ = qkv.shape
    tq = min(DSWA_ROWS, n)
    rr = DSWA_ROWS // tq
    nqb = tq // ATT_BLK

    def cur(off):
        return pl.BlockSpec((None, rr, tq, A_GROUP_W), lambda b, r, j: (b, r, j, off))

    def prev(off):
        return pl.BlockSpec((None, rr, ATT_BLK, A_GROUP_W),
                            lambda b, r, j: (b, r, jnp.maximum(j * nqb - 1, 0), off))

    return pl.pallas_call(
        functools.partial(_dswa_kernel, tq=tq),
        name=f"dswa_d{dilation}",
        grid=(batch, dilation // rr, n // tq),
        in_specs=[cur(0), cur(1), prev(1), cur(2), prev(2),
                  pl.BlockSpec((None, HEADS_PER_GROUP, ATT_BLK, 2 * ATT_BLK),
                               lambda b, r, j: (jnp.minimum(j, 1), 0, 0, 0)),
                  pl.BlockSpec((None, HEADS_PER_GROUP, ATT_BLK, 2 * ATT_BLK), lambda b, r, j: (1, 0, 0, 0))],
        out_specs=[pl.BlockSpec((None, rr, tq, A_GROUP_W), lambda b, r, j: (b, r, j, 0)),
                   pl.BlockSpec((None, rr, tq, LANES), lambda b, r, j: (b, r, j, 0))],
        out_shape=[jax.ShapeDtypeStruct((batch, dilation, n, A_GROUP_W), F32),
                   jax.ShapeDtypeStruct((batch, dilation, n, LANES), F32)],
        compiler_params=_cparams(("parallel", "parallel", "parallel")),
    )(qkv, qkv, qkv, qkv, qkv, bias, bias)


def _t5_causal_bucket(dist):
    num_exact = NUM_BUCKETS // 2
    d = jnp.maximum(dist, 1).astype(F32)
    large = num_exact + (jnp.log(d / num_exact) / math.log(MAX_DISTANCE / num_exact)
                         * (NUM_BUCKETS - num_exact)).astype(jnp.int32)
    large = jnp.minimum(large, NUM_BUCKETS - 1)
    return jnp.where(dist < num_exact, dist, large)


def _band_bias(rel_bias, gi, dilation):
    qi = jnp.arange(ATT_BLK)[:, None] + ATT_BLK
    kj = jnp.arange(2 * ATT_BLK)[None, :]
    band = (qi - kj >= 0) & (qi - kj <= ATT_BLK)
    delta = jnp.maximum(qi - kj, 0) * dilation
    table = rel_bias[:, gi * HEADS_PER_GROUP:(gi + 1) * HEADS_PER_GROUP].astype(F32)
    onehot = (_t5_causal_bucket(delta)[..., None] == jnp.arange(NUM_BUCKETS)).astype(F32)
    bias = jnp.einsum('qkn,nh->hqk', onehot, table, precision=lax.Precision.HIGHEST) / (HEAD_DIM ** -0.5)
    return jnp.stack([jnp.where(band & (kj >= ATT_BLK), bias, NEG_BIG), jnp.where(band, bias, NEG_BIG)])


def _dot16(a, b):
    return jnp.dot(a, b, preferred_element_type=F32)


def _unit_lower_inverse(a, row, col):
    base = 16
    heads = list(a)
    eye = (row == col).astype(F32)
    same = lambda sz: (row // sz) == (col // sz)
    blk = same(base)
    a_d = {h: jnp.where(blk, a[h], 0.0) for h in heads}
    t = {h: eye - a_d[h] for h in heads}
    p = {h: a_d[h].astype(BF16) for h in heads}
    for _ in range(3):
        p = {h: _dot16(p[h], p[h]).astype(BF16) for h in heads}
        t = {h: t[h] + _dot16(t[h].astype(BF16), p[h]) for h in heads}
    sz = 2 * base
    while sz <= GDN_CHUNK:
        off = same(sz) & jnp.logical_not(same(sz // 2))
        tb = {h: t[h].astype(BF16) for h in heads}
        m = {h: _dot16(jnp.where(off, a[h], 0.0).astype(BF16), tb[h]).astype(BF16) for h in heads}
        t = {h: t[h] - _dot16(tb[h], m[h]) for h in heads}
        sz *= 2
    return t


def _gdn_kernel(x_ref, halo_ref, z_ref, ba_ref, cw_ref, alog_ref, dtb_ref, onw_ref, y_ref, state_ref):
    c = pl.program_id(1)

    @pl.when(c == 0)
    def _():
        state_ref[...] = jnp.zeros_like(state_ref)

    C = GDN_CHUNK
    chunks = range(x_ref.shape[0] // C)
    rows = {cc: slice(cc * C, (cc + 1) * C) for cc in chunks}
    keep_halo = (c > 0).astype(F32)
    row = lax.broadcasted_iota(jnp.int32, (C, C), 0)
    col = lax.broadcasted_iota(jnp.int32, (C, C), 1)
    incl = row >= col
    strict = row > col
    ones_l = incl.astype(BF16)

    def log_decay_cumsum(cc):
        ba = ba_ref[rows[cc], :]
        a_in = ba[:, LANES:] + dtb_ref[...]
        softplus = jnp.maximum(a_in, 0.0) + jnp.log(1.0 + jnp.exp(-jnp.abs(a_in)))
        g_all = -jnp.exp(alog_ref[...]) * softplus
        g1 = g_all.astype(BF16)
        r1 = g_all - g1.astype(F32)
        g2 = r1.astype(BF16)
        g3 = (r1 - g2.astype(F32)).astype(BF16)
        return (jnp.dot(ones_l, g1, preferred_element_type=F32) + jnp.dot(ones_l, g2, preferred_element_type=F32)
                + jnp.dot(ones_l, g3, preferred_element_type=F32))

    beta_all = {cc: _sigmoid(ba_ref[rows[cc], :LANES]) for cc in chunks}
    G = {cc: log_decay_cumsum(cc) for cc in chunks}
    GT = {cc: G[cc].T for cc in chunks}
    exp_g = {cc: jnp.exp(G[cc]) for cc in chunks}
    beta_exp_g = {cc: beta_all[cc] * exp_g[cc] for cc in chunks}
    exp_rest = {cc: jnp.exp(G[cc][C - 1:C, :] - G[cc]) for cc in chunks}

    def conv_silu(cc, col0):
        cs = slice(col0, col0 + HEAD_DIM)
        xc = x_ref[rows[cc], cs]
        halo = halo_ref[:, cs] * keep_halo if cc == 0 else x_ref[cc * C - SUBLANES:cc * C, cs]
        xf = jnp.concatenate([halo, xc], axis=0)
        w = cw_ref[:, cs]
        y = w[CONV_WIDTH - 1:CONV_WIDTH] * xc
        for i in range(CONV_WIDTH - 1):
            off = SUBLANES - (CONV_WIDTH - 1) + i
            y = y + w[i:i + 1] * xf[off:off + C]
        return _silu(y)

    def l2norm(t, scale=1.0):
        return t * (lax.rsqrt(jnp.sum(t * t, axis=-1, keepdims=True) + RMS_EPS) * scale)

    items = [(cc, h) for cc in chunks for h in range(N_HEADS_B)]
    q = {(cc, h): l2norm(conv_silu(cc, h * HEAD_DIM), HEAD_DIM ** -0.5) for cc, h in items}
    k = {(cc, h): l2norm(conv_silu(cc, B_W + h * HEAD_DIM)) for cc, h in items}
    v = {(cc, h): conv_silu(cc, 2 * B_W + h * HEAD_DIM) for cc, h in items}
    beta = {(cc, h): beta_all[cc][:, h:h + 1] for cc, h in items}
    gc = {(cc, h): G[cc][:, h:h + 1] for cc, h in items}
    g_last = {(cc, h): G[cc][C - 1:C, h:h + 1] for cc, h in items}
    decay = {(cc, h): jnp.exp(jnp.where(incl, gc[cc, h] - GT[cc][h:h + 1, :], NEG_BIG)) for cc, h in items}
    eg = {(cc, h): exp_g[cc][:, h:h + 1] for cc, h in items}
    kb ={it: k[it].astype(BF16) for it in items}
    kq = {it: lax.dot_general(jnp.concatenate([kb[it], q[it].astype(BF16)], axis=0), kb[it],
                              (((1,), (1,)), ((), ())), preferred_element_type=F32) for it in items}
    a = {it: jnp.where(strict, beta[it] * kq[it][:C] * decay[it], 0.0) for it in items}
    t_inv = _unit_lower_inverse(a, row, col)
    rhs = {(cc, h): jnp.concatenate([beta[cc, h] * v[cc, h], beta_exp_g[cc][:, h:h + 1] * k[cc, h]],
                                    axis=1).astype(BF16) for cc, h in items}
    sol = {it: _dot16(t_inv[it].astype(BF16), rhs[it]) for it in items}
    qk = {it: (kq[it][C:] * decay[it]).astype(BF16) for it in items}
    wq = {it: jnp.concatenate([sol[it][:, HEAD_DIM:], q[it] * eg[it]], axis=0).astype(BF16) for it in items}
    k_dec = {(cc, h): (k[cc, h] * exp_rest[cc][:, h:h + 1]).astype(BF16) for cc, h in items}

    heads = range(N_HEADS_B)
    state = {h: state_ref[h] for h in heads}
    for cc in chunks:
        ws = {h: _dot16(wq[cc, h], state[h].astype(BF16)) for h in heads}
        u = {h: (sol[cc, h][:, :HEAD_DIM] - ws[h][:C]).astype(BF16) for h in heads}
        o = {h: ws[h][C:] + _dot16(qk[cc, h], u[h]) for h in heads}
        state = {h: jnp.exp(g_last[cc, h]) * state[h] + lax.dot_general(
            k_dec[cc, h], u[h], (((0,), (0,)), ((), ())), preferred_element_type=F32) for h in heads}
        for h in heads:
            oh = o[h] * lax.rsqrt(jnp.mean(o[h] * o[h], axis=-1, keepdims=True) + RMS_EPS) * onw_ref[...]
            hs = slice(h * HEAD_DIM, (h + 1) * HEAD_DIM)
            y_ref[rows[cc], hs] = (oh * _silu(z_ref[rows[cc], hs])).astype(y_ref.dtype)
    for h in heads:
        state_ref[h] = state[h]


def _gated_deltanet(proj_b, conv_w, a_log, dt_bias, o_norm_w, batch, seq):
    C = GDN_STEP_CHUNKS * GDN_CHUNK
    wb = proj_b.shape[1]
    pb = proj_b.reshape(batch, seq, wb)
    ba_blk = (wb - 2 * LANES) // (2 * LANES)
    pad = lambda t: jnp.pad(t.astype(F32), (0, LANES - t.shape[0])).reshape(1, LANES)
    const = lambda shape: pl.BlockSpec(shape, lambda b, c: (0,) * len(shape))
    y = pl.pallas_call(
        _gdn_kernel,
        name="gated_deltanet",
        grid=(batch, seq // C),
        in_specs=[pl.BlockSpec((None, C, 3 * B_W), lambda b, c: (b, c, 0)),
                  pl.BlockSpec((None, SUBLANES, 3 * B_W),
                               lambda b, c: (b, jnp.maximum(c * (C // SUBLANES) - 1, 0), 0)),
                  pl.BlockSpec((None, C, B_W), lambda b, c: (b, c, 3)),
                  pl.BlockSpec((None, C, 2 * LANES), lambda b, c: (b, c, ba_blk)),
                  const((CONV_WIDTH, 3 * B_W)), const((1, LANES)), const((1, LANES)), const((1, LANES))],
        out_specs=pl.BlockSpec((None, C, B_W), lambda b, c: (b, c, 0)),
        out_shape=jax.ShapeDtypeStruct((batch, seq, B_W), BF16),
        scratch_shapes=[pltpu.VMEM((N_HEADS_B, HEAD_DIM, HEAD_DIM), F32)],
        compiler_params=_cparams(("parallel", "arbitrary")),
    )(pb, pb, pb, pb, conv_w.astype(F32), pad(a_log), pad(dt_bias), o_norm_w.astype(F32).reshape(1, LANES))
    return y.reshape(batch * seq, B_W)


def _mix_out_kernel(o0, o1, o2, l0, l1, l2, yb_ref, ga_ref, gb_ref, x_ref, woa_ref, wob_ref, wout_ref,
                    g_ref, b_ref, *rest, with_router):
    if with_router:
        wr_ref, out_ref, logits_ref, *scratch = rest
    else:
        out_ref, *scratch = rest

    def token_order(ref, scr):
        dilation, rows, width = ref.shape
        if dilation == 1:
            return ref[0]
        planes = []
        for t in range(width // LANES):
            for r in range(dilation):
                scr[t, pl.ds(r, rows, stride=dilation), :] = ref[r, :, t * LANES:(t + 1) * LANES]
            planes.append(scr[t])
        return jnp.concatenate(planes, axis=1)

    outs = (token_order(o0, None), token_order(o1, scratch[0]), token_order(o2, scratch[1]))
    lses = (token_order(l0, None), token_order(l1, scratch[2]), token_order(l2, scratch[3]))
    m = jnp.maximum(jnp.maximum(lses[0], lses[1]), lses[2])
    es = [jnp.exp(t - m) for t in lses]
    inv = 1.0 / (es[0] + es[1] + es[2])
    wgt = [e * inv for e in es]
    ya = []
    for h in range(HEADS_PER_GROUP):
        hs = slice(h * HEAD_DIM, (h + 1) * HEAD_DIM)
        ya.append(wgt[0][:, h:h + 1] * outs[0][:, hs] + wgt[1][:, h:h + 1] * outs[1][:, hs]
                  + wgt[2][:, h:h + 1] * outs[2][:, hs])
    ya = jnp.concatenate(ya, axis=1)
    pa = _bdot(ya, woa_ref[...])
    pb = _bdot(yb_ref[...], wob_ref[...])
    merged = _sigmoid(ga_ref[...]) * pa + _sigmoid(gb_ref[...]) * pb
    mix = _bdot(merged, wout_ref[...])
    out = _layer_norm(ALPHA * x_ref[...] + mix, g_ref[...], b_ref[...])
    out_ref[...] = out
    if with_router:
        logits_ref[...] = _bdot(out, wr_ref[...])


def _mix_out(outs, lses, yb, proj_b, x, w_oa, w_ob, w_out, ln_g, ln_b, w_router=None, tm=OUT_TM):
    n = x.shape[0]
    with_router = w_router is not None
    seq = outs[0].shape[1] * outs[0].shape[2]
    tpb = seq // tm
    rowblk = lambda w, cb=0: pl.BlockSpec((tm, w), lambda i: (i, cb))
    const = lambda a: pl.BlockSpec(a.shape, lambda i: (0, 0), pipeline_mode=pl.Buffered(1))

    def grouped(a):
        d, w = a.shape[1], a.shape[3]
        return pl.BlockSpec((None, d, tm // d, w), lambda i: (i // tpb, 0, i % tpb, 0))

    wa, wb, wo = w_oa.astype(BF16), w_ob.astype(BF16), w_out.astype(BF16)
    g, b = ln_g.reshape(1, D_MODEL), ln_b.reshape(1, D_MODEL)
    router = [w_router] if with_router else []
    out_specs = [rowblk(D_MODEL)] + ([rowblk(LANES)] if with_router else [])
    out_shape = [jax.ShapeDtypeStruct((n, D_MODEL), F32)] + (
        [jax.ShapeDtypeStruct((n, LANES), F32)] if with_router else [])
    res = pl.pallas_call(
        functools.partial(_mix_out_kernel, with_router=with_router),
        name="mix_out_ln",
        grid=(n // tm,),
        in_specs=[grouped(a) for a in (*outs, *lses)]
        + [rowblk(B_W), rowblk(D_MODEL, 4), rowblk(D_MODEL, 5), rowblk(D_MODEL),
           const(wa), const(wb), const(wo), const(g), const(b)] + [const(a) for a in router],
        out_specs=out_specs,
        out_shape=out_shape,
        scratch_shapes=[pltpu.VMEM((A_GROUP_W // LANES, tm, LANES), F32)] * 2 + [pltpu.VMEM((1, tm, LANES), F32)] * 2,
        compiler_params=_cparams(("parallel",)),
    )(*outs, *lses, yb, proj_b, proj_b, x, wa, wb, wo, g, b, *router)
    return tuple(res) if with_router else res[0]


def _ffn_kernel(x_ref, wg_ref, wu_ref, wd_ref, g_ref, b_ref, out_ref):
    x = x_ref[...]
    xb = x.astype(BF16)
    dff = wg_ref.shape[1]
    acc = None
    for c0 in range(0, dff, FFN_TF):
        c1 = min(c0 + FFN_TF, dff)
        gate = jnp.dot(xb, wg_ref[:, c0:c1], preferred_element_type=F32)
        up = jnp.dot(xb, wu_ref[:, c0:c1], preferred_element_type=F32)
        part = _bdot(_silu(gate) * up, wd_ref[c0:c1, :])
        acc = part if acc is None else acc + part
    out_ref[...] = _layer_norm(ALPHA * x + acc, g_ref[...], b_ref[...])


def _dense_ffn(x, w_gate, w_up, w_down, ln_g, ln_b, tm=FFN_TM):
    n = x.shape[0]
    g, b = ln_g.reshape(1, D_MODEL), ln_b.reshape(1, D_MODEL)
    resident = lambda a: pl.BlockSpec(a.shape, lambda i: (0, 0), pipeline_mode=pl.Buffered(1))
    wg, wu, wd = w_gate.astype(BF16), w_up.astype(BF16), w_down.astype(BF16)
    return pl.pallas_call(
        _ffn_kernel,
        name="dense_ffn_ln",
        grid=(n // tm,),
        in_specs=[pl.BlockSpec((tm, D_MODEL), lambda i: (i, 0)),
                  resident(wg), resident(wu), resident(wd), resident(g), resident(b)],
        out_specs=pl.BlockSpec((tm, D_MODEL), lambda i: (i, 0)),
        out_shape=jax.ShapeDtypeStruct((n, D_MODEL), F32),
        compiler_params=_cparams(("parallel",)),
    )(x, wg, wu, wd, g, b)


def _gather_rows(idx_ref, src_hbm, dst, sem, count):
    def body(i, carry):
        pltpu.make_async_copy(src_hbm.at[pl.ds(idx_ref[0, i], 1)], dst.at[pl.ds(i, 1)], sem).start()
        return carry
    lax.fori_loop(0, count, body, 0, unroll=8)


def _wait_rows(src_hbm, dst, sem, count):
    pltpu.make_async_copy(src_hbm.at[pl.ds(0, count)], dst, sem).wait()


def _dispatch_kernel(dest_ref, pad_ref, x_ref, xs_out, zero_ref, sem):
    tm = x_ref.shape[0]

    @pl.when(pl.program_id(0) == 0)
    def _():
        zero_ref[...] = jnp.zeros_like(zero_ref)
        zero_row = lambda r: pltpu.make_async_copy(zero_ref.at[pl.ds(0, 1)], xs_out.at[pl.ds(r, 1)], sem.at[1])
        for e in range(N_EXPERTS):
            lo, hi = pad_ref[0, e], pad_ref[0, N_EXPERTS + e]
            lax.fori_loop(lo, hi, lambda r, c: (zero_row(r).start(), c)[1], 0)
        for e in range(N_EXPERTS):
            lo, hi = pad_ref[0, e], pad_ref[0, N_EXPERTS + e]
            lax.fori_loop(lo, hi, lambda r, c: (zero_row(r).wait(), c)[1], 0)

    def body(i, carry):
        for k in range(TOP_K):
            pltpu.make_async_copy(x_ref.at[pl.ds(i, 1)], xs_out.at[pl.ds(dest_ref[0, TOP_K * i + k], 1)],
                                  sem.at[0]).start()
        return carry
    lax.fori_loop(0, tm, body, 0, unroll=4)
    for _ in range(TOP_K):
        pltpu.make_async_copy(x_ref, xs_out.at[pl.ds(0, tm)], sem.at[0]).wait()


def _moe_dispatch(x, dest, pad_ranges, n_rows, tm=DSP_TM):
    n = x.shape[0]
    dest3 = dest.reshape(n // tm, 1, TOP_K * tm)
    return pl.pallas_call(
        _dispatch_kernel,
        name="moe_dispatch",
        grid=(n // tm,),
        in_specs=[pl.BlockSpec((None, 1, TOP_K * tm), lambda i: (i, 0, 0), memory_space=pltpu.SMEM),
                  pl.BlockSpec((1, 2 * N_EXPERTS), lambda i: (0, 0), memory_space=pltpu.SMEM),
                  pl.BlockSpec((tm, D_MODEL), lambda i: (i, 0))],
        out_specs=pl.BlockSpec(memory_space=pl.ANY),
        out_shape=jax.ShapeDtypeStruct((n_rows, D_MODEL), F32),
        scratch_shapes=[pltpu.VMEM((SUBLANES, D_MODEL), F32), pltpu.SemaphoreType.DMA((2,))],
        compiler_params=_cparams(("arbitrary",), disable_bounds_checks=True),
    )(dest3, pad_ranges.reshape(1, 2 * N_EXPERTS), x)


def _moe_kernel(meta_ref, x_ref, wg_ref, wu_ref, wd_ref, y_ref, xb_ref, acc_ref):
    b = pl.program_id(0)
    f = pl.program_id(1)
    nblk = pl.num_programs(0)
    used = b < meta_ref[nblk]

    @pl.when(used)
    def _():
        @pl.when(f == 0)
        def _():
            xb_ref[...] = x_ref[...].astype(BF16)
            acc_ref[...] = jnp.zeros_like(acc_ref)

        xb = xb_ref[...]
        gate = jnp.dot(xb, wg_ref[...], preferred_element_type=F32)
        up = jnp.dot(xb, wu_ref[...], preferred_element_type=F32)
        acc_ref[...] += _bdot(_silu(gate) * up, wd_ref[...])

    @pl.when(f == pl.num_programs(1) - 1)
    def _():
        @pl.when(used)
        def _():
            y_ref[...] = acc_ref[...]

        @pl.when(jnp.logical_not(used))
        def _():
            y_ref[...] = jnp.zeros_like(y_ref)


def _moe_experts(xs, meta, w_gate, w_up, w_down, tb=MOE_TB, tf=MOE_TF):
    nblk = xs.shape[0] // tb
    dffe = w_gate.shape[2]
    grid_spec = pltpu.PrefetchScalarGridSpec(
        num_scalar_prefetch=1,
        grid=(nblk, dffe // tf),
        in_specs=[pl.BlockSpec((tb, D_MODEL), lambda b, f, m: (jnp.minimum(b, m[nblk] - 1), 0)),
                  pl.BlockSpec((None, D_MODEL, tf), lambda b, f, m: (m[b], 0, f)),
                  pl.BlockSpec((None, D_MODEL, tf), lambda b, f, m: (m[b], 0, f)),
                  pl.BlockSpec((None, tf, D_MODEL), lambda b, f, m: (m[b], f, 0))],
        out_specs=pl.BlockSpec((tb, D_MODEL), lambda b, f, m: (b, 0)),
        scratch_shapes=[pltpu.VMEM((tb, D_MODEL), BF16), pltpu.VMEM((tb, D_MODEL), F32)],
    )
    return pl.pallas_call(
        _moe_kernel,
        name="moe_experts",
        grid_spec=grid_spec,
        out_shape=jax.ShapeDtypeStruct((nblk * tb, D_MODEL), F32),
        compiler_params=_cparams(("parallel", "arbitrary")),
    )(meta, xs, w_gate.astype(BF16), w_up.astype(BF16), w_down.astype(BF16))


def _combine_kernel(pos0_ref, posn_ref, y_hbm, gates_ref, x_ref, g_ref, b_ref, out_ref, ybuf, sem):
    i = pl.program_id(0)
    tm = out_ref.shape[0]
    slot = i % 2

    @pl.when(i == 0)
    def _():
        _gather_rows(pos0_ref, y_hbm, ybuf.at[0], sem.at[0], TOP_K * tm)

    _wait_rows(y_hbm, ybuf.at[slot], sem.at[slot], TOP_K * tm)

    @pl.when(i + 1 < pl.num_programs(0))
    def _():
        _gather_rows(posn_ref, y_hbm, ybuf.at[1 - slot], sem.at[1 - slot], TOP_K * tm)

    gates = gates_ref[...]
    f = gates[:, 0:1] * ybuf[slot, :tm] + gates[:, 1:2] * ybuf[slot, tm:]
    out_ref[...] = _layer_norm(ALPHA * x_ref[...] + f, g_ref[...], b_ref[...])


def _moe_combine(y, pos, gates, x, ln_g, ln_b, tm=CMB_TM):
    n = x.shape[0]
    nt = n // tm
    pos3 = pos.reshape(nt, tm, TOP_K).transpose(0, 2, 1).reshape(nt, 1, TOP_K * tm)
    g, b = ln_g.reshape(1, D_MODEL), ln_b.reshape(1, D_MODEL)
    smem_blk = lambda imap: pl.BlockSpec((None, 1, TOP_K * tm), imap, memory_space=pltpu.SMEM)
    return pl.pallas_call(
        _combine_kernel,
        name="moe_combine_ln",
        grid=(nt,),
        in_specs=[smem_blk(lambda i: (0, 0, 0)),
                  smem_blk(lambda i: (jnp.minimum(i + 1, nt - 1), 0, 0)),
                  pl.BlockSpec(memory_space=pl.ANY),
                  pl.BlockSpec((tm, TOP_K), lambda i: (i, 0)),
                  pl.BlockSpec((tm, D_MODEL), lambda i: (i, 0)),
                  pl.BlockSpec((1, D_MODEL), lambda i: (0, 0)),
                  pl.BlockSpec((1, D_MODEL), lambda i: (0, 0))],
        out_specs=pl.BlockSpec((tm, D_MODEL), lambda i: (i, 0)),
        out_shape=jax.ShapeDtypeStruct((n, D_MODEL), F32),
        scratch_shapes=[pltpu.VMEM((2, TOP_K * tm, D_MODEL), F32), pltpu.SemaphoreType.DMA((2,))],
        compiler_params=_cparams(("arbitrary",), disable_bounds_checks=True),
    )(pos3, pos3, y, gates, x, g, b)


def _moe_routing(logits, tb):
    n = logits.shape[0]
    top_logit, top_idx = lax.top_k(logits, TOP_K)
    gates = jax.nn.softmax(top_logit, axis=-1)
    na = n * TOP_K
    e_flat = top_idx.reshape(-1).astype(jnp.int32)
    onehot = (e_flat[None, :] == jnp.arange(N_EXPERTS, dtype=jnp.int32)[:, None]).astype(jnp.int32)
    running = jnp.cumsum(onehot, axis=1)
    counts = running[:, -1]
    padded = (counts + tb - 1) // tb * tb
    pend = jnp.cumsum(padded)
    pstart = pend - padded
    dest = jnp.sum(onehot * (pstart[:, None] + running - 1), axis=0).reshape(n, TOP_K)
    nblk = -(-na // tb) + N_EXPERTS
    block_expert = jnp.minimum(jnp.searchsorted(pend, jnp.arange(nblk, dtype=jnp.int32) * tb, side='right'),
                               N_EXPERTS - 1).astype(jnp.int32)
    meta = jnp.concatenate([block_expert, (pend[-1:] // tb).astype(jnp.int32)])
    pad_ranges = jnp.concatenate([pstart + counts, pend[:-1], jnp.full((1,), nblk * tb)]).astype(jnp.int32)
    return dest.astype(jnp.int32), gates, meta, pad_ranges, nblk


def _router_weight(w_router):
    return jnp.pad(w_router, ((0, 0), (0, LANES - N_EXPERTS))).astype(BF16)


def _moe_ffn(x, logits, w_gate, w_up, w_down, ln_g, ln_b):
    dest, gates, meta, pad_ranges, nblk = _moe_routing(logits[:, :N_EXPERTS], MOE_TB)
    xs = _moe_dispatch(x, dest, pad_ranges, nblk * MOE_TB)
    y = _moe_experts(xs, meta, w_gate, w_up, w_down)
    return _moe_combine(y, dest, gates, x, ln_g, ln_b)


def _split_w_in(w):
    a_end = 3 * A_QKV_W
    bz_end = a_end + 3 * B_W + B_W
    w_groups = [jnp.concatenate([w[:, s * A_QKV_W + gi * A_GROUP_W:s * A_QKV_W + (gi + 1) * A_GROUP_W]
                                 for s in range(3)], axis=1).astype(BF16) for gi in range(N_GROUPS)]
    zpad = jnp.zeros((w.shape[0], LANES - N_HEADS_B), w.dtype)
    w_b = jnp.concatenate([w[:, a_end:bz_end], w[:, bz_end + 2 * N_HEADS_B:],
                           w[:, bz_end:bz_end + N_HEADS_B], zpad,
                           w[:, bz_end + N_HEADS_B:bz_end + 2 * N_HEADS_B], zpad], axis=1)
    return w_groups, w_b.astype(BF16)


def _hybrid_layer(x, batch, seq, rel_bias, w_in, conv_w, a_log, dt_bias, o_norm_w, w_oa, w_ob, w_out, ln_g, ln_b,
                  w_router=None):
    w_groups, w_b = _split_w_in(w_in)
    proj_b = _matmul(x, w_b, F32, "in_proj_gdn_gates")
    outs, lses = [], []
    for gi, (_, dilation) in enumerate(DSWA_PATTERNS):
        qkv = _in_proj_strided(x, w_groups[gi], dilation, batch, seq)
        o, lse = _dswa_group(qkv, _band_bias(rel_bias, gi, dilation), dilation)
        outs.append(o)
        lses.append(lse)
    yb = _gated_deltanet(proj_b, conv_w, a_log, dt_bias, o_norm_w, batch, seq)
    return _mix_out(outs, lses, yb, proj_b, x, w_oa, w_ob, w_out, ln_g, ln_b, w_router)


def kernel(x, rel_bias, w_in, conv_w, a_log, dt_bias, o_norm_w, w_oa, w_ob, w_out, ln1_g, ln1_b,
           ffn_w_gate, ffn_w_up, ffn_w_down, moe_router, moe_w_gate, moe_w_up, moe_w_down, ln2_g, ln2_b):
    batch, seq, d = x.shape
    h = x.reshape(batch * seq, d)
    for layer in range(DEPTH):
        j = layer // 2
        is_moe = layer % 2 == 1
        h = _hybrid_layer(h, batch, seq, rel_bias, w_in[layer], conv_w[layer], a_log[layer], dt_bias[layer],
                          o_norm_w[layer], w_oa[layer], w_ob[layer], w_out[layer], ln1_g[layer], ln1_b[layer],
                          _router_weight(moe_router[j]) if is_moe else None)
        if is_moe:
            h, logits = h
            h = _moe_ffn(h, logits, moe_w_gate[j], moe_w_up[j], moe_w_down[j], ln2_g[layer], ln2_b[layer])
        else:
            h = _dense_ffn(h, ffn_w_gate[j], ffn_w_up[j], ffn_w_down[j], ln2_g[layer], ln2_b[layer])
    return h.reshape(batch, seq, d)
```

```python
import functools
import math

import jax
import jax.numpy as jnp
from jax import lax
from jax.experimental import pallas as pl
from jax.experimental.pallas import tpu as pltpu

F32 = jnp.float32
BF16 = jnp.bfloat16

D_MODEL = 1024
DEPTH = 2
DSWA_PATTERNS = ((128, 1), (512, 4), (2048, 16))
N_GROUPS = 3
HEADS_PER_GROUP = 4
HEAD_DIM = 128
A_QKV_W = N_GROUPS * HEADS_PER_GROUP * HEAD_DIM
A_GROUP_W = HEADS_PER_GROUP * HEAD_DIM
NUM_BUCKETS = 32
MAX_DISTANCE = 2048
N_HEADS_B = 8
B_W = N_HEADS_B * HEAD_DIM
CONV_WIDTH = 4
N_EXPERTS = 8
TOP_K = 2
ALPHA = (2 * DEPTH) ** 0.25
LN_EPS = 1e-5
RMS_EPS = 1e-6

LANES = 128
SUBLANES = 8
VMEM_LIMIT = 56 * 1024 * 1024

ATT_BLK = 128
DSWA_ROWS = 2048
GDN_CHUNK = 128
GDN_STEP_CHUNKS = 2
MM_TM = 512
MM_TN = 3200
PROJ_A_TM = 1024
OUT_TM = 512
OUT_SUBTILES = 2
FFN_TM = 512
FFN_TF = 1024
MOE_TB = 512
MOE_TF = 1792
CMB_TM = 512
DSP_TM = 512

NEG_BIG = -1e30


def _cparams(sem, vmem=VMEM_LIMIT, **kw):
    return pltpu.CompilerParams(dimension_semantics=sem, vmem_limit_bytes=vmem, **kw)


def _bdot(a, b):
    return jnp.dot(a.astype(BF16), b.astype(BF16), preferred_element_type=F32)


def _bdot_nt(a, b):
    return lax.dot_general(a.astype(BF16), b.astype(BF16), (((1,), (1,)), ((), ())),
                           preferred_element_type=F32)


def _bdot_tn(a, b):
    return lax.dot_general(a.astype(BF16), b.astype(BF16), (((0,), (0,)), ((), ())),
                           preferred_element_type=F32)


def _sigmoid(v):
    return 1.0 / (1.0 + jnp.exp(-v))


def _silu(v):
    return v * _sigmoid(v)


def _layer_norm(v, g, b):
    mu = jnp.mean(v, axis=-1, keepdims=True)
    d = v - mu
    var = jnp.mean(d * d, axis=-1, keepdims=True)
    return d * lax.rsqrt(var + LN_EPS) * g + b


def _mm_kernel(x_ref, w_ref, o_ref):
    o_ref[...] = _bdot(x_ref[...], w_ref[...]).astype(o_ref.dtype)


def _matmul(x, w, out_dtype, name, tm=MM_TM, tn=MM_TN):
    m, k = x.shape
    n = w.shape[1]
    tn = min(tn, n)
    return pl.pallas_call(
        _mm_kernel,
        name=name,
        grid=(n // tn, m // tm),
        in_specs=[pl.BlockSpec((tm, k), lambda j, i: (i, 0)),
                  pl.BlockSpec((k, tn), lambda j, i: (0, j))],
        out_specs=pl.BlockSpec((tm, tn), lambda j, i: (i, j)),
        out_shape=jax.ShapeDtypeStruct((m, n), out_dtype),
        compiler_params=_cparams(("parallel", "parallel")),
    )(x, w)


def _dswa_kernel(q_ref, kc_ref, kp_ref, vc_ref, vp_ref, bias0_ref, bias_ref, o_ref, lse_ref, *, tq):
    nqb = tq // ATT_BLK
    lane = lax.broadcasted_iota(jnp.int32, (ATT_BLK, LANES), 1)
    scale = HEAD_DIM ** -0.5
    exp2_scale = scale * math.log2(math.e)
    for ri in range(q_ref.shape[0]):
        q = q_ref[ri]
        kwin = jnp.concatenate([kp_ref[ri], kc_ref[ri]], axis=0)
        vwin = jnp.concatenate([vp_ref[ri], vc_ref[ri]], axis=0)
        for c in range(nqb):
            b_ref = bias0_ref if c == 0 else bias_ref
            rows = slice(c * ATT_BLK, (c + 1) * ATT_BLK)
            lse_tile = jnp.zeros((ATT_BLK, LANES), F32)
            for h in range(HEADS_PER_GROUP):
                hs = slice(h * HEAD_DIM, (h + 1) * HEAD_DIM)
                kh = kwin[c * ATT_BLK:(c + 2) * ATT_BLK, hs]
                vh = vwin[c * ATT_BLK:(c + 2) * ATT_BLK, hs]
                t = _bdot_nt(q[rows, hs], kh) + b_ref[h]
                m = jnp.max(t, axis=-1, keepdims=True)
                p = jnp.exp2((t - m) * exp2_scale)
                l = jnp.sum(p, axis=-1, keepdims=True)
                o_ref[ri, rows, hs] = _bdot(p, vh) / l
                lse_tile = jnp.where(lane == h, m * scale + jnp.log(l), lse_tile)
            lse_ref[ri, rows, :] = lse_tile


def _in_proj_strided_kernel(x_ref, w_ref, o_ref, *scratch, dilation):
    res = _bdot(x_ref[...], w_ref[...])
    if dilation == 1:
        o_ref[0] = res.astype(o_ref.dtype)
    else:
        res_ref, = scratch
        rows = res.shape[0] // dilation
        for t in range(res.shape[1] // LANES):
            ls = slice(t * LANES, (t + 1) * LANES)
            res_ref[t] = res[:, ls]
            for r in range(dilation):
                o_ref[r, :, ls] = res_ref[t, pl.ds(r, rows, stride=dilation), :].astype(o_ref.dtype)


def _in_proj_strided(x, w, dilation, batch, seq, tm=PROJ_A_TM):
    k = x.shape[1]
    wn = tn = w.shape[1]
    tpb = seq // tm
    scratch = [] if dilation == 1 else [pltpu.VMEM((tn // LANES, tm, LANES), F32)]
    return pl.pallas_call(
        functools.partial(_in_proj_strided_kernel, dilation=dilation),
        name=f"in_proj_attn_d{dilation}",
        grid=(batch * tpb, wn // tn),
        in_specs=[pl.BlockSpec((tm, k), lambda i, j: (i, 0)),
                  pl.BlockSpec((k, tn), lambda i, j: (0, j))],
        out_specs=pl.BlockSpec((None, dilation, tm // dilation, tn), lambda i, j: (i // tpb, 0, i % tpb, j)),
        out_shape=jax.ShapeDtypeStruct((batch, dilation, seq // dilation, wn), BF16),
        scratch_shapes=scratch,
        compiler_params=_cparams(("parallel", "parallel")),
    )(x, w)


def _dswa_group(qkv, bias, dilation):
    batch, _, n, _ = qkv.shape
    tq = min(DSWA_ROWS, n)
    rr = DSWA_ROWS // tq
    nqb = tq // ATT_BLK

    def cur(off):
        return pl.BlockSpec((None, rr, tq, A_GROUP_W), lambda b, r, j: (b, r, j, off))

    def prev(off):
        return pl.BlockSpec((None, rr, ATT_BLK, A_GROUP_W),
                            lambda b, r, j: (b, r, jnp.maximum(j * nqb - 1, 0), off))

    return pl.pallas_call(
        functools.partial(_dswa_kernel, tq=tq),
        name=f"dswa_d{dilation}",
        grid=(batch, dilation // rr, n // tq),
        in_specs=[cur(0), cur(1), prev(1), cur(2), prev(2),
                  pl.BlockSpec((None, HEADS_PER_GROUP, ATT_BLK, 2 * ATT_BLK),
                               lambda b, r, j: (jnp.minimum(j, 1), 0, 0, 0)),
                  pl.BlockSpec((None, HEADS_PER_GROUP, ATT_BLK, 2 * ATT_BLK), lambda b, r, j: (1, 0, 0, 0))],
        out_specs=[pl.BlockSpec((None, rr, tq, A_GROUP_W), lambda b, r, j: (b, r, j, 0)),
                   pl.BlockSpec((None, rr, tq, LANES), lambda b, r, j: (b, r, j, 0))],
        out_shape=[jax.ShapeDtypeStruct((batch, dilation, n, A_GROUP_W), F32),
                   jax.ShapeDtypeStruct((batch, dilation, n, LANES), F32)],
        compiler_params=_cparams(("parallel", "parallel", "parallel")),
    )(qkv, qkv, qkv, qkv, qkv, bias, bias)


def _t5_causal_bucket(dist):
    num_exact = NUM_BUCKETS // 2
    d = jnp.maximum(dist, 1).astype(F32)
    large = num_exact + (jnp.log(d / num_exact) / math.log(MAX_DISTANCE / num_exact)
                         * (NUM_BUCKETS - num_exact)).astype(jnp.int32)
    large = jnp.minimum(large, NUM_BUCKETS - 1)
    return jnp.where(dist < num_exact, dist, large)


def _band_bias(rel_bias, gi, dilation):
    qi = jnp.arange(ATT_BLK)[:, None] + ATT_BLK
    kj = jnp.arange(2 * ATT_BLK)[None, :]
    band = (qi - kj >= 0) & (qi - kj <= ATT_BLK)
    delta = jnp.maximum(qi - kj, 0) * dilation
    table = rel_bias[:, gi * HEADS_PER_GROUP:(gi + 1) * HEADS_PER_GROUP].astype(F32)
    onehot = (_t5_causal_bucket(delta)[..., None] == jnp.arange(NUM_BUCKETS)).astype(F32)
    bias = jnp.einsum('qkn,nh->hqk', onehot, table, precision=lax.Precision.HIGHEST) / (HEAD_DIM ** -0.5)
    return jnp.stack([jnp.where(band & (kj >= ATT_BLK), bias, NEG_BIG), jnp.where(band, bias, NEG_BIG)])


def _dot16(a, b):
    return jnp.dot(a, b, preferred_element_type=F32)


def _unit_lower_inverse(a, row, col):
    base = 16
    heads = list(a)
    eye = (row == col).astype(F32)
    same = lambda sz: (row // sz) == (col // sz)
    blk = same(base)
    a_d = {h: jnp.where(blk, a[h], 0.0) for h in heads}
    t = {h: eye - a_d[h] for h in heads}
    p = {h: a_d[h].astype(BF16) for h in heads}
    for _ in range(3):
        p = {h: _dot16(p[h], p[h]).astype(BF16) for h in heads}
        t = {h: t[h] + _dot16(t[h].astype(BF16), p[h]) for h in heads}
    sz = 2 * base
    while sz <= GDN_CHUNK:
        off = same(sz) & jnp.logical_not(same(sz // 2))
        tb = {h: t[h].astype(BF16) for h in heads}
        m = {h: _dot16(jnp.where(off, a[h], 0.0).astype(BF16), tb[h]).astype(BF16) for h in heads}
        t = {h: t[h] - _dot16(tb[h], m[h]) for h in heads}
        sz *= 2
    return t


def _gdn_kernel(x_ref, halo_ref, z_ref, ba_ref, cw_ref, alog_ref, dtb_ref, onw_ref, y_ref, state_ref):
    c = pl.program_id(1)

    @pl.when(c == 0)
    def _():
        state_ref[...] = jnp.zeros_like(state_ref)

    C = GDN_CHUNK
    chunks = range(x_ref.shape[0] // C)
    rows = {cc: slice(cc * C, (cc + 1) * C) for cc in chunks}
    keep_halo = (c > 0).astype(F32)
    row = lax.broadcasted_iota(jnp.int32, (C, C), 0)
    col = lax.broadcasted_iota(jnp.int32, (C, C), 1)
    incl = row >= col
    strict = row > col
    ones_l = incl.astype(BF16)

    def log_decay_cumsum(cc):
        ba = ba_ref[rows[cc], :]
        a_in = ba[:, LANES:] + dtb_ref[...]
        softplus = jnp.maximum(a_in, 0.0) + jnp.log(1.0 + jnp.exp(-jnp.abs(a_in)))
        g_all = -jnp.exp(alog_ref[...]) * softplus
        g1 = g_all.astype(BF16)
        r1 = g_all - g1.astype(F32)
        g2 = r1.astype(BF16)
        g3 = (r1 - g2.astype(F32)).astype(BF16)
        return (jnp.dot(ones_l, g1, preferred_element_type=F32) + jnp.dot(ones_l, g2, preferred_element_type=F32)
                + jnp.dot(ones_l, g3, preferred_element_type=F32))

    beta_all = {cc: _sigmoid(ba_ref[rows[cc], :LANES]) for cc in chunks}
    G = {cc: log_decay_cumsum(cc) for cc in chunks}
    GT = {cc: G[cc].T for cc in chunks}
    exp_g = {cc: jnp.exp(G[cc]) for cc in chunks}
    beta_exp_g = {cc: beta_all[cc] * exp_g[cc] for cc in chunks}
    exp_rest = {cc: jnp.exp(G[cc][C - 1:C, :] - G[cc]) for cc in chunks}

    def conv_silu(cc, col0):
        cs = slice(col0, col0 + HEAD_DIM)
        xc = x_ref[rows[cc], cs]
        halo = halo_ref[:, cs] * keep_halo if cc == 0 else x_ref[cc * C - SUBLANES:cc * C, cs]
        xf = jnp.concatenate([halo, xc], axis=0)
        w = cw_ref[:, cs]
        y = w[CONV_WIDTH - 1:CONV_WIDTH] * xc
        for i in range(CONV_WIDTH - 1):
            off = SUBLANES - (CONV_WIDTH - 1) + i
            y = y + w[i:i + 1] * xf[off:off + C]
        return _silu(y)

    def l2norm(t, scale=1.0):
        return t * (lax.rsqrt(jnp.sum(t * t, axis=-1, keepdims=True) + RMS_EPS) * scale)

    items = [(cc, h) for cc in chunks for h in range(N_HEADS_B)]
    q = {(cc, h): l2norm(conv_silu(cc, h * HEAD_DIM), HEAD_DIM ** -0.5) for cc, h in items}
    k = {(cc, h): l2norm(conv_silu(cc, B_W + h * HEAD_DIM)) for cc, h in items}
    v = {(cc, h): conv_silu(cc, 2 * B_W + h * HEAD_DIM) for cc, h in items}
    beta = {(cc, h): beta_all[cc][:, h:h + 1] for cc, h in items}
    gc = {(cc, h): G[cc][:, h:h + 1] for cc, h in items}
    g_last = {(cc, h): G[cc][C - 1:C, h:h + 1] for cc, h in items}
    decay = {(cc, h): jnp.exp(jnp.where(incl, gc[cc, h] - GT[cc][h:h + 1, :], NEG_BIG)) for cc, h in items}
    eg = {(cc, h): exp_g[cc][:, h:h + 1] for cc, h in items}
    kb ={it: k[it].astype(BF16) for it in items}
    kq = {it: lax.dot_general(jnp.concatenate([kb[it], q[it].astype(BF16)], axis=0), kb[it],
                              (((1,), (1,)), ((), ())), preferred_element_type=F32) for it in items}
    a = {it: jnp.where(strict, beta[it] * kq[it][:C] * decay[it], 0.0) for it in items}
    t_inv = _unit_lower_inverse(a, row, col)
    rhs = {(cc, h): jnp.concatenate([beta[cc, h] * v[cc, h], beta_exp_g[cc][:, h:h + 1] * k[cc, h]],
                                    axis=1).astype(BF16) for cc, h in items}
    sol = {it: _dot16(t_inv[it].astype(BF16), rhs[it]) for it in items}
    qk = {it: (kq[it][C:] * decay[it]).astype(BF16) for it in items}
    wq = {it: jnp.concatenate([sol[it][:, HEAD_DIM:], q[it] * eg[it]], axis=0).astype(BF16) for it in items}
    k_dec = {(cc, h): (k[cc, h] * exp_rest[cc][:, h:h + 1]).astype(BF16) for cc, h in items}

    heads = range(N_HEADS_B)
    state = {h: state_ref[h] for h in heads}
    for cc in chunks:
        ws = {h: _dot16(wq[cc, h], state[h].astype(BF16)) for h in heads}
        u = {h: (sol[cc, h][:, :HEAD_DIM] - ws[h][:C]).astype(BF16) for h in heads}
        o = {h: ws[h][C:] + _dot16(qk[cc, h], u[h]) for h in heads}
        state = {h: jnp.exp(g_last[cc, h]) * state[h] + lax.dot_general(
            k_dec[cc, h], u[h], (((0,), (0,)), ((), ())), preferred_element_type=F32) for h in heads}
        for h in heads:
            oh = o[h] * lax.rsqrt(jnp.mean(o[h] * o[h], axis=-1, keepdims=True) + RMS_EPS) * onw_ref[...]
            hs = slice(h * HEAD_DIM, (h + 1) * HEAD_DIM)
            y_ref[rows[cc], hs] = (oh * _silu(z_ref[rows[cc], hs])).astype(y_ref.dtype)
    for h in heads:
        state_ref[h] = state[h]


def _gated_deltanet(proj_b, conv_w, a_log, dt_bias, o_norm_w, batch, seq):
    C = GDN_STEP_CHUNKS * GDN_CHUNK
    wb = proj_b.shape[1]
    pb = proj_b.reshape(batch, seq, wb)
    ba_blk = (wb - 2 * LANES) // (2 * LANES)
    pad = lambda t: jnp.pad(t.astype(F32), (0, LANES - t.shape[0])).reshape(1, LANES)
    const = lambda shape: pl.BlockSpec(shape, lambda b, c: (0,) * len(shape))
    y = pl.pallas_call(
        _gdn_kernel,
        name="gated_deltanet",
        grid=(batch, seq // C),
        in_specs=[pl.BlockSpec((None, C, 3 * B_W), lambda b, c: (b, c, 0)),
                  pl.BlockSpec((None, SUBLANES, 3 * B_W),
                               lambda b, c: (b, jnp.maximum(c * (C // SUBLANES) - 1, 0), 0)),
                  pl.BlockSpec((None, C, B_W), lambda b, c: (b, c, 3)),
                  pl.BlockSpec((None, C, 2 * LANES), lambda b, c: (b, c, ba_blk)),
                  const((CONV_WIDTH, 3 * B_W)), const((1, LANES)), const((1, LANES)), const((1, LANES))],
        out_specs=pl.BlockSpec((None, C, B_W), lambda b, c: (b, c, 0)),
        out_shape=jax.ShapeDtypeStruct((batch, seq, B_W), BF16),
        scratch_shapes=[pltpu.VMEM((N_HEADS_B, HEAD_DIM, HEAD_DIM), F32)],
        compiler_params=_cparams(("parallel", "arbitrary")),
    )(pb, pb, pb, pb, conv_w.astype(F32), pad(a_log), pad(dt_bias), o_norm_w.astype(F32).reshape(1, LANES))
    return y.reshape(batch * seq, B_W)


def _mix_out_kernel(o0, o1, o2, l0, l1, l2, yb_ref, ga_ref, gb_ref, x_ref, woa_ref, wob_ref, wout_ref,
                    g_ref, b_ref, *rest, with_router):
    if with_router:
        wr_ref, out_ref, logits_ref, *scratch = rest
    else:
        out_ref, *scratch = rest

    def token_order(ref, scr):
        dilation, rows, width = ref.shape
        if dilation == 1:
            return ref[0]
        planes = []
        for t in range(width // LANES):
            for r in range(dilation):
                scr[t, pl.ds(r, rows, stride=dilation), :] = ref[r, :, t * LANES:(t + 1) * LANES]
            planes.append(scr[t])
        return jnp.concatenate(planes, axis=1)

    outs = (token_order(o0, None), token_order(o1, scratch[0]), token_order(o2, scratch[1]))
    lses = (token_order(l0, None), token_order(l1, scratch[2]), token_order(l2, scratch[3]))
    tm = x_ref.shape[0]
    for sub in range(OUT_SUBTILES):
        rs = slice(sub * tm // OUT_SUBTILES, (sub + 1) * tm // OUT_SUBTILES)
        lse_s = [t[rs] for t in lses]
        m = jnp.maximum(jnp.maximum(lse_s[0], lse_s[1]), lse_s[2])
        es = [jnp.exp(t - m) for t in lse_s]
        inv = 1.0 / (es[0] + es[1] + es[2])
        wgt = [e * inv for e in es]
        ya = []
        for h in range(HEADS_PER_GROUP):
            hs = slice(h * HEAD_DIM, (h + 1) * HEAD_DIM)
            ya.append(wgt[0][:, h:h + 1] * outs[0][rs, hs] + wgt[1][:, h:h + 1] * outs[1][rs, hs]
                      + wgt[2][:, h:h + 1] * outs[2][rs, hs])
        ya = jnp.concatenate(ya, axis=1)
        pa = _bdot(ya, woa_ref[...])
        pb = _bdot(yb_ref[rs, :], wob_ref[...])
        merged = _sigmoid(ga_ref[rs, :]) * pa + _sigmoid(gb_ref[rs, :]) * pb
        mix = _bdot(merged, wout_ref[...])
        out = _layer_norm(ALPHA * x_ref[rs, :] + mix, g_ref[...], b_ref[...])
        out_ref[rs, :] = out
        if with_router:
            logits_ref[rs, :] = _bdot(out, wr_ref[...])


def _mix_out(outs, lses, yb, proj_b, x, w_oa, w_ob, w_out, ln_g, ln_b, w_router=None, tm=OUT_TM):
    n = x.shape[0]
    with_router = w_router is not None
    seq = outs[0].shape[1] * outs[0].shape[2]
    tpb = seq // tm
    rowblk = lambda w, cb=0: pl.BlockSpec((tm, w), lambda i: (i, cb))
    const = lambda a: pl.BlockSpec(a.shape, lambda i: (0, 0), pipeline_mode=pl.Buffered(1))

    def grouped(a):
        d, w = a.shape[1], a.shape[3]
        return pl.BlockSpec((None, d, tm // d, w), lambda i: (i // tpb, 0, i % tpb, 0))

    wa, wb, wo = w_oa.astype(BF16), w_ob.astype(BF16), w_out.astype(BF16)
    g, b = ln_g.reshape(1, D_MODEL), ln_b.reshape(1, D_MODEL)
    router = [w_router] if with_router else []
    out_specs = [rowblk(D_MODEL)] + ([rowblk(LANES)] if with_router else [])
    out_shape = [jax.ShapeDtypeStruct((n, D_MODEL), F32)] + (
        [jax.ShapeDtypeStruct((n, LANES), F32)] if with_router else [])
    res = pl.pallas_call(
        functools.partial(_mix_out_kernel, with_router=with_router),
        name="mix_out_ln",
        grid=(n // tm,),
        in_specs=[grouped(a) for a in (*outs, *lses)]
        + [rowblk(B_W), rowblk(D_MODEL, 4), rowblk(D_MODEL, 5), rowblk(D_MODEL),
           const(wa), const(wb), const(wo), const(g), const(b)] + [const(a) for a in router],
        out_specs=out_specs,
        out_shape=out_shape,
        scratch_shapes=[pltpu.VMEM((A_GROUP_W // LANES, tm, LANES), F32)] * 2 + [pltpu.VMEM((1, tm, LANES), F32)] * 2,
        compiler_params=_cparams(("parallel",)),
    )(*outs, *lses, yb, proj_b, proj_b, x, wa, wb, wo, g, b, *router)
    return tuple(res) if with_router else res[0]


def _ffn_kernel(x_ref, wg_ref, wu_ref, wd_ref, g_ref, b_ref, out_ref):
    x = x_ref[...]
    xb = x.astype(BF16)
    dff = wg_ref.shape[1]
    acc = None
    for c0 in range(0, dff, FFN_TF):
        c1 = min(c0 + FFN_TF, dff)
        gate = jnp.dot(xb, wg_ref[:, c0:c1], preferred_element_type=F32)
        up = jnp.dot(xb, wu_ref[:, c0:c1], preferred_element_type=F32)
        part = _bdot(_silu(gate) * up, wd_ref[c0:c1, :])
        acc = part if acc is None else acc + part
    out_ref[...] = _layer_norm(ALPHA * x + acc, g_ref[...], b_ref[...])


def _dense_ffn(x, w_gate, w_up, w_down, ln_g, ln_b, tm=FFN_TM):
    n = x.shape[0]
    g, b = ln_g.reshape(1, D_MODEL), ln_b.reshape(1, D_MODEL)
    resident = lambda a: pl.BlockSpec(a.shape, lambda i: (0, 0), pipeline_mode=pl.Buffered(1))
    wg, wu, wd = w_gate.astype(BF16), w_up.astype(BF16), w_down.astype(BF16)
    return pl.pallas_call(
        _ffn_kernel,
        name="dense_ffn_ln",
        grid=(n // tm,),
        in_specs=[pl.BlockSpec((tm, D_MODEL), lambda i: (i, 0)),
                  resident(wg), resident(wu), resident(wd), resident(g), resident(b)],
        out_specs=pl.BlockSpec((tm, D_MODEL), lambda i: (i, 0)),
        out_shape=jax.ShapeDtypeStruct((n, D_MODEL), F32),
        compiler_params=_cparams(("parallel",)),
    )(x, wg, wu, wd, g, b)


def _gather_rows(idx_ref, src_hbm, dst, sem, count):
    def body(i, carry):
        pltpu.make_async_copy(src_hbm.at[pl.ds(idx_ref[0, i], 1)], dst.at[pl.ds(i, 1)], sem).start()
        return carry
    lax.fori_loop(0, count, body, 0, unroll=8)


def _wait_rows(src_hbm, dst, sem, count):
    pltpu.make_async_copy(src_hbm.at[pl.ds(0, count)], dst, sem).wait()


def _dispatch_kernel(dest_ref, pad_ref, x_ref, xs_out, zero_ref, sem):
    tm = x_ref.shape[0]
    zero_row = lambda r: pltpu.make_async_copy(zero_ref.at[pl.ds(0, 1)], xs_out.at[pl.ds(r, 1)], sem.at[1])

    def for_padding_rows(fn):
        for e in range(N_EXPERTS):
            lax.fori_loop(pad_ref[0, e], pad_ref[0, N_EXPERTS + e], lambda r, c: (fn(zero_row(r)), c)[1], 0)

    @pl.when(pl.program_id(0) == 0)
    def _():
        zero_ref[...] = jnp.zeros_like(zero_ref)
        for_padding_rows(lambda copy: copy.start())

    @pl.when(pl.program_id(0) == pl.num_programs(0) - 1)
    def _():
        for_padding_rows(lambda copy: copy.wait())

    def body(i, carry):
        for k in range(TOP_K):
            pltpu.make_async_copy(x_ref.at[pl.ds(i, 1)], xs_out.at[pl.ds(dest_ref[0, TOP_K * i + k], 1)],
                                  sem.at[0]).start()
        return carry
    lax.fori_loop(0, tm, body, 0, unroll=4)
    for _ in range(TOP_K):
        pltpu.make_async_copy(x_ref, xs_out.at[pl.ds(0, tm)], sem.at[0]).wait()


def _moe_dispatch(x, dest, pad_ranges, n_rows, tm=DSP_TM):
    n = x.shape[0]
    dest3 = dest.reshape(n // tm, 1, TOP_K * tm)
    return pl.pallas_call(
        _dispatch_kernel,
        name="moe_dispatch",
        grid=(n // tm,),
        in_specs=[pl.BlockSpec((None, 1, TOP_K * tm), lambda i: (i, 0, 0), memory_space=pltpu.SMEM),
                  pl.BlockSpec((1, 2 * N_EXPERTS), lambda i: (0, 0), memory_space=pltpu.SMEM),
                  pl.BlockSpec((tm, D_MODEL), lambda i: (i, 0))],
        out_specs=pl.BlockSpec(memory_space=pl.ANY),
        out_shape=jax.ShapeDtypeStruct((n_rows, D_MODEL), F32),
        scratch_shapes=[pltpu.VMEM((SUBLANES, D_MODEL), F32), pltpu.SemaphoreType.DMA((2,))],
        compiler_params=_cparams(("arbitrary",), disable_bounds_checks=True),
    )(dest3, pad_ranges.reshape(1, 2 * N_EXPERTS), x)


def _moe_kernel(meta_ref, x_ref, wg_ref, wu_ref, wd_ref, y_ref, xb_ref, acc_ref):
    b = pl.program_id(0)
    f = pl.program_id(1)
    nblk = pl.num_programs(0)
    used = b < meta_ref[nblk]

    @pl.when(used)
    def _():
        @pl.when(f == 0)
        def _():
            xb_ref[...] = x_ref[...].astype(BF16)
            acc_ref[...] = jnp.zeros_like(acc_ref)

        xb = xb_ref[...]
        gate = jnp.dot(xb, wg_ref[...], preferred_element_type=F32)
        up = jnp.dot(xb, wu_ref[...], preferred_element_type=F32)
        acc_ref[...] += _bdot(_silu(gate) * up, wd_ref[...])

    @pl.when(f == pl.num_programs(1) - 1)
    def _():
        @pl.when(used)
        def _():
            y_ref[...] = acc_ref[...]

        @pl.when(jnp.logical_not(used))
        def _():
            y_ref[...] = jnp.zeros_like(y_ref)


def _moe_experts(xs, meta, w_gate, w_up, w_down, tb=MOE_TB, tf=MOE_TF):
    nblk = xs.shape[0] // tb
    dffe = w_gate.shape[2]
    grid_spec = pltpu.PrefetchScalarGridSpec(
        num_scalar_prefetch=1,
        grid=(nblk, dffe // tf),
        in_specs=[pl.BlockSpec((tb, D_MODEL), lambda b, f, m: (jnp.minimum(b, m[nblk] - 1), 0)),
                  pl.BlockSpec((None, D_MODEL, tf), lambda b, f, m: (m[b], 0, f)),
                  pl.BlockSpec((None, D_MODEL, tf), lambda b, f, m: (m[b], 0, f)),
                  pl.BlockSpec((None, tf, D_MODEL), lambda b, f, m: (m[b], f, 0))],
        out_specs=pl.BlockSpec((tb, D_MODEL), lambda b, f, m: (b, 0)),
        scratch_shapes=[pltpu.VMEM((tb, D_MODEL), BF16), pltpu.VMEM((tb, D_MODEL), F32)],
    )
    return pl.pallas_call(
        _moe_kernel,
        name="moe_experts",
        grid_spec=grid_spec,
        out_shape=jax.ShapeDtypeStruct((nblk * tb, D_MODEL), F32),
        compiler_params=_cparams(("parallel", "arbitrary")),
    )(meta, xs, w_gate.astype(BF16), w_up.astype(BF16), w_down.astype(BF16))


def _combine_kernel(pos0_ref, posn_ref, y_hbm, gates_ref, x_ref, g_ref, b_ref, out_ref, ybuf, sem):
    i = pl.program_id(0)
    tm = out_ref.shape[0]
    slot = i % 2

    @pl.when(i == 0)
    def _():
        _gather_rows(pos0_ref, y_hbm, ybuf.at[0], sem.at[0], TOP_K * tm)

    _wait_rows(y_hbm, ybuf.at[slot], sem.at[slot], TOP_K * tm)

    @pl.when(i + 1 < pl.num_programs(0))
    def _():
        _gather_rows(posn_ref, y_hbm, ybuf.at[1 - slot], sem.at[1 - slot], TOP_K * tm)

    gates = gates_ref[...]
    f = gates[:, 0:1] * ybuf[slot, :tm] + gates[:, 1:2] * ybuf[slot, tm:]
    out_ref[...] = _layer_norm(ALPHA * x_ref[...] + f, g_ref[...], b_ref[...])


def _moe_combine(y, pos, gates, x, ln_g, ln_b, tm=CMB_TM):
    n = x.shape[0]
    nt = n // tm
    pos3 = pos.reshape(nt, tm, TOP_K).transpose(0, 2, 1).reshape(nt, 1, TOP_K * tm)
    g, b = ln_g.reshape(1, D_MODEL), ln_b.reshape(1, D_MODEL)
    smem_blk = lambda imap: pl.BlockSpec((None, 1, TOP_K * tm), imap, memory_space=pltpu.SMEM)
    return pl.pallas_call(
        _combine_kernel,
        name="moe_combine_ln",
        grid=(nt,),
        in_specs=[smem_blk(lambda i: (0, 0, 0)),
                  smem_blk(lambda i: (jnp.minimum(i + 1, nt - 1), 0, 0)),
                  pl.BlockSpec(memory_space=pl.ANY),
                  pl.BlockSpec((tm, TOP_K), lambda i: (i, 0)),
                  pl.BlockSpec((tm, D_MODEL), lambda i: (i, 0)),
                  pl.BlockSpec((1, D_MODEL), lambda i: (0, 0)),
                  pl.BlockSpec((1, D_MODEL), lambda i: (0, 0))],
        out_specs=pl.BlockSpec((tm, D_MODEL), lambda i: (i, 0)),
        out_shape=jax.ShapeDtypeStruct((n, D_MODEL), F32),
        scratch_shapes=[pltpu.VMEM((2, TOP_K * tm, D_MODEL), F32), pltpu.SemaphoreType.DMA((2,))],
        compiler_params=_cparams(("arbitrary",), disable_bounds_checks=True),
    )(pos3, pos3, y, gates, x, g, b)


def _moe_routing(logits, tb):
    n = logits.shape[0]
    top_logit, top_idx = lax.top_k(logits, TOP_K)
    gates = jax.nn.softmax(top_logit, axis=-1)
    na = n * TOP_K
    e_flat = top_idx.reshape(-1).astype(jnp.int32)
    onehot = (e_flat[None, :] == jnp.arange(N_EXPERTS, dtype=jnp.int32)[:, None]).astype(jnp.int32)
    running = jnp.cumsum(onehot, axis=1)
    counts = running[:, -1]
    padded = (counts + tb - 1) // tb * tb
    pend = jnp.cumsum(padded)
    pstart = pend - padded
    dest = jnp.sum(onehot * (pstart[:, None] + running - 1), axis=0).reshape(n, TOP_K)
    nblk = -(-na // tb) + N_EXPERTS
    block_expert = jnp.minimum(jnp.searchsorted(pend, jnp.arange(nblk, dtype=jnp.int32) * tb, side='right'),
                               N_EXPERTS - 1).astype(jnp.int32)
    meta = jnp.concatenate([block_expert, (pend[-1:] // tb).astype(jnp.int32)])
    pad_ranges = jnp.concatenate([pstart + counts, pend[:-1], jnp.full((1,), nblk * tb)]).astype(jnp.int32)
    return dest.astype(jnp.int32), gates, meta, pad_ranges, nblk


def _router_weight(w_router):
    return jnp.pad(w_router, ((0, 0), (0, LANES - N_EXPERTS))).astype(BF16)


def _moe_ffn(x, logits, w_gate, w_up, w_down, ln_g, ln_b):
    dest, gates, meta, pad_ranges, nblk = _moe_routing(logits[:, :N_EXPERTS], MOE_TB)
    xs = _moe_dispatch(x, dest, pad_ranges, nblk * MOE_TB)
    y = _moe_experts(xs, meta, w_gate, w_up, w_down)
    return _moe_combine(y, dest, gates, x, ln_g, ln_b)


def _split_w_in(w):
    a_end = 3 * A_QKV_W
    bz_end = a_end + 3 * B_W + B_W
    w_groups = [jnp.concatenate([w[:, s * A_QKV_W + gi * A_GROUP_W:s * A_QKV_W + (gi + 1) * A_GROUP_W]
                                 for s in range(3)], axis=1).astype(BF16) for gi in range(N_GROUPS)]
    zpad = jnp.zeros((w.shape[0], LANES - N_HEADS_B), w.dtype)
    w_b = jnp.concatenate([w[:, a_end:bz_end], w[:, bz_end + 2 * N_HEADS_B:],
                           w[:, bz_end:bz_end + N_HEADS_B], zpad,
                           w[:, bz_end + N_HEADS_B:bz_end + 2 * N_HEADS_B], zpad], axis=1)
    return w_groups, w_b.astype(BF16)


def _hybrid_layer(x, batch, seq, rel_bias, w_in, conv_w, a_log, dt_bias, o_norm_w, w_oa, w_ob, w_out, ln_g, ln_b,
                  w_router=None):
    w_groups, w_b = _split_w_in(w_in)
    proj_b = _matmul(x, w_b, F32, "in_proj_gdn_gates")
    outs, lses = [], []
    for gi, (_, dilation) in enumerate(DSWA_PATTERNS):
        qkv = _in_proj_strided(x, w_groups[gi], dilation, batch, seq)
        o, lse = _dswa_group(qkv, _band_bias(rel_bias, gi, dilation), dilation)
        outs.append(o)
        lses.append(lse)
    yb = _gated_deltanet(proj_b, conv_w, a_log, dt_bias, o_norm_w, batch, seq)
    return _mix_out(outs, lses, yb, proj_b, x, w_oa, w_ob, w_out, ln_g, ln_b, w_router)


def kernel(x, rel_bias, w_in, conv_w, a_log, dt_bias, o_norm_w, w_oa, w_ob, w_out, ln1_g, ln1_b,
           ffn_w_gate, ffn_w_up, ffn_w_down, moe_router, moe_w_gate, moe_w_up, moe_w_down, ln2_g, ln2_b):
    batch, seq, d = x.shape
    h = x.reshape(batch * seq, d)
    for layer in range(DEPTH):
        j = layer // 2
        is_moe = layer % 2 == 1
        h = _hybrid_layer(h, batch, seq, rel_bias, w_in[layer], conv_w[layer], a_log[layer], dt_bias[layer],
                          o_norm_w[layer], w_oa[layer], w_ob[layer], w_out[layer], ln1_g[layer], ln1_b[layer],
                          _router_weight(moe_router[j]) if is_moe else None)
        if is_moe:
            h, logits = h
            h = _moe_ffn(h, logits, moe_w_gate[j], moe_w_up[j], moe_w_down[j], ln2_g[layer], ln2_b[layer])
        else:
            h = _dense_ffn(h, ffn_w_gate[j], ffn_w_up[j], ffn_w_down[j], ln2_g[layer], ln2_b[layer])
    return h.reshape(batch, seq, d)
```

```python
import functools
import math

import jax
import jax.numpy as jnp
from jax import lax
from jax.experimental import pallas as pl
from jax.experimental.pallas import tpu as pltpu

F32 = jnp.float32
BF16 = jnp.bfloat16

D_MODEL = 1024
DEPTH = 2
DSWA_PATTERNS = ((128, 1), (512, 4), (2048, 16))
N_GROUPS = 3
HEADS_PER_GROUP = 4
HEAD_DIM = 128
A_QKV_W = N_GROUPS * HEADS_PER_GROUP * HEAD_DIM
A_GROUP_W = HEADS_PER_GROUP * HEAD_DIM
NUM_BUCKETS = 32
MAX_DISTANCE = 2048
N_HEADS_B = 8
B_W = N_HEADS_B * HEAD_DIM
CONV_WIDTH = 4
N_EXPERTS = 8
TOP_K = 2
ALPHA = (2 * DEPTH) ** 0.25
LN_EPS = 1e-5
RMS_EPS = 1e-6

LANES = 128
SUBLANES = 8
SUBLANE_BITS = SUBLANES.bit_length() - 1
VMEM_LIMIT = 56 * 1024 * 1024

ATT_BLK = 128
DSWA_ROWS = 2048
GDN_CHUNK = 128
GDN_STEP_CHUNKS = 2
MM_TM = 512
MM_TN = 3200
PROJ_A_TM = 1024
OUT_TM = 512
OUT_SUBTILES = 2
FFN_TM = 512
FFN_TF = 1024
MOE_TB = 512
MOE_TF = 1792
CMB_TM = 512
DSP_TM = 512

NEG_BIG = -1e30


def _cparams(sem, vmem=VMEM_LIMIT, **kw):
    return pltpu.CompilerParams(dimension_semantics=sem, vmem_limit_bytes=vmem, **kw)


def _bdot(a, b):
    return jnp.dot(a.astype(BF16), b.astype(BF16), preferred_element_type=F32)


def _bdot_nt(a, b):
    return lax.dot_general(a.astype(BF16), b.astype(BF16), (((1,), (1,)), ((), ())),
                           preferred_element_type=F32)


def _bdot_tn(a, b):
    return lax.dot_general(a.astype(BF16), b.astype(BF16), (((0,), (0,)), ((), ())),
                           preferred_element_type=F32)


def _sigmoid(v):
    return 1.0 / (1.0 + jnp.exp(-v))


def _silu(v):
    return v * _sigmoid(v)


def _layer_norm(v, g, b):
    mu = jnp.mean(v, axis=-1, keepdims=True)
    d = v - mu
    var = jnp.mean(d * d, axis=-1, keepdims=True)
    return d * lax.rsqrt(var + LN_EPS) * g + b


def _mm_kernel(x_ref, w_ref, o_ref):
    o_ref[...] = _bdot(x_ref[...], w_ref[...]).astype(o_ref.dtype)


def _matmul(x, w, out_dtype, name, tm=MM_TM, tn=MM_TN):
    m, k = x.shape
    n = w.shape[1]
    tn = min(tn, n)
    return pl.pallas_call(
        _mm_kernel,
        name=name,
        grid=(n // tn, m // tm),
        in_specs=[pl.BlockSpec((tm, k), lambda j, i: (i, 0)),
                  pl.BlockSpec((k, tn), lambda j, i: (0, j))],
        out_specs=pl.BlockSpec((tm, tn), lambda j, i: (i, j)),
        out_shape=jax.ShapeDtypeStruct((m, n), out_dtype),
        compiler_params=_cparams(("parallel", "parallel")),
    )(x, w)


def _dswa_kernel(q_ref, kc_ref, kp_ref, vc_ref, vp_ref, bias0_ref, bias_ref, o_ref, lse_ref, *, tq):
    nqb = tq // ATT_BLK
    lane = lax.broadcasted_iota(jnp.int32, (ATT_BLK, LANES), 1)
    scale = HEAD_DIM ** -0.5
    exp2_scale = scale * math.log2(math.e)
    for ri in range(q_ref.shape[0]):
        q = q_ref[ri]
        kwin = jnp.concatenate([kp_ref[ri], kc_ref[ri]], axis=0)
        vwin = jnp.concatenate([vp_ref[ri], vc_ref[ri]], axis=0)
        for c in range(nqb):
            b_ref = bias0_ref if c == 0 else bias_ref
            rows = slice(c * ATT_BLK, (c + 1) * ATT_BLK)
            lse_tile = jnp.zeros((ATT_BLK, LANES), F32)
            for h in range(HEADS_PER_GROUP):
                hs = slice(h * HEAD_DIM, (h + 1) * HEAD_DIM)
                kh = kwin[c * ATT_BLK:(c + 2) * ATT_BLK, hs]
                vh = vwin[c * ATT_BLK:(c + 2) * ATT_BLK, hs]
                t = _bdot_nt(q[rows, hs], kh) + b_ref[h]
                m = jnp.max(t, axis=-1, keepdims=True)
                p = jnp.exp2((t - m) * exp2_scale)
                l = jnp.sum(p, axis=-1, keepdims=True)
                o_ref[ri, rows, hs] = _bdot(p, vh) / l
                lse_tile = jnp.where(lane == h, m * scale + jnp.log(l), lse_tile)
            lse_ref[ri, rows, :] = lse_tile


def _in_proj_strided_kernel(x_ref, w_ref, o_ref, *scratch, dilation):
    res = _bdot(x_ref[...], w_ref[...])
    if dilation == 1:
        o_ref[0] = res.astype(o_ref.dtype)
    else:
        res_ref, = scratch
        rows = res.shape[0] // dilation
        for t in range(res.shape[1] // LANES):
            ls = slice(t * LANES, (t + 1) * LANES)
            res_ref[t] = res[:, ls]
            for r in range(dilation):
                o_ref[r, :, ls] = res_ref[t, pl.ds(r, rows, stride=dilation), :].astype(o_ref.dtype)


def _in_proj_strided(x, w, dilation, batch, seq, tm=PROJ_A_TM):
    k = x.shape[1]
    wn = tn = w.shape[1]
    tpb = seq // tm
    scratch = [] if dilation == 1 else [pltpu.VMEM((tn // LANES, tm, LANES), F32)]
    return pl.pallas_call(
        functools.partial(_in_proj_strided_kernel, dilation=dilation),
        name=f"in_proj_attn_d{dilation}",
        grid=(batch * tpb, wn // tn),
        in_specs=[pl.BlockSpec((tm, k), lambda i, j: (i, 0)),
                  pl.BlockSpec((k, tn), lambda i, j: (0, j))],
        out_specs=pl.BlockSpec((None, dilation, tm // dilation, tn), lambda i, j: (i // tpb, 0, i % tpb, j)),
        out_shape=jax.ShapeDtypeStruct((batch, dilation, seq // dilation, wn), BF16),
        scratch_shapes=scratch,
        compiler_params=_cparams(("parallel", "parallel")),
    )(x, w)


def _dswa_group(qkv, bias, dilation):
    batch, _, n, _ = qkv.shape
    tq = min(DSWA_ROWS, n)
    rr = DSWA_ROWS // tq
    nqb = tq // ATT_BLK

    def cur(off):
        return pl.BlockSpec((None, rr, tq, A_GROUP_W), lambda b, r, j: (b, r, j, off))

    def prev(off):
        return pl.BlockSpec((None, rr, ATT_BLK, A_GROUP_W),
                            lambda b, r, j: (b, r, jnp.maximum(j * nqb - 1, 0), off))

    return pl.pallas_call(
        functools.partial(_dswa_kernel, tq=tq),
        name=f"dswa_d{dilation}",
        grid=(batch, dilation // rr, n // tq),
        in_specs=[cur(0), cur(1), prev(1), cur(2), prev(2),
                  pl.BlockSpec((None, HEADS_PER_GROUP, ATT_BLK, 2 * ATT_BLK),
                               lambda b, r, j: (jnp.minimum(j, 1), 0, 0, 0)),
                  pl.BlockSpec((None, HEADS_PER_GROUP, ATT_BLK, 2 * ATT_BLK), lambda b, r, j: (1, 0, 0, 0))],
        out_specs=[pl.BlockSpec((None, rr, tq, A_GROUP_W), lambda b, r, j: (b, r, j, 0)),
                   pl.BlockSpec((None, rr, tq, LANES), lambda b, r, j: (b, r, j, 0))],
        out_shape=[jax.ShapeDtypeStruct((batch, dilation, n, A_GROUP_W), F32),
                   jax.ShapeDtypeStruct((batch, dilation, n, LANES), F32)],
        compiler_params=_cparams(("parallel", "parallel", "parallel")),
    )(qkv, qkv, qkv, qkv, qkv, bias, bias)


def _t5_causal_bucket(dist):
    num_exact = NUM_BUCKETS // 2
    d = jnp.maximum(dist, 1).astype(F32)
    large = num_exact + (jnp.log(d / num_exact) / math.log(MAX_DISTANCE / num_exact)
                         * (NUM_BUCKETS - num_exact)).astype(jnp.int32)
    large = jnp.minimum(large, NUM_BUCKETS - 1)
    return jnp.where(dist < num_exact, dist, large)


def _band_bias(rel_bias, gi, dilation):
    qi = jnp.arange(ATT_BLK)[:, None] + ATT_BLK
    kj = jnp.arange(2 * ATT_BLK)[None, :]
    band = (qi - kj >= 0) & (qi - kj <= ATT_BLK)
    delta = jnp.maximum(qi - kj, 0) * dilation
    table = rel_bias[:, gi * HEADS_PER_GROUP:(gi + 1) * HEADS_PER_GROUP].astype(F32)
    onehot = (_t5_causal_bucket(delta)[..., None] == jnp.arange(NUM_BUCKETS)).astype(F32)
    bias = jnp.einsum('qkn,nh->hqk', onehot, table, precision=lax.Precision.HIGHEST) / (HEAD_DIM ** -0.5)
    return jnp.stack([jnp.where(band & (kj >= ATT_BLK), bias, NEG_BIG), jnp.where(band, bias, NEG_BIG)])


def _dot16(a, b):
    return jnp.dot(a, b, preferred_element_type=F32)


def _unit_lower_inverse(a, row, col):
    base = 16
    heads = list(a)
    eye = (row == col).astype(F32)
    same = lambda sz: (row // sz) == (col // sz)
    blk = same(base)
    a_d = {h: jnp.where(blk, a[h], 0.0) for h in heads}
    t = {h: eye - a_d[h] for h in heads}
    p = {h: a_d[h].astype(BF16) for h in heads}
    for _ in range(3):
        p = {h: _dot16(p[h], p[h]).astype(BF16) for h in heads}
        t = {h: t[h] + _dot16(t[h].astype(BF16), p[h]) for h in heads}
    sz = 2 * base
    while sz <= GDN_CHUNK:
        off = same(sz) & jnp.logical_not(same(sz // 2))
        tb = {h: t[h].astype(BF16) for h in heads}
        m = {h: _dot16(jnp.where(off, a[h], 0.0).astype(BF16), tb[h]).astype(BF16) for h in heads}
        t = {h: t[h] - _dot16(tb[h], m[h]) for h in heads}
        sz *= 2
    return t


def _gdn_kernel(x_ref, halo_ref, z_ref, ba_ref, cw_ref, alog_ref, dtb_ref, onw_ref, y_ref, state_ref):
    c = pl.program_id(1)

    @pl.when(c == 0)
    def _():
        state_ref[...] = jnp.zeros_like(state_ref)

    C = GDN_CHUNK
    chunks = range(x_ref.shape[0] // C)
    rows = {cc: slice(cc * C, (cc + 1) * C) for cc in chunks}
    keep_halo = (c > 0).astype(F32)
    row = lax.broadcasted_iota(jnp.int32, (C, C), 0)
    col = lax.broadcasted_iota(jnp.int32, (C, C), 1)
    incl = row >= col
    strict = row > col
    ones_l = incl.astype(BF16)

    def log_decay_cumsum(cc):
        ba = ba_ref[rows[cc], :]
        a_in = ba[:, LANES:] + dtb_ref[...]
        softplus = jnp.maximum(a_in, 0.0) + jnp.log(1.0 + jnp.exp(-jnp.abs(a_in)))
        g_all = -jnp.exp(alog_ref[...]) * softplus
        g1 = g_all.astype(BF16)
        r1 = g_all - g1.astype(F32)
        g2 = r1.astype(BF16)
        g3 = (r1 - g2.astype(F32)).astype(BF16)
        return (jnp.dot(ones_l, g1, preferred_element_type=F32) + jnp.dot(ones_l, g2, preferred_element_type=F32)
                + jnp.dot(ones_l, g3, preferred_element_type=F32))

    beta_all = {cc: _sigmoid(ba_ref[rows[cc], :LANES]) for cc in chunks}
    G = {cc: log_decay_cumsum(cc) for cc in chunks}
    GT = {cc: G[cc].T for cc in chunks}
    exp_g = {cc: jnp.exp(G[cc]) for cc in chunks}
    beta_exp_g = {cc: beta_all[cc] * exp_g[cc] for cc in chunks}
    exp_rest = {cc: jnp.exp(G[cc][C - 1:C, :] - G[cc]) for cc in chunks}

    def conv_silu(cc, col0):
        cs = slice(col0, col0 + HEAD_DIM)
        xc = x_ref[rows[cc], cs]
        halo = halo_ref[:, cs] * keep_halo if cc == 0 else x_ref[cc * C - SUBLANES:cc * C, cs]
        xf = jnp.concatenate([halo, xc], axis=0)
        w = cw_ref[:, cs]
        y = w[CONV_WIDTH - 1:CONV_WIDTH] * xc
        for i in range(CONV_WIDTH - 1):
            off = SUBLANES - (CONV_WIDTH - 1) + i
            y = y + w[i:i + 1] * xf[off:off + C]
        return _silu(y)

    def l2norm(t, scale=1.0):
        return t * (lax.rsqrt(jnp.sum(t * t, axis=-1, keepdims=True) + RMS_EPS) * scale)

    items = [(cc, h) for cc in chunks for h in range(N_HEADS_B)]
    q = {(cc, h): l2norm(conv_silu(cc, h * HEAD_DIM), HEAD_DIM ** -0.5) for cc, h in items}
    k = {(cc, h): l2norm(conv_silu(cc, B_W + h * HEAD_DIM)) for cc, h in items}
    v = {(cc, h): conv_silu(cc, 2 * B_W + h * HEAD_DIM) for cc, h in items}
    beta = {(cc, h): beta_all[cc][:, h:h + 1] for cc, h in items}
    gc = {(cc, h): G[cc][:, h:h + 1] for cc, h in items}
    g_last = {(cc, h): G[cc][C - 1:C, h:h + 1] for cc, h in items}
    decay = {(cc, h): jnp.exp(jnp.where(incl, gc[cc, h] - GT[cc][h:h + 1, :], NEG_BIG)) for cc, h in items}
    eg = {(cc, h): exp_g[cc][:, h:h + 1] for cc, h in items}
    kb ={it: k[it].astype(BF16) for it in items}
    kq = {it: lax.dot_general(jnp.concatenate([kb[it], q[it].astype(BF16)], axis=0), kb[it],
                              (((1,), (1,)), ((), ())), preferred_element_type=F32) for it in items}
    a = {it: jnp.where(strict, beta[it] * kq[it][:C] * decay[it], 0.0) for it in items}
    t_inv = _unit_lower_inverse(a, row, col)
    rhs = {(cc, h): jnp.concatenate([beta[cc, h] * v[cc, h], beta_exp_g[cc][:, h:h + 1] * k[cc, h]],
                                    axis=1).astype(BF16) for cc, h in items}
    sol = {it: _dot16(t_inv[it].astype(BF16), rhs[it]) for it in items}
    qk = {it: (kq[it][C:] * decay[it]).astype(BF16) for it in items}
    wq = {it: jnp.concatenate([sol[it][:, HEAD_DIM:], q[it] * eg[it]], axis=0).astype(BF16) for it in items}
    k_dec = {(cc, h): (k[cc, h] * exp_rest[cc][:, h:h + 1]).astype(BF16) for cc, h in items}

    heads = range(N_HEADS_B)
    state = {h: state_ref[h] for h in heads}
    for cc in chunks:
        ws = {h: _dot16(wq[cc, h], state[h].astype(BF16)) for h in heads}
        u = {h: (sol[cc, h][:, :HEAD_DIM] - ws[h][:C]).astype(BF16) for h in heads}
        o = {h: ws[h][C:] + _dot16(qk[cc, h], u[h]) for h in heads}
        state = {h: jnp.exp(g_last[cc, h]) * state[h] + lax.dot_general(
            k_dec[cc, h], u[h], (((0,), (0,)), ((), ())), preferred_element_type=F32) for h in heads}
        for h in heads:
            oh = o[h] * lax.rsqrt(jnp.mean(o[h] * o[h], axis=-1, keepdims=True) + RMS_EPS) * onw_ref[...]
            hs = slice(h * HEAD_DIM, (h + 1) * HEAD_DIM)
            y_ref[rows[cc], hs] = (oh * _silu(z_ref[rows[cc], hs])).astype(y_ref.dtype)
    for h in heads:
        state_ref[h] = state[h]


def _gated_deltanet(proj_b, conv_w, a_log, dt_bias, o_norm_w, batch, seq):
    C = GDN_STEP_CHUNKS * GDN_CHUNK
    wb = proj_b.shape[1]
    pb = proj_b.reshape(batch, seq, wb)
    ba_blk = (wb - 2 * LANES) // (2 * LANES)
    pad = lambda t: jnp.pad(t.astype(F32), (0, LANES - t.shape[0])).reshape(1, LANES)
    const = lambda shape: pl.BlockSpec(shape, lambda b, c: (0,) * len(shape))
    y = pl.pallas_call(
        _gdn_kernel,
        name="gated_deltanet",
        grid=(batch, seq // C),
        in_specs=[pl.BlockSpec((None, C, 3 * B_W), lambda b, c: (b, c, 0)),
                  pl.BlockSpec((None, SUBLANES, 3 * B_W),
                               lambda b, c: (b, jnp.maximum(c * (C // SUBLANES) - 1, 0), 0)),
                  pl.BlockSpec((None, C, B_W), lambda b, c: (b, c, 3)),
                  pl.BlockSpec((None, C, 2 * LANES), lambda b, c: (b, c, ba_blk)),
                  const((CONV_WIDTH, 3 * B_W)), const((1, LANES)), const((1, LANES)), const((1, LANES))],
        out_specs=pl.BlockSpec((None, C, B_W), lambda b, c: (b, c, 0)),
        out_shape=jax.ShapeDtypeStruct((batch, seq, B_W), BF16),
        scratch_shapes=[pltpu.VMEM((N_HEADS_B, HEAD_DIM, HEAD_DIM), F32)],
        compiler_params=_cparams(("parallel", "arbitrary")),
    )(pb, pb, pb, pb, conv_w.astype(F32), pad(a_log), pad(dt_bias), o_norm_w.astype(F32).reshape(1, LANES))
    return y.reshape(batch * seq, B_W)


def _mix_out_kernel(o0, o1, o2, l0, l1, l2, yb_ref, ga_ref, gb_ref, x_ref, woa_ref, wob_ref, wout_ref,
                    g_ref, b_ref, *rest, with_router):
    if with_router:
        wr_ref, out_ref, logits_ref, *scratch = rest
    else:
        out_ref, *scratch = rest

    def token_order(ref, scr):
        dilation, rows, width = ref.shape
        if dilation == 1:
            return ref[0]
        planes = []
        for t in range(width // LANES):
            for r in range(dilation):
                scr[t, pl.ds(r, rows, stride=dilation), :] = ref[r, :, t * LANES:(t + 1) * LANES]
            planes.append(scr[t])
        return jnp.concatenate(planes, axis=1)

    outs = (token_order(o0, None), token_order(o1, scratch[0]), token_order(o2, scratch[1]))
    lses = (token_order(l0, None), token_order(l1, scratch[2]), token_order(l2, scratch[3]))
    tm = x_ref.shape[0]
    for sub in range(OUT_SUBTILES):
        rs = slice(sub * tm // OUT_SUBTILES, (sub + 1) * tm // OUT_SUBTILES)
        lse_s = [t[rs] for t in lses]
        m = jnp.maximum(jnp.maximum(lse_s[0], lse_s[1]), lse_s[2])
        es = [jnp.exp(t - m) for t in lse_s]
        inv = 1.0 / (es[0] + es[1] + es[2])
        wgt = [e * inv for e in es]
        ya = []
        for h in range(HEADS_PER_GROUP):
            hs = slice(h * HEAD_DIM, (h + 1) * HEAD_DIM)
            ya.append(wgt[0][:, h:h + 1] * outs[0][rs, hs] + wgt[1][:, h:h + 1] * outs[1][rs, hs]
                      + wgt[2][:, h:h + 1] * outs[2][rs, hs])
        ya = jnp.concatenate(ya, axis=1)
        pa = _bdot(ya, woa_ref[...])
        pb = _bdot(yb_ref[rs, :], wob_ref[...])
        merged = _sigmoid(ga_ref[rs, :]) * pa + _sigmoid(gb_ref[rs, :]) * pb
        mix = _bdot(merged, wout_ref[...])
        out = _layer_norm(ALPHA * x_ref[rs, :] + mix, g_ref[...], b_ref[...])
        out_ref[rs, :] = out
        if with_router:
            logits_ref[rs, :] = _bdot(out, wr_ref[...])


def _mix_out(outs, lses, yb, proj_b, x, w_oa, w_ob, w_out, ln_g, ln_b, w_router=None, tm=OUT_TM):
    n = x.shape[0]
    with_router = w_router is not None
    seq = outs[0].shape[1] * outs[0].shape[2]
    tpb = seq // tm
    rowblk = lambda w, cb=0: pl.BlockSpec((tm, w), lambda i: (i, cb))
    const = lambda a: pl.BlockSpec(a.shape, lambda i: (0, 0), pipeline_mode=pl.Buffered(1))

    def grouped(a):
        d, w = a.shape[1], a.shape[3]
        return pl.BlockSpec((None, d, tm // d, w), lambda i: (i // tpb, 0, i % tpb, 0))

    wa, wb, wo = w_oa.astype(BF16), w_ob.astype(BF16), w_out.astype(BF16)
    g, b = ln_g.reshape(1, D_MODEL), ln_b.reshape(1, D_MODEL)
    router = [w_router] if with_router else []
    out_specs = [rowblk(D_MODEL)] + ([rowblk(LANES)] if with_router else [])
    out_shape = [jax.ShapeDtypeStruct((n, D_MODEL), F32)] + (
        [jax.ShapeDtypeStruct((n, LANES), F32)] if with_router else [])
    res = pl.pallas_call(
        functools.partial(_mix_out_kernel, with_router=with_router),
        name="mix_out_ln",
        grid=(n // tm,),
        in_specs=[grouped(a) for a in (*outs, *lses)]
        + [rowblk(B_W), rowblk(D_MODEL, 4), rowblk(D_MODEL, 5), rowblk(D_MODEL),
           const(wa), const(wb), const(wo), const(g), const(b)] + [const(a) for a in router],
        out_specs=out_specs,
        out_shape=out_shape,
        scratch_shapes=[pltpu.VMEM((A_GROUP_W // LANES, tm, LANES), F32)] * 2 + [pltpu.VMEM((1, tm, LANES), F32)] * 2,
        compiler_params=_cparams(("parallel",)),
    )(*outs, *lses, yb, proj_b, proj_b, x, wa, wb, wo, g, b, *router)
    return tuple(res) if with_router else res[0]


def _ffn_kernel(x_ref, wg_ref, wu_ref, wd_ref, g_ref, b_ref, out_ref):
    x = x_ref[...]
    xb = x.astype(BF16)
    dff = wg_ref.shape[1]
    acc = None
    for c0 in range(0, dff, FFN_TF):
        c1 = min(c0 + FFN_TF, dff)
        gate = jnp.dot(xb, wg_ref[:, c0:c1], preferred_element_type=F32)
        up = jnp.dot(xb, wu_ref[:, c0:c1], preferred_element_type=F32)
        part = _bdot(_silu(gate) * up, wd_ref[c0:c1, :])
        acc = part if acc is None else acc + part
    out_ref[...] = _layer_norm(ALPHA * x + acc, g_ref[...], b_ref[...])


def _dense_ffn(x, w_gate, w_up, w_down, ln_g, ln_b, tm=FFN_TM):
    n = x.shape[0]
    g, b = ln_g.reshape(1, D_MODEL), ln_b.reshape(1, D_MODEL)
    resident = lambda a: pl.BlockSpec(a.shape, lambda i: (0, 0), pipeline_mode=pl.Buffered(1))
    wg, wu, wd = w_gate.astype(BF16), w_up.astype(BF16), w_down.astype(BF16)
    return pl.pallas_call(
        _ffn_kernel,
        name="dense_ffn_ln",
        grid=(n // tm,),
        in_specs=[pl.BlockSpec((tm, D_MODEL), lambda i: (i, 0)),
                  resident(wg), resident(wu), resident(wd), resident(g), resident(b)],
        out_specs=pl.BlockSpec((tm, D_MODEL), lambda i: (i, 0)),
        out_shape=jax.ShapeDtypeStruct((n, D_MODEL), F32),
        compiler_params=_cparams(("parallel",)),
    )(x, wg, wu, wd, g, b)


def _gather_rows(idx_ref, src_hbm, dst, sem, count):
    def body(i, carry):
        pltpu.make_async_copy(src_hbm.at[pl.ds(idx_ref[0, i], 1)], dst.at[pl.ds(i, 1)], sem).start()
        return carry
    lax.fori_loop(0, count, body, 0, unroll=8)


def _wait_rows(src_hbm, dst, sem, count):
    pltpu.make_async_copy(src_hbm.at[pl.ds(0, count)], dst, sem).wait()


def _dispatch_kernel(dest_ref, pad_ref, x_ref, xs_out, zero_ref, sem):
    tm = x_ref.shape[0]
    zero_row = lambda r: pltpu.make_async_copy(zero_ref.at[pl.ds(0, 1)], xs_out.at[pl.ds(r, 1)], sem.at[1])
    zero_tile = lambda t: pltpu.make_async_copy(
        zero_ref, xs_out.at[pl.ds(pl.multiple_of(t * SUBLANES, SUBLANES), SUBLANES)], sem.at[1])

    def for_padding_rows(fn):
        for e in range(N_EXPERTS):
            lo, hi = pad_ref[0, e], pad_ref[0, N_EXPERTS + e]
            lo_tile = lax.shift_right_logical(lo + (SUBLANES - 1), SUBLANE_BITS)
            hi_tile = lax.shift_right_logical(hi, SUBLANE_BITS)
            lax.fori_loop(lo, jnp.minimum(lo_tile * SUBLANES, hi), lambda r, c: (fn(zero_row(r)), c)[1], 0)
            lax.fori_loop(lo_tile, hi_tile, lambda t, c: (fn(zero_tile(t)), c)[1], 0)

    @pl.when(pl.program_id(0) == 0)
    def _():
        zero_ref[...] = jnp.zeros_like(zero_ref)
        for_padding_rows(lambda copy: copy.start())

    @pl.when(pl.program_id(0) == pl.num_programs(0) - 1)
    def _():
        for_padding_rows(lambda copy: copy.wait())

    def body(i, carry):
        for k in range(TOP_K):
            pltpu.make_async_copy(x_ref.at[pl.ds(i, 1)], xs_out.at[pl.ds(dest_ref[0, TOP_K * i + k], 1)],
                                  sem.at[0]).start()
        return carry
    lax.fori_loop(0, tm, body, 0, unroll=4)
    for _ in range(TOP_K):
        pltpu.make_async_copy(x_ref, xs_out.at[pl.ds(0, tm)], sem.at[0]).wait()


def _moe_dispatch(x, dest, pad_ranges, n_rows, tm=DSP_TM):
    n = x.shape[0]
    dest3 = dest.reshape(n // tm, 1, TOP_K * tm)
    return pl.pallas_call(
        _dispatch_kernel,
        name="moe_dispatch",
        grid=(n // tm,),
        in_specs=[pl.BlockSpec((None, 1, TOP_K * tm), lambda i: (i, 0, 0), memory_space=pltpu.SMEM),
                  pl.BlockSpec((1, 2 * N_EXPERTS), lambda i: (0, 0), memory_space=pltpu.SMEM),
                  pl.BlockSpec((tm, D_MODEL), lambda i: (i, 0))],
        out_specs=pl.BlockSpec(memory_space=pl.ANY),
        out_shape=jax.ShapeDtypeStruct((n_rows, D_MODEL), F32),
        scratch_shapes=[pltpu.VMEM((SUBLANES, D_MODEL), F32), pltpu.SemaphoreType.DMA((2,))],
        compiler_params=_cparams(("arbitrary",), disable_bounds_checks=True),
    )(dest3, pad_ranges.reshape(1, 2 * N_EXPERTS), x)


def _moe_kernel(meta_ref, x_ref, wg_ref, wu_ref, wd_ref, y_ref, xb_ref, acc_ref):
    b = pl.program_id(0)
    f = pl.program_id(1)
    nblk = pl.num_programs(0)
    used = b < meta_ref[nblk]

    @pl.when(used)
    def _():
        @pl.when(f == 0)
        def _():
            xb_ref[...] = x_ref[...].astype(BF16)
            acc_ref[...] = jnp.zeros_like(acc_ref)

        xb = xb_ref[...]
        gate = jnp.dot(xb, wg_ref[...], preferred_element_type=F32)
        up = jnp.dot(xb, wu_ref[...], preferred_element_type=F32)
        acc_ref[...] += _bdot(_silu(gate) * up, wd_ref[...])

    @pl.when(f == pl.num_programs(1) - 1)
    def _():
        @pl.when(used)
        def _():
            y_ref[...] = acc_ref[...]

        @pl.when(jnp.logical_not(used))
        def _():
            y_ref[...] = jnp.zeros_like(y_ref)


def _moe_experts(xs, meta, w_gate, w_up, w_down, tb=MOE_TB, tf=MOE_TF):
    nblk = xs.shape[0] // tb
    dffe = w_gate.shape[2]
    grid_spec = pltpu.PrefetchScalarGridSpec(
        num_scalar_prefetch=1,
        grid=(nblk, dffe // tf),
        in_specs=[pl.BlockSpec((tb, D_MODEL), lambda b, f, m: (jnp.minimum(b, m[nblk] - 1), 0)),
                  pl.BlockSpec((None, D_MODEL, tf), lambda b, f, m: (m[b], 0, f)),
                  pl.BlockSpec((None, D_MODEL, tf), lambda b, f, m: (m[b], 0, f)),
                  pl.BlockSpec((None, tf, D_MODEL), lambda b, f, m: (m[b], f, 0))],
        out_specs=pl.BlockSpec((tb, D_MODEL), lambda b, f, m: (b, 0)),
        scratch_shapes=[pltpu.VMEM((tb, D_MODEL), BF16), pltpu.VMEM((tb, D_MODEL), F32)],
    )
    return pl.pallas_call(
        _moe_kernel,
        name="moe_experts",
        grid_spec=grid_spec,
        out_shape=jax.ShapeDtypeStruct((nblk * tb, D_MODEL), F32),
        compiler_params=_cparams(("parallel", "arbitrary")),
    )(meta, xs, w_gate.astype(BF16), w_up.astype(BF16), w_down.astype(BF16))


def _combine_kernel(pos0_ref, posn_ref, y_hbm, gates_ref, x_ref, g_ref, b_ref, out_ref, ybuf, sem):
    i = pl.program_id(0)
    tm = out_ref.shape[0]
    slot = i % 2

    @pl.when(i == 0)
    def _():
        _gather_rows(pos0_ref, y_hbm, ybuf.at[0], sem.at[0], TOP_K * tm)

    _wait_rows(y_hbm, ybuf.at[slot], sem.at[slot], TOP_K * tm)

    @pl.when(i + 1 < pl.num_programs(0))
    def _():
        _gather_rows(posn_ref, y_hbm, ybuf.at[1 - slot], sem.at[1 - slot], TOP_K * tm)

    gates = gates_ref[...]
    f = gates[:, 0:1] * ybuf[slot, :tm] + gates[:, 1:2] * ybuf[slot, tm:]
    out_ref[...] = _layer_norm(ALPHA * x_ref[...] + f, g_ref[...], b_ref[...])


def _moe_combine(y, pos, gates, x, ln_g, ln_b, tm=CMB_TM):
    n = x.shape[0]
    nt = n // tm
    pos3 = pos.reshape(nt, tm, TOP_K).transpose(0, 2, 1).reshape(nt, 1, TOP_K * tm)
    g, b = ln_g.reshape(1, D_MODEL), ln_b.reshape(1, D_MODEL)
    smem_blk = lambda imap: pl.BlockSpec((None, 1, TOP_K * tm), imap, memory_space=pltpu.SMEM)
    return pl.pallas_call(
        _combine_kernel,
        name="moe_combine_ln",
        grid=(nt,),
        in_specs=[smem_blk(lambda i: (0, 0, 0)),
                  smem_blk(lambda i: (jnp.minimum(i + 1, nt - 1), 0, 0)),
                  pl.BlockSpec(memory_space=pl.ANY),
                  pl.BlockSpec((tm, TOP_K), lambda i: (i, 0)),
                  pl.BlockSpec((tm, D_MODEL), lambda i: (i, 0)),
                  pl.BlockSpec((1, D_MODEL), lambda i: (0, 0)),
                  pl.BlockSpec((1, D_MODEL), lambda i: (0, 0))],
        out_specs=pl.BlockSpec((tm, D_MODEL), lambda i: (i, 0)),
        out_shape=jax.ShapeDtypeStruct((n, D_MODEL), F32),
        scratch_shapes=[pltpu.VMEM((2, TOP_K * tm, D_MODEL), F32), pltpu.SemaphoreType.DMA((2,))],
        compiler_params=_cparams(("arbitrary",), disable_bounds_checks=True),
    )(pos3, pos3, y, gates, x, g, b)


def _moe_routing(logits, tb):
    n = logits.shape[0]
    top_logit, top_idx = lax.top_k(logits, TOP_K)
    gates = jax.nn.softmax(top_logit, axis=-1)
    na = n * TOP_K
    e_flat = top_idx.reshape(-1).astype(jnp.int32)
    onehot = (e_flat[None, :] == jnp.arange(N_EXPERTS, dtype=jnp.int32)[:, None]).astype(jnp.int32)
    running = jnp.cumsum(onehot, axis=1)
    counts = running[:, -1]
    padded = (counts + tb - 1) // tb * tb
    pend = jnp.cumsum(padded)
    pstart = pend - padded
    dest = jnp.sum(onehot * (pstart[:, None] + running - 1), axis=0).reshape(n, TOP_K)
    nblk = -(-na // tb) + N_EXPERTS
    block_expert = jnp.minimum(jnp.searchsorted(pend, jnp.arange(nblk, dtype=jnp.int32) * tb, side='right'),
                               N_EXPERTS - 1).astype(jnp.int32)
    meta = jnp.concatenate([block_expert, (pend[-1:] // tb).astype(jnp.int32)])
    pad_ranges = jnp.concatenate([pstart + counts, pend[:-1], jnp.full((1,), nblk * tb)]).astype(jnp.int32)
    return dest.astype(jnp.int32), gates, meta, pad_ranges, nblk


def _router_weight(w_router):
    return jnp.pad(w_router, ((0, 0), (0, LANES - N_EXPERTS))).astype(BF16)


def _moe_ffn(x, logits, w_gate, w_up, w_down, ln_g, ln_b):
    dest, gates, meta, pad_ranges, nblk = _moe_routing(logits[:, :N_EXPERTS], MOE_TB)
    xs = _moe_dispatch(x, dest, pad_ranges, nblk * MOE_TB)
    y = _moe_experts(xs, meta, w_gate, w_up, w_down)
    return _moe_combine(y, dest, gates, x, ln_g, ln_b)


def _split_w_in(w):
    a_end = 3 * A_QKV_W
    bz_end = a_end + 3 * B_W + B_W
    w_groups = [jnp.concatenate([w[:, s * A_QKV_W + gi * A_GROUP_W:s * A_QKV_W + (gi + 1) * A_GROUP_W]
                                 for s in range(3)], axis=1).astype(BF16) for gi in range(N_GROUPS)]
    zpad = jnp.zeros((w.shape[0], LANES - N_HEADS_B), w.dtype)
    w_b = jnp.concatenate([w[:, a_end:bz_end], w[:, bz_end + 2 * N_HEADS_B:],
                           w[:, bz_end:bz_end + N_HEADS_B], zpad,
                           w[:, bz_end + N_HEADS_B:bz_end + 2 * N_HEADS_B], zpad], axis=1)
    return w_groups, w_b.astype(BF16)


def _hybrid_layer(x, batch, seq, rel_bias, w_in, conv_w, a_log, dt_bias, o_norm_w, w_oa, w_ob, w_out, ln_g, ln_b,
                  w_router=None):
    w_groups, w_b = _split_w_in(w_in)
    proj_b = _matmul(x, w_b, F32, "in_proj_gdn_gates")
    outs, lses = [], []
    for gi, (_, dilation) in enumerate(DSWA_PATTERNS):
        qkv = _in_proj_strided(x, w_groups[gi], dilation, batch, seq)
        o, lse = _dswa_group(qkv, _band_bias(rel_bias, gi, dilation), dilation)
        outs.append(o)
        lses.append(lse)
    yb = _gated_deltanet(proj_b, conv_w, a_log, dt_bias, o_norm_w, batch, seq)
    return _mix_out(outs, lses, yb, proj_b, x, w_oa, w_ob, w_out, ln_g, ln_b, w_router)


def kernel(x, rel_bias, w_in, conv_w, a_log, dt_bias, o_norm_w, w_oa, w_ob, w_out, ln1_g, ln1_b,
           ffn_w_gate, ffn_w_up, ffn_w_down, moe_router, moe_w_gate, moe_w_up, moe_w_down, ln2_g, ln2_b):
    batch, seq, d = x.shape
    h = x.reshape(batch * seq, d)
    for layer in range(DEPTH):
        j = layer // 2
        is_moe = layer % 2 == 1
        h = _hybrid_layer(h, batch, seq, rel_bias, w_in[layer], conv_w[layer], a_log[layer], dt_bias[layer],
                          o_norm_w[layer], w_oa[layer], w_ob[layer], w_out[layer], ln1_g[layer], ln1_b[layer],
                          _router_weight(moe_router[j]) if is_moe else None)
        if is_moe:
            h, logits = h
            h = _moe_ffn(h, logits, moe_w_gate[j], moe_w_up[j], moe_w_down[j], ln2_g[layer], ln2_b[layer])
        else:
            h = _dense_ffn(h, ffn_w_gate[j], ffn_w_up[j], ffn_w_down[j], ln2_g[layer], ln2_b[layer])
    return h.reshape(batch, seq, d)
```

```python
import functools
import math

import jax
import jax.numpy as jnp
from jax import lax
from jax.experimental import pallas as pl
from jax.experimental.pallas import tpu as pltpu

F32 = jnp.float32
BF16 = jnp.bfloat16

D_MODEL = 1024
DEPTH = 2
DSWA_PATTERNS = ((128, 1), (512, 4), (2048, 16))
N_GROUPS = 3
HEADS_PER_GROUP = 4
HEAD_DIM = 128
A_QKV_W = N_GROUPS * HEADS_PER_GROUP * HEAD_DIM
A_GROUP_W = HEADS_PER_GROUP * HEAD_DIM
NUM_BUCKETS = 32
MAX_DISTANCE = 2048
N_HEADS_B = 8
B_W = N_HEADS_B * HEAD_DIM
CONV_WIDTH = 4
N_EXPERTS = 8
TOP_K = 2
ALPHA = (2 * DEPTH) ** 0.25
LN_EPS = 1e-5
RMS_EPS = 1e-6

LANES = 128
SUBLANES = 8
SUBLANE_BITS = SUBLANES.bit_length() - 1
VMEM_LIMIT = 56 * 1024 * 1024

ATT_BLK = 128
DSWA_ROWS = 2048
GDN_CHUNK = 128
GDN_STEP_CHUNKS = 2
MM_TM = 512
MM_TN = 3200
PROJ_A_TM = 1024
OUT_TM = 512
OUT_SUBTILES = 2
FFN_TM = 512
FFN_TF = 1024
MOE_TB = 512
MOE_TF = 1792
CMB_TM = 512
DSP_TM = 512

NEG_BIG = -1e30


def _cparams(sem, vmem=VMEM_LIMIT, **kw):
    return pltpu.CompilerParams(dimension_semantics=sem, vmem_limit_bytes=vmem, **kw)


def _bdot(a, b):
    return jnp.dot(a.astype(BF16), b.astype(BF16), preferred_element_type=F32)


def _bdot_nt(a, b):
    return lax.dot_general(a.astype(BF16), b.astype(BF16), (((1,), (1,)), ((), ())),
                           preferred_element_type=F32)


def _sigmoid(v):
    return 1.0 / (1.0 + jnp.exp(-v))


def _silu(v):
    return v * _sigmoid(v)


def _layer_norm(v, g, b):
    mu = jnp.mean(v, axis=-1, keepdims=True)
    d = v - mu
    var = jnp.mean(d * d, axis=-1, keepdims=True)
    return d * lax.rsqrt(var + LN_EPS) * g + b


def _mm_kernel(x_ref, w_ref, o_ref):
    o_ref[...] = _bdot(x_ref[...], w_ref[...]).astype(o_ref.dtype)


def _matmul(x, w, out_dtype, name, tm=MM_TM, tn=MM_TN):
    m, k = x.shape
    n = w.shape[1]
    tn = min(tn, n)
    return pl.pallas_call(
        _mm_kernel,
        name=name,
        grid=(n // tn, m // tm),
        in_specs=[pl.BlockSpec((tm, k), lambda j, i: (i, 0)),
                  pl.BlockSpec((k, tn), lambda j, i: (0, j))],
        out_specs=pl.BlockSpec((tm, tn), lambda j, i: (i, j)),
        out_shape=jax.ShapeDtypeStruct((m, n), out_dtype),
        compiler_params=_cparams(("parallel", "parallel")),
    )(x, w)


def _dswa_kernel(q_ref, kc_ref, kp_ref, vc_ref, vp_ref, bias0_ref, bias_ref, o_ref, lse_ref, *, tq):
    nqb = tq // ATT_BLK
    lane = lax.broadcasted_iota(jnp.int32, (ATT_BLK, LANES), 1)
    scale = HEAD_DIM ** -0.5
    exp2_scale = scale * math.log2(math.e)
    for ri in range(q_ref.shape[0]):
        q = q_ref[ri]
        kwin = jnp.concatenate([kp_ref[ri], kc_ref[ri]], axis=0)
        vwin = jnp.concatenate([vp_ref[ri], vc_ref[ri]], axis=0)
        for c in range(nqb):
            b_ref = bias0_ref if c == 0 else bias_ref
            rows = slice(c * ATT_BLK, (c + 1) * ATT_BLK)
            lse_tile = jnp.zeros((ATT_BLK, LANES), F32)
            for h in range(HEADS_PER_GROUP):
                hs = slice(h * HEAD_DIM, (h + 1) * HEAD_DIM)
                kh = kwin[c * ATT_BLK:(c + 2) * ATT_BLK, hs]
                vh = vwin[c * ATT_BLK:(c + 2) * ATT_BLK, hs]
                t = _bdot_nt(q[rows, hs], kh) + b_ref[h]
                m = jnp.max(t, axis=-1, keepdims=True)
                p = jnp.exp2((t - m) * exp2_scale)
                l = jnp.sum(p, axis=-1, keepdims=True)
                o_ref[ri, rows, hs] = _bdot(p, vh) / l
                lse_tile = jnp.where(lane == h, m * scale + jnp.log(l), lse_tile)
            lse_ref[ri, rows, :] = lse_tile


def _in_proj_strided_kernel(x_ref, w_ref, o_ref, *scratch, dilation):
    res = _bdot(x_ref[...], w_ref[...])
    if dilation == 1:
        o_ref[0] = res.astype(o_ref.dtype)
    else:
        res_ref, = scratch
        rows = res.shape[0] // dilation
        for t in range(res.shape[1] // LANES):
            ls = slice(t * LANES, (t + 1) * LANES)
            res_ref[t] = res[:, ls]
            for r in range(dilation):
                o_ref[r, :, ls] = res_ref[t, pl.ds(r, rows, stride=dilation), :].astype(o_ref.dtype)


def _in_proj_strided(x, w, dilation, batch, seq, tm=PROJ_A_TM):
    k = x.shape[1]
    wn = tn = w.shape[1]
    tpb = seq // tm
    scratch = [] if dilation == 1 else [pltpu.VMEM((tn // LANES, tm, LANES), F32)]
    return pl.pallas_call(
        functools.partial(_in_proj_strided_kernel, dilation=dilation),
        name=f"in_proj_attn_d{dilation}",
        grid=(batch * tpb, wn // tn),
        in_specs=[pl.BlockSpec((tm, k), lambda i, j: (i, 0)),
                  pl.BlockSpec((k, tn), lambda i, j: (0, j))],
        out_specs=pl.BlockSpec((None, dilation, tm // dilation, tn), lambda i, j: (i // tpb, 0, i % tpb, j)),
        out_shape=jax.ShapeDtypeStruct((batch, dilation, seq // dilation, wn), BF16),
        scratch_shapes=scratch,
        compiler_params=_cparams(("parallel", "parallel")),
    )(x, w)


def _dswa_group(qkv, bias, dilation):
    batch, _, n, _ = qkv.shape
    tq = min(DSWA_ROWS, n)
    rr = DSWA_ROWS // tq
    nqb = tq // ATT_BLK

    def cur(off):
        return pl.BlockSpec((None, rr, tq, A_GROUP_W), lambda b, r, j: (b, r, j, off))

    def prev(off):
        return pl.BlockSpec((None, rr, ATT_BLK, A_GROUP_W),
                            lambda b, r, j: (b, r, jnp.maximum(j * nqb - 1, 0), off))

    return pl.pallas_call(
        functools.partial(_dswa_kernel, tq=tq),
        name=f"dswa_d{dilation}",
        grid=(batch, dilation // rr, n // tq),
        in_specs=[cur(0), cur(1), prev(1), cur(2), prev(2),
                  pl.BlockSpec((None, HEADS_PER_GROUP, ATT_BLK, 2 * ATT_BLK),
                               lambda b, r, j: (jnp.minimum(j, 1), 0, 0, 0)),
                  pl.BlockSpec((None, HEADS_PER_GROUP, ATT_BLK, 2 * ATT_BLK), lambda b, r, j: (1, 0, 0, 0))],
        out_specs=[pl.BlockSpec((None, rr, tq, A_GROUP_W), lambda b, r, j: (b, r, j, 0)),
                   pl.BlockSpec((None, rr, tq, LANES), lambda b, r, j: (b, r, j, 0))],
        out_shape=[jax.ShapeDtypeStruct((batch, dilation, n, A_GROUP_W), F32),
                   jax.ShapeDtypeStruct((batch, dilation, n, LANES), F32)],
        compiler_params=_cparams(("parallel", "parallel", "parallel")),
    )(qkv, qkv, qkv, qkv, qkv, bias, bias)


def _t5_causal_bucket(dist):
    num_exact = NUM_BUCKETS // 2
    d = jnp.maximum(dist, 1).astype(F32)
    large = num_exact + (jnp.log(d / num_exact) / math.log(MAX_DISTANCE / num_exact)
                         * (NUM_BUCKETS - num_exact)).astype(jnp.int32)
    large = jnp.minimum(large, NUM_BUCKETS - 1)
    return jnp.where(dist < num_exact, dist, large)


def _band_bias(rel_bias, gi, dilation):
    qi = jnp.arange(ATT_BLK)[:, None] + ATT_BLK
    kj = jnp.arange(2 * ATT_BLK)[None, :]
    band = (qi - kj >= 0) & (qi - kj <= ATT_BLK)
    delta = jnp.maximum(qi - kj, 0) * dilation
    table = rel_bias[:, gi * HEADS_PER_GROUP:(gi + 1) * HEADS_PER_GROUP].astype(F32)
    onehot = (_t5_causal_bucket(delta)[..., None] == jnp.arange(NUM_BUCKETS)).astype(F32)
    bias = jnp.einsum('qkn,nh->hqk', onehot, table, precision=lax.Precision.HIGHEST) / (HEAD_DIM ** -0.5)
    return jnp.stack([jnp.where(band & (kj >= ATT_BLK), bias, NEG_BIG), jnp.where(band, bias, NEG_BIG)])


def _dot16(a, b):
    return jnp.dot(a, b, preferred_element_type=F32)


def _unit_lower_inverse(a, row, col):
    base = 16
    heads = list(a)
    eye = (row == col).astype(F32)
    same = lambda sz: (row // sz) == (col // sz)
    blk = same(base)
    a_d = {h: jnp.where(blk, a[h], 0.0) for h in heads}
    t = {h: eye - a_d[h] for h in heads}
    p = {h: a_d[h].astype(BF16) for h in heads}
    for _ in range(3):
        p = {h: _dot16(p[h], p[h]).astype(BF16) for h in heads}
        t = {h: t[h] + _dot16(t[h].astype(BF16), p[h]) for h in heads}
    sz = 2 * base
    while sz <= GDN_CHUNK:
        off = same(sz) & jnp.logical_not(same(sz // 2))
        tb = {h: t[h].astype(BF16) for h in heads}
        m = {h: _dot16(jnp.where(off, a[h], 0.0).astype(BF16), tb[h]).astype(BF16) for h in heads}
        t = {h: t[h] - _dot16(tb[h], m[h]) for h in heads}
        sz *= 2
    return t


def _gdn_kernel(x_ref, halo_ref, z_ref, ba_ref, cw_ref, alog_ref, dtb_ref, onw_ref, y_ref, state_ref):
    c = pl.program_id(1)

    @pl.when(c == 0)
    def _():
        state_ref[...] = jnp.zeros_like(state_ref)

    C = GDN_CHUNK
    chunks = range(x_ref.shape[0] // C)
    rows = {cc: slice(cc * C, (cc + 1) * C) for cc in chunks}
    keep_halo = (c > 0).astype(F32)
    row = lax.broadcasted_iota(jnp.int32, (C, C), 0)
    col = lax.broadcasted_iota(jnp.int32, (C, C), 1)
    incl = row >= col
    strict = row > col
    ones_l = incl.astype(BF16)

    def log_decay_cumsum(cc):
        ba = ba_ref[rows[cc], :]
        a_in = ba[:, LANES:] + dtb_ref[...]
        softplus = jnp.maximum(a_in, 0.0) + jnp.log(1.0 + jnp.exp(-jnp.abs(a_in)))
        g_all = -jnp.exp(alog_ref[...]) * softplus
        g1 = g_all.astype(BF16)
        r1 = g_all - g1.astype(F32)
        g2 = r1.astype(BF16)
        g3 = (r1 - g2.astype(F32)).astype(BF16)
        return (jnp.dot(ones_l, g1, preferred_element_type=F32) + jnp.dot(ones_l, g2, preferred_element_type=F32)
                + jnp.dot(ones_l, g3, preferred_element_type=F32))

    beta_all = {cc: _sigmoid(ba_ref[rows[cc], :LANES]) for cc in chunks}
    G = {cc: log_decay_cumsum(cc) for cc in chunks}
    GT = {cc: G[cc].T for cc in chunks}
    exp_g = {cc: jnp.exp(G[cc]) for cc in chunks}
    beta_exp_g = {cc: beta_all[cc] * exp_g[cc] for cc in chunks}
    exp_rest = {cc: jnp.exp(G[cc][C - 1:C, :] - G[cc]) for cc in chunks}

    def conv_silu(cc, col0):
        cs = slice(col0, col0 + HEAD_DIM)
        xc = x_ref[rows[cc], cs]
        halo = halo_ref[:, cs] * keep_halo if cc == 0 else x_ref[cc * C - SUBLANES:cc * C, cs]
        xf = jnp.concatenate([halo, xc], axis=0)
        w = cw_ref[:, cs]
        y = w[CONV_WIDTH - 1:CONV_WIDTH] * xc
        for i in range(CONV_WIDTH - 1):
            off = SUBLANES - (CONV_WIDTH - 1) + i
            y = y + w[i:i + 1] * xf[off:off + C]
        return _silu(y)

    def l2norm(t, scale=1.0):
        return t * (lax.rsqrt(jnp.sum(t * t, axis=-1, keepdims=True) + RMS_EPS) * scale)

    items = [(cc, h) for cc in chunks for h in range(N_HEADS_B)]
    q = {(cc, h): l2norm(conv_silu(cc, h * HEAD_DIM), HEAD_DIM ** -0.5) for cc, h in items}
    k = {(cc, h): l2norm(conv_silu(cc, B_W + h * HEAD_DIM)) for cc, h in items}
    v = {(cc, h): conv_silu(cc, 2 * B_W + h * HEAD_DIM) for cc, h in items}
    beta = {(cc, h): beta_all[cc][:, h:h + 1] for cc, h in items}
    gc = {(cc, h): G[cc][:, h:h + 1] for cc, h in items}
    g_last = {(cc, h): G[cc][C - 1:C, h:h + 1] for cc, h in items}
    decay = {(cc, h): jnp.exp(jnp.where(incl, gc[cc, h] - GT[cc][h:h + 1, :], NEG_BIG)) for cc, h in items}
    eg = {(cc, h): exp_g[cc][:, h:h + 1] for cc, h in items}
    kb ={it: k[it].astype(BF16) for it in items}
    kq = {it: lax.dot_general(jnp.concatenate([kb[it], q[it].astype(BF16)], axis=0), kb[it],
                              (((1,), (1,)), ((), ())), preferred_element_type=F32) for it in items}
    a = {it: jnp.where(strict, beta[it] * kq[it][:C] * decay[it], 0.0) for it in items}
    t_inv = _unit_lower_inverse(a, row, col)
    rhs = {(cc, h): jnp.concatenate([beta[cc, h] * v[cc, h], beta_exp_g[cc][:, h:h + 1] * k[cc, h]],
                                    axis=1).astype(BF16) for cc, h in items}
    sol = {it: _dot16(t_inv[it].astype(BF16), rhs[it]) for it in items}
    qk = {it: (kq[it][C:] * decay[it]).astype(BF16) for it in items}
    wq = {it: jnp.concatenate([sol[it][:, HEAD_DIM:], q[it] * eg[it]], axis=0).astype(BF16) for it in items}
    k_dec = {(cc, h): (k[cc, h] * exp_rest[cc][:, h:h + 1]).astype(BF16) for cc, h in items}

    heads = range(N_HEADS_B)
    state = {h: state_ref[h] for h in heads}
    for cc in chunks:
        ws = {h: _dot16(wq[cc, h], state[h].astype(BF16)) for h in heads}
        u = {h: (sol[cc, h][:, :HEAD_DIM] - ws[h][:C]).astype(BF16) for h in heads}
        o = {h: ws[h][C:] + _dot16(qk[cc, h], u[h]) for h in heads}
        state = {h: jnp.exp(g_last[cc, h]) * state[h] + lax.dot_general(
            k_dec[cc, h], u[h], (((0,), (0,)), ((), ())), preferred_element_type=F32) for h in heads}
        for h in heads:
            oh = o[h] * lax.rsqrt(jnp.mean(o[h] * o[h], axis=-1, keepdims=True) + RMS_EPS) * onw_ref[...]
            hs = slice(h * HEAD_DIM, (h + 1) * HEAD_DIM)
            y_ref[rows[cc], hs] = (oh * _silu(z_ref[rows[cc], hs])).astype(y_ref.dtype)
    for h in heads:
        state_ref[h] = state[h]


def _gated_deltanet(proj_b, conv_w, a_log, dt_bias, o_norm_w, batch, seq):
    C = GDN_STEP_CHUNKS * GDN_CHUNK
    wb = proj_b.shape[1]
    pb = proj_b.reshape(batch, seq, wb)
    ba_blk = (wb - 2 * LANES) // (2 * LANES)
    pad = lambda t: jnp.pad(t.astype(F32), (0, LANES - t.shape[0])).reshape(1, LANES)
    const = lambda shape: pl.BlockSpec(shape, lambda b, c: (0,) * len(shape))
    y = pl.pallas_call(
        _gdn_kernel,
        name="gated_deltanet",
        grid=(batch, seq // C),
        in_specs=[pl.BlockSpec((None, C, 3 * B_W), lambda b, c: (b, c, 0)),
                  pl.BlockSpec((None, SUBLANES, 3 * B_W),
                               lambda b, c: (b, jnp.maximum(c * (C // SUBLANES) - 1, 0), 0)),
                  pl.BlockSpec((None, C, B_W), lambda b, c: (b, c, 3)),
                  pl.BlockSpec((None, C, 2 * LANES), lambda b, c: (b, c, ba_blk)),
                  const((CONV_WIDTH, 3 * B_W)), const((1, LANES)), const((1, LANES)), const((1, LANES))],
        out_specs=pl.BlockSpec((None, C, B_W), lambda b, c: (b, c, 0)),
        out_shape=jax.ShapeDtypeStruct((batch, seq, B_W), BF16),
        scratch_shapes=[pltpu.VMEM((N_HEADS_B, HEAD_DIM, HEAD_DIM), F32)],
        compiler_params=_cparams(("parallel", "arbitrary")),
    )(pb, pb, pb, pb, conv_w.astype(F32), pad(a_log), pad(dt_bias), o_norm_w.astype(F32).reshape(1, LANES))
    return y.reshape(batch * seq, B_W)


def _mix_out_kernel(o0, o1, o2, l0, l1, l2, yb_ref, ga_ref, gb_ref, x_ref, woa_ref, wob_ref, wout_ref,
                    g_ref, b_ref, *rest, with_router):
    if with_router:
        wr_ref, out_ref, logits_ref, *scratch = rest
    else:
        out_ref, *scratch = rest

    def token_order(ref, scr):
        dilation, rows, width = ref.shape
        if dilation == 1:
            return ref[0]
        planes = []
        for t in range(width // LANES):
            for r in range(dilation):
                scr[t, pl.ds(r, rows, stride=dilation), :] = ref[r, :, t * LANES:(t + 1) * LANES]
            planes.append(scr[t])
        return jnp.concatenate(planes, axis=1)

    outs = (token_order(o0, None), token_order(o1, scratch[0]), token_order(o2, scratch[1]))
    lses = (token_order(l0, None), token_order(l1, scratch[2]), token_order(l2, scratch[3]))
    tm = x_ref.shape[0]
    for sub in range(OUT_SUBTILES):
        rs = slice(sub * tm // OUT_SUBTILES, (sub + 1) * tm // OUT_SUBTILES)
        lse_s = [t[rs] for t in lses]
        m = jnp.maximum(jnp.maximum(lse_s[0], lse_s[1]), lse_s[2])
        es = [jnp.exp(t - m) for t in lse_s]
        inv = 1.0 / (es[0] + es[1] + es[2])
        wgt = [e * inv for e in es]
        ya = []
        for h in range(HEADS_PER_GROUP):
            hs = slice(h * HEAD_DIM, (h + 1) * HEAD_DIM)
            ya.append(wgt[0][:, h:h + 1] * outs[0][rs, hs] + wgt[1][:, h:h + 1] * outs[1][rs, hs]
                      + wgt[2][:, h:h + 1] * outs[2][rs, hs])
        ya = jnp.concatenate(ya, axis=1)
        pa = _bdot(ya, woa_ref[...])
        pb = _bdot(yb_ref[rs, :], wob_ref[...])
        merged = _sigmoid(ga_ref[rs, :]) * pa + _sigmoid(gb_ref[rs, :]) * pb
        mix = _bdot(merged, wout_ref[...])
        out = _layer_norm(ALPHA * x_ref[rs, :] + mix, g_ref[...], b_ref[...])
        out_ref[rs, :] = out
        if with_router:
            logits_ref[rs, :] = _bdot(out, wr_ref[...])


def _mix_out(outs, lses, yb, proj_b, x, w_oa, w_ob, w_out, ln_g, ln_b, w_router=None, tm=OUT_TM):
    n = x.shape[0]
    with_router = w_router is not None
    seq = outs[0].shape[1] * outs[0].shape[2]
    tpb = seq // tm
    rowblk = lambda w, cb=0: pl.BlockSpec((tm, w), lambda i: (i, cb))
    const = lambda a: pl.BlockSpec(a.shape, lambda i: (0, 0), pipeline_mode=pl.Buffered(1))

    def grouped(a):
        d, w = a.shape[1], a.shape[3]
        return pl.BlockSpec((None, d, tm // d, w), lambda i: (i // tpb, 0, i % tpb, 0))

    wa, wb, wo = w_oa.astype(BF16), w_ob.astype(BF16), w_out.astype(BF16)
    g, b = ln_g.reshape(1, D_MODEL), ln_b.reshape(1, D_MODEL)
    router = [w_router] if with_router else []
    out_specs = [rowblk(D_MODEL)] + ([rowblk(LANES)] if with_router else [])
    out_shape = [jax.ShapeDtypeStruct((n, D_MODEL), F32)] + (
        [jax.ShapeDtypeStruct((n, LANES), F32)] if with_router else [])
    res = pl.pallas_call(
        functools.partial(_mix_out_kernel, with_router=with_router),
        name="mix_out_ln",
        grid=(n // tm,),
        in_specs=[grouped(a) for a in (*outs, *lses)]
        + [rowblk(B_W), rowblk(D_MODEL, 4), rowblk(D_MODEL, 5), rowblk(D_MODEL),
           const(wa), const(wb), const(wo), const(g), const(b)] + [const(a) for a in router],
        out_specs=out_specs,
        out_shape=out_shape,
        scratch_shapes=[pltpu.VMEM((A_GROUP_W // LANES, tm, LANES), F32)] * 2 + [pltpu.VMEM((1, tm, LANES), F32)] * 2,
        compiler_params=_cparams(("parallel",)),
    )(*outs, *lses, yb, proj_b, proj_b, x, wa, wb, wo, g, b, *router)
    return tuple(res) if with_router else res[0]


def _ffn_kernel(x_ref, wg_ref, wu_ref, wd_ref, g_ref, b_ref, out_ref):
    x = x_ref[...]
    xb = x.astype(BF16)
    dff = wg_ref.shape[1]
    acc = None
    for c0 in range(0, dff, FFN_TF):
        c1 = min(c0 + FFN_TF, dff)
        gate = jnp.dot(xb, wg_ref[:, c0:c1], preferred_element_type=F32)
        up = jnp.dot(xb, wu_ref[:, c0:c1], preferred_element_type=F32)
        part = _bdot(_silu(gate) * up, wd_ref[c0:c1, :])
        acc = part if acc is None else acc + part
    out_ref[...] = _layer_norm(ALPHA * x + acc, g_ref[...], b_ref[...])


def _dense_ffn(x, w_gate, w_up, w_down, ln_g, ln_b, tm=FFN_TM):
    n = x.shape[0]
    g, b = ln_g.reshape(1, D_MODEL), ln_b.reshape(1, D_MODEL)
    resident = lambda a: pl.BlockSpec(a.shape, lambda i: (0, 0), pipeline_mode=pl.Buffered(1))
    wg, wu, wd = w_gate.astype(BF16), w_up.astype(BF16), w_down.astype(BF16)
    return pl.pallas_call(
        _ffn_kernel,
        name="dense_ffn_ln",
        grid=(n // tm,),
        in_specs=[pl.BlockSpec((tm, D_MODEL), lambda i: (i, 0)),
                  resident(wg), resident(wu), resident(wd), resident(g), resident(b)],
        out_specs=pl.BlockSpec((tm, D_MODEL), lambda i: (i, 0)),
        out_shape=jax.ShapeDtypeStruct((n, D_MODEL), F32),
        compiler_params=_cparams(("parallel",)),
    )(x, wg, wu, wd, g, b)


def _gather_rows(idx_ref, src_hbm, dst, sem, count):
    def body(i, carry):
        pltpu.make_async_copy(src_hbm.at[pl.ds(idx_ref[0, i], 1)], dst.at[pl.ds(i, 1)], sem).start()
        return carry
    lax.fori_loop(0, count, body, 0, unroll=8)


def _wait_rows(src_hbm, dst, sem, count):
    pltpu.make_async_copy(src_hbm.at[pl.ds(0, count)], dst, sem).wait()


def _dispatch_kernel(dest_ref, pad_ref, x_ref, xs_out, zero_ref, sem):
    tm = x_ref.shape[0]
    zero_row = lambda r: pltpu.make_async_copy(zero_ref.at[pl.ds(0, 1)], xs_out.at[pl.ds(r, 1)], sem.at[1])
    zero_tile = lambda t: pltpu.make_async_copy(
        zero_ref, xs_out.at[pl.ds(pl.multiple_of(t * SUBLANES, SUBLANES), SUBLANES)], sem.at[1])

    def for_padding_rows(fn):
        for e in range(N_EXPERTS):
            lo, hi = pad_ref[0, e], pad_ref[0, N_EXPERTS + e]
            lo_tile = lax.shift_right_logical(lo + (SUBLANES - 1), SUBLANE_BITS)
            hi_tile = lax.shift_right_logical(hi, SUBLANE_BITS)
            lax.fori_loop(lo, jnp.minimum(lo_tile * SUBLANES, hi), lambda r, c: (fn(zero_row(r)), c)[1], 0)
            lax.fori_loop(lo_tile, hi_tile, lambda t, c: (fn(zero_tile(t)), c)[1], 0)

    @pl.when(pl.program_id(0) == 0)
    def _():
        zero_ref[...] = jnp.zeros_like(zero_ref)
        for_padding_rows(lambda copy: copy.start())

    @pl.when(pl.program_id(0) == pl.num_programs(0) - 1)
    def _():
        for_padding_rows(lambda copy: copy.wait())

    def body(i, carry):
        for k in range(TOP_K):
            pltpu.make_async_copy(x_ref.at[pl.ds(i, 1)], xs_out.at[pl.ds(dest_ref[0, TOP_K * i + k], 1)],
                                  sem.at[0]).start()
        return carry
    lax.fori_loop(0, tm, body, 0, unroll=4)
    for _ in range(TOP_K):
        pltpu.make_async_copy(x_ref, xs_out.at[pl.ds(0, tm)], sem.at[0]).wait()


def _moe_dispatch(x, dest, pad_ranges, n_rows, tm=DSP_TM):
    n = x.shape[0]
    dest3 = dest.reshape(n // tm, 1, TOP_K * tm)
    return pl.pallas_call(
        _dispatch_kernel,
        name="moe_dispatch",
        grid=(n // tm,),
        in_specs=[pl.BlockSpec((None, 1, TOP_K * tm), lambda i: (i, 0, 0), memory_space=pltpu.SMEM),
                  pl.BlockSpec((1, 2 * N_EXPERTS), lambda i: (0, 0), memory_space=pltpu.SMEM),
                  pl.BlockSpec((tm, D_MODEL), lambda i: (i, 0))],
        out_specs=pl.BlockSpec(memory_space=pl.ANY),
        out_shape=jax.ShapeDtypeStruct((n_rows, D_MODEL), F32),
        scratch_shapes=[pltpu.VMEM((SUBLANES, D_MODEL), F32), pltpu.SemaphoreType.DMA((2,))],
        compiler_params=_cparams(("arbitrary",), disable_bounds_checks=True),
    )(dest3, pad_ranges.reshape(1, 2 * N_EXPERTS), x)


def _moe_kernel(meta_ref, x_ref, wg_ref, wu_ref, wd_ref, y_ref, xb_ref, acc_ref):
    b = pl.program_id(0)
    f = pl.program_id(1)
    nblk = pl.num_programs(0)
    used = b < meta_ref[nblk]

    @pl.when(used)
    def _():
        @pl.when(f == 0)
        def _():
            xb_ref[...] = x_ref[...].astype(BF16)
            acc_ref[...] = jnp.zeros_like(acc_ref)

        xb = xb_ref[...]
        gate = jnp.dot(xb, wg_ref[...], preferred_element_type=F32)
        up = jnp.dot(xb, wu_ref[...], preferred_element_type=F32)
        acc_ref[...] += _bdot(_silu(gate) * up, wd_ref[...])

    @pl.when(f == pl.num_programs(1) - 1)
    def _():
        @pl.when(used)
        def _():
            y_ref[...] = acc_ref[...]

        @pl.when(jnp.logical_not(used))
        def _():
            y_ref[...] = jnp.zeros_like(y_ref)


def _moe_experts(xs, meta, w_gate, w_up, w_down, tb=MOE_TB, tf=MOE_TF):
    nblk = xs.shape[0] // tb
    dffe = w_gate.shape[2]
    grid_spec = pltpu.PrefetchScalarGridSpec(
        num_scalar_prefetch=1,
        grid=(nblk, dffe // tf),
        in_specs=[pl.BlockSpec((tb, D_MODEL), lambda b, f, m: (jnp.minimum(b, m[nblk] - 1), 0)),
                  pl.BlockSpec((None, D_MODEL, tf), lambda b, f, m: (m[b], 0, f)),
                  pl.BlockSpec((None, D_MODEL, tf), lambda b, f, m: (m[b], 0, f)),
                  pl.BlockSpec((None, tf, D_MODEL), lambda b, f, m: (m[b], f, 0))],
        out_specs=pl.BlockSpec((tb, D_MODEL), lambda b, f, m: (b, 0)),
        scratch_shapes=[pltpu.VMEM((tb, D_MODEL), BF16), pltpu.VMEM((tb, D_MODEL), F32)],
    )
    return pl.pallas_call(
        _moe_kernel,
        name="moe_experts",
        grid_spec=grid_spec,
        out_shape=jax.ShapeDtypeStruct((nblk * tb, D_MODEL), F32),
        compiler_params=_cparams(("parallel", "arbitrary")),
    )(meta, xs, w_gate.astype(BF16), w_up.astype(BF16), w_down.astype(BF16))


def _combine_kernel(pos0_ref, posn_ref, y_hbm, gates_ref, x_ref, g_ref, b_ref, out_ref, ybuf, sem):
    i = pl.program_id(0)
    tm = out_ref.shape[0]
    slot = i % 2

    @pl.when(i == 0)
    def _():
        _gather_rows(pos0_ref, y_hbm, ybuf.at[0], sem.at[0], TOP_K * tm)

    _wait_rows(y_hbm, ybuf.at[slot], sem.at[slot], TOP_K * tm)

    @pl.when(i + 1 < pl.num_programs(0))
    def _():
        _gather_rows(posn_ref, y_hbm, ybuf.at[1 - slot], sem.at[1 - slot], TOP_K * tm)

    gates = gates_ref[...]
    f = gates[:, 0:1] * ybuf[slot, :tm] + gates[:, 1:2] * ybuf[slot, tm:]
    out_ref[...] = _layer_norm(ALPHA * x_ref[...] + f, g_ref[...], b_ref[...])


def _moe_combine(y, pos, gates, x, ln_g, ln_b, tm=CMB_TM):
    n = x.shape[0]
    nt = n // tm
    pos3 = pos.reshape(nt, tm, TOP_K).transpose(0, 2, 1).reshape(nt, 1, TOP_K * tm)
    g, b = ln_g.reshape(1, D_MODEL), ln_b.reshape(1, D_MODEL)
    smem_blk = lambda imap: pl.BlockSpec((None, 1, TOP_K * tm), imap, memory_space=pltpu.SMEM)
    return pl.pallas_call(
        _combine_kernel,
        name="moe_combine_ln",
        grid=(nt,),
        in_specs=[smem_blk(lambda i: (0, 0, 0)),
                  smem_blk(lambda i: (jnp.minimum(i + 1, nt - 1), 0, 0)),
                  pl.BlockSpec(memory_space=pl.ANY),
                  pl.BlockSpec((tm, TOP_K), lambda i: (i, 0)),
                  pl.BlockSpec((tm, D_MODEL), lambda i: (i, 0)),
                  pl.BlockSpec((1, D_MODEL), lambda i: (0, 0)),
                  pl.BlockSpec((1, D_MODEL), lambda i: (0, 0))],
        out_specs=pl.BlockSpec((tm, D_MODEL), lambda i: (i, 0)),
        out_shape=jax.ShapeDtypeStruct((n, D_MODEL), F32),
        scratch_shapes=[pltpu.VMEM((2, TOP_K * tm, D_MODEL), F32), pltpu.SemaphoreType.DMA((2,))],
        compiler_params=_cparams(("arbitrary",), disable_bounds_checks=True),
    )(pos3, pos3, y, gates, x, g, b)


def _moe_routing(logits, tb):
    n = logits.shape[0]
    top_logit, top_idx = lax.top_k(logits, TOP_K)
    gates = jax.nn.softmax(top_logit, axis=-1)
    na = n * TOP_K
    e_flat = top_idx.reshape(-1).astype(jnp.int32)
    onehot = (e_flat[None, :] == jnp.arange(N_EXPERTS, dtype=jnp.int32)[:, None]).astype(jnp.int32)
    running = jnp.cumsum(onehot, axis=1)
    counts = running[:, -1]
    padded = (counts + tb - 1) // tb * tb
    pend = jnp.cumsum(padded)
    pstart = pend - padded
    dest = jnp.sum(onehot * (pstart[:, None] + running - 1), axis=0).reshape(n, TOP_K)
    nblk = -(-na // tb) + N_EXPERTS
    block_expert = jnp.minimum(jnp.searchsorted(pend, jnp.arange(nblk, dtype=jnp.int32) * tb, side='right'),
                               N_EXPERTS - 1).astype(jnp.int32)
    meta = jnp.concatenate([block_expert, (pend[-1:] // tb).astype(jnp.int32)])
    pad_ranges = jnp.concatenate([pstart + counts, pend[:-1], jnp.full((1,), nblk * tb)]).astype(jnp.int32)
    return dest.astype(jnp.int32), gates, meta, pad_ranges, nblk


def _router_weight(w_router):
    return jnp.pad(w_router, ((0, 0), (0, LANES - N_EXPERTS))).astype(BF16)


def _moe_ffn(x, logits, w_gate, w_up, w_down, ln_g, ln_b):
    dest, gates, meta, pad_ranges, nblk = _moe_routing(logits[:, :N_EXPERTS], MOE_TB)
    xs = _moe_dispatch(x, dest, pad_ranges, nblk * MOE_TB)
    y = _moe_experts(xs, meta, w_gate, w_up, w_down)
    return _moe_combine(y, dest, gates, x, ln_g, ln_b)


def _split_w_in(w):
    a_end = 3 * A_QKV_W
    bz_end = a_end + 3 * B_W + B_W
    w_groups = [jnp.concatenate([w[:, s * A_QKV_W + gi * A_GROUP_W:s * A_QKV_W + (gi + 1) * A_GROUP_W]
                                 for s in range(3)], axis=1).astype(BF16) for gi in range(N_GROUPS)]
    zpad = jnp.zeros((w.shape[0], LANES - N_HEADS_B), w.dtype)
    w_b = jnp.concatenate([w[:, a_end:bz_end], w[:, bz_end + 2 * N_HEADS_B:],
                           w[:, bz_end:bz_end + N_HEADS_B], zpad,
                           w[:, bz_end + N_HEADS_B:bz_end + 2 * N_HEADS_B], zpad], axis=1)
    return w_groups, w_b.astype(BF16)


def _hybrid_layer(x, batch, seq, rel_bias, w_in, conv_w, a_log, dt_bias, o_norm_w, w_oa, w_ob, w_out, ln_g, ln_b,
                  w_router=None):
    w_groups, w_b = _split_w_in(w_in)
    proj_b = _matmul(x, w_b, F32, "in_proj_gdn_gates")
    outs, lses = [], []
    for gi, (_, dilation) in enumerate(DSWA_PATTERNS):
        qkv = _in_proj_strided(x, w_groups[gi], dilation, batch, seq)
        o, lse = _dswa_group(qkv, _band_bias(rel_bias, gi, dilation), dilation)
        outs.append(o)
        lses.append(lse)
    yb = _gated_deltanet(proj_b, conv_w, a_log, dt_bias, o_norm_w, batch, seq)
    return _mix_out(outs, lses, yb, proj_b, x, w_oa, w_ob, w_out, ln_g, ln_b, w_router)


def kernel(x, rel_bias, w_in, conv_w, a_log, dt_bias, o_norm_w, w_oa, w_ob, w_out, ln1_g, ln1_b,
           ffn_w_gate, ffn_w_up, ffn_w_down, moe_router, moe_w_gate, moe_w_up, moe_w_down, ln2_g, ln2_b):
    batch, seq, d = x.shape
    assert d == D_MODEL and w_in.shape[0] == DEPTH
    assert all(seq % t == 0 for t in (PROJ_A_TM, OUT_TM, GDN_STEP_CHUNKS * GDN_CHUNK, DSWA_ROWS))
    assert all(seq % (dilation * ATT_BLK) == 0 for _, dilation in DSWA_PATTERNS)
    assert all((batch * seq) % t == 0 for t in (MM_TM, FFN_TM, MOE_TB, CMB_TM, DSP_TM))
    h = x.reshape(batch * seq, d)
    for layer in range(DEPTH):
        j = layer // 2
        is_moe = layer % 2 == 1
        h = _hybrid_layer(h, batch, seq, rel_bias, w_in[layer], conv_w[layer], a_log[layer], dt_bias[layer],
                          o_norm_w[layer], w_oa[layer], w_ob[layer], w_out[layer], ln1_g[layer], ln1_b[layer],
                          _router_weight(moe_router[j]) if is_moe else None)
        if is_moe:
            h, logits = h
            h = _moe_ffn(h, logits, moe_w_gate[j], moe_w_up[j], moe_w_down[j], ln2_g[layer], ln2_b[layer])
        else:
            h = _dense_ffn(h, ffn_w_gate[j], ffn_w_up[j], ffn_w_down[j], ln2_g[layer], ln2_b[layer])
    return h.reshape(batch, seq, d)
```

```python
import functools
import math

import jax
import jax.numpy as jnp
from jax import lax
from jax.experimental import pallas as pl
from jax.experimental.pallas import tpu as pltpu

F32 = jnp.float32
BF16 = jnp.bfloat16

D_MODEL = 1024
DEPTH = 2
DSWA_PATTERNS = ((128, 1), (512, 4), (2048, 16))
N_GROUPS = 3
HEADS_PER_GROUP = 4
HEAD_DIM = 128
A_QKV_W = N_GROUPS * HEADS_PER_GROUP * HEAD_DIM
A_GROUP_W = HEADS_PER_GROUP * HEAD_DIM
NUM_BUCKETS = 32
MAX_DISTANCE = 2048
N_HEADS_B = 8
B_W = N_HEADS_B * HEAD_DIM
CONV_WIDTH = 4
N_EXPERTS = 8
TOP_K = 2
ALPHA = (2 * DEPTH) ** 0.25
LN_EPS = 1e-5
RMS_EPS = 1e-6

LANES = 128
SUBLANES = 8
SUBLANE_BITS = SUBLANES.bit_length() - 1
VMEM_LIMIT = 56 * 1024 * 1024

ATT_BLK = 128
DSWA_ROWS = 2048
GDN_CHUNK = 128
GDN_STEP_CHUNKS = 2
MM_TM = 512
MM_TN = 3200
PROJ_A_TM = 1024
OUT_TM = 512
OUT_SUBTILES = 2
FFN_TM = 512
FFN_TF = 1024
MOE_TB = 512
MOE_TF = 1792
CMB_TM = 512
DSP_TM = 512

NEG_BIG = -1e30


def _cparams(sem, vmem=VMEM_LIMIT, **kw):
    return pltpu.CompilerParams(dimension_semantics=sem, vmem_limit_bytes=vmem, **kw)


def _bdot(a, b):
    return jnp.dot(a.astype(BF16), b.astype(BF16), preferred_element_type=F32)


def _bdot_nt(a, b):
    return lax.dot_general(a.astype(BF16), b.astype(BF16), (((1,), (1,)), ((), ())),
                           preferred_element_type=F32)


def _sigmoid(v):
    return 1.0 / (1.0 + jnp.exp(-v))


def _silu(v):
    return v * _sigmoid(v)


def _layer_norm(v, g, b):
    mu = jnp.mean(v, axis=-1, keepdims=True)
    d = v - mu
    var = jnp.mean(d * d, axis=-1, keepdims=True)
    return d * lax.rsqrt(var + LN_EPS) * g + b


def _mm_kernel(x_ref, w_ref, o_ref):
    o_ref[...] = _bdot(x_ref[...], w_ref[...]).astype(o_ref.dtype)


def _matmul(x, w, out_dtype, name, tm=MM_TM, tn=MM_TN):
    m, k = x.shape
    n = w.shape[1]
    tn = min(tn, n)
    return pl.pallas_call(
        _mm_kernel,
        name=name,
        grid=(n // tn, m // tm),
        in_specs=[pl.BlockSpec((tm, k), lambda j, i: (i, 0)),
                  pl.BlockSpec((k, tn), lambda j, i: (0, j))],
        out_specs=pl.BlockSpec((tm, tn), lambda j, i: (i, j)),
        out_shape=jax.ShapeDtypeStruct((m, n), out_dtype),
        compiler_params=_cparams(("parallel", "parallel")),
    )(x, w)


def _dswa_kernel(q_ref, kc_ref, kp_ref, vc_ref, vp_ref, bias0_ref, bias_ref, o_ref, lse_ref, *, tq):
    nqb = tq // ATT_BLK
    lane = lax.broadcasted_iota(jnp.int32, (ATT_BLK, LANES), 1)
    scale = HEAD_DIM ** -0.5
    exp2_scale = scale * math.log2(math.e)
    for ri in range(q_ref.shape[0]):
        q = q_ref[ri]
        kwin = jnp.concatenate([kp_ref[ri], kc_ref[ri]], axis=0)
        vwin = jnp.concatenate([vp_ref[ri], vc_ref[ri]], axis=0)
        for c in range(nqb):
            b_ref = bias0_ref if c == 0 else bias_ref
            rows = slice(c * ATT_BLK, (c + 1) * ATT_BLK)
            lse_tile = jnp.zeros((ATT_BLK, LANES), F32)
            for h in range(HEADS_PER_GROUP):
                hs = slice(h * HEAD_DIM, (h + 1) * HEAD_DIM)
                kh = kwin[c * ATT_BLK:(c + 2) * ATT_BLK, hs]
                vh = vwin[c * ATT_BLK:(c + 2) * ATT_BLK, hs]
                t = _bdot_nt(q[rows, hs], kh) + b_ref[h]
                m = jnp.max(t, axis=-1, keepdims=True)
                p = jnp.exp2((t - m) * exp2_scale)
                l = jnp.sum(p, axis=-1, keepdims=True)
                o_ref[ri, rows, hs] = _bdot(p, vh) / l
                lse_tile = jnp.where(lane == h, m * scale + jnp.log(l), lse_tile)
            lse_ref[ri, rows, :] = lse_tile


def _in_proj_strided_kernel(x_ref, w_ref, o_ref, *, dilation):
    res = _bdot(x_ref[...], w_ref[...])
    if dilation == 1:
        o_ref[0] = res.astype(o_ref.dtype)
    else:
        rows = res.shape[0] // dilation
        o_ref[...] = jnp.swapaxes(res.reshape(rows, dilation, res.shape[1]), 0, 1).astype(o_ref.dtype)


def _in_proj_strided(x, w, dilation, batch, seq, tm=PROJ_A_TM):
    k = x.shape[1]
    wn = tn = w.shape[1]
    tpb = seq // tm
    return pl.pallas_call(
        functools.partial(_in_proj_strided_kernel, dilation=dilation),
        name=f"in_proj_attn_d{dilation}",
        grid=(batch * tpb, wn // tn),
        in_specs=[pl.BlockSpec((tm, k), lambda i, j: (i, 0)),
                  pl.BlockSpec((k, tn), lambda i, j: (0, j))],
        out_specs=pl.BlockSpec((None, dilation, tm // dilation, tn), lambda i, j: (i // tpb, 0, i % tpb, j)),
        out_shape=jax.ShapeDtypeStruct((batch, dilation, seq // dilation, wn), BF16),
        compiler_params=_cparams(("parallel", "parallel")),
    )(x, w)


def _dswa_group(qkv, bias, dilation):
    batch, _, n, _ = qkv.shape
    tq = min(DSWA_ROWS, n)
    rr = DSWA_ROWS // tq
    nqb = tq // ATT_BLK

    def cur(off):
        return pl.BlockSpec((None, rr, tq, A_GROUP_W), lambda b, r, j: (b, r, j, off))

    def prev(off):
        return pl.BlockSpec((None, rr, ATT_BLK, A_GROUP_W),
                            lambda b, r, j: (b, r, jnp.maximum(j * nqb - 1, 0), off))

    return pl.pallas_call(
        functools.partial(_dswa_kernel, tq=tq),
        name=f"dswa_d{dilation}",
        grid=(batch, dilation // rr, n // tq),
        in_specs=[cur(0), cur(1), prev(1), cur(2), prev(2),
                  pl.BlockSpec((None, HEADS_PER_GROUP, ATT_BLK, 2 * ATT_BLK),
                               lambda b, r, j: (jnp.minimum(j, 1), 0, 0, 0)),
                  pl.BlockSpec((None, HEADS_PER_GROUP, ATT_BLK, 2 * ATT_BLK), lambda b, r, j: (1, 0, 0, 0))],
        out_specs=[pl.BlockSpec((None, rr, tq, A_GROUP_W), lambda b, r, j: (b, r, j, 0)),
                   pl.BlockSpec((None, rr, tq, LANES), lambda b, r, j: (b, r, j, 0))],
        out_shape=[jax.ShapeDtypeStruct((batch, dilation, n, A_GROUP_W), F32),
                   jax.ShapeDtypeStruct((batch, dilation, n, LANES), F32)],
        compiler_params=_cparams(("parallel", "parallel", "parallel")),
    )(qkv, qkv, qkv, qkv, qkv, bias, bias)


def _t5_causal_bucket(dist):
    num_exact = NUM_BUCKETS // 2
    d = jnp.maximum(dist, 1).astype(F32)
    large = num_exact + (jnp.log(d / num_exact) / math.log(MAX_DISTANCE / num_exact)
                         * (NUM_BUCKETS - num_exact)).astype(jnp.int32)
    large = jnp.minimum(large, NUM_BUCKETS - 1)
    return jnp.where(dist < num_exact, dist, large)


def _band_bias(rel_bias, gi, dilation):
    qi = jnp.arange(ATT_BLK)[:, None] + ATT_BLK
    kj = jnp.arange(2 * ATT_BLK)[None, :]
    band = (qi - kj >= 0) & (qi - kj <= ATT_BLK)
    delta = jnp.maximum(qi - kj, 0) * dilation
    table = rel_bias[:, gi * HEADS_PER_GROUP:(gi + 1) * HEADS_PER_GROUP].astype(F32)
    onehot = (_t5_causal_bucket(delta)[..., None] == jnp.arange(NUM_BUCKETS)).astype(F32)
    bias = jnp.einsum('qkn,nh->hqk', onehot, table, precision=lax.Precision.HIGHEST) / (HEAD_DIM ** -0.5)
    return jnp.stack([jnp.where(band & (kj >= ATT_BLK), bias, NEG_BIG), jnp.where(band, bias, NEG_BIG)])


def _dot16(a, b):
    return jnp.dot(a, b, preferred_element_type=F32)


def _unit_lower_inverse(a, row, col):
    base = 16
    heads = list(a)
    eye = (row == col).astype(F32)
    same = lambda sz: (row // sz) == (col // sz)
    blk = same(base)
    a_d = {h: jnp.where(blk, a[h], 0.0) for h in heads}
    t = {h: eye - a_d[h] for h in heads}
    p = {h: a_d[h].astype(BF16) for h in heads}
    for _ in range(3):
        p = {h: _dot16(p[h], p[h]).astype(BF16) for h in heads}
        t = {h: t[h] + _dot16(t[h].astype(BF16), p[h]) for h in heads}
    sz = 2 * base
    while sz <= GDN_CHUNK:
        off = same(sz) & jnp.logical_not(same(sz // 2))
        tb = {h: t[h].astype(BF16) for h in heads}
        m = {h: _dot16(jnp.where(off, a[h], 0.0).astype(BF16), tb[h]).astype(BF16) for h in heads}
        t = {h: t[h] - _dot16(tb[h], m[h]) for h in heads}
        sz *= 2
    return t


def _gdn_kernel(x_ref, halo_ref, z_ref, ba_ref, cw_ref, alog_ref, dtb_ref, onw_ref, y_ref, state_ref):
    c = pl.program_id(1)

    @pl.when(c == 0)
    def _():
        state_ref[...] = jnp.zeros_like(state_ref)

    C = GDN_CHUNK
    chunks = range(x_ref.shape[0] // C)
    rows = {cc: slice(cc * C, (cc + 1) * C) for cc in chunks}
    keep_halo = (c > 0).astype(F32)
    row = lax.broadcasted_iota(jnp.int32, (C, C), 0)
    col = lax.broadcasted_iota(jnp.int32, (C, C), 1)
    incl = row >= col
    strict = row > col
    ones_l = incl.astype(BF16)

    def log_decay_cumsum(cc):
        ba = ba_ref[rows[cc], :]
        a_in = ba[:, LANES:] + dtb_ref[...]
        softplus = jnp.maximum(a_in, 0.0) + jnp.log(1.0 + jnp.exp(-jnp.abs(a_in)))
        g_all = -jnp.exp(alog_ref[...]) * softplus
        g1 = g_all.astype(BF16)
        r1 = g_all - g1.astype(F32)
        g2 = r1.astype(BF16)
        g3 = (r1 - g2.astype(F32)).astype(BF16)
        return (jnp.dot(ones_l, g1, preferred_element_type=F32) + jnp.dot(ones_l, g2, preferred_element_type=F32)
                + jnp.dot(ones_l, g3, preferred_element_type=F32))

    beta_all = {cc: _sigmoid(ba_ref[rows[cc], :LANES]) for cc in chunks}
    G = {cc: log_decay_cumsum(cc) for cc in chunks}
    GT = {cc: G[cc].T for cc in chunks}
    exp_g = {cc: jnp.exp(G[cc]) for cc in chunks}
    beta_exp_g = {cc: beta_all[cc] * exp_g[cc] for cc in chunks}
    exp_rest = {cc: jnp.exp(G[cc][C - 1:C, :] - G[cc]) for cc in chunks}

    def conv_silu(cc, col0):
        cs = slice(col0, col0 + HEAD_DIM)
        xc = x_ref[rows[cc], cs]
        halo = halo_ref[:, cs] * keep_halo if cc == 0 else x_ref[cc * C - SUBLANES:cc * C, cs]
        xf = jnp.concatenate([halo, xc], axis=0)
        w = cw_ref[:, cs]
        y = w[CONV_WIDTH - 1:CONV_WIDTH] * xc
        for i in range(CONV_WIDTH - 1):
            off = SUBLANES - (CONV_WIDTH - 1) + i
            y = y + w[i:i + 1] * xf[off:off + C]
        return _silu(y)

    def l2norm(t, scale=1.0):
        return t * (lax.rsqrt(jnp.sum(t * t, axis=-1, keepdims=True) + RMS_EPS) * scale)

    items = [(cc, h) for cc in chunks for h in range(N_HEADS_B)]
    q = {(cc, h): l2norm(conv_silu(cc, h * HEAD_DIM), HEAD_DIM ** -0.5) for cc, h in items}
    k = {(cc, h): l2norm(conv_silu(cc, B_W + h * HEAD_DIM)) for cc, h in items}
    v = {(cc, h): conv_silu(cc, 2 * B_W + h * HEAD_DIM) for cc, h in items}
    beta = {(cc, h): beta_all[cc][:, h:h + 1] for cc, h in items}
    gc = {(cc, h): G[cc][:, h:h + 1] for cc, h in items}
    g_last = {(cc, h): G[cc][C - 1:C, h:h + 1] for cc, h in items}
    decay = {(cc, h): jnp.exp(jnp.where(incl, gc[cc, h] - GT[cc][h:h + 1, :], NEG_BIG)) for cc, h in items}
    eg = {(cc, h): exp_g[cc][:, h:h + 1] for cc, h in items}
    kb ={it: k[it].astype(BF16) for it in items}
    kq = {it: lax.dot_general(jnp.concatenate([kb[it], q[it].astype(BF16)], axis=0), kb[it],
                              (((1,), (1,)), ((), ())), preferred_element_type=F32) for it in items}
    a = {it: jnp.where(strict, beta[it] * kq[it][:C] * decay[it], 0.0) for it in items}
    t_inv = _unit_lower_inverse(a, row, col)
    rhs = {(cc, h): jnp.concatenate([beta[cc, h] * v[cc, h], beta_exp_g[cc][:, h:h + 1] * k[cc, h]],
                                    axis=1).astype(BF16) for cc, h in items}
    sol = {it: _dot16(t_inv[it].astype(BF16), rhs[it]) for it in items}
    qk = {it: (kq[it][C:] * decay[it]).astype(BF16) for it in items}
    wq = {it: jnp.concatenate([sol[it][:, HEAD_DIM:], q[it] * eg[it]], axis=0).astype(BF16) for it in items}
    k_dec = {(cc, h): (k[cc, h] * exp_rest[cc][:, h:h + 1]).astype(BF16) for cc, h in items}

    heads = range(N_HEADS_B)
    state = {h: state_ref[h] for h in heads}
    for cc in chunks:
        ws = {h: _dot16(wq[cc, h], state[h].astype(BF16)) for h in heads}
        u = {h: (sol[cc, h][:, :HEAD_DIM] - ws[h][:C]).astype(BF16) for h in heads}
        o = {h: ws[h][C:] + _dot16(qk[cc, h], u[h]) for h in heads}
        state = {h: jnp.exp(g_last[cc, h]) * state[h] + lax.dot_general(
            k_dec[cc, h], u[h], (((0,), (0,)), ((), ())), preferred_element_type=F32) for h in heads}
        for h in heads:
            oh = o[h] * lax.rsqrt(jnp.mean(o[h] * o[h], axis=-1, keepdims=True) + RMS_EPS) * onw_ref[...]
            hs = slice(h * HEAD_DIM, (h + 1) * HEAD_DIM)
            y_ref[rows[cc], hs] = (oh * _silu(z_ref[rows[cc], hs])).astype(y_ref.dtype)
    for h in heads:
        state_ref[h] = state[h]


def _gated_deltanet(proj_b, conv_w, a_log, dt_bias, o_norm_w, batch, seq):
    C = GDN_STEP_CHUNKS * GDN_CHUNK
    wb = proj_b.shape[1]
    pb = proj_b.reshape(batch, seq, wb)
    ba_blk = (wb - 2 * LANES) // (2 * LANES)
    pad = lambda t: jnp.pad(t.astype(F32), (0, LANES - t.shape[0])).reshape(1, LANES)
    const = lambda shape: pl.BlockSpec(shape, lambda b, c: (0,) * len(shape))
    y = pl.pallas_call(
        _gdn_kernel,
        name="gated_deltanet",
        grid=(batch, seq // C),
        in_specs=[pl.BlockSpec((None, C, 3 * B_W), lambda b, c: (b, c, 0)),
                  pl.BlockSpec((None, SUBLANES, 3 * B_W),
                               lambda b, c: (b, jnp.maximum(c * (C // SUBLANES) - 1, 0), 0)),
                  pl.BlockSpec((None, C, B_W), lambda b, c: (b, c, 3)),
                  pl.BlockSpec((None, C, 2 * LANES), lambda b, c: (b, c, ba_blk)),
                  const((CONV_WIDTH, 3 * B_W)), const((1, LANES)), const((1, LANES)), const((1, LANES))],
        out_specs=pl.BlockSpec((None, C, B_W), lambda b, c: (b, c, 0)),
        out_shape=jax.ShapeDtypeStruct((batch, seq, B_W), BF16),
        scratch_shapes=[pltpu.VMEM((N_HEADS_B, HEAD_DIM, HEAD_DIM), F32)],
        compiler_params=_cparams(("parallel", "arbitrary")),
    )(pb, pb, pb, pb, conv_w.astype(F32), pad(a_log), pad(dt_bias), o_norm_w.astype(F32).reshape(1, LANES))
    return y.reshape(batch * seq, B_W)


def _mix_out_kernel(o0, o1, o2, l0, l1, l2, yb_ref, ga_ref, gb_ref, x_ref, woa_ref, wob_ref, wout_ref,
                    g_ref, b_ref, *rest, with_router):
    if with_router:
        wr_ref, out_ref, logits_ref, *scratch = rest
    else:
        out_ref, *scratch = rest

    def token_order(ref, scr):
        dilation, rows, width = ref.shape
        if dilation == 1:
            return ref[0]
        planes = []
        for t in range(width // LANES):
            for r in range(dilation):
                scr[t, pl.ds(r, rows, stride=dilation), :] = ref[r, :, t * LANES:(t + 1) * LANES]
            planes.append(scr[t])
        return jnp.concatenate(planes, axis=1)

    outs = (token_order(o0, None), token_order(o1, scratch[0]), token_order(o2, scratch[1]))
    lses = (token_order(l0, None), token_order(l1, scratch[2]), token_order(l2, scratch[3]))
    tm = x_ref.shape[0]
    for sub in range(OUT_SUBTILES):
        rs = slice(sub * tm // OUT_SUBTILES, (sub + 1) * tm // OUT_SUBTILES)
        lse_s = [t[rs] for t in lses]
        m = jnp.maximum(jnp.maximum(lse_s[0], lse_s[1]), lse_s[2])
        es = [jnp.exp(t - m) for t in lse_s]
        inv = 1.0 / (es[0] + es[1] + es[2])
        wgt = [e * inv for e in es]
        ya = []
        for h in range(HEADS_PER_GROUP):
            hs = slice(h * HEAD_DIM, (h + 1) * HEAD_DIM)
            ya.append(wgt[0][:, h:h + 1] * outs[0][rs, hs] + wgt[1][:, h:h + 1] * outs[1][rs, hs]
                      + wgt[2][:, h:h + 1] * outs[2][rs, hs])
        ya = jnp.concatenate(ya, axis=1)
        pa = _bdot(ya, woa_ref[...])
        pb = _bdot(yb_ref[rs, :], wob_ref[...])
        merged = _sigmoid(ga_ref[rs, :]) * pa + _sigmoid(gb_ref[rs, :]) * pb
        mix = _bdot(merged, wout_ref[...])
        out = _layer_norm(ALPHA * x_ref[rs, :] + mix, g_ref[...], b_ref[...])
        out_ref[rs, :] = out
        if with_router:
            logits_ref[rs, :] = _bdot(out, wr_ref[...])


def _mix_out(outs, lses, yb, proj_b, x, w_oa, w_ob, w_out, ln_g, ln_b, w_router=None, tm=OUT_TM):
    n = x.shape[0]
    with_router = w_router is not None
    seq = outs[0].shape[1] * outs[0].shape[2]
    tpb = seq // tm
    rowblk = lambda w, cb=0: pl.BlockSpec((tm, w), lambda i: (i, cb))
    const = lambda a: pl.BlockSpec(a.shape, lambda i: (0, 0), pipeline_mode=pl.Buffered(1))

    def grouped(a):
        d, w = a.shape[1], a.shape[3]
        return pl.BlockSpec((None, d, tm // d, w), lambda i: (i // tpb, 0, i % tpb, 0))

    wa, wb, wo = w_oa.astype(BF16), w_ob.astype(BF16), w_out.astype(BF16)
    g, b = ln_g.reshape(1, D_MODEL), ln_b.reshape(1, D_MODEL)
    router = [w_router] if with_router else []
    out_specs = [rowblk(D_MODEL)] + ([rowblk(LANES)] if with_router else [])
    out_shape = [jax.ShapeDtypeStruct((n, D_MODEL), F32)] + (
        [jax.ShapeDtypeStruct((n, LANES), F32)] if with_router else [])
    res = pl.pallas_call(
        functools.partial(_mix_out_kernel, with_router=with_router),
        name="mix_out_ln",
        grid=(n // tm,),
        in_specs=[grouped(a) for a in (*outs, *lses)]
        + [rowblk(B_W), rowblk(D_MODEL, 4), rowblk(D_MODEL, 5), rowblk(D_MODEL),
           const(wa), const(wb), const(wo), const(g), const(b)] + [const(a) for a in router],
        out_specs=out_specs,
        out_shape=out_shape,
        scratch_shapes=[pltpu.VMEM((A_GROUP_W // LANES, tm, LANES), F32)] * 2 + [pltpu.VMEM((1, tm, LANES), F32)] * 2,
        compiler_params=_cparams(("parallel",)),
    )(*outs, *lses, yb, proj_b, proj_b, x, wa, wb, wo, g, b, *router)
    return tuple(res) if with_router else res[0]


def _ffn_kernel(x_ref, wg_ref, wu_ref, wd_ref, g_ref, b_ref, out_ref):
    tm = x_ref.shape[0]
    dff = wg_ref.shape[1]
    for sub in range(OUT_SUBTILES):
        rs = slice(sub * tm // OUT_SUBTILES, (sub + 1) * tm // OUT_SUBTILES)
        x = x_ref[rs, :]
        xb = x.astype(BF16)
        acc = None
        for c0 in range(0, dff, FFN_TF):
            c1 = min(c0 + FFN_TF, dff)
            gate = jnp.dot(xb, wg_ref[:, c0:c1], preferred_element_type=F32)
            up = jnp.dot(xb, wu_ref[:, c0:c1], preferred_element_type=F32)
            part = _bdot(_silu(gate) * up, wd_ref[c0:c1, :])
            acc = part if acc is None else acc + part
        out_ref[rs, :] = _layer_norm(ALPHA * x + acc, g_ref[...], b_ref[...])


def _dense_ffn(x, w_gate, w_up, w_down, ln_g, ln_b, tm=FFN_TM):
    n = x.shape[0]
    g, b = ln_g.reshape(1, D_MODEL), ln_b.reshape(1, D_MODEL)
    resident = lambda a: pl.BlockSpec(a.shape, lambda i: (0, 0), pipeline_mode=pl.Buffered(1))
    wg, wu, wd = w_gate.astype(BF16), w_up.astype(BF16), w_down.astype(BF16)
    return pl.pallas_call(
        _ffn_kernel,
        name="dense_ffn_ln",
        grid=(n // tm,),
        in_specs=[pl.BlockSpec((tm, D_MODEL), lambda i: (i, 0)),
                  resident(wg), resident(wu), resident(wd), resident(g), resident(b)],
        out_specs=pl.BlockSpec((tm, D_MODEL), lambda i: (i, 0)),
        out_shape=jax.ShapeDtypeStruct((n, D_MODEL), F32),
        compiler_params=_cparams(("parallel",)),
    )(x, wg, wu, wd, g, b)


def _gather_rows(idx_ref, src_hbm, dst, sem, count):
    def body(i, carry):
        pltpu.make_async_copy(src_hbm.at[pl.ds(idx_ref[0, i], 1)], dst.at[pl.ds(i, 1)], sem).start()
        return carry
    lax.fori_loop(0, count, body, 0, unroll=8)


def _wait_rows(src_hbm, dst, sem, count):
    pltpu.make_async_copy(src_hbm.at[pl.ds(0, count)], dst, sem).wait()


def _dispatch_kernel(dest_ref, pad_ref, x_ref, xs_out, zero_ref, sem):
    tm = x_ref.shape[0]
    zero_row = lambda r: pltpu.make_async_copy(zero_ref.at[pl.ds(0, 1)], xs_out.at[pl.ds(r, 1)], sem.at[1])
    zero_tile = lambda t: pltpu.make_async_copy(
        zero_ref, xs_out.at[pl.ds(pl.multiple_of(t * SUBLANES, SUBLANES), SUBLANES)], sem.at[1])

    def for_padding_rows(fn):
        for e in range(N_EXPERTS):
            lo, hi = pad_ref[0, e], pad_ref[0, N_EXPERTS + e]
            lo_tile = lax.shift_right_logical(lo + (SUBLANES - 1), SUBLANE_BITS)
            hi_tile = lax.shift_right_logical(hi, SUBLANE_BITS)
            lax.fori_loop(lo, jnp.minimum(lo_tile * SUBLANES, hi), lambda r, c: (fn(zero_row(r)), c)[1], 0)
            lax.fori_loop(lo_tile, hi_tile, lambda t, c: (fn(zero_tile(t)), c)[1], 0)

    @pl.when(pl.program_id(0) == 0)
    def _():
        zero_ref[...] = jnp.zeros_like(zero_ref)
        for_padding_rows(lambda copy: copy.start())

    @pl.when(pl.program_id(0) == pl.num_programs(0) - 1)
    def _():
        for_padding_rows(lambda copy: copy.wait())

    def body(i, carry):
        for k in range(TOP_K):
            pltpu.make_async_copy(x_ref.at[pl.ds(i, 1)], xs_out.at[pl.ds(dest_ref[0, TOP_K * i + k], 1)],
                                  sem.at[0]).start()
        return carry
    lax.fori_loop(0, tm, body, 0, unroll=4)
    for _ in range(TOP_K):
        pltpu.make_async_copy(x_ref, xs_out.at[pl.ds(0, tm)], sem.at[0]).wait()


def _moe_dispatch(x, dest, pad_ranges, n_rows, tm=DSP_TM):
    n = x.shape[0]
    dest3 = dest.reshape(n // tm, 1, TOP_K * tm)
    return pl.pallas_call(
        _dispatch_kernel,
        name="moe_dispatch",
        grid=(n // tm,),
        in_specs=[pl.BlockSpec((None, 1, TOP_K * tm), lambda i: (i, 0, 0), memory_space=pltpu.SMEM),
                  pl.BlockSpec((1, 2 * N_EXPERTS), lambda i: (0, 0), memory_space=pltpu.SMEM),
                  pl.BlockSpec((tm, D_MODEL), lambda i: (i, 0))],
        out_specs=pl.BlockSpec(memory_space=pl.ANY),
        out_shape=jax.ShapeDtypeStruct((n_rows, D_MODEL), F32),
        scratch_shapes=[pltpu.VMEM((SUBLANES, D_MODEL), F32), pltpu.SemaphoreType.DMA((2,))],
        compiler_params=_cparams(("arbitrary",), disable_bounds_checks=True),
    )(dest3, pad_ranges.reshape(1, 2 * N_EXPERTS), x)


def _moe_kernel(meta_ref, x_ref, wg_ref, wu_ref, wd_ref, y_ref, xb_ref, acc_ref):
    b = pl.program_id(0)
    f = pl.program_id(1)
    nblk = pl.num_programs(0)
    used = b < meta_ref[nblk]

    @pl.when(used)
    def _():
        @pl.when(f == 0)
        def _():
            xb_ref[...] = x_ref[...].astype(BF16)
            acc_ref[...] = jnp.zeros_like(acc_ref)

        xb = xb_ref[...]
        gate = jnp.dot(xb, wg_ref[...], preferred_element_type=F32)
        up = jnp.dot(xb, wu_ref[...], preferred_element_type=F32)
        acc_ref[...] += _bdot(_silu(gate) * up, wd_ref[...])

    @pl.when(f == pl.num_programs(1) - 1)
    def _():
        @pl.when(used)
        def _():
            y_ref[...] = acc_ref[...]

        @pl.when(jnp.logical_not(used))
        def _():
            y_ref[...] = jnp.zeros_like(y_ref)


def _moe_experts(xs, meta, w_gate, w_up, w_down, tb=MOE_TB, tf=MOE_TF):
    nblk = xs.shape[0] // tb
    dffe = w_gate.shape[2]
    grid_spec = pltpu.PrefetchScalarGridSpec(
        num_scalar_prefetch=1,
        grid=(nblk, dffe // tf),
        in_specs=[pl.BlockSpec((tb, D_MODEL), lambda b, f, m: (jnp.minimum(b, m[nblk] - 1), 0)),
                  pl.BlockSpec((None, D_MODEL, tf), lambda b, f, m: (m[b], 0, f)),
                  pl.BlockSpec((None, D_MODEL, tf), lambda b, f, m: (m[b], 0, f)),
                  pl.BlockSpec((None, tf, D_MODEL), lambda b, f, m: (m[b], f, 0))],
        out_specs=pl.BlockSpec((tb, D_MODEL), lambda b, f, m: (b, 0)),
        scratch_shapes=[pltpu.VMEM((tb, D_MODEL), BF16), pltpu.VMEM((tb, D_MODEL), F32)],
    )
    return pl.pallas_call(
        _moe_kernel,
        name="moe_experts",
        grid_spec=grid_spec,
        out_shape=jax.ShapeDtypeStruct((nblk * tb, D_MODEL), F32),
        compiler_params=_cparams(("parallel", "arbitrary")),
    )(meta, xs, w_gate.astype(BF16), w_up.astype(BF16), w_down.astype(BF16))


def _combine_kernel(pos0_ref, posn_ref, y_hbm, gates_ref, x_ref, g_ref, b_ref, out_ref, ybuf, sem):
    i = pl.program_id(0)
    tm = out_ref.shape[0]
    slot = i % 2

    @pl.when(i == 0)
    def _():
        _gather_rows(pos0_ref, y_hbm, ybuf.at[0], sem.at[0], TOP_K * tm)

    _wait_rows(y_hbm, ybuf.at[slot], sem.at[slot], TOP_K * tm)

    @pl.when(i + 1 < pl.num_programs(0))
    def _():
        _gather_rows(posn_ref, y_hbm, ybuf.at[1 - slot], sem.at[1 - slot], TOP_K * tm)

    gates = gates_ref[...]
    f = gates[:, 0:1] * ybuf[slot, :tm] + gates[:, 1:2] * ybuf[slot, tm:]
    out_ref[...] = _layer_norm(ALPHA * x_ref[...] + f, g_ref[...], b_ref[...])


def _moe_combine(y, pos, gates, x, ln_g, ln_b, tm=CMB_TM):
    n = x.shape[0]
    nt = n // tm
    pos3 = pos.reshape(nt, tm, TOP_K).transpose(0, 2, 1).reshape(nt, 1, TOP_K * tm)
    g, b = ln_g.reshape(1, D_MODEL), ln_b.reshape(1, D_MODEL)
    smem_blk = lambda imap: pl.BlockSpec((None, 1, TOP_K * tm), imap, memory_space=pltpu.SMEM)
    return pl.pallas_call(
        _combine_kernel,
        name="moe_combine_ln",
        grid=(nt,),
        in_specs=[smem_blk(lambda i: (0, 0, 0)),
                  smem_blk(lambda i: (jnp.minimum(i + 1, nt - 1), 0, 0)),
                  pl.BlockSpec(memory_space=pl.ANY),
                  pl.BlockSpec((tm, TOP_K), lambda i: (i, 0)),
                  pl.BlockSpec((tm, D_MODEL), lambda i: (i, 0)),
                  pl.BlockSpec((1, D_MODEL), lambda i: (0, 0)),
                  pl.BlockSpec((1, D_MODEL), lambda i: (0, 0))],
        out_specs=pl.BlockSpec((tm, D_MODEL), lambda i: (i, 0)),
        out_shape=jax.ShapeDtypeStruct((n, D_MODEL), F32),
        scratch_shapes=[pltpu.VMEM((2, TOP_K * tm, D_MODEL), F32), pltpu.SemaphoreType.DMA((2,))],
        compiler_params=_cparams(("arbitrary",), disable_bounds_checks=True),
    )(pos3, pos3, y, gates, x, g, b)


def _moe_routing(logits, tb):
    n = logits.shape[0]
    top_logit, top_idx = lax.top_k(logits, TOP_K)
    gates = jax.nn.softmax(top_logit, axis=-1)
    na = n * TOP_K
    e_flat = top_idx.reshape(-1).astype(jnp.int32)
    onehot = (e_flat[None, :] == jnp.arange(N_EXPERTS, dtype=jnp.int32)[:, None]).astype(jnp.int32)
    running = jnp.cumsum(onehot, axis=1)
    counts = running[:, -1]
    padded = (counts + tb - 1) // tb * tb
    pend = jnp.cumsum(padded)
    pstart = pend - padded
    dest = jnp.sum(onehot * (pstart[:, None] + running - 1), axis=0).reshape(n, TOP_K)
    nblk = -(-na // tb) + N_EXPERTS
    block_expert = jnp.minimum(jnp.searchsorted(pend, jnp.arange(nblk, dtype=jnp.int32) * tb, side='right'),
                               N_EXPERTS - 1).astype(jnp.int32)
    meta = jnp.concatenate([block_expert, (pend[-1:] // tb).astype(jnp.int32)])
    pad_ranges = jnp.concatenate([pstart + counts, pend[:-1], jnp.full((1,), nblk * tb)]).astype(jnp.int32)
    return dest.astype(jnp.int32), gates, meta, pad_ranges, nblk


def _router_weight(w_router):
    return jnp.pad(w_router, ((0, 0), (0, LANES - N_EXPERTS))).astype(BF16)


def _moe_ffn(x, logits, w_gate, w_up, w_down, ln_g, ln_b):
    dest, gates, meta, pad_ranges, nblk = _moe_routing(logits[:, :N_EXPERTS], MOE_TB)
    xs = _moe_dispatch(x, dest, pad_ranges, nblk * MOE_TB)
    y = _moe_experts(xs, meta, w_gate, w_up, w_down)
    return _moe_combine(y, dest, gates, x, ln_g, ln_b)


def _split_w_in(w):
    a_end = 3 * A_QKV_W
    bz_end = a_end + 3 * B_W + B_W
    w_groups = [jnp.concatenate([w[:, s * A_QKV_W + gi * A_GROUP_W:s * A_QKV_W + (gi + 1) * A_GROUP_W]
                                 for s in range(3)], axis=1).astype(BF16) for gi in range(N_GROUPS)]
    zpad = jnp.zeros((w.shape[0], LANES - N_HEADS_B), w.dtype)
    w_b = jnp.concatenate([w[:, a_end:bz_end], w[:, bz_end + 2 * N_HEADS_B:],
                           w[:, bz_end:bz_end + N_HEADS_B], zpad,
                           w[:, bz_end + N_HEADS_B:bz_end + 2 * N_HEADS_B], zpad], axis=1)
    return w_groups, w_b.astype(BF16)


def _hybrid_layer(x, batch, seq, rel_bias, w_in, conv_w, a_log, dt_bias, o_norm_w, w_oa, w_ob, w_out, ln_g, ln_b,
                  w_router=None):
    w_groups, w_b = _split_w_in(w_in)
    proj_b = _matmul(x, w_b, F32, "in_proj_gdn_gates")
    outs, lses = [], []
    for gi, (_, dilation) in enumerate(DSWA_PATTERNS):
        qkv = _in_proj_strided(x, w_groups[gi], dilation, batch, seq)
        o, lse = _dswa_group(qkv, _band_bias(rel_bias, gi, dilation), dilation)
        outs.append(o)
        lses.append(lse)
    yb = _gated_deltanet(proj_b, conv_w, a_log, dt_bias, o_norm_w, batch, seq)
    return _mix_out(outs, lses, yb, proj_b, x, w_oa, w_ob, w_out, ln_g, ln_b, w_router)


def kernel(x, rel_bias, w_in, conv_w, a_log, dt_bias, o_norm_w, w_oa, w_ob, w_out, ln1_g, ln1_b,
           ffn_w_gate, ffn_w_up, ffn_w_down, moe_router, moe_w_gate, moe_w_up, moe_w_down, ln2_g, ln2_b):
    batch, seq, d = x.shape
    assert d == D_MODEL and w_in.shape[0] == DEPTH
    assert all(seq % t == 0 for t in (PROJ_A_TM, OUT_TM, GDN_STEP_CHUNKS * GDN_CHUNK, DSWA_ROWS))
    assert all(seq % (dilation * ATT_BLK) == 0 for _, dilation in DSWA_PATTERNS)
    assert all((batch * seq) % t == 0 for t in (MM_TM, FFN_TM, MOE_TB, CMB_TM, DSP_TM))
    h = x.reshape(batch * seq, d)
    for layer in range(DEPTH):
        j = layer // 2
        is_moe = layer % 2 == 1
        h = _hybrid_layer(h, batch, seq, rel_bias, w_in[layer], conv_w[layer], a_log[layer], dt_bias[layer],
                          o_norm_w[layer], w_oa[layer], w_ob[layer], w_out[layer], ln1_g[layer], ln1_b[layer],
                          _router_weight(moe_router[j]) if is_moe else None)
        if is_moe:
            h, logits = h
            h = _moe_ffn(h, logits, moe_w_gate[j], moe_w_up[j], moe_w_down[j], ln2_g[layer], ln2_b[layer])
        else:
            h = _dense_ffn(h, ffn_w_gate[j], ffn_w_up[j], ffn_w_down[j], ln2_g[layer], ln2_b[layer])
    return h.reshape(batch, seq, d)
```

```python
import functools
import math

import jax
import jax.numpy as jnp
from jax import lax
from jax.experimental import pallas as pl
from jax.experimental.pallas import tpu as pltpu

F32 = jnp.float32
BF16 = jnp.bfloat16

D_MODEL = 1024
DEPTH = 2
DSWA_PATTERNS = ((128, 1), (512, 4), (2048, 16))
N_GROUPS = 3
HEADS_PER_GROUP = 4
HEAD_DIM = 128
A_QKV_W = N_GROUPS * HEADS_PER_GROUP * HEAD_DIM
A_GROUP_W = HEADS_PER_GROUP * HEAD_DIM
NUM_BUCKETS = 32
MAX_DISTANCE = 2048
N_HEADS_B = 8
B_W = N_HEADS_B * HEAD_DIM
CONV_WIDTH = 4
N_EXPERTS = 8
TOP_K = 2
ALPHA = (2 * DEPTH) ** 0.25
LN_EPS = 1e-5
RMS_EPS = 1e-6

LANES = 128
SUBLANES = 8
SUBLANE_BITS = SUBLANES.bit_length() - 1
VMEM_LIMIT = 56 * 1024 * 1024

ATT_BLK = 128
DSWA_ROWS = 2048
GDN_CHUNK = 128
GDN_STEP_CHUNKS = 2
MM_TM = 512
MM_TN = 3200
PROJ_A_TM = 1024
OUT_TM = 512
OUT_SUBTILES = 2
FFN_TM = 1024
FFN_TF = 1024
MOE_TB = 512
MOE_TF = 1792
CMB_TM = 1024
DSP_TM = 1024

NEG_BIG = -1e30


def _cparams(sem, vmem=VMEM_LIMIT, **kw):
    return pltpu.CompilerParams(dimension_semantics=sem, vmem_limit_bytes=vmem, **kw)


def _bdot(a, b):
    return jnp.dot(a.astype(BF16), b.astype(BF16), preferred_element_type=F32)


def _bdot_nt(a, b):
    return lax.dot_general(a.astype(BF16), b.astype(BF16), (((1,), (1,)), ((), ())),
                           preferred_element_type=F32)


def _sigmoid(v):
    return 1.0 / (1.0 + jnp.exp(-v))


def _silu(v):
    return v * _sigmoid(v)


def _layer_norm(v, g, b):
    mu = jnp.mean(v, axis=-1, keepdims=True)
    d = v - mu
    var = jnp.mean(d * d, axis=-1, keepdims=True)
    return d * lax.rsqrt(var + LN_EPS) * g + b


def _mm_kernel(x_ref, w_ref, o_ref):
    o_ref[...] = _bdot(x_ref[...], w_ref[...]).astype(o_ref.dtype)


def _matmul(x, w, out_dtype, name, tm=MM_TM, tn=MM_TN):
    m, k = x.shape
    n = w.shape[1]
    tn = min(tn, n)
    return pl.pallas_call(
        _mm_kernel,
        name=name,
        grid=(n // tn, m // tm),
        in_specs=[pl.BlockSpec((tm, k), lambda j, i: (i, 0)),
                  pl.BlockSpec((k, tn), lambda j, i: (0, j))],
        out_specs=pl.BlockSpec((tm, tn), lambda j, i: (i, j)),
        out_shape=jax.ShapeDtypeStruct((m, n), out_dtype),
        compiler_params=_cparams(("parallel", "parallel")),
    )(x, w)


def _dswa_kernel(q_ref, kc_ref, kp_ref, vc_ref, vp_ref, bias0_ref, bias_ref, o_ref, lse_ref, *, tq):
    nqb = tq // ATT_BLK
    lane = lax.broadcasted_iota(jnp.int32, (ATT_BLK, LANES), 1)
    scale = HEAD_DIM ** -0.5
    exp2_scale = scale * math.log2(math.e)
    for ri in range(q_ref.shape[0]):
        q = q_ref[ri]
        kwin = jnp.concatenate([kp_ref[ri], kc_ref[ri]], axis=0)
        vwin = jnp.concatenate([vp_ref[ri], vc_ref[ri]], axis=0)
        for c in range(nqb):
            b_ref = bias0_ref if c == 0 else bias_ref
            rows = slice(c * ATT_BLK, (c + 1) * ATT_BLK)
            lse_tile = jnp.zeros((ATT_BLK, LANES), F32)
            for h in range(HEADS_PER_GROUP):
                hs = slice(h * HEAD_DIM, (h + 1) * HEAD_DIM)
                kh = kwin[c * ATT_BLK:(c + 2) * ATT_BLK, hs]
                vh = vwin[c * ATT_BLK:(c + 2) * ATT_BLK, hs]
                t = _bdot_nt(q[rows, hs], kh) + b_ref[h]
                m = jnp.max(t, axis=-1, keepdims=True)
                p = jnp.exp2((t - m) * exp2_scale)
                l = jnp.sum(p, axis=-1, keepdims=True)
                o_ref[ri, rows, hs] = _bdot(p, vh) / l
                lse_tile = jnp.where(lane == h, m * scale + jnp.log(l), lse_tile)
            lse_ref[ri, rows, :] = lse_tile


def _in_proj_strided_kernel(x_ref, w_ref, o_ref, *, dilation):
    res = _bdot(x_ref[...], w_ref[...])
    if dilation == 1:
        o_ref[0] = res.astype(o_ref.dtype)
    else:
        rows = res.shape[0] // dilation
        o_ref[...] = jnp.swapaxes(res.reshape(rows, dilation, res.shape[1]), 0, 1).astype(o_ref.dtype)


def _in_proj_strided(x, w, dilation, batch, seq, tm=PROJ_A_TM):
    k = x.shape[1]
    wn = tn = w.shape[1]
    tpb = seq // tm
    return pl.pallas_call(
        functools.partial(_in_proj_strided_kernel, dilation=dilation),
        name=f"in_proj_attn_d{dilation}",
        grid=(batch * tpb, wn // tn),
        in_specs=[pl.BlockSpec((tm, k), lambda i, j: (i, 0)),
                  pl.BlockSpec((k, tn), lambda i, j: (0, j))],
        out_specs=pl.BlockSpec((None, dilation, tm // dilation, tn), lambda i, j: (i // tpb, 0, i % tpb, j)),
        out_shape=jax.ShapeDtypeStruct((batch, dilation, seq // dilation, wn), BF16),
        compiler_params=_cparams(("parallel", "parallel")),
    )(x, w)


def _dswa_group(qkv, bias, dilation):
    batch, _, n, _ = qkv.shape
    tq = min(DSWA_ROWS, n)
    rr = DSWA_ROWS // tq
    nqb = tq // ATT_BLK

    def cur(off):
        return pl.BlockSpec((None, rr, tq, A_GROUP_W), lambda b, r, j: (b, r, j, off))

    def prev(off):
        return pl.BlockSpec((None, rr, ATT_BLK, A_GROUP_W),
                            lambda b, r, j: (b, r, jnp.maximum(j * nqb - 1, 0), off))

    return pl.pallas_call(
        functools.partial(_dswa_kernel, tq=tq),
        name=f"dswa_d{dilation}",
        grid=(batch, dilation // rr, n // tq),
        in_specs=[cur(0), cur(1), prev(1), cur(2), prev(2),
                  pl.BlockSpec((None, HEADS_PER_GROUP, ATT_BLK, 2 * ATT_BLK),
                               lambda b, r, j: (jnp.minimum(j, 1), 0, 0, 0)),
                  pl.BlockSpec((None, HEADS_PER_GROUP, ATT_BLK, 2 * ATT_BLK), lambda b, r, j: (1, 0, 0, 0))],
        out_specs=[pl.BlockSpec((None, rr, tq, A_GROUP_W), lambda b, r, j: (b, r, j, 0)),
                   pl.BlockSpec((None, rr, tq, LANES), lambda b, r, j: (b, r, j, 0))],
        out_shape=[jax.ShapeDtypeStruct((batch, dilation, n, A_GROUP_W), F32),
                   jax.ShapeDtypeStruct((batch, dilation, n, LANES), F32)],
        compiler_params=_cparams(("parallel", "parallel", "parallel")),
    )(qkv, qkv, qkv, qkv, qkv, bias, bias)


def _t5_causal_bucket(dist):
    num_exact = NUM_BUCKETS // 2
    d = jnp.maximum(dist, 1).astype(F32)
    large = num_exact + (jnp.log(d / num_exact) / math.log(MAX_DISTANCE / num_exact)
                         * (NUM_BUCKETS - num_exact)).astype(jnp.int32)
    large = jnp.minimum(large, NUM_BUCKETS - 1)
    return jnp.where(dist < num_exact, dist, large)


def _band_bias(rel_bias, gi, dilation):
    qi = jnp.arange(ATT_BLK)[:, None] + ATT_BLK
    kj = jnp.arange(2 * ATT_BLK)[None, :]
    band = (qi - kj >= 0) & (qi - kj <= ATT_BLK)
    delta = jnp.maximum(qi - kj, 0) * dilation
    table = rel_bias[:, gi * HEADS_PER_GROUP:(gi + 1) * HEADS_PER_GROUP].astype(F32)
    onehot = (_t5_causal_bucket(delta)[..., None] == jnp.arange(NUM_BUCKETS)).astype(F32)
    bias = jnp.einsum('qkn,nh->hqk', onehot, table, precision=lax.Precision.HIGHEST) / (HEAD_DIM ** -0.5)
    return jnp.stack([jnp.where(band & (kj >= ATT_BLK), bias, NEG_BIG), jnp.where(band, bias, NEG_BIG)])


def _dot16(a, b):
    return jnp.dot(a, b, preferred_element_type=F32)


def _unit_lower_inverse(a, row, col):
    base = 16
    heads = list(a)
    eye = (row == col).astype(F32)
    same = lambda sz: (row // sz) == (col // sz)
    blk = same(base)
    a_d = {h: jnp.where(blk, a[h], 0.0) for h in heads}
    t = {h: eye - a_d[h] for h in heads}
    p = {h: a_d[h].astype(BF16) for h in heads}
    for _ in range(3):
        p = {h: _dot16(p[h], p[h]).astype(BF16) for h in heads}
        t = {h: t[h] + _dot16(t[h].astype(BF16), p[h]) for h in heads}
    sz = 2 * base
    while sz <= GDN_CHUNK:
        off = same(sz) & jnp.logical_not(same(sz // 2))
        tb = {h: t[h].astype(BF16) for h in heads}
        m = {h: _dot16(jnp.where(off, a[h], 0.0).astype(BF16), tb[h]).astype(BF16) for h in heads}
        t = {h: t[h] - _dot16(tb[h], m[h]) for h in heads}
        sz *= 2
    return t


def _gdn_kernel(x_ref, halo_ref, z_ref, ba_ref, cw_ref, alog_ref, dtb_ref, onw_ref, y_ref, state_ref):
    c = pl.program_id(1)

    @pl.when(c == 0)
    def _():
        state_ref[...] = jnp.zeros_like(state_ref)

    C = GDN_CHUNK
    chunks = range(x_ref.shape[0] // C)
    rows = {cc: slice(cc * C, (cc + 1) * C) for cc in chunks}
    keep_halo = (c > 0).astype(F32)
    row = lax.broadcasted_iota(jnp.int32, (C, C), 0)
    col = lax.broadcasted_iota(jnp.int32, (C, C), 1)
    incl = row >= col
    strict = row > col
    ones_l = incl.astype(BF16)

    def log_decay_cumsum(cc):
        ba = ba_ref[rows[cc], :]
        a_in = ba[:, LANES:] + dtb_ref[...]
        softplus = jnp.maximum(a_in, 0.0) + jnp.log(1.0 + jnp.exp(-jnp.abs(a_in)))
        g_all = -jnp.exp(alog_ref[...]) * softplus
        g1 = g_all.astype(BF16)
        r1 = g_all - g1.astype(F32)
        g2 = r1.astype(BF16)
        g3 = (r1 - g2.astype(F32)).astype(BF16)
        return (jnp.dot(ones_l, g1, preferred_element_type=F32) + jnp.dot(ones_l, g2, preferred_element_type=F32)
                + jnp.dot(ones_l, g3, preferred_element_type=F32))

    beta_all = {cc: _sigmoid(ba_ref[rows[cc], :LANES]) for cc in chunks}
    G = {cc: log_decay_cumsum(cc) for cc in chunks}
    GT = {cc: G[cc].T for cc in chunks}
    exp_g = {cc: jnp.exp(G[cc]) for cc in chunks}
    beta_exp_g = {cc: beta_all[cc] * exp_g[cc] for cc in chunks}
    exp_rest = {cc: jnp.exp(G[cc][C - 1:C, :] - G[cc]) for cc in chunks}

    def conv_silu(cc, col0):
        cs = slice(col0, col0 + HEAD_DIM)
        xc = x_ref[rows[cc], cs]
        halo = halo_ref[:, cs] * keep_halo if cc == 0 else x_ref[cc * C - SUBLANES:cc * C, cs]
        xf = jnp.concatenate([halo, xc], axis=0)
        w = cw_ref[:, cs]
        y = w[CONV_WIDTH - 1:CONV_WIDTH] * xc
        for i in range(CONV_WIDTH - 1):
            off = SUBLANES - (CONV_WIDTH - 1) + i
            y = y + w[i:i + 1] * xf[off:off + C]
        return _silu(y)

    def l2norm(t, scale=1.0):
        return t * (lax.rsqrt(jnp.sum(t * t, axis=-1, keepdims=True) + RMS_EPS) * scale)

    items = [(cc, h) for cc in chunks for h in range(N_HEADS_B)]
    q = {(cc, h): l2norm(conv_silu(cc, h * HEAD_DIM), HEAD_DIM ** -0.5) for cc, h in items}
    k = {(cc, h): l2norm(conv_silu(cc, B_W + h * HEAD_DIM)) for cc, h in items}
    v = {(cc, h): conv_silu(cc, 2 * B_W + h * HEAD_DIM) for cc, h in items}
    beta = {(cc, h): beta_all[cc][:, h:h + 1] for cc, h in items}
    gc = {(cc, h): G[cc][:, h:h + 1] for cc, h in items}
    g_last = {(cc, h): G[cc][C - 1:C, h:h + 1] for cc, h in items}
    decay = {(cc, h): jnp.exp(jnp.where(incl, gc[cc, h] - GT[cc][h:h + 1, :], NEG_BIG)) for cc, h in items}
    eg = {(cc, h): exp_g[cc][:, h:h + 1] for cc, h in items}
    kb ={it: k[it].astype(BF16) for it in items}
    kq = {it: lax.dot_general(jnp.concatenate([kb[it], q[it].astype(BF16)], axis=0), kb[it],
                              (((1,), (1,)), ((), ())), preferred_element_type=F32) for it in items}
    a = {it: jnp.where(strict, beta[it] * kq[it][:C] * decay[it], 0.0) for it in items}
    t_inv = _unit_lower_inverse(a, row, col)
    rhs = {(cc, h): jnp.concatenate([beta[cc, h] * v[cc, h], beta_exp_g[cc][:, h:h + 1] * k[cc, h]],
                                    axis=1).astype(BF16) for cc, h in items}
    sol = {it: _dot16(t_inv[it].astype(BF16), rhs[it]) for it in items}
    qk = {it: (kq[it][C:] * decay[it]).astype(BF16) for it in items}
    wq = {it: jnp.concatenate([sol[it][:, HEAD_DIM:], q[it] * eg[it]], axis=0).astype(BF16) for it in items}
    k_dec = {(cc, h): (k[cc, h] * exp_rest[cc][:, h:h + 1]).astype(BF16) for cc, h in items}

    heads = range(N_HEADS_B)
    state = {h: state_ref[h] for h in heads}
    for cc in chunks:
        ws = {h: _dot16(wq[cc, h], state[h].astype(BF16)) for h in heads}
        u = {h: (sol[cc, h][:, :HEAD_DIM] - ws[h][:C]).astype(BF16) for h in heads}
        o = {h: ws[h][C:] + _dot16(qk[cc, h], u[h]) for h in heads}
        state = {h: jnp.exp(g_last[cc, h]) * state[h] + lax.dot_general(
            k_dec[cc, h], u[h], (((0,), (0,)), ((), ())), preferred_element_type=F32) for h in heads}
        for h in heads:
            oh = o[h] * lax.rsqrt(jnp.mean(o[h] * o[h], axis=-1, keepdims=True) + RMS_EPS) * onw_ref[...]
            hs = slice(h * HEAD_DIM, (h + 1) * HEAD_DIM)
            y_ref[rows[cc], hs] = (oh * _silu(z_ref[rows[cc], hs])).astype(y_ref.dtype)
    for h in heads:
        state_ref[h] = state[h]


def _gated_deltanet(proj_b, conv_w, a_log, dt_bias, o_norm_w, batch, seq):
    C = GDN_STEP_CHUNKS * GDN_CHUNK
    wb = proj_b.shape[1]
    pb = proj_b.reshape(batch, seq, wb)
    ba_blk = (wb - 2 * LANES) // (2 * LANES)
    pad = lambda t: jnp.pad(t.astype(F32), (0, LANES - t.shape[0])).reshape(1, LANES)
    const = lambda shape: pl.BlockSpec(shape, lambda b, c: (0,) * len(shape))
    y = pl.pallas_call(
        _gdn_kernel,
        name="gated_deltanet",
        grid=(batch, seq // C),
        in_specs=[pl.BlockSpec((None, C, 3 * B_W), lambda b, c: (b, c, 0)),
                  pl.BlockSpec((None, SUBLANES, 3 * B_W),
                               lambda b, c: (b, jnp.maximum(c * (C // SUBLANES) - 1, 0), 0)),
                  pl.BlockSpec((None, C, B_W), lambda b, c: (b, c, 3)),
                  pl.BlockSpec((None, C, 2 * LANES), lambda b, c: (b, c, ba_blk)),
                  const((CONV_WIDTH, 3 * B_W)), const((1, LANES)), const((1, LANES)), const((1, LANES))],
        out_specs=pl.BlockSpec((None, C, B_W), lambda b, c: (b, c, 0)),
        out_shape=jax.ShapeDtypeStruct((batch, seq, B_W), BF16),
        scratch_shapes=[pltpu.VMEM((N_HEADS_B, HEAD_DIM, HEAD_DIM), F32)],
        compiler_params=_cparams(("parallel", "arbitrary")),
    )(pb, pb, pb, pb, conv_w.astype(F32), pad(a_log), pad(dt_bias), o_norm_w.astype(F32).reshape(1, LANES))
    return y.reshape(batch * seq, B_W)


def _mix_out_kernel(o0, o1, o2, l0, l1, l2, yb_ref, ga_ref, gb_ref, x_ref, woa_ref, wob_ref, wout_ref,
                    g_ref, b_ref, *rest, with_router):
    if with_router:
        wr_ref, out_ref, logits_ref, *scratch = rest
    else:
        out_ref, *scratch = rest

    def token_order(ref, scr):
        dilation, rows, width = ref.shape
        if dilation == 1:
            return ref[0]
        if dilation % SUBLANES == 0:
            return jnp.swapaxes(ref[...], 0, 1).reshape(dilation * rows, width)
        planes = []
        for t in range(width // LANES):
            for r in range(dilation):
                scr[t, pl.ds(r, rows, stride=dilation), :] = ref[r, :, t * LANES:(t + 1) * LANES]
            planes.append(scr[t])
        return jnp.concatenate(planes, axis=1)

    outs = (token_order(o0, None), token_order(o1, scratch[0]), token_order(o2, scratch[1]))
    lses = (token_order(l0, None), token_order(l1, scratch[2]), token_order(l2, scratch[3]))
    tm = x_ref.shape[0]
    for sub in range(OUT_SUBTILES):
        rs = slice(sub * tm // OUT_SUBTILES, (sub + 1) * tm // OUT_SUBTILES)
        lse_s = [t[rs] for t in lses]
        m = jnp.maximum(jnp.maximum(lse_s[0], lse_s[1]), lse_s[2])
        es = [jnp.exp(t - m) for t in lse_s]
        inv = 1.0 / (es[0] + es[1] + es[2])
        wgt = [e * inv for e in es]
        ya = []
        for h in range(HEADS_PER_GROUP):
            hs = slice(h * HEAD_DIM, (h + 1) * HEAD_DIM)
            ya.append(wgt[0][:, h:h + 1] * outs[0][rs, hs] + wgt[1][:, h:h + 1] * outs[1][rs, hs]
                      + wgt[2][:, h:h + 1] * outs[2][rs, hs])
        ya = jnp.concatenate(ya, axis=1)
        pa = _bdot(ya, woa_ref[...])
        pb = _bdot(yb_ref[rs, :], wob_ref[...])
        merged = _sigmoid(ga_ref[rs, :]) * pa + _sigmoid(gb_ref[rs, :]) * pb
        mix = _bdot(merged, wout_ref[...])
        out = _layer_norm(ALPHA * x_ref[rs, :] + mix, g_ref[...], b_ref[...])
        out_ref[rs, :] = out
        if with_router:
            logits_ref[rs, :] = _bdot(out, wr_ref[...])


def _mix_out(outs, lses, yb, proj_b, x, w_oa, w_ob, w_out, ln_g, ln_b, w_router=None, tm=OUT_TM):
    n = x.shape[0]
    with_router = w_router is not None
    seq = outs[0].shape[1] * outs[0].shape[2]
    tpb = seq // tm
    rowblk = lambda w, cb=0: pl.BlockSpec((tm, w), lambda i: (i, cb))
    const = lambda a: pl.BlockSpec(a.shape, lambda i: (0, 0), pipeline_mode=pl.Buffered(1))

    def grouped(a):
        d, w = a.shape[1], a.shape[3]
        return pl.BlockSpec((None, d, tm // d, w), lambda i: (i // tpb, 0, i % tpb, 0))

    wa, wb, wo = w_oa.astype(BF16), w_ob.astype(BF16), w_out.astype(BF16)
    g, b = ln_g.reshape(1, D_MODEL), ln_b.reshape(1, D_MODEL)
    router = [w_router] if with_router else []
    out_specs = [rowblk(D_MODEL)] + ([rowblk(LANES)] if with_router else [])
    out_shape = [jax.ShapeDtypeStruct((n, D_MODEL), F32)] + (
        [jax.ShapeDtypeStruct((n, LANES), F32)] if with_router else [])
    res = pl.pallas_call(
        functools.partial(_mix_out_kernel, with_router=with_router),
        name="mix_out_ln",
        grid=(n // tm,),
        in_specs=[grouped(a) for a in (*outs, *lses)]
        + [rowblk(B_W), rowblk(D_MODEL, 4), rowblk(D_MODEL, 5), rowblk(D_MODEL),
           const(wa), const(wb), const(wo), const(g), const(b)] + [const(a) for a in router],
        out_specs=out_specs,
        out_shape=out_shape,
        scratch_shapes=[pltpu.VMEM((A_GROUP_W // LANES, tm, LANES), F32)] * 2 + [pltpu.VMEM((1, tm, LANES), F32)] * 2,
        compiler_params=_cparams(("parallel",)),
    )(*outs, *lses, yb, proj_b, proj_b, x, wa, wb, wo, g, b, *router)
    return tuple(res) if with_router else res[0]


def _ffn_kernel(x_ref, wg_ref, wu_ref, wd_ref, g_ref, b_ref, out_ref):
    tm = x_ref.shape[0]
    dff = wg_ref.shape[1]
    for sub in range(OUT_SUBTILES):
        rs = slice(sub * tm // OUT_SUBTILES, (sub + 1) * tm // OUT_SUBTILES)
        x = x_ref[rs, :]
        xb = x.astype(BF16)
        acc = None
        for c0 in range(0, dff, FFN_TF):
            c1 = min(c0 + FFN_TF, dff)
            gate = jnp.dot(xb, wg_ref[:, c0:c1], preferred_element_type=F32)
            up = jnp.dot(xb, wu_ref[:, c0:c1], preferred_element_type=F32)
            part = _bdot(_silu(gate) * up, wd_ref[c0:c1, :])
            acc = part if acc is None else acc + part
        out_ref[rs, :] = _layer_norm(ALPHA * x + acc, g_ref[...], b_ref[...])


def _dense_ffn(x, w_gate, w_up, w_down, ln_g, ln_b, tm=FFN_TM):
    n = x.shape[0]
    g, b = ln_g.reshape(1, D_MODEL), ln_b.reshape(1, D_MODEL)
    resident = lambda a: pl.BlockSpec(a.shape, lambda i: (0, 0), pipeline_mode=pl.Buffered(1))
    wg, wu, wd = w_gate.astype(BF16), w_up.astype(BF16), w_down.astype(BF16)
    return pl.pallas_call(
        _ffn_kernel,
        name="dense_ffn_ln",
        grid=(n // tm,),
        in_specs=[pl.BlockSpec((tm, D_MODEL), lambda i: (i, 0)),
                  resident(wg), resident(wu), resident(wd), resident(g), resident(b)],
        out_specs=pl.BlockSpec((tm, D_MODEL), lambda i: (i, 0)),
        out_shape=jax.ShapeDtypeStruct((n, D_MODEL), F32),
        compiler_params=_cparams(("parallel",)),
    )(x, wg, wu, wd, g, b)


def _gather_rows(idx_ref, src_hbm, dst, sem, count):
    def body(i, carry):
        pltpu.make_async_copy(src_hbm.at[pl.ds(idx_ref[0, i], 1)], dst.at[pl.ds(i, 1)], sem).start()
        return carry
    lax.fori_loop(0, count, body, 0, unroll=8)


def _wait_rows(src_hbm, dst, sem, count):
    pltpu.make_async_copy(src_hbm.at[pl.ds(0, count)], dst, sem).wait()


def _dispatch_kernel(dest_ref, pad_ref, x_ref, xs_out, zero_ref, sem):
    tm = x_ref.shape[0]
    zero_row = lambda r: pltpu.make_async_copy(zero_ref.at[pl.ds(0, 1)], xs_out.at[pl.ds(r, 1)], sem.at[1])
    zero_tile = lambda t: pltpu.make_async_copy(
        zero_ref, xs_out.at[pl.ds(pl.multiple_of(t * SUBLANES, SUBLANES), SUBLANES)], sem.at[1])

    def for_padding_rows(fn):
        for e in range(N_EXPERTS):
            lo, hi = pad_ref[0, e], pad_ref[0, N_EXPERTS + e]
            lo_tile = lax.shift_right_logical(lo + (SUBLANES - 1), SUBLANE_BITS)
            hi_tile = lax.shift_right_logical(hi, SUBLANE_BITS)
            lax.fori_loop(lo, jnp.minimum(lo_tile * SUBLANES, hi), lambda r, c: (fn(zero_row(r)), c)[1], 0)
            lax.fori_loop(lo_tile, hi_tile, lambda t, c: (fn(zero_tile(t)), c)[1], 0)

    @pl.when(pl.program_id(0) == 0)
    def _():
        zero_ref[...] = jnp.zeros_like(zero_ref)
        for_padding_rows(lambda copy: copy.start())

    @pl.when(pl.program_id(0) == pl.num_programs(0) - 1)
    def _():
        for_padding_rows(lambda copy: copy.wait())

    def body(i, carry):
        for k in range(TOP_K):
            pltpu.make_async_copy(x_ref.at[pl.ds(i, 1)], xs_out.at[pl.ds(dest_ref[0, TOP_K * i + k], 1)],
                                  sem.at[0]).start()
        return carry
    lax.fori_loop(0, tm, body, 0, unroll=4)
    for _ in range(TOP_K):
        pltpu.make_async_copy(x_ref, xs_out.at[pl.ds(0, tm)], sem.at[0]).wait()


def _moe_dispatch(x, dest, pad_ranges, n_rows, tm=DSP_TM):
    n = x.shape[0]
    dest3 = dest.reshape(n // tm, 1, TOP_K * tm)
    return pl.pallas_call(
        _dispatch_kernel,
        name="moe_dispatch",
        grid=(n // tm,),
        in_specs=[pl.BlockSpec((None, 1, TOP_K * tm), lambda i: (i, 0, 0), memory_space=pltpu.SMEM),
                  pl.BlockSpec((1, 2 * N_EXPERTS), lambda i: (0, 0), memory_space=pltpu.SMEM),
                  pl.BlockSpec((tm, D_MODEL), lambda i: (i, 0))],
        out_specs=pl.BlockSpec(memory_space=pl.ANY),
        out_shape=jax.ShapeDtypeStruct((n_rows, D_MODEL), F32),
        scratch_shapes=[pltpu.VMEM((SUBLANES, D_MODEL), F32), pltpu.SemaphoreType.DMA((2,))],
        compiler_params=_cparams(("arbitrary",), disable_bounds_checks=True),
    )(dest3, pad_ranges.reshape(1, 2 * N_EXPERTS), x)


def _moe_kernel(meta_ref, x_ref, wg_ref, wu_ref, wd_ref, y_ref, xb_ref, acc_ref):
    b = pl.program_id(0)
    f = pl.program_id(1)
    nblk = pl.num_programs(0)
    used = b < meta_ref[nblk]

    @pl.when(used)
    def _():
        @pl.when(f == 0)
        def _():
            xb_ref[...] = x_ref[...].astype(BF16)
            acc_ref[...] = jnp.zeros_like(acc_ref)

        xb = xb_ref[...]
        gate = jnp.dot(xb, wg_ref[...], preferred_element_type=F32)
        up = jnp.dot(xb, wu_ref[...], preferred_element_type=F32)
        acc_ref[...] += _bdot(_silu(gate) * up, wd_ref[...])

    @pl.when(f == pl.num_programs(1) - 1)
    def _():
        @pl.when(used)
        def _():
            y_ref[...] = acc_ref[...]

        @pl.when(jnp.logical_not(used))
        def _():
            y_ref[...] = jnp.zeros_like(y_ref)


def _moe_experts(xs, meta, w_gate, w_up, w_down, tb=MOE_TB, tf=MOE_TF):
    nblk = xs.shape[0] // tb
    dffe = w_gate.shape[2]
    grid_spec = pltpu.PrefetchScalarGridSpec(
        num_scalar_prefetch=1,
        grid=(nblk, dffe // tf),
        in_specs=[pl.BlockSpec((tb, D_MODEL), lambda b, f, m: (jnp.minimum(b, m[nblk] - 1), 0)),
                  pl.BlockSpec((None, D_MODEL, tf), lambda b, f, m: (m[b], 0, f)),
                  pl.BlockSpec((None, D_MODEL, tf), lambda b, f, m: (m[b], 0, f)),
                  pl.BlockSpec((None, tf, D_MODEL), lambda b, f, m: (m[b], f, 0))],
        out_specs=pl.BlockSpec((tb, D_MODEL), lambda b, f, m: (b, 0)),
        scratch_shapes=[pltpu.VMEM((tb, D_MODEL), BF16), pltpu.VMEM((tb, D_MODEL), F32)],
    )
    return pl.pallas_call(
        _moe_kernel,
        name="moe_experts",
        grid_spec=grid_spec,
        out_shape=jax.ShapeDtypeStruct((nblk * tb, D_MODEL), F32),
        compiler_params=_cparams(("parallel", "arbitrary")),
    )(meta, xs, w_gate.astype(BF16), w_up.astype(BF16), w_down.astype(BF16))


def _combine_kernel(pos0_ref, posn_ref, y_hbm, gates_ref, x_ref, g_ref, b_ref, out_ref, ybuf, sem):
    i = pl.program_id(0)
    tm = out_ref.shape[0]
    slot = i % 2

    @pl.when(i == 0)
    def _():
        _gather_rows(pos0_ref, y_hbm, ybuf.at[0], sem.at[0], TOP_K * tm)

    _wait_rows(y_hbm, ybuf.at[slot], sem.at[slot], TOP_K * tm)

    @pl.when(i + 1 < pl.num_programs(0))
    def _():
        _gather_rows(posn_ref, y_hbm, ybuf.at[1 - slot], sem.at[1 - slot], TOP_K * tm)

    gates = gates_ref[...]
    f = gates[:, 0:1] * ybuf[slot, :tm] + gates[:, 1:2] * ybuf[slot, tm:]
    out_ref[...] = _layer_norm(ALPHA * x_ref[...] + f, g_ref[...], b_ref[...])


def _moe_combine(y, pos, gates, x, ln_g, ln_b, tm=CMB_TM):
    n = x.shape[0]
    nt = n // tm
    pos3 = pos.reshape(nt, tm, TOP_K).transpose(0, 2, 1).reshape(nt, 1, TOP_K * tm)
    g, b = ln_g.reshape(1, D_MODEL), ln_b.reshape(1, D_MODEL)
    smem_blk = lambda imap: pl.BlockSpec((None, 1, TOP_K * tm), imap, memory_space=pltpu.SMEM)
    return pl.pallas_call(
        _combine_kernel,
        name="moe_combine_ln",
        grid=(nt,),
        in_specs=[smem_blk(lambda i: (0, 0, 0)),
                  smem_blk(lambda i: (jnp.minimum(i + 1, nt - 1), 0, 0)),
                  pl.BlockSpec(memory_space=pl.ANY),
                  pl.BlockSpec((tm, TOP_K), lambda i: (i, 0)),
                  pl.BlockSpec((tm, D_MODEL), lambda i: (i, 0)),
                  pl.BlockSpec((1, D_MODEL), lambda i: (0, 0)),
                  pl.BlockSpec((1, D_MODEL), lambda i: (0, 0))],
        out_specs=pl.BlockSpec((tm, D_MODEL), lambda i: (i, 0)),
        out_shape=jax.ShapeDtypeStruct((n, D_MODEL), F32),
        scratch_shapes=[pltpu.VMEM((2, TOP_K * tm, D_MODEL), F32), pltpu.SemaphoreType.DMA((2,))],
        compiler_params=_cparams(("arbitrary",), disable_bounds_checks=True),
    )(pos3, pos3, y, gates, x, g, b)


def _moe_routing(logits, tb):
    n = logits.shape[0]
    top_logit, top_idx = lax.top_k(logits, TOP_K)
    gates = jax.nn.softmax(top_logit, axis=-1)
    na = n * TOP_K
    e_flat = top_idx.reshape(-1).astype(jnp.int32)
    onehot = (e_flat[None, :] == jnp.arange(N_EXPERTS, dtype=jnp.int32)[:, None]).astype(jnp.int32)
    running = jnp.cumsum(onehot, axis=1)
    counts = running[:, -1]
    padded = (counts + tb - 1) // tb * tb
    pend = jnp.cumsum(padded)
    pstart = pend - padded
    dest = jnp.sum(onehot * (pstart[:, None] + running - 1), axis=0).reshape(n, TOP_K)
    nblk = -(-na // tb) + N_EXPERTS
    block_expert = jnp.minimum(jnp.searchsorted(pend, jnp.arange(nblk, dtype=jnp.int32) * tb, side='right'),
                               N_EXPERTS - 1).astype(jnp.int32)
    meta = jnp.concatenate([block_expert, (pend[-1:] // tb).astype(jnp.int32)])
    pad_ranges = jnp.concatenate([pstart + counts, pend[:-1], jnp.full((1,), nblk * tb)]).astype(jnp.int32)
    return dest.astype(jnp.int32), gates, meta, pad_ranges, nblk


def _router_weight(w_router):
    return jnp.pad(w_router, ((0, 0), (0, LANES - N_EXPERTS))).astype(BF16)


def _moe_ffn(x, logits, w_gate, w_up, w_down, ln_g, ln_b):
    dest, gates, meta, pad_ranges, nblk = _moe_routing(logits[:, :N_EXPERTS], MOE_TB)
    xs = _moe_dispatch(x, dest, pad_ranges, nblk * MOE_TB)
    y = _moe_experts(xs, meta, w_gate, w_up, w_down)
    return _moe_combine(y, dest, gates, x, ln_g, ln_b)


def _split_w_in(w):
    a_end = 3 * A_QKV_W
    bz_end = a_end + 3 * B_W + B_W
    w_groups = [jnp.concatenate([w[:, s * A_QKV_W + gi * A_GROUP_W:s * A_QKV_W + (gi + 1) * A_GROUP_W]
                                 for s in range(3)], axis=1).astype(BF16) for gi in range(N_GROUPS)]
    zpad = jnp.zeros((w.shape[0], LANES - N_HEADS_B), w.dtype)
    w_b = jnp.concatenate([w[:, a_end:bz_end], w[:, bz_end + 2 * N_HEADS_B:],
                           w[:, bz_end:bz_end + N_HEADS_B], zpad,
                           w[:, bz_end + N_HEADS_B:bz_end + 2 * N_HEADS_B], zpad], axis=1)
    return w_groups, w_b.astype(BF16)


def _hybrid_layer(x, batch, seq, rel_bias, w_in, conv_w, a_log, dt_bias, o_norm_w, w_oa, w_ob, w_out, ln_g, ln_b,
                  w_router=None):
    w_groups, w_b = _split_w_in(w_in)
    proj_b = _matmul(x, w_b, F32, "in_proj_gdn_gates")
    outs, lses = [], []
    for gi, (_, dilation) in enumerate(DSWA_PATTERNS):
        qkv = _in_proj_strided(x, w_groups[gi], dilation, batch, seq)
        o, lse = _dswa_group(qkv, _band_bias(rel_bias, gi, dilation), dilation)
        outs.append(o)
        lses.append(lse)
    yb = _gated_deltanet(proj_b, conv_w, a_log, dt_bias, o_norm_w, batch, seq)
    return _mix_out(outs, lses, yb, proj_b, x, w_oa, w_ob, w_out, ln_g, ln_b, w_router)


def kernel(x, rel_bias, w_in, conv_w, a_log, dt_bias, o_norm_w, w_oa, w_ob, w_out, ln1_g, ln1_b,
           ffn_w_gate, ffn_w_up, ffn_w_down, moe_router, moe_w_gate, moe_w_up, moe_w_down, ln2_g, ln2_b):
    batch, seq, d = x.shape
    assert d == D_MODEL and w_in.shape[0] == DEPTH
    assert all(seq % t == 0 for t in (PROJ_A_TM, OUT_TM, GDN_STEP_CHUNKS * GDN_CHUNK, DSWA_ROWS))
    assert all(seq % (dilation * ATT_BLK) == 0 for _, dilation in DSWA_PATTERNS)
    assert all((batch * seq) % t == 0 for t in (MM_TM, FFN_TM, MOE_TB, CMB_TM, DSP_TM))
    h = x.reshape(batch * seq, d)
    for layer in range(DEPTH):
        j = layer // 2
        is_moe = layer % 2 == 1
        h = _hybrid_layer(h, batch, seq, rel_bias, w_in[layer], conv_w[layer], a_log[layer], dt_bias[layer],
                          o_norm_w[layer], w_oa[layer], w_ob[layer], w_out[layer], ln1_g[layer], ln1_b[layer],
                          _router_weight(moe_router[j]) if is_moe else None)
        if is_moe:
            h, logits = h
            h = _moe_ffn(h, logits, moe_w_gate[j], moe_w_up[j], moe_w_down[j], ln2_g[layer], ln2_b[layer])
        else:
            h = _dense_ffn(h, ffn_w_gate[j], ffn_w_up[j], ffn_w_down[j], ln2_g[layer], ln2_b[layer])
    return h.reshape(batch, seq, d)
```

```python
import functools
import math

import jax
import jax.numpy as jnp
from jax import lax
from jax.experimental import pallas as pl
from jax.experimental.pallas import tpu as pltpu

F32 = jnp.float32
BF16 = jnp.bfloat16

D_MODEL = 1024
DEPTH = 2
DSWA_PATTERNS = ((128, 1), (512, 4), (2048, 16))
N_GROUPS = 3
HEADS_PER_GROUP = 4
HEAD_DIM = 128
A_QKV_W = N_GROUPS * HEADS_PER_GROUP * HEAD_DIM
A_GROUP_W = HEADS_PER_GROUP * HEAD_DIM
NUM_BUCKETS = 32
MAX_DISTANCE = 2048
N_HEADS_B = 8
B_W = N_HEADS_B * HEAD_DIM
CONV_WIDTH = 4
N_EXPERTS = 8
TOP_K = 2
ALPHA = (2 * DEPTH) ** 0.25
LN_EPS = 1e-5
RMS_EPS = 1e-6

LANES = 128
SUBLANES = 8
SUBLANE_BITS = SUBLANES.bit_length() - 1
VMEM_LIMIT = 56 * 1024 * 1024

ATT_BLK = 128
DSWA_ROWS = 2048
GDN_CHUNK = 128
GDN_STEP_CHUNKS = 2
MM_TM = 512
MM_TN = 3200
PROJ_A_TM = 1024
OUT_TM = 512
OUT_SUBTILES = 2
FFN_TM = 1024
FFN_TF = 1024
MOE_TB = 512
MOE_TF = 1792
CMB_TM = 1024
DSP_TM = 1024

DMA_PRIORITIES = 2

NEG_BIG = -1e30


def _cparams(sem, vmem=VMEM_LIMIT, **kw):
    return pltpu.CompilerParams(dimension_semantics=sem, vmem_limit_bytes=vmem, **kw)


def _bdot(a, b):
    return jnp.dot(a.astype(BF16), b.astype(BF16), preferred_element_type=F32)


def _bdot_nt(a, b):
    return lax.dot_general(a.astype(BF16), b.astype(BF16), (((1,), (1,)), ((), ())),
                           preferred_element_type=F32)


def _sigmoid(v):
    return 1.0 / (1.0 + jnp.exp(-v))


def _silu(v):
    return v * _sigmoid(v)


def _layer_norm(v, g, b):
    mu = jnp.mean(v, axis=-1, keepdims=True)
    d = v - mu
    var = jnp.mean(d * d, axis=-1, keepdims=True)
    return d * lax.rsqrt(var + LN_EPS) * g + b


def _mm_kernel(x_ref, w_ref, o_ref):
    o_ref[...] = _bdot(x_ref[...], w_ref[...]).astype(o_ref.dtype)


def _matmul(x, w, out_dtype, name, tm=MM_TM, tn=MM_TN):
    m, k = x.shape
    n = w.shape[1]
    tn = min(tn, n)
    return pl.pallas_call(
        _mm_kernel,
        name=name,
        grid=(n // tn, m // tm),
        in_specs=[pl.BlockSpec((tm, k), lambda j, i: (i, 0)),
                  pl.BlockSpec((k, tn), lambda j, i: (0, j))],
        out_specs=pl.BlockSpec((tm, tn), lambda j, i: (i, j)),
        out_shape=jax.ShapeDtypeStruct((m, n), out_dtype),
        compiler_params=_cparams(("parallel", "parallel")),
    )(x, w)


def _dswa_kernel(q_ref, kc_ref, kp_ref, vc_ref, vp_ref, bias0_ref, bias_ref, o_ref, lse_ref, *, tq):
    nqb = tq // ATT_BLK
    lane = lax.broadcasted_iota(jnp.int32, (ATT_BLK, LANES), 1)
    scale = HEAD_DIM ** -0.5
    exp2_scale = scale * math.log2(math.e)
    for ri in range(q_ref.shape[0]):
        q = q_ref[ri]
        kwin = jnp.concatenate([kp_ref[ri], kc_ref[ri]], axis=0)
        vwin = jnp.concatenate([vp_ref[ri], vc_ref[ri]], axis=0)
        for c in range(nqb):
            b_ref = bias0_ref if c == 0 else bias_ref
            rows = slice(c * ATT_BLK, (c + 1) * ATT_BLK)
            lse_tile = jnp.zeros((ATT_BLK, LANES), F32)
            for h in range(HEADS_PER_GROUP):
                hs = slice(h * HEAD_DIM, (h + 1) * HEAD_DIM)
                kh = kwin[c * ATT_BLK:(c + 2) * ATT_BLK, hs]
                vh = vwin[c * ATT_BLK:(c + 2) * ATT_BLK, hs]
                t = _bdot_nt(q[rows, hs], kh) + b_ref[h]
                m = jnp.max(t, axis=-1, keepdims=True)
                p = jnp.exp2((t - m) * exp2_scale)
                l = jnp.sum(p, axis=-1, keepdims=True)
                o_ref[ri, rows, hs] = _bdot(p, vh) / l
                lse_tile = jnp.where(lane == h, m * scale + jnp.log(l), lse_tile)
            lse_ref[ri, rows, :] = lse_tile


def _in_proj_strided_kernel(x_ref, w_ref, o_ref, *, dilation):
    res = _bdot(x_ref[...], w_ref[...])
    if dilation == 1:
        o_ref[0] = res.astype(o_ref.dtype)
    else:
        rows = res.shape[0] // dilation
        o_ref[...] = jnp.swapaxes(res.reshape(rows, dilation, res.shape[1]), 0, 1).astype(o_ref.dtype)


def _in_proj_strided(x, w, dilation, batch, seq, tm=PROJ_A_TM):
    k = x.shape[1]
    wn = tn = w.shape[1]
    tpb = seq // tm
    return pl.pallas_call(
        functools.partial(_in_proj_strided_kernel, dilation=dilation),
        name=f"in_proj_attn_d{dilation}",
        grid=(batch * tpb, wn // tn),
        in_specs=[pl.BlockSpec((tm, k), lambda i, j: (i, 0)),
                  pl.BlockSpec((k, tn), lambda i, j: (0, j))],
        out_specs=pl.BlockSpec((None, dilation, tm // dilation, tn), lambda i, j: (i // tpb, 0, i % tpb, j)),
        out_shape=jax.ShapeDtypeStruct((batch, dilation, seq // dilation, wn), BF16),
        compiler_params=_cparams(("parallel", "parallel")),
    )(x, w)


def _dswa_group(qkv, bias, dilation):
    batch, _, n, _ = qkv.shape
    tq = min(DSWA_ROWS, n)
    rr = DSWA_ROWS // tq
    nqb = tq // ATT_BLK

    def cur(off):
        return pl.BlockSpec((None, rr, tq, A_GROUP_W), lambda b, r, j: (b, r, j, off))

    def prev(off):
        return pl.BlockSpec((None, rr, ATT_BLK, A_GROUP_W),
                            lambda b, r, j: (b, r, jnp.maximum(j * nqb - 1, 0), off))

    return pl.pallas_call(
        functools.partial(_dswa_kernel, tq=tq),
        name=f"dswa_d{dilation}",
        grid=(batch, dilation // rr, n // tq),
        in_specs=[cur(0), cur(1), prev(1), cur(2), prev(2),
                  pl.BlockSpec((None, HEADS_PER_GROUP, ATT_BLK, 2 * ATT_BLK),
                               lambda b, r, j: (jnp.minimum(j, 1), 0, 0, 0)),
                  pl.BlockSpec((None, HEADS_PER_GROUP, ATT_BLK, 2 * ATT_BLK), lambda b, r, j: (1, 0, 0, 0))],
        out_specs=[pl.BlockSpec((None, rr, tq, A_GROUP_W), lambda b, r, j: (b, r, j, 0)),
                   pl.BlockSpec((None, rr, tq, LANES), lambda b, r, j: (b, r, j, 0))],
        out_shape=[jax.ShapeDtypeStruct((batch, dilation, n, A_GROUP_W), F32),
                   jax.ShapeDtypeStruct((batch, dilation, n, LANES), F32)],
        compiler_params=_cparams(("parallel", "parallel", "parallel")),
    )(qkv, qkv, qkv, qkv, qkv, bias, bias)


def _t5_causal_bucket(dist):
    num_exact = NUM_BUCKETS // 2
    d = jnp.maximum(dist, 1).astype(F32)
    large = num_exact + (jnp.log(d / num_exact) / math.log(MAX_DISTANCE / num_exact)
                         * (NUM_BUCKETS - num_exact)).astype(jnp.int32)
    large = jnp.minimum(large, NUM_BUCKETS - 1)
    return jnp.where(dist < num_exact, dist, large)


def _band_bias(rel_bias, gi, dilation):
    qi = jnp.arange(ATT_BLK)[:, None] + ATT_BLK
    kj = jnp.arange(2 * ATT_BLK)[None, :]
    band = (qi - kj >= 0) & (qi - kj <= ATT_BLK)
    delta = jnp.maximum(qi - kj, 0) * dilation
    table = rel_bias[:, gi * HEADS_PER_GROUP:(gi + 1) * HEADS_PER_GROUP].astype(F32)
    onehot = (_t5_causal_bucket(delta)[..., None] == jnp.arange(NUM_BUCKETS)).astype(F32)
    bias = jnp.einsum('qkn,nh->hqk', onehot, table, precision=lax.Precision.HIGHEST) / (HEAD_DIM ** -0.5)
    return jnp.stack([jnp.where(band & (kj >= ATT_BLK), bias, NEG_BIG), jnp.where(band, bias, NEG_BIG)])


def _dot16(a, b):
    return jnp.dot(a, b, preferred_element_type=F32)


def _unit_lower_inverse(a, row, col):
    base = 16
    heads = list(a)
    eye = (row == col).astype(F32)
    same = lambda sz: (row // sz) == (col // sz)
    blk = same(base)
    a_d = {h: jnp.where(blk, a[h], 0.0) for h in heads}
    t = {h: eye - a_d[h] for h in heads}
    p = {h: a_d[h].astype(BF16) for h in heads}
    for _ in range(3):
        p = {h: _dot16(p[h], p[h]).astype(BF16) for h in heads}
        t = {h: t[h] + _dot16(t[h].astype(BF16), p[h]) for h in heads}
    sz = 2 * base
    while sz <= GDN_CHUNK:
        off = same(sz) & jnp.logical_not(same(sz // 2))
        tb = {h: t[h].astype(BF16) for h in heads}
        m = {h: _dot16(jnp.where(off, a[h], 0.0).astype(BF16), tb[h]).astype(BF16) for h in heads}
        t = {h: t[h] - _dot16(tb[h], m[h]) for h in heads}
        sz *= 2
    return t


def _gdn_kernel(x_ref, halo_ref, z_ref, ba_ref, cw_ref, alog_ref, dtb_ref, onw_ref, y_ref, state_ref):
    c = pl.program_id(1)

    @pl.when(c == 0)
    def _():
        state_ref[...] = jnp.zeros_like(state_ref)

    C = GDN_CHUNK
    chunks = range(x_ref.shape[0] // C)
    rows = {cc: slice(cc * C, (cc + 1) * C) for cc in chunks}
    keep_halo = (c > 0).astype(F32)
    row = lax.broadcasted_iota(jnp.int32, (C, C), 0)
    col = lax.broadcasted_iota(jnp.int32, (C, C), 1)
    incl = row >= col
    strict = row > col
    ones_l = incl.astype(BF16)

    def log_decay_cumsum(cc):
        ba = ba_ref[rows[cc], :]
        a_in = ba[:, LANES:] + dtb_ref[...]
        softplus = jnp.maximum(a_in, 0.0) + jnp.log(1.0 + jnp.exp(-jnp.abs(a_in)))
        g_all = -jnp.exp(alog_ref[...]) * softplus
        g1 = g_all.astype(BF16)
        r1 = g_all - g1.astype(F32)
        g2 = r1.astype(BF16)
        g3 = (r1 - g2.astype(F32)).astype(BF16)
        return (jnp.dot(ones_l, g1, preferred_element_type=F32) + jnp.dot(ones_l, g2, preferred_element_type=F32)
                + jnp.dot(ones_l, g3, preferred_element_type=F32))

    beta_all = {cc: _sigmoid(ba_ref[rows[cc], :LANES]) for cc in chunks}
    G = {cc: log_decay_cumsum(cc) for cc in chunks}
    GT = {cc: G[cc].T for cc in chunks}
    exp_g = {cc: jnp.exp(G[cc]) for cc in chunks}
    beta_exp_g = {cc: beta_all[cc] * exp_g[cc] for cc in chunks}
    exp_rest = {cc: jnp.exp(G[cc][C - 1:C, :] - G[cc]) for cc in chunks}

    def conv_silu(cc, col0):
        cs = slice(col0, col0 + HEAD_DIM)
        xc = x_ref[rows[cc], cs]
        halo = halo_ref[:, cs] * keep_halo if cc == 0 else x_ref[cc * C - SUBLANES:cc * C, cs]
        xf = jnp.concatenate([halo, xc], axis=0)
        w = cw_ref[:, cs]
        y = w[CONV_WIDTH - 1:CONV_WIDTH] * xc
        for i in range(CONV_WIDTH - 1):
            off = SUBLANES - (CONV_WIDTH - 1) + i
            y = y + w[i:i + 1] * xf[off:off + C]
        return _silu(y)

    def l2norm(t, scale=1.0):
        return t * (lax.rsqrt(jnp.sum(t * t, axis=-1, keepdims=True) + RMS_EPS) * scale)

    items = [(cc, h) for cc in chunks for h in range(N_HEADS_B)]
    q = {(cc, h): l2norm(conv_silu(cc, h * HEAD_DIM), HEAD_DIM ** -0.5) for cc, h in items}
    k = {(cc, h): l2norm(conv_silu(cc, B_W + h * HEAD_DIM)) for cc, h in items}
    v = {(cc, h): conv_silu(cc, 2 * B_W + h * HEAD_DIM) for cc, h in items}
    beta = {(cc, h): beta_all[cc][:, h:h + 1] for cc, h in items}
    gc = {(cc, h): G[cc][:, h:h + 1] for cc, h in items}
    g_last = {(cc, h): G[cc][C - 1:C, h:h + 1] for cc, h in items}
    decay = {(cc, h): jnp.exp(jnp.where(incl, gc[cc, h] - GT[cc][h:h + 1, :], NEG_BIG)) for cc, h in items}
    eg = {(cc, h): exp_g[cc][:, h:h + 1] for cc, h in items}
    kb ={it: k[it].astype(BF16) for it in items}
    kq = {it: lax.dot_general(jnp.concatenate([kb[it], q[it].astype(BF16)], axis=0), kb[it],
                              (((1,), (1,)), ((), ())), preferred_element_type=F32) for it in items}
    a = {it: jnp.where(strict, beta[it] * kq[it][:C] * decay[it], 0.0) for it in items}
    t_inv = _unit_lower_inverse(a, row, col)
    rhs = {(cc, h): jnp.concatenate([beta[cc, h] * v[cc, h], beta_exp_g[cc][:, h:h + 1] * k[cc, h]],
                                    axis=1).astype(BF16) for cc, h in items}
    sol = {it: _dot16(t_inv[it].astype(BF16), rhs[it]) for it in items}
    qk = {it: (kq[it][C:] * decay[it]).astype(BF16) for it in items}
    wq = {it: jnp.concatenate([sol[it][:, HEAD_DIM:], q[it] * eg[it]], axis=0).astype(BF16) for it in items}
    k_dec = {(cc, h): (k[cc, h] * exp_rest[cc][:, h:h + 1]).astype(BF16) for cc, h in items}

    heads = range(N_HEADS_B)
    state = {h: state_ref[h] for h in heads}
    for cc in chunks:
        ws = {h: _dot16(wq[cc, h], state[h].astype(BF16)) for h in heads}
        u = {h: (sol[cc, h][:, :HEAD_DIM] - ws[h][:C]).astype(BF16) for h in heads}
        o = {h: ws[h][C:] + _dot16(qk[cc, h], u[h]) for h in heads}
        state = {h: jnp.exp(g_last[cc, h]) * state[h] + lax.dot_general(
            k_dec[cc, h], u[h], (((0,), (0,)), ((), ())), preferred_element_type=F32) for h in heads}
        for h in heads:
            oh = o[h] * lax.rsqrt(jnp.mean(o[h] * o[h], axis=-1, keepdims=True) + RMS_EPS) * onw_ref[...]
            hs = slice(h * HEAD_DIM, (h + 1) * HEAD_DIM)
            y_ref[rows[cc], hs] = (oh * _silu(z_ref[rows[cc], hs])).astype(y_ref.dtype)
    for h in heads:
        state_ref[h] = state[h]


def _gated_deltanet(proj_b, conv_w, a_log, dt_bias, o_norm_w, batch, seq):
    C = GDN_STEP_CHUNKS * GDN_CHUNK
    wb = proj_b.shape[1]
    pb = proj_b.reshape(batch, seq, wb)
    ba_blk = (wb - 2 * LANES) // (2 * LANES)
    pad = lambda t: jnp.pad(t.astype(F32), (0, LANES - t.shape[0])).reshape(1, LANES)
    const = lambda shape: pl.BlockSpec(shape, lambda b, c: (0,) * len(shape))
    y = pl.pallas_call(
        _gdn_kernel,
        name="gated_deltanet",
        grid=(batch, seq // C),
        in_specs=[pl.BlockSpec((None, C, 3 * B_W), lambda b, c: (b, c, 0)),
                  pl.BlockSpec((None, SUBLANES, 3 * B_W),
                               lambda b, c: (b, jnp.maximum(c * (C // SUBLANES) - 1, 0), 0)),
                  pl.BlockSpec((None, C, B_W), lambda b, c: (b, c, 3)),
                  pl.BlockSpec((None, C, 2 * LANES), lambda b, c: (b, c, ba_blk)),
                  const((CONV_WIDTH, 3 * B_W)), const((1, LANES)), const((1, LANES)), const((1, LANES))],
        out_specs=pl.BlockSpec((None, C, B_W), lambda b, c: (b, c, 0)),
        out_shape=jax.ShapeDtypeStruct((batch, seq, B_W), BF16),
        scratch_shapes=[pltpu.VMEM((N_HEADS_B, HEAD_DIM, HEAD_DIM), F32)],
        compiler_params=_cparams(("parallel", "arbitrary")),
    )(pb, pb, pb, pb, conv_w.astype(F32), pad(a_log), pad(dt_bias), o_norm_w.astype(F32).reshape(1, LANES))
    return y.reshape(batch * seq, B_W)


def _mix_out_kernel(o0, o1, o2, l0, l1, l2, yb_ref, ga_ref, gb_ref, x_ref, woa_ref, wob_ref, wout_ref,
                    g_ref, b_ref, *rest, with_router):
    if with_router:
        wr_ref, out_ref, logits_ref, *scratch = rest
    else:
        out_ref, *scratch = rest

    def token_order(ref, scr):
        dilation, rows, width = ref.shape
        if dilation == 1:
            return ref[0]
        if dilation % SUBLANES == 0:
            return jnp.swapaxes(ref[...], 0, 1).reshape(dilation * rows, width)
        planes = []
        for t in range(width // LANES):
            for r in range(dilation):
                scr[t, pl.ds(r, rows, stride=dilation), :] = ref[r, :, t * LANES:(t + 1) * LANES]
            planes.append(scr[t])
        return jnp.concatenate(planes, axis=1)

    outs = (token_order(o0, None), token_order(o1, scratch[0]), token_order(o2, scratch[1]))
    lses = (token_order(l0, None), token_order(l1, scratch[2]), token_order(l2, scratch[3]))
    tm = x_ref.shape[0]
    for sub in range(OUT_SUBTILES):
        rs = slice(sub * tm // OUT_SUBTILES, (sub + 1) * tm // OUT_SUBTILES)
        lse_s = [t[rs] for t in lses]
        m = jnp.maximum(jnp.maximum(lse_s[0], lse_s[1]), lse_s[2])
        es = [jnp.exp(t - m) for t in lse_s]
        inv = 1.0 / (es[0] + es[1] + es[2])
        wgt = [e * inv for e in es]
        ya = []
        for h in range(HEADS_PER_GROUP):
            hs = slice(h * HEAD_DIM, (h + 1) * HEAD_DIM)
            ya.append(wgt[0][:, h:h + 1] * outs[0][rs, hs] + wgt[1][:, h:h + 1] * outs[1][rs, hs]
                      + wgt[2][:, h:h + 1] * outs[2][rs, hs])
        ya = jnp.concatenate(ya, axis=1)
        pa = _bdot(ya, woa_ref[...])
        pb = _bdot(yb_ref[rs, :], wob_ref[...])
        merged = _sigmoid(ga_ref[rs, :]) * pa + _sigmoid(gb_ref[rs, :]) * pb
        mix = _bdot(merged, wout_ref[...])
        out = _layer_norm(ALPHA * x_ref[rs, :] + mix, g_ref[...], b_ref[...])
        out_ref[rs, :] = out
        if with_router:
            logits_ref[rs, :] = _bdot(out, wr_ref[...])


def _mix_out(outs, lses, yb, proj_b, x, w_oa, w_ob, w_out, ln_g, ln_b, w_router=None, tm=OUT_TM):
    n = x.shape[0]
    with_router = w_router is not None
    seq = outs[0].shape[1] * outs[0].shape[2]
    tpb = seq // tm
    rowblk = lambda w, cb=0: pl.BlockSpec((tm, w), lambda i: (i, cb))
    const = lambda a: pl.BlockSpec(a.shape, lambda i: (0, 0), pipeline_mode=pl.Buffered(1))

    def grouped(a):
        d, w = a.shape[1], a.shape[3]
        return pl.BlockSpec((None, d, tm // d, w), lambda i: (i // tpb, 0, i % tpb, 0))

    wa, wb, wo = w_oa.astype(BF16), w_ob.astype(BF16), w_out.astype(BF16)
    g, b = ln_g.reshape(1, D_MODEL), ln_b.reshape(1, D_MODEL)
    router = [w_router] if with_router else []
    out_specs = [rowblk(D_MODEL)] + ([rowblk(LANES)] if with_router else [])
    out_shape = [jax.ShapeDtypeStruct((n, D_MODEL), F32)] + (
        [jax.ShapeDtypeStruct((n, LANES), F32)] if with_router else [])
    res = pl.pallas_call(
        functools.partial(_mix_out_kernel, with_router=with_router),
        name="mix_out_ln",
        grid=(n // tm,),
        in_specs=[grouped(a) for a in (*outs, *lses)]
        + [rowblk(B_W), rowblk(D_MODEL, 4), rowblk(D_MODEL, 5), rowblk(D_MODEL),
           const(wa), const(wb), const(wo), const(g), const(b)] + [const(a) for a in router],
        out_specs=out_specs,
        out_shape=out_shape,
        scratch_shapes=[pltpu.VMEM((A_GROUP_W // LANES, tm, LANES), F32)] * 2 + [pltpu.VMEM((1, tm, LANES), F32)] * 2,
        compiler_params=_cparams(("parallel",)),
    )(*outs, *lses, yb, proj_b, proj_b, x, wa, wb, wo, g, b, *router)
    return tuple(res) if with_router else res[0]


def _ffn_kernel(x_ref, wg_ref, wu_ref, wd_ref, g_ref, b_ref, out_ref):
    tm = x_ref.shape[0]
    dff = wg_ref.shape[1]
    for sub in range(OUT_SUBTILES):
        rs = slice(sub * tm // OUT_SUBTILES, (sub + 1) * tm // OUT_SUBTILES)
        x = x_ref[rs, :]
        xb = x.astype(BF16)
        acc = None
        for c0 in range(0, dff, FFN_TF):
            c1 = min(c0 + FFN_TF, dff)
            gate = jnp.dot(xb, wg_ref[:, c0:c1], preferred_element_type=F32)
            up = jnp.dot(xb, wu_ref[:, c0:c1], preferred_element_type=F32)
            part = _bdot(_silu(gate) * up, wd_ref[c0:c1, :])
            acc = part if acc is None else acc + part
        out_ref[rs, :] = _layer_norm(ALPHA * x + acc, g_ref[...], b_ref[...])


def _dense_ffn(x, w_gate, w_up, w_down, ln_g, ln_b, tm=FFN_TM):
    n = x.shape[0]
    g, b = ln_g.reshape(1, D_MODEL), ln_b.reshape(1, D_MODEL)
    resident = lambda a: pl.BlockSpec(a.shape, lambda i: (0, 0), pipeline_mode=pl.Buffered(1))
    wg, wu, wd = w_gate.astype(BF16), w_up.astype(BF16), w_down.astype(BF16)
    return pl.pallas_call(
        _ffn_kernel,
        name="dense_ffn_ln",
        grid=(n // tm,),
        in_specs=[pl.BlockSpec((tm, D_MODEL), lambda i: (i, 0)),
                  resident(wg), resident(wu), resident(wd), resident(g), resident(b)],
        out_specs=pl.BlockSpec((tm, D_MODEL), lambda i: (i, 0)),
        out_shape=jax.ShapeDtypeStruct((n, D_MODEL), F32),
        compiler_params=_cparams(("parallel",)),
    )(x, wg, wu, wd, g, b)


def _gather_rows(idx_ref, src_hbm, dst, sem, count):
    def body(j, carry):
        for u in range(DMA_PRIORITIES):
            i = DMA_PRIORITIES * j + u
            pltpu.make_async_copy(src_hbm.at[pl.ds(idx_ref[0, i], 1)], dst.at[pl.ds(i, 1)], sem).start(priority=u)
        return carry
    lax.fori_loop(0, count // DMA_PRIORITIES, body, 0, unroll=4)


def _wait_rows(src_hbm, dst, sem, count):
    pltpu.make_async_copy(src_hbm.at[pl.ds(0, count)], dst, sem).wait()


def _dispatch_kernel(dest_ref, pad_ref, x_ref, xs_out, zero_ref, sem):
    tm = x_ref.shape[0]
    zero_row = lambda r: pltpu.make_async_copy(zero_ref.at[pl.ds(0, 1)], xs_out.at[pl.ds(r, 1)], sem.at[1])
    zero_tile = lambda t: pltpu.make_async_copy(
        zero_ref, xs_out.at[pl.ds(pl.multiple_of(t * SUBLANES, SUBLANES), SUBLANES)], sem.at[1])

    def for_padding_rows(fn):
        for e in range(N_EXPERTS):
            lo, hi = pad_ref[0, e], pad_ref[0, N_EXPERTS + e]
            lo_tile = lax.shift_right_logical(lo + (SUBLANES - 1), SUBLANE_BITS)
            hi_tile = lax.shift_right_logical(hi, SUBLANE_BITS)
            lax.fori_loop(lo, jnp.minimum(lo_tile * SUBLANES, hi), lambda r, c: (fn(zero_row(r)), c)[1], 0)
            lax.fori_loop(lo_tile, hi_tile, lambda t, c: (fn(zero_tile(t)), c)[1], 0)

    @pl.when(pl.program_id(0) == 0)
    def _():
        zero_ref[...] = jnp.zeros_like(zero_ref)
        for_padding_rows(lambda copy: copy.start())

    @pl.when(pl.program_id(0) == pl.num_programs(0) - 1)
    def _():
        for_padding_rows(lambda copy: copy.wait())

    def body(i, carry):
        for k in range(TOP_K):
            pltpu.make_async_copy(x_ref.at[pl.ds(i, 1)], xs_out.at[pl.ds(dest_ref[0, TOP_K * i + k], 1)],
                                  sem.at[0]).start(priority=k % DMA_PRIORITIES)
        return carry
    lax.fori_loop(0, tm, body, 0, unroll=4)
    for _ in range(TOP_K):
        pltpu.make_async_copy(x_ref, xs_out.at[pl.ds(0, tm)], sem.at[0]).wait()


def _moe_dispatch(x, dest, pad_ranges, n_rows, tm=DSP_TM):
    n = x.shape[0]
    dest3 = dest.reshape(n // tm, 1, TOP_K * tm)
    return pl.pallas_call(
        _dispatch_kernel,
        name="moe_dispatch",
        grid=(n // tm,),
        in_specs=[pl.BlockSpec((None, 1, TOP_K * tm), lambda i: (i, 0, 0), memory_space=pltpu.SMEM),
                  pl.BlockSpec((1, 2 * N_EXPERTS), lambda i: (0, 0), memory_space=pltpu.SMEM),
                  pl.BlockSpec((tm, D_MODEL), lambda i: (i, 0))],
        out_specs=pl.BlockSpec(memory_space=pl.ANY),
        out_shape=jax.ShapeDtypeStruct((n_rows, D_MODEL), F32),
        scratch_shapes=[pltpu.VMEM((SUBLANES, D_MODEL), F32), pltpu.SemaphoreType.DMA((2,))],
        compiler_params=_cparams(("arbitrary",), disable_bounds_checks=True),
    )(dest3, pad_ranges.reshape(1, 2 * N_EXPERTS), x)


def _moe_kernel(meta_ref, x_ref, wg_ref, wu_ref, wd_ref, y_ref, xb_ref, acc_ref):
    b = pl.program_id(0)
    f = pl.program_id(1)
    nblk = pl.num_programs(0)
    used = b < meta_ref[nblk]

    @pl.when(used)
    def _():
        @pl.when(f == 0)
        def _():
            xb_ref[...] = x_ref[...].astype(BF16)
            acc_ref[...] = jnp.zeros_like(acc_ref)

        xb = xb_ref[...]
        gate = jnp.dot(xb, wg_ref[...], preferred_element_type=F32)
        up = jnp.dot(xb, wu_ref[...], preferred_element_type=F32)
        acc_ref[...] += _bdot(_silu(gate) * up, wd_ref[...])

    @pl.when(f == pl.num_programs(1) - 1)
    def _():
        @pl.when(used)
        def _():
            y_ref[...] = acc_ref[...]

        @pl.when(jnp.logical_not(used))
        def _():
            y_ref[...] = jnp.zeros_like(y_ref)


def _moe_experts(xs, meta, w_gate, w_up, w_down, tb=MOE_TB, tf=MOE_TF):
    nblk = xs.shape[0] // tb
    dffe = w_gate.shape[2]
    grid_spec = pltpu.PrefetchScalarGridSpec(
        num_scalar_prefetch=1,
        grid=(nblk, dffe // tf),
        in_specs=[pl.BlockSpec((tb, D_MODEL), lambda b, f, m: (jnp.minimum(b, m[nblk] - 1), 0)),
                  pl.BlockSpec((None, D_MODEL, tf), lambda b, f, m: (m[b], 0, f)),
                  pl.BlockSpec((None, D_MODEL, tf), lambda b, f, m: (m[b], 0, f)),
                  pl.BlockSpec((None, tf, D_MODEL), lambda b, f, m: (m[b], f, 0))],
        out_specs=pl.BlockSpec((tb, D_MODEL), lambda b, f, m: (b, 0)),
        scratch_shapes=[pltpu.VMEM((tb, D_MODEL), BF16), pltpu.VMEM((tb, D_MODEL), F32)],
    )
    return pl.pallas_call(
        _moe_kernel,
        name="moe_experts",
        grid_spec=grid_spec,
        out_shape=jax.ShapeDtypeStruct((nblk * tb, D_MODEL), F32),
        compiler_params=_cparams(("parallel", "arbitrary")),
    )(meta, xs, w_gate.astype(BF16), w_up.astype(BF16), w_down.astype(BF16))


def _combine_kernel(pos0_ref, posn_ref, y_hbm, gates_ref, x_ref, g_ref, b_ref, out_ref, ybuf, sem):
    i = pl.program_id(0)
    tm = out_ref.shape[0]
    slot = i % 2

    @pl.when(i == 0)
    def _():
        _gather_rows(pos0_ref, y_hbm, ybuf.at[0], sem.at[0], TOP_K * tm)

    _wait_rows(y_hbm, ybuf.at[slot], sem.at[slot], TOP_K * tm)

    @pl.when(i + 1 < pl.num_programs(0))
    def _():
        _gather_rows(posn_ref, y_hbm, ybuf.at[1 - slot], sem.at[1 - slot], TOP_K * tm)

    gates = gates_ref[...]
    f = gates[:, 0:1] * ybuf[slot, :tm] + gates[:, 1:2] * ybuf[slot, tm:]
    out_ref[...] = _layer_norm(ALPHA * x_ref[...] + f, g_ref[...], b_ref[...])


def _moe_combine(y, pos, gates, x, ln_g, ln_b, tm=CMB_TM):
    n = x.shape[0]
    nt = n // tm
    pos3 = pos.reshape(nt, tm, TOP_K).transpose(0, 2, 1).reshape(nt, 1, TOP_K * tm)
    g, b = ln_g.reshape(1, D_MODEL), ln_b.reshape(1, D_MODEL)
    smem_blk = lambda imap: pl.BlockSpec((None, 1, TOP_K * tm), imap, memory_space=pltpu.SMEM)
    return pl.pallas_call(
        _combine_kernel,
        name="moe_combine_ln",
        grid=(nt,),
        in_specs=[smem_blk(lambda i: (0, 0, 0)),
                  smem_blk(lambda i: (jnp.minimum(i + 1, nt - 1), 0, 0)),
                  pl.BlockSpec(memory_space=pl.ANY),
                  pl.BlockSpec((tm, TOP_K), lambda i: (i, 0)),
                  pl.BlockSpec((tm, D_MODEL), lambda i: (i, 0)),
                  pl.BlockSpec((1, D_MODEL), lambda i: (0, 0)),
                  pl.BlockSpec((1, D_MODEL), lambda i: (0, 0))],
        out_specs=pl.BlockSpec((tm, D_MODEL), lambda i: (i, 0)),
        out_shape=jax.ShapeDtypeStruct((n, D_MODEL), F32),
        scratch_shapes=[pltpu.VMEM((2, TOP_K * tm, D_MODEL), F32), pltpu.SemaphoreType.DMA((2,))],
        compiler_params=_cparams(("arbitrary",), disable_bounds_checks=True),
    )(pos3, pos3, y, gates, x, g, b)


def _moe_routing(logits, tb):
    n = logits.shape[0]
    top_logit, top_idx = lax.top_k(logits, TOP_K)
    gates = jax.nn.softmax(top_logit, axis=-1)
    na = n * TOP_K
    e_flat = top_idx.reshape(-1).astype(jnp.int32)
    onehot = (e_flat[None, :] == jnp.arange(N_EXPERTS, dtype=jnp.int32)[:, None]).astype(jnp.int32)
    running = jnp.cumsum(onehot, axis=1)
    counts = running[:, -1]
    padded = (counts + tb - 1) // tb * tb
    pend = jnp.cumsum(padded)
    pstart = pend - padded
    dest = jnp.sum(onehot * (pstart[:, None] + running - 1), axis=0).reshape(n, TOP_K)
    nblk = -(-na // tb) + N_EXPERTS
    block_expert = jnp.minimum(jnp.searchsorted(pend, jnp.arange(nblk, dtype=jnp.int32) * tb, side='right'),
                               N_EXPERTS - 1).astype(jnp.int32)
    meta = jnp.concatenate([block_expert, (pend[-1:] // tb).astype(jnp.int32)])
    pad_ranges = jnp.concatenate([pstart + counts, pend[:-1], jnp.full((1,), nblk * tb)]).astype(jnp.int32)
    return dest.astype(jnp.int32), gates, meta, pad_ranges, nblk


def _router_weight(w_router):
    return jnp.pad(w_router, ((0, 0), (0, LANES - N_EXPERTS))).astype(BF16)


def _moe_ffn(x, logits, w_gate, w_up, w_down, ln_g, ln_b):
    dest, gates, meta, pad_ranges, nblk = _moe_routing(logits[:, :N_EXPERTS], MOE_TB)
    xs = _moe_dispatch(x, dest, pad_ranges, nblk * MOE_TB)
    y = _moe_experts(xs, meta, w_gate, w_up, w_down)
    return _moe_combine(y, dest, gates, x, ln_g, ln_b)


def _split_w_in(w):
    a_end = 3 * A_QKV_W
    bz_end = a_end + 3 * B_W + B_W
    w_groups = [jnp.concatenate([w[:, s * A_QKV_W + gi * A_GROUP_W:s * A_QKV_W + (gi + 1) * A_GROUP_W]
                                 for s in range(3)], axis=1).astype(BF16) for gi in range(N_GROUPS)]
    zpad = jnp.zeros((w.shape[0], LANES - N_HEADS_B), w.dtype)
    w_b = jnp.concatenate([w[:, a_end:bz_end], w[:, bz_end + 2 * N_HEADS_B:],
                           w[:, bz_end:bz_end + N_HEADS_B], zpad,
                           w[:, bz_end + N_HEADS_B:bz_end + 2 * N_HEADS_B], zpad], axis=1)
    return w_groups, w_b.astype(BF16)


def _hybrid_layer(x, batch, seq, rel_bias, w_in, conv_w, a_log, dt_bias, o_norm_w, w_oa, w_ob, w_out, ln_g, ln_b,
                  w_router=None):
    w_groups, w_b = _split_w_in(w_in)
    proj_b = _matmul(x, w_b, F32, "in_proj_gdn_gates")
    outs, lses = [], []
    for gi, (_, dilation) in enumerate(DSWA_PATTERNS):
        qkv = _in_proj_strided(x, w_groups[gi], dilation, batch, seq)
        o, lse = _dswa_group(qkv, _band_bias(rel_bias, gi, dilation), dilation)
        outs.append(o)
        lses.append(lse)
    yb = _gated_deltanet(proj_b, conv_w, a_log, dt_bias, o_norm_w, batch, seq)
    return _mix_out(outs, lses, yb, proj_b, x, w_oa, w_ob, w_out, ln_g, ln_b, w_router)


def kernel(x, rel_bias, w_in, conv_w, a_log, dt_bias, o_norm_w, w_oa, w_ob, w_out, ln1_g, ln1_b,
           ffn_w_gate, ffn_w_up, ffn_w_down, moe_router, moe_w_gate, moe_w_up, moe_w_down, ln2_g, ln2_b):
    batch, seq, d = x.shape
    assert d == D_MODEL and w_in.shape[0] == DEPTH
    assert all(seq % t == 0 for t in (PROJ_A_TM, OUT_TM, GDN_STEP_CHUNKS * GDN_CHUNK, DSWA_ROWS))
    assert all(seq % (dilation * ATT_BLK) == 0 for _, dilation in DSWA_PATTERNS)
    assert all((batch * seq) % t == 0 for t in (MM_TM, FFN_TM, MOE_TB, CMB_TM, DSP_TM))
    h = x.reshape(batch * seq, d)
    for layer in range(DEPTH):
        j = layer // 2
        is_moe = layer % 2 == 1
        h = _hybrid_layer(h, batch, seq, rel_bias, w_in[layer], conv_w[layer], a_log[layer], dt_bias[layer],
                          o_norm_w[layer], w_oa[layer], w_ob[layer], w_out[layer], ln1_g[layer], ln1_b[layer],
                          _router_weight(moe_router[j]) if is_moe else None)
        if is_moe:
            h, logits = h
            h = _moe_ffn(h, logits, moe_w_gate[j], moe_w_up[j], moe_w_down[j], ln2_g[layer], ln2_b[layer])
        else:
            h = _dense_ffn(h, ffn_w_gate[j], ffn_w_up[j], ffn_w_down[j], ln2_g[layer], ln2_b[layer])
    return h.reshape(batch, seq, d)
```
